```python
import math
import jax
import jax.numpy as jnp
from jax import lax
import numpy as np

D_MODEL = 1024
BATCH = 8
SEQ = 4096
DEPTH = 1

CHUNK = 64
Q_BLOCK = 128
A_HEADS = 4
A_HEAD_DIM = 64
A_WIDTH = A_HEADS * 2 * A_HEAD_DIM
B_HEADS = 8
B_HEAD_DIM = 64
B_WIDTH = B_HEADS * B_HEAD_DIM
B_PAST_CHUNKS = 8
B_BAND = B_PAST_CHUNKS + 1
REL_CLIP = 256
MIX_WIDTH = A_WIDTH + B_WIDTH
N_EXPERTS = 32
TOP_K = 4
D_FF_EXPERT = D_MODEL
SWIGLU_ALPHA = 1.702
SWIGLU_LIMIT = 7.0
MOE_BLOCK = 512
LN_EPS = 1e-5
RMS_EPS = 1e-5
DEEPNORM_ALPHA = (2 * DEPTH) ** 0.25
DEEPNORM_BETA = (8 * DEPTH) ** -0.25

kernel_name = 'hybrid_diffattn_chunkband_moe_deepnorm'


def layer_norm(x, g, b):
    xf = x.astype(jnp.float32)
    mu = jnp.mean(xf, axis=-1, keepdims=True)
    var = jnp.mean(jnp.square(xf - mu), axis=-1, keepdims=True)
    y = (xf - mu) * lax.rsqrt(var + LN_EPS) * g.astype(jnp.float32) + b.astype(jnp.float32)
    return y.astype(x.dtype)


def rms_norm(x, g):
    xf = x.astype(jnp.float32)
    y = xf * lax.rsqrt(jnp.mean(jnp.square(xf), axis=-1, keepdims=True) + RMS_EPS) * g.astype(jnp.float32)
    return y.astype(x.dtype)


def alibi_slopes():
    return jnp.exp2(-8.0 * jnp.arange(1, A_HEADS + 1, dtype=jnp.float32) / A_HEADS)


def diff_attention(q, k, v, lam, subln_g, lam_init):
    bsz, seq = q.shape[0], q.shape[1]
    nb = seq // Q_BLOCK
    scale = A_HEAD_DIM ** -0.5
    slopes = alibi_slopes()
    kpos = jnp.arange(seq)
    qb = jnp.moveaxis(q.reshape(bsz, nb, Q_BLOCK, A_HEADS, 2, A_HEAD_DIM), 1, 0)

    def block(args):
        q_blk, bi = args
        qpos = bi * Q_BLOCK + jnp.arange(Q_BLOCK)
        dist = jnp.abs(qpos[:, None] - kpos[None, :]).astype(jnp.float32)
        allowed = (kpos[None, :] // CHUNK) <= (qpos[:, None] // CHUNK)
        alibi = -slopes[:, None, None] * dist[None]
        s = jnp.einsum('bqhmd,bkhmd->bhmqk', q_blk, k, preferred_element_type=jnp.float32) * scale
        s = jnp.where(allowed, s + alibi[None, :, None], -jnp.inf)
        p = jax.nn.softmax(s, axis=-1)
        a = p[:, :, 0] - lam * p[:, :, 1]
        return jnp.einsum('bhqk,bkhe->bqhe', a.astype(v.dtype), v)

    o = lax.map(block, (qb, jnp.arange(nb)))
    o = jnp.moveaxis(o, 0, 1).reshape(bsz, seq, A_HEADS, 2 * A_HEAD_DIM)
    o = rms_norm(o, subln_g) * (1.0 - lam_init)
    return o.reshape(bsz, seq, A_WIDTH)


def chunk_band_attention(q, k, v, rel_bias):
    bsz, seq = q.shape[0], q.shape[1]
    nc = seq // CHUNK
    band = B_BAND * CHUNK
    scale = B_HEAD_DIM ** -0.5
    qc = q.reshape(bsz, nc, CHUNK, B_HEADS, B_HEAD_DIM)
    pad = ((0, 0), (B_PAST_CHUNKS, 0), (0, 0), (0, 0), (0, 0))
    kc = jnp.pad(k.reshape(bsz, nc, CHUNK, B_HEADS, B_HEAD_DIM), pad)
    vc = jnp.pad(v.reshape(bsz, nc, CHUNK, B_HEADS, B_HEAD_DIM), pad)
    band_idx = jnp.arange(nc)[:, None] + jnp.arange(B_BAND)[None, :]
    kb = kc[:, band_idx].reshape(bsz, nc, band, B_HEADS, B_HEAD_DIM)
    vb = vc[:, band_idx].reshape(bsz, nc, band, B_HEADS, B_HEAD_DIM)
    qi = jnp.arange(CHUNK)
    kj = jnp.arange(band)
    rel = qi[:, None] + B_PAST_CHUNKS * CHUNK - kj[None, :]
    rel_idx = jnp.clip(rel, -REL_CLIP, REL_CLIP) + REL_CLIP
    bias = rel_bias[:, rel_idx].astype(jnp.float32)
    valid = (jnp.arange(nc)[:, None] - B_PAST_CHUNKS + kj[None, :] // CHUNK) >= 0
    s = jnp.einsum('bcqhd,bckhd->bhcqk', qc, kb, preferred_element_type=jnp.float32) * scale
    s = jnp.where(valid[None, None, :, None, :], s + bias[:, None], -jnp.inf)
    p = jax.nn.softmax(s, axis=-1)
    o = jnp.einsum('bhcqk,bckhd->bcqhd', p.astype(v.dtype), vb)
    return o.reshape(bsz, seq, B_WIDTH)


def moe_ffn(h, router_w, router_b, w1, b1, w2, b2):
    bsz, seq, d = h.shape
    t = h.reshape(-1, d)
    n_tok = t.shape[0]
    logits = jnp.einsum('td,de->te', t, router_w, preferred_element_type=jnp.float32) + router_b.astype(jnp.float32)
    top_logits, top_idx = lax.top_k(logits, TOP_K)
    gates = jax.nn.softmax(top_logits, axis=-1)
    m = n_tok * TOP_K
    flat_e = top_idx.reshape(m)
    order = jnp.argsort(flat_e)
    sorted_e = flat_e[order]
    counts = jnp.zeros((N_EXPERTS,), jnp.int32).at[flat_e].add(1)
    padded = (counts + MOE_BLOCK - 1) // MOE_BLOCK * MOE_BLOCK
    start = jnp.cumsum(counts) - counts
    pend = jnp.cumsum(padded)
    pstart = pend - padded
    dest = pstart[sorted_e] + jnp.arange(m) - start[sorted_e]
    n_blocks = -(-m // MOE_BLOCK) + N_EXPERTS
    n_slots = n_blocks * MOE_BLOCK
    slot_token = jnp.full((n_slots,), n_tok, jnp.int32).at[dest].set((order // TOP_K).astype(jnp.int32))
    slot_gate = jnp.zeros((n_slots,), jnp.float32).at[dest].set(gates.reshape(m)[order])
    block_expert = jnp.minimum(jnp.searchsorted(pend, jnp.arange(n_blocks) * MOE_BLOCK, side='right'), N_EXPERTS - 1)
    t_pad = jnp.concatenate([t, jnp.zeros((1, d), t.dtype)], axis=0)
    xs = t_pad[slot_token].reshape(n_blocks, MOE_BLOCK, d)

    def expert_block(args):
        xb, e = args
        hid = xb @ w1[e] + b1[e]
        x_glu = jnp.minimum(hid[:, ::2], SWIGLU_LIMIT)
        x_lin = jnp.clip(hid[:, 1::2], -SWIGLU_LIMIT, SWIGLU_LIMIT)
        act = x_glu * jax.nn.sigmoid(SWIGLU_ALPHA * x_glu) * (x_lin + 1.0)
        return act @ w2[e] + b2[e]

    ys = lax.map(expert_block, (xs, block_expert)).reshape(n_slots, d)
    y = jax.ops.segment_sum(ys * slot_gate[:, None].astype(ys.dtype), slot_token, num_segments=n_tok + 1)[:n_tok]
    return y.reshape(bsz, seq, d)


def setup_inputs(seed: int = 0) -> dict:
    key = jax.random.key(seed)
    ks = jax.random.split(key, 24)
    nrm = jax.random.normal
    f32 = jnp.float32
    x = nrm(ks[0], (BATCH, SEQ, D_MODEL), f32)
    ln_in_g = 1.0 + 0.02 * nrm(ks[1], (D_MODEL,), f32)
    ln_in_b = 0.02 * nrm(ks[2], (D_MODEL,), f32)
    beta = DEEPNORM_BETA
    col_scale = jnp.concatenate([
        jnp.ones((2 * A_WIDTH,), f32), jnp.full((A_WIDTH,), beta, f32),
        jnp.ones((2 * B_WIDTH,), f32), jnp.full((B_WIDTH,), beta, f32)])
    w_in = nrm(ks[3], (DEPTH, D_MODEL, 3 * MIX_WIDTH), f32) * D_MODEL ** -0.5 * col_scale
    lambda_q1 = 0.1 * nrm(ks[4], (DEPTH, A_HEAD_DIM), f32)
    lambda_k1 = 0.1 * nrm(ks[5], (DEPTH, A_HEAD_DIM), f32)
    lambda_q2 = 0.1 * nrm(ks[6], (DEPTH, A_HEAD_DIM), f32)
    lambda_k2 = 0.1 * nrm(ks[7], (DEPTH, A_HEAD_DIM), f32)
    subln_g = 1.0 + 0.02 * nrm(ks[8], (DEPTH, 2 * A_HEAD_DIM), f32)
    rel_bias = 0.1 * nrm(ks[9], (DEPTH, B_HEADS, 2 * REL_CLIP + 1), f32)
    w_out = nrm(ks[10], (DEPTH, MIX_WIDTH, D_MODEL), f32) * MIX_WIDTH ** -0.5 * beta
    ln1_g = 1.0 + 0.02 * nrm(ks[11], (DEPTH, D_MODEL), f32)
    ln1_b = 0.02 * nrm(ks[12], (DEPTH, D_MODEL), f32)
    router_w = nrm(ks[13], (DEPTH, D_MODEL, N_EXPERTS), f32) * D_MODEL ** -0.5
    router_b = 0.01 * nrm(ks[14], (DEPTH, N_EXPERTS), f32)
    w1 = nrm(ks[15], (DEPTH, N_EXPERTS, D_MODEL, 2 * D_FF_EXPERT), f32) * D_MODEL ** -0.5
    b1 = 0.01 * nrm(ks[16], (DEPTH, N_EXPERTS, 2 * D_FF_EXPERT), f32)
    w2 = nrm(ks[17], (DEPTH, N_EXPERTS, D_FF_EXPERT, D_MODEL), f32) * D_FF_EXPERT ** -0.5 * beta
    b2 = 0.01 * nrm(ks[18], (DEPTH, N_EXPERTS, D_MODEL), f32)
    ln2_g = 1.0 + 0.02 * nrm(ks[19], (DEPTH, D_MODEL), f32)
    ln2_b = 0.02 * nrm(ks[20], (DEPTH, D_MODEL), f32)
    return {'x': x, 'ln_in_g': ln_in_g, 'ln_in_b': ln_in_b, 'w_in': w_in,
            'lambda_q1': lambda_q1, 'lambda_k1': lambda_k1, 'lambda_q2': lambda_q2, 'lambda_k2': lambda_k2,
            'subln_g': subln_g, 'rel_bias': rel_bias, 'w_out': w_out, 'ln1_g': ln1_g, 'ln1_b': ln1_b,
            'router_w': router_w, 'router_b': router_b, 'w1': w1, 'b1': b1, 'w2': w2, 'b2': b2,
            'ln2_g': ln2_g, 'ln2_b': ln2_b}


def reference(x, ln_in_g, ln_in_b, w_in, lambda_q1, lambda_k1, lambda_q2, lambda_k2, subln_g, rel_bias,
              w_out, ln1_g, ln1_b, router_w, router_b, w1, b1, w2, b2, ln2_g, ln2_b):
    bsz, seq, _ = x.shape
    f32 = jnp.float32
    split_at = [A_WIDTH, 2 * A_WIDTH, 3 * A_WIDTH, 3 * A_WIDTH + B_WIDTH, 3 * A_WIDTH + 2 * B_WIDTH]
    h = layer_norm(x, ln_in_g, ln_in_b)
    for l in range(DEPTH):
        lam_init = 0.8 - 0.6 * math.exp(-0.3 * l)
        lam = (jnp.exp(jnp.sum(lambda_q1[l].astype(f32) * lambda_k1[l].astype(f32)))
               - jnp.exp(jnp.sum(lambda_q2[l].astype(f32) * lambda_k2[l].astype(f32))) + lam_init)
        proj = jnp.einsum('bsd,de->bse', h, w_in[l])
        a_q, a_k, a_v, b_q, b_k, b_v = jnp.split(proj, split_at, axis=-1)
        o_a = diff_attention(a_q.reshape(bsz, seq, A_HEADS, 2, A_HEAD_DIM),
                             a_k.reshape(bsz, seq, A_HEADS, 2, A_HEAD_DIM),
                             a_v.reshape(bsz, seq, A_HEADS, 2 * A_HEAD_DIM),
                             lam, subln_g[l], lam_init)
        o_b = chunk_band_attention(b_q.reshape(bsz, seq, B_HEADS, B_HEAD_DIM),
                                   b_k.reshape(bsz, seq, B_HEADS, B_HEAD_DIM),
                                   b_v.reshape(bsz, seq, B_HEADS, B_HEAD_DIM),
                                   rel_bias[l])
        mix = jnp.einsum('bse,ed->bsd', jnp.concatenate([o_a, o_b], axis=-1), w_out[l])
        h = layer_norm(DEEPNORM_ALPHA * h + mix, ln1_g[l], ln1_b[l])
        ffn = moe_ffn(h, router_w[l], router_b[l], w1[l], b1[l], w2[l], b2[l])
        h = layer_norm(DEEPNORM_ALPHA * h + ffn, ln2_g[l], ln2_b[l])
    return h
```

```python
import functools
import math

import jax
import jax.numpy as jnp
from jax import lax
from jax.experimental import pallas as pl
from jax.experimental.pallas import tpu as pltpu

F32 = jnp.float32
BF16 = jnp.bfloat16

CHUNK = 64
A_HEADS = 4
A_HEAD_DIM = 64
A_WIDTH = A_HEADS * 2 * A_HEAD_DIM
B_HEADS = 8
B_HEAD_DIM = 64
B_WIDTH = B_HEADS * B_HEAD_DIM
B_PAST_CHUNKS = 8
REL_CLIP = 256
N_EXPERTS = 32
TOP_K = 4
SWIGLU_ALPHA = 1.702
SWIGLU_LIMIT = 7.0
MOE_BLOCK = 512
LN_EPS = 1e-5
RMS_EPS = 1e-5
DEPTH = 1
DEEPNORM_ALPHA = (2 * DEPTH) ** 0.25
LAM_INIT = 0.8 - 0.6 * math.exp(-0.3 * 0)

LOG2E = 1.4426950408889634
NEG = -1e30
LANES = 128
SUBLANES = 8
ROW_TILE = 512
VMEM_LIMIT = 48 * 1024 * 1024


def _layer_norm(x, g, b):
    mu = jnp.mean(x, axis=-1, keepdims=True)
    xc = x - mu
    var = jnp.mean(xc * xc, axis=-1, keepdims=True)
    return xc * lax.rsqrt(var + LN_EPS) * g + b


def _ln_qkv_kernel(x_ref, g_ref, b_ref, w_ref, cs_ref, o_ref):
    h = _layer_norm(x_ref[...], g_ref[...], b_ref[...])
    hb = h.astype(BF16)
    n_out = w_ref.shape[1]
    for c in range(n_out // ROW_TILE):
        sl = slice(c * ROW_TILE, (c + 1) * ROW_TILE)
        acc = jnp.dot(hb, w_ref[:, sl], preferred_element_type=F32)
        o_ref[:, sl] = (acc * cs_ref[:, sl]).astype(BF16)


def _ln_qkv(x2, g, b, w_bf, col_scale):
    t, d = x2.shape
    n_out = w_bf.shape[1]
    return pl.pallas_call(
        _ln_qkv_kernel,
        grid=(t // ROW_TILE,),
        in_specs=[
            pl.BlockSpec((ROW_TILE, d), lambda i: (i, 0)),
            pl.BlockSpec((1, d), lambda i: (0, 0)),
            pl.BlockSpec((1, d), lambda i: (0, 0)),
            pl.BlockSpec((d, n_out), lambda i: (0, 0)),
            pl.BlockSpec((1, n_out), lambda i: (0, 0)),
        ],
        out_specs=pl.BlockSpec((ROW_TILE, n_out), lambda i: (i, 0)),
        out_shape=jax.ShapeDtypeStruct((t, n_out), BF16),
        compiler_params=pltpu.CompilerParams(
            dimension_semantics=("arbitrary",), vmem_limit_bytes=VMEM_LIMIT),
        name="ln_qkv",
    )(x2, g, b, w_bf, col_scale)


def _diff_attn_kernel(lam_ref, g_ref, cbase_ref, cslope_ref, q_ref, k_ref, v_ref, o_ref, *, tq, tk):
    i = pl.program_id(2)
    q = q_ref[0]
    lane = lax.broadcasted_iota(jnp.int32, q.shape, 1)
    zero = jnp.zeros_like(q)
    qq = jnp.concatenate([jnp.where(lane < A_HEAD_DIM, q, zero),
                          jnp.where(lane >= A_HEAD_DIM, q, zero)], axis=0)
    q0 = i * tq
    jd = q0 // tk
    cbase = cbase_ref[0]
    cslope = cslope_ref[0]

    def step(j, carry, diag):
        m, l, acc = carry
        start = pl.multiple_of(j * tk, tk)
        kb = k_ref[0, pl.ds(start, tk), :]
        vb = v_ref[0, pl.ds(start, tk), :]
        s = lax.dot_general(qq, kb, (((1,), (1,)), ((), ())), preferred_element_type=F32)
        off = j * tk - q0
        if diag:
            rowp = lax.broadcasted_iota(jnp.int32, s.shape, 0)
            rowp = jnp.where(rowp >= tq, rowp - tq, rowp)
            colp = lax.broadcasted_iota(jnp.int32, s.shape, 1) + off
            allowed = (colp >> 6) <= (rowp >> 6)
            bias = cslope * (rowp - jnp.abs(rowp - colp)).astype(F32)
            s = jnp.where(allowed, s + bias, NEG)
        else:
            s = s + (cbase + cslope * off.astype(F32))
        m_new = jnp.maximum(m, jnp.max(s, axis=1, keepdims=True))
        alpha = jnp.exp2(m - m_new)
        p = jnp.exp2(s - m_new)
        l = alpha * l + jnp.sum(p, axis=1, keepdims=True)
        acc = alpha * acc + jnp.dot(p.astype(BF16), vb, preferred_element_type=F32)
        return m_new, l, acc

    init = (jnp.full((2 * tq, 1), NEG, F32), jnp.zeros((2 * tq, 1), F32),
            jnp.zeros((2 * tq, LANES), F32))
    carry = lax.fori_loop(0, jd, lambda j, c: step(j, c, False), init)
    _, l, acc = step(jd, carry, True)
    o_all = acc / l
    lv = lam_ref[...]
    lam = (jnp.exp(jnp.sum(lv[0:1] * lv[1:2], axis=1, keepdims=True))
           - jnp.exp(jnp.sum(lv[2:3] * lv[3:4], axis=1, keepdims=True)) + LAM_INIT)
    o = o_all[:tq] - lam * o_all[tq:]
    ms = jnp.mean(o * o, axis=-1, keepdims=True)
    o = o * lax.rsqrt(ms + RMS_EPS) * g_ref[...] * (1.0 - LAM_INIT)
    o_ref[0] = o.astype(BF16)


def _diff_attn(proj3, lam4, subln_g, tq, tk):
    bsz, seq, _ = proj3.shape
    slopes = [2.0 ** (-8.0 * (h + 1) / A_HEADS) for h in range(A_HEADS)]
    c = jnp.asarray(slopes, F32)[:, None, None] * LOG2E
    cslope = jnp.broadcast_to(c, (A_HEADS, 1, tk))
    cbase = c * jnp.arange(tk, dtype=F32)[None, None, :]
    kblk = A_WIDTH // LANES
    return pl.pallas_call(
        functools.partial(_diff_attn_kernel, tq=tq, tk=tk),
        grid=(bsz, A_HEADS, seq // tq),
        in_specs=[
            pl.BlockSpec((4, A_HEAD_DIM), lambda b, h, i: (0, 0)),
            pl.BlockSpec((1, LANES), lambda b, h, i: (0, 0)),
            pl.BlockSpec((1, 1, tk), lambda b, h, i: (h, 0, 0)),
            pl.BlockSpec((1, 1, tk), lambda b, h, i: (h, 0, 0)),
            pl.BlockSpec((1, tq, LANES), lambda b, h, i: (b, i, h)),
            pl.BlockSpec((1, seq, LANES), lambda b, h, i: (b, 0, kblk + h)),
            pl.BlockSpec((1, seq, LANES), lambda b, h, i: (b, 0, 2 * kblk + h)),
        ],
        out_specs=pl.BlockSpec((1, tq, LANES), lambda b, h, i: (b, i, h)),
        out_shape=jax.ShapeDtypeStruct((bsz, seq, A_WIDTH), BF16),
        compiler_params=pltpu.CompilerParams(
            dimension_semantics=("arbitrary", "arbitrary", "arbitrary"),
            vmem_limit_bytes=VMEM_LIMIT),
        name="diff_attn",
    )(lam4, subln_g, cbase, cslope, proj3, proj3, proj3)


def _band_attn_kernel(bias_ref, q_ref, k_ref, v_ref, o_ref, kpad, vpad, *, tq, band, pad):
    i = pl.program_id(2)
    seq = k_ref.shape[1]

    @pl.when(i == 0)
    def _():
        kpad[0:pad, :] = jnp.zeros((pad, LANES), BF16)
        vpad[0:pad, :] = jnp.zeros((pad, LANES), BF16)
        kpad[pad:pad + seq, :] = k_ref[0]
        vpad[pad:pad + seq, :] = v_ref[0]

    q = q_ref[0]
    lane = lax.broadcasted_iota(jnp.int32, q.shape, 1)
    zero = jnp.zeros_like(q)
    qq = jnp.concatenate([jnp.where(lane < B_HEAD_DIM, q, zero),
                          jnp.where(lane >= B_HEAD_DIM, q, zero)], axis=0)
    ks = pl.multiple_of(i * tq, tq)
    kb = kpad[pl.ds(ks, band), :]
    vb = vpad[pl.ds(ks, band), :]
    s = lax.dot_general(qq, kb, (((1,), (1,)), ((), ())), preferred_element_type=F32)
    s = s + bias_ref[0]
    col = lax.broadcasted_iota(jnp.int32, (1, band), 1) + ks
    s = s + jnp.where(col < pad, NEG, 0.0)
    m = jnp.max(s, axis=1, keepdims=True)
    p = jnp.exp2(s - m)
    l = jnp.sum(p, axis=1, keepdims=True)
    o = jnp.dot(p.astype(BF16), vb, preferred_element_type=F32) / l
    o_ref[0] = jnp.where(lane < B_HEAD_DIM, o[:tq], o[tq:]).astype(BF16)


def _band_bias(rel_bias, tq):
    band = tq + B_PAST_CHUNKS * CHUNK
    qi = jnp.arange(tq)
    kj = jnp.arange(band)
    rel = qi[:, None] + B_PAST_CHUNKS * CHUNK - kj[None, :]
    rel_idx = jnp.clip(rel, -REL_CLIP, REL_CLIP) + REL_CLIP
    cq = qi[:, None] // CHUNK
    ck = kj[None, :] // CHUNK
    allowed = (ck >= cq) & (ck <= cq + B_PAST_CHUNKS)
    bias = rel_bias.astype(F32)[:, rel_idx] * LOG2E
    bias = jnp.where(allowed[None], bias, NEG)
    return bias.reshape(B_HEADS // 2, 2 * tq, band)


def _band_attn(proj3, rel_bias, tq):
    bsz, seq, _ = proj3.shape
    pad = B_PAST_CHUNKS * CHUNK
    band = tq + pad
    bias = _band_bias(rel_bias, tq)
    groups = B_HEADS // 2
    qblk = 3 * A_WIDTH // LANES
    kblk = qblk + B_WIDTH // LANES
    vblk = kblk + B_WIDTH // LANES
    return pl.pallas_call(
        functools.partial(_band_attn_kernel, tq=tq, band=band, pad=pad),
        grid=(bsz, groups, seq // tq),
        in_specs=[
            pl.BlockSpec((1, 2 * tq, band), lambda b, g, i: (g, 0, 0)),
            pl.BlockSpec((1, tq, LANES), lambda b, g, i: (b, i, qblk + g)),
            pl.BlockSpec((1, seq, LANES), lambda b, g, i: (b, 0, kblk + g)),
            pl.BlockSpec((1, seq, LANES), lambda b, g, i: (b, 0, vblk + g)),
        ],
        out_specs=pl.BlockSpec((1, tq, LANES), lambda b, g, i: (b, i, g)),
        out_shape=jax.ShapeDtypeStruct((bsz, seq, B_WIDTH), BF16),
        scratch_shapes=[pltpu.VMEM((seq + pad, LANES), BF16),
                        pltpu.VMEM((seq + pad, LANES), BF16)],
        compiler_params=pltpu.CompilerParams(
            dimension_semantics=("arbitrary", "arbitrary", "arbitrary"),
            vmem_limit_bytes=VMEM_LIMIT),
        name="band_attn",
    )(bias, proj3, proj3, proj3)


def _out_router_kernel(x_ref, gi_ref, bi_ref, oa_ref, ob_ref, wa_ref, wb_ref, g1_ref, b1_ref,
                       rwt_ref, rb_ref, h1_ref, idx_ref, gate_ref):
    h = _layer_norm(x_ref[...], gi_ref[...], bi_ref[...])
    mix = (jnp.dot(oa_ref[...], wa_ref[...], preferred_element_type=F32)
           + jnp.dot(ob_ref[...], wb_ref[...], preferred_element_type=F32))
    h1 = _layer_norm(DEEPNORM_ALPHA * h + mix, g1_ref[...], b1_ref[...])
    h1_ref[...] = h1
    lt = lax.dot_general(rwt_ref[...], h1, (((1,), (1,)), ((), ())),
                         precision=lax.Precision.HIGHEST, preferred_element_type=F32)
    lt = lt + rb_ref[...]
    eidx = lax.broadcasted_iota(jnp.int32, lt.shape, 0)
    vals, idxs = [], []
    for _ in range(TOP_K):
        mx = jnp.max(lt, axis=0, keepdims=True)
        am = jnp.min(jnp.where(lt == mx, eidx, N_EXPERTS), axis=0, keepdims=True)
        vals.append(mx)
        idxs.append(am)
        lt = jnp.where(eidx == am, -jnp.inf, lt)
    ex = [jnp.exp(v - vals[0]) for v in vals]
    den = ex[0] + ex[1] + ex[2] + ex[3]
    idx_ref[...] = jnp.concatenate(idxs, axis=0)
    gate_ref[...] = jnp.concatenate([e / den for e in ex], axis=0)


def _out_router(x2, gi, bi, o_a, o_b, wa, wb, g1, b1, rwt, rb):
    t, d = x2.shape
    row = lambda i: (i, 0)
    fixed = lambda i: (0, 0)
    return pl.pallas_call(
        _out_router_kernel,
        grid=(t // ROW_TILE,),
        in_specs=[
            pl.BlockSpec((ROW_TILE, d), row),
            pl.BlockSpec((1, d), fixed),
            pl.BlockSpec((1, d), fixed),
            pl.BlockSpec((ROW_TILE, A_WIDTH), row),
            pl.BlockSpec((ROW_TILE, B_WIDTH), row),
            pl.BlockSpec((A_WIDTH, d), fixed),
            pl.BlockSpec((B_WIDTH, d), fixed),
            pl.BlockSpec((1, d), fixed),
            pl.BlockSpec((1, d), fixed),
            pl.BlockSpec((N_EXPERTS, d), fixed),
            pl.BlockSpec((N_EXPERTS, 1), fixed),
        ],
        out_specs=[
            pl.BlockSpec((ROW_TILE, d), row),
            pl.BlockSpec((TOP_K, ROW_TILE), lambda i: (0, i)),
            pl.BlockSpec((TOP_K, ROW_TILE), lambda i: (0, i)),
        ],
        out_shape=[
            jax.ShapeDtypeStruct((t, d), F32),
            jax.ShapeDtypeStruct((TOP_K, t), jnp.int32),
            jax.ShapeDtypeStruct((TOP_K, t), F32),
        ],
        compiler_params=pltpu.CompilerParams(
            dimension_semantics=("arbitrary",), vmem_limit_bytes=VMEM_LIMIT),
        name="out_router",
    )(x2, gi, bi, o_a, o_b, wa, wb, g1, b1, rwt, rb)


def _moe_kernel(be_ref, nused_ref, nvalid_ref,
                tok_ref, tokn_ref, dst_ref, gate_ref,
                w1g_ref, w1l_ref, b1g_ref, b1l_ref, w2_ref, b2_ref, h_hbm,
                buf_hbm, xbuf, ybuf, gsem, ssem):
    del be_ref
    i = pl.program_id(0)
    nused = nused_ref[0]
    d_model = w1g_ref.shape[1]
    nchunk = d_model // LANES
    blk_rows = MOE_BLOCK * nchunk

    def row_slice(r):
        return pl.ds(pl.multiple_of(r * nchunk, nchunk), nchunk)

    def issue_gather(idx_ref, slot):
        def body(r, carry):
            pltpu.make_async_copy(h_hbm.at[row_slice(idx_ref[0, 0, r]), :],
                                  xbuf.at[slot, row_slice(r), :], gsem.at[slot]).start()
            return carry
        lax.fori_loop(0, MOE_BLOCK, body, 0)

    def wait_gather(slot):
        pltpu.make_async_copy(h_hbm.at[pl.ds(0, blk_rows), :], xbuf.at[slot], gsem.at[slot]).wait()

    def issue_scatter(n):
        def body(r, carry):
            pltpu.make_async_copy(ybuf.at[row_slice(r), :],
                                  buf_hbm.at[row_slice(dst_ref[0, 0, r]), :], ssem.at[0]).start()
            return carry
        lax.fori_loop(0, n, body, 0)

    def wait_scatter(n):
        rows = pl.ds(0, pl.multiple_of(n * nchunk, nchunk))
        pltpu.make_async_copy(ybuf.at[rows, :], buf_hbm.at[rows, :], ssem.at[0]).wait()

    slot = i % 2

    @pl.when(i == 0)
    def _():
        issue_gather(tok_ref, 0)

    @pl.when(i + 1 < nused)
    def _():
        issue_gather(tokn_ref, 1 - slot)

    @pl.when(i < nused)
    def _():
        wait_gather(slot)
        x = jnp.concatenate(
            [xbuf[slot, pl.ds(c, MOE_BLOCK, stride=nchunk), :] for c in range(nchunk)],
            axis=1).astype(BF16)
        hg = jnp.dot(x, w1g_ref[0], preferred_element_type=F32) + b1g_ref[0]
        hl = jnp.dot(x, w1l_ref[0], preferred_element_type=F32) + b1l_ref[0]
        xg = jnp.minimum(hg, SWIGLU_LIMIT)
        xl = jnp.clip(hl, -SWIGLU_LIMIT, SWIGLU_LIMIT)
        act = xg * jax.nn.sigmoid(SWIGLU_ALPHA * xg) * (xl + 1.0)
        y = jnp.dot(act.astype(BF16), w2_ref[0], preferred_element_type=F32) + b2_ref[0]
        y = y * gate_ref[0]

        @pl.when(i > 0)
        def _():
            wait_scatter(nvalid_ref[jnp.maximum(i - 1, 0)])

        for c in range(nchunk):
            ybuf[pl.ds(c, MOE_BLOCK, stride=nchunk), :] = y[:, c * LANES:(c + 1) * LANES]
        issue_scatter(nvalid_ref[i])

        @pl.when(i == nused - 1)
        def _():
            wait_scatter(nvalid_ref[i])


def _moe(h1, block_expert, nused, nvalid, slot_token, slot_dst, slot_gate, w1g, w1l, b1g, b1l, w2, b2,
         n_rows_out):
    t, d = h1.shape
    f = w1g.shape[2]
    nchunk = d // LANES
    n_blocks = block_expert.shape[0]
    h_tiles = h1.reshape(t * nchunk, LANES)
    tok3 = slot_token.reshape(n_blocks, 1, MOE_BLOCK)
    dst3 = slot_dst.reshape(n_blocks, 1, MOE_BLOCK)
    gate3 = slot_gate.reshape(n_blocks, MOE_BLOCK, 1)
    smem_blk = lambda fn: pl.BlockSpec((1, 1, MOE_BLOCK), fn, memory_space=pltpu.SMEM)
    grid_spec = pltpu.PrefetchScalarGridSpec(
        num_scalar_prefetch=3,
        grid=(n_blocks,),
        in_specs=[
            smem_blk(lambda i, be, nu, nv: (i, 0, 0)),
            smem_blk(lambda i, be, nu, nv: (jnp.minimum(i + 1, n_blocks - 1), 0, 0)),
            smem_blk(lambda i, be, nu, nv: (i, 0, 0)),
            pl.BlockSpec((1, MOE_BLOCK, 1), lambda i, be, nu, nv: (i, 0, 0)),
            pl.BlockSpec((1, d, f), lambda i, be, nu, nv: (be[i], 0, 0)),
            pl.BlockSpec((1, d, f), lambda i, be, nu, nv: (be[i], 0, 0)),
            pl.BlockSpec((1, 1, f), lambda i, be, nu, nv: (be[i], 0, 0)),
            pl.BlockSpec((1, 1, f), lambda i, be, nu, nv: (be[i], 0, 0)),
            pl.BlockSpec((1, f, d), lambda i, be, nu, nv: (be[i], 0, 0)),
            pl.BlockSpec((1, 1, d), lambda i, be, nu, nv: (be[i], 0, 0)),
            pl.BlockSpec(memory_space=pl.ANY),
        ],
        out_specs=pl.BlockSpec(memory_space=pl.ANY),
        scratch_shapes=[
            pltpu.VMEM((2, MOE_BLOCK * nchunk, LANES), F32),
            pltpu.VMEM((MOE_BLOCK * nchunk, LANES), F32),
            pltpu.SemaphoreType.DMA((2,)),
            pltpu.SemaphoreType.DMA((1,)),
        ],
    )
    buf = pl.pallas_call(
        _moe_kernel,
        grid_spec=grid_spec,
        out_shape=jax.ShapeDtypeStruct((n_rows_out * nchunk, LANES), F32),
        compiler_params=pltpu.CompilerParams(
            dimension_semantics=("arbitrary",), vmem_limit_bytes=VMEM_LIMIT),
        name="moe",
    )(block_expert, nused, nvalid, tok3, tok3, dst3, gate3, w1g, w1l, b1g, b1l, w2, b2, h_tiles)
    return buf.reshape(n_rows_out, d)


def _combine_kernel(h1_ref, y0_ref, y1_ref, y2_ref, y3_ref, g_ref, b_ref, o_ref):
    ffn = (y0_ref[...] + y1_ref[...]) + (y2_ref[...] + y3_ref[...])
    o_ref[...] = _layer_norm(DEEPNORM_ALPHA * h1_ref[...] + ffn, g_ref[...], b_ref[...])


def _combine(h1, buf, g2, b2):
    t, d = h1.shape
    nt = t // ROW_TILE
    ysp = lambda k: pl.BlockSpec((ROW_TILE, d), lambda i: (k * nt + i, 0))
    return pl.pallas_call(
        _combine_kernel,
        grid=(nt,),
        in_specs=[pl.BlockSpec((ROW_TILE, d), lambda i: (i, 0)),
                  ysp(0), ysp(1), ysp(2), ysp(3),
                  pl.BlockSpec((1, d), lambda i: (0, 0)),
                  pl.BlockSpec((1, d), lambda i: (0, 0))],
        out_specs=pl.BlockSpec((ROW_TILE, d), lambda i: (i, 0)),
        out_shape=jax.ShapeDtypeStruct((t, d), F32),
        compiler_params=pltpu.CompilerParams(
            dimension_semantics=("arbitrary",), vmem_limit_bytes=VMEM_LIMIT),
        name="combine",
    )(h1, buf, buf, buf, buf, g2, b2)


def _routing_tables(top_idx, gates, n_tok):
    m = n_tok * TOP_K
    e_tk = top_idx.T
    oh = jnp.sum((e_tk[:, :, None] == jnp.arange(N_EXPERTS)[None, None, :]).astype(jnp.int32), axis=1)
    csum = jnp.cumsum(oh, axis=0)
    rank = csum - oh
    counts = csum[-1]
    padded = (counts + MOE_BLOCK - 1) // MOE_BLOCK * MOE_BLOCK
    pend = jnp.cumsum(padded)
    pstart = pend - padded
    dest = (pstart[e_tk] + jnp.take_along_axis(rank, e_tk, axis=1)).reshape(m)
    n_blocks = -(-m // MOE_BLOCK) + N_EXPERTS
    n_slots = n_blocks * MOE_BLOCK
    tok = jnp.repeat(jnp.arange(n_tok, dtype=jnp.int32), TOP_K)
    out_row = (jnp.arange(TOP_K, dtype=jnp.int32)[None, :] * n_tok
               + jnp.arange(n_tok, dtype=jnp.int32)[:, None]).reshape(m)
    slot_token = jnp.zeros((n_slots,), jnp.int32).at[dest].set(tok)
    slot_dst = jnp.zeros((n_slots,), jnp.int32).at[dest].set(out_row)
    slot_gate = jnp.zeros((n_slots,), F32).at[dest].set(gates.T.reshape(m))
    block_start = jnp.arange(n_blocks, dtype=jnp.int32) * MOE_BLOCK
    block_expert = jnp.minimum(
        jnp.searchsorted(pend, block_start, side='right'), N_EXPERTS - 1).astype(jnp.int32)
    nused = (pend[-1] // MOE_BLOCK).astype(jnp.int32).reshape(1)
    real_end = (pstart + counts)[block_expert]
    nvalid = jnp.where(block_start < pend[-1],
                       jnp.clip(real_end - block_start, 0, MOE_BLOCK), 0).astype(jnp.int32)
    return block_expert, nused, nvalid, slot_token, slot_dst, slot_gate


def kernel(x, ln_in_g, ln_in_b, w_in, lambda_q1, lambda_k1, lambda_q2, lambda_k2, subln_g, rel_bias,
           w_out, ln1_g, ln1_b, router_w, router_b, w1, b1, w2, b2, ln2_g, ln2_b):
    bsz, seq, d = x.shape
    t = bsz * seq
    x2 = x.reshape(t, d)
    row = lambda v: v.reshape(1, -1).astype(F32)

    qs = A_HEAD_DIM ** -0.5 * LOG2E
    col_scale = jnp.concatenate([
        jnp.full((A_WIDTH,), qs, F32), jnp.ones((2 * A_WIDTH,), F32),
        jnp.full((B_WIDTH,), B_HEAD_DIM ** -0.5 * LOG2E, F32), jnp.ones((2 * B_WIDTH,), F32)]).reshape(1, -1)
    proj = _ln_qkv(x2, row(ln_in_g), row(ln_in_b), w_in[0].astype(BF16), col_scale)
    proj3 = proj.reshape(bsz, seq, -1)

    lam4 = jnp.stack([lambda_q1[0], lambda_k1[0], lambda_q2[0], lambda_k2[0]]).astype(F32)
    o_a = _diff_attn(proj3, lam4, row(subln_g[0]), tq=128, tk=min(512, seq))
    o_b = _band_attn(proj3, rel_bias[0], tq=128)

    w_o = w_out[0].astype(BF16)
    h1, top_idx, gates = _out_router(
        x2, row(ln_in_g), row(ln_in_b), o_a.reshape(t, A_WIDTH), o_b.reshape(t, B_WIDTH),
        w_o[:A_WIDTH], w_o[A_WIDTH:], row(ln1_g[0]), row(ln1_b[0]),
        router_w[0].T.astype(F32), router_b[0].reshape(-1, 1).astype(F32))

    block_expert, nused, nvalid, slot_token, slot_dst, slot_gate = _routing_tables(top_idx, gates, t)
    w1e = w1[0]
    buf = _moe(h1, block_expert, nused, nvalid, slot_token, slot_dst, slot_gate,
               w1e[:, :, 0::2].astype(BF16), w1e[:, :, 1::2].astype(BF16),
               b1[0][:, None, 0::2].astype(F32), b1[0][:, None, 1::2].astype(F32),
               w2[0].astype(BF16), b2[0][:, None, :].astype(F32), t * TOP_K)
    out = _combine(h1, buf, row(ln2_g[0]), row(ln2_b[0]))
    return out.reshape(bsz, seq, d)
```

```python
import functools
import math

import jax
import jax.numpy as jnp
from jax import lax
from jax.experimental import pallas as pl
from jax.experimental.pallas import tpu as pltpu

F32 = jnp.float32
BF16 = jnp.bfloat16

CHUNK = 64
A_HEADS = 4
A_HEAD_DIM = 64
A_WIDTH = A_HEADS * 2 * A_HEAD_DIM
B_HEADS = 8
B_HEAD_DIM = 64
B_WIDTH = B_HEADS * B_HEAD_DIM
B_PAST_CHUNKS = 8
REL_CLIP = 256
N_EXPERTS = 32
TOP_K = 4
SWIGLU_ALPHA = 1.702
SWIGLU_LIMIT = 7.0
MOE_BLOCK = 512
LN_EPS = 1e-5
RMS_EPS = 1e-5
DEPTH = 1
DEEPNORM_ALPHA = (2 * DEPTH) ** 0.25
LAM_INIT = 0.8 - 0.6 * math.exp(-0.3 * 0)

LOG2E = 1.4426950408889634
NEG = -1e30
LANES = 128
SUBLANES = 8
ROW_TILE = 512
VMEM_LIMIT = 48 * 1024 * 1024


def _layer_norm(x, g, b):
    mu = jnp.mean(x, axis=-1, keepdims=True)
    xc = x - mu
    var = jnp.mean(xc * xc, axis=-1, keepdims=True)
    return xc * lax.rsqrt(var + LN_EPS) * g + b


def _ln_qkv_kernel(x_ref, g_ref, b_ref, w_ref, cs_ref, o_ref):
    h = _layer_norm(x_ref[...], g_ref[...], b_ref[...])
    hb = h.astype(BF16)
    n_out = w_ref.shape[1]
    for c in range(n_out // ROW_TILE):
        sl = slice(c * ROW_TILE, (c + 1) * ROW_TILE)
        acc = jnp.dot(hb, w_ref[:, sl], preferred_element_type=F32)
        o_ref[:, sl] = (acc * cs_ref[:, sl]).astype(BF16)


def _ln_qkv(x2, g, b, w_bf, col_scale):
    t, d = x2.shape
    n_out = w_bf.shape[1]
    return pl.pallas_call(
        _ln_qkv_kernel,
        grid=(t // ROW_TILE,),
        in_specs=[
            pl.BlockSpec((ROW_TILE, d), lambda i: (i, 0)),
            pl.BlockSpec((1, d), lambda i: (0, 0)),
            pl.BlockSpec((1, d), lambda i: (0, 0)),
            pl.BlockSpec((d, n_out), lambda i: (0, 0)),
            pl.BlockSpec((1, n_out), lambda i: (0, 0)),
        ],
        out_specs=pl.BlockSpec((ROW_TILE, n_out), lambda i: (i, 0)),
        out_shape=jax.ShapeDtypeStruct((t, n_out), BF16),
        compiler_params=pltpu.CompilerParams(
            dimension_semantics=("arbitrary",), vmem_limit_bytes=VMEM_LIMIT),
        name="ln_qkv",
    )(x2, g, b, w_bf, col_scale)


def _diff_attn_kernel(lam_ref, g_ref, cbase_ref, cslope_ref, q_ref, k_ref, v_ref, o_ref, *, tq, tk):
    i = pl.program_id(2)
    q = q_ref[0]
    lane = lax.broadcasted_iota(jnp.int32, q.shape, 1)
    zero = jnp.zeros_like(q)
    qq = jnp.concatenate([jnp.where(lane < A_HEAD_DIM, q, zero),
                          jnp.where(lane >= A_HEAD_DIM, q, zero)], axis=0)
    q0 = i * tq
    jd = q0 // tk
    cbase = cbase_ref[0]
    cslope = cslope_ref[0]

    def step(j, carry, diag):
        m, l, acc = carry
        start = pl.multiple_of(j * tk, tk)
        kb = k_ref[0, pl.ds(start, tk), :]
        vb = v_ref[0, pl.ds(start, tk), :]
        s = lax.dot_general(qq, kb, (((1,), (1,)), ((), ())), preferred_element_type=F32)
        off = j * tk - q0
        if diag:
            rowp = lax.broadcasted_iota(jnp.int32, s.shape, 0)
            rowp = jnp.where(rowp >= tq, rowp - tq, rowp)
            colp = lax.broadcasted_iota(jnp.int32, s.shape, 1) + off
            allowed = (colp >> 6) <= (rowp >> 6)
            bias = cslope * (rowp - jnp.abs(rowp - colp)).astype(F32)
            s = jnp.where(allowed, s + bias, NEG)
        else:
            s = s + (cbase + cslope * off.astype(F32))
        m_new = jnp.maximum(m, jnp.max(s, axis=1, keepdims=True))
        alpha = jnp.exp2(m - m_new)
        p = jnp.exp2(s - m_new)
        l = alpha * l + jnp.sum(p, axis=1, keepdims=True)
        acc = alpha * acc + jnp.dot(p.astype(BF16), vb, preferred_element_type=F32)
        return m_new, l, acc

    init = (jnp.full((2 * tq, 1), NEG, F32), jnp.zeros((2 * tq, 1), F32),
            jnp.zeros((2 * tq, LANES), F32))
    carry = lax.fori_loop(0, jd, lambda j, c: step(j, c, False), init)
    _, l, acc = step(jd, carry, True)
    o_all = acc / l
    lv = lam_ref[...]
    lam = (jnp.exp(jnp.sum(lv[0:1] * lv[1:2], axis=1, keepdims=True))
           - jnp.exp(jnp.sum(lv[2:3] * lv[3:4], axis=1, keepdims=True)) + LAM_INIT)
    o = o_all[:tq] - lam * o_all[tq:]
    ms = jnp.mean(o * o, axis=-1, keepdims=True)
    o = o * lax.rsqrt(ms + RMS_EPS) * g_ref[...] * (1.0 - LAM_INIT)
    o_ref[0] = o.astype(BF16)


def _diff_attn(proj3, lam4, subln_g, tq, tk):
    bsz, seq, _ = proj3.shape
    slopes = [2.0 ** (-8.0 * (h + 1) / A_HEADS) for h in range(A_HEADS)]
    c = jnp.asarray(slopes, F32)[:, None, None] * LOG2E
    cslope = jnp.broadcast_to(c, (A_HEADS, 1, tk))
    cbase = c * jnp.arange(tk, dtype=F32)[None, None, :]
    kblk = A_WIDTH // LANES
    return pl.pallas_call(
        functools.partial(_diff_attn_kernel, tq=tq, tk=tk),
        grid=(bsz, A_HEADS, seq // tq),
        in_specs=[
            pl.BlockSpec((4, A_HEAD_DIM), lambda b, h, i: (0, 0)),
            pl.BlockSpec((1, LANES), lambda b, h, i: (0, 0)),
            pl.BlockSpec((1, 1, tk), lambda b, h, i: (h, 0, 0)),
            pl.BlockSpec((1, 1, tk), lambda b, h, i: (h, 0, 0)),
            pl.BlockSpec((1, tq, LANES), lambda b, h, i: (b, i, h)),
            pl.BlockSpec((1, seq, LANES), lambda b, h, i: (b, 0, kblk + h)),
            pl.BlockSpec((1, seq, LANES), lambda b, h, i: (b, 0, 2 * kblk + h)),
        ],
        out_specs=pl.BlockSpec((1, tq, LANES), lambda b, h, i: (b, i, h)),
        out_shape=jax.ShapeDtypeStruct((bsz, seq, A_WIDTH), BF16),
        compiler_params=pltpu.CompilerParams(
            dimension_semantics=("arbitrary", "arbitrary", "arbitrary"),
            vmem_limit_bytes=VMEM_LIMIT),
        name="diff_attn",
    )(lam4, subln_g, cbase, cslope, proj3, proj3, proj3)


def _band_attn_kernel(bias_ref, q_ref, k_ref, v_ref, o_ref, kpad, vpad, *, tq, band, pad):
    i = pl.program_id(2)
    seq = k_ref.shape[1]

    @pl.when(i == 0)
    def _():
        kpad[0:pad, :] = jnp.zeros((pad, LANES), BF16)
        vpad[0:pad, :] = jnp.zeros((pad, LANES), BF16)
        kpad[pad:pad + seq, :] = k_ref[0]
        vpad[pad:pad + seq, :] = v_ref[0]

    q = q_ref[0]
    lane = lax.broadcasted_iota(jnp.int32, q.shape, 1)
    zero = jnp.zeros_like(q)
    qq = jnp.concatenate([jnp.where(lane < B_HEAD_DIM, q, zero),
                          jnp.where(lane >= B_HEAD_DIM, q, zero)], axis=0)
    ks = pl.multiple_of(i * tq, tq)
    kb = kpad[pl.ds(ks, band), :]
    vb = vpad[pl.ds(ks, band), :]
    s = lax.dot_general(qq, kb, (((1,), (1,)), ((), ())), preferred_element_type=F32)
    s = s + bias_ref[0]
    col = lax.broadcasted_iota(jnp.int32, (1, band), 1) + ks
    s = s + jnp.where(col < pad, NEG, 0.0)
    m = jnp.max(s, axis=1, keepdims=True)
    p = jnp.exp2(s - m)
    l = jnp.sum(p, axis=1, keepdims=True)
    o = jnp.dot(p.astype(BF16), vb, preferred_element_type=F32) / l
    o_ref[0] = jnp.where(lane < B_HEAD_DIM, o[:tq], o[tq:]).astype(BF16)


def _band_bias(rel_bias, tq):
    band = tq + B_PAST_CHUNKS * CHUNK
    qi = jnp.arange(tq)
    kj = jnp.arange(band)
    rel = qi[:, None] + B_PAST_CHUNKS * CHUNK - kj[None, :]
    rel_idx = jnp.clip(rel, -REL_CLIP, REL_CLIP) + REL_CLIP
    cq = qi[:, None] // CHUNK
    ck = kj[None, :] // CHUNK
    allowed = (ck >= cq) & (ck <= cq + B_PAST_CHUNKS)
    bias = rel_bias.astype(F32)[:, rel_idx] * LOG2E
    bias = jnp.where(allowed[None], bias, NEG)
    return bias.reshape(B_HEADS // 2, 2 * tq, band)


def _band_attn(proj3, rel_bias, tq):
    bsz, seq, _ = proj3.shape
    pad = B_PAST_CHUNKS * CHUNK
    band = tq + pad
    bias = _band_bias(rel_bias, tq)
    groups = B_HEADS // 2
    qblk = 3 * A_WIDTH // LANES
    kblk = qblk + B_WIDTH // LANES
    vblk = kblk + B_WIDTH // LANES
    return pl.pallas_call(
        functools.partial(_band_attn_kernel, tq=tq, band=band, pad=pad),
        grid=(bsz, groups, seq // tq),
        in_specs=[
            pl.BlockSpec((1, 2 * tq, band), lambda b, g, i: (g, 0, 0)),
            pl.BlockSpec((1, tq, LANES), lambda b, g, i: (b, i, qblk + g)),
            pl.BlockSpec((1, seq, LANES), lambda b, g, i: (b, 0, kblk + g)),
            pl.BlockSpec((1, seq, LANES), lambda b, g, i: (b, 0, vblk + g)),
        ],
        out_specs=pl.BlockSpec((1, tq, LANES), lambda b, g, i: (b, i, g)),
        out_shape=jax.ShapeDtypeStruct((bsz, seq, B_WIDTH), BF16),
        scratch_shapes=[pltpu.VMEM((seq + pad, LANES), BF16),
                        pltpu.VMEM((seq + pad, LANES), BF16)],
        compiler_params=pltpu.CompilerParams(
            dimension_semantics=("arbitrary", "arbitrary", "arbitrary"),
            vmem_limit_bytes=VMEM_LIMIT),
        name="band_attn",
    )(bias, proj3, proj3, proj3)


def _out_router_kernel(x_ref, gi_ref, bi_ref, oa_ref, ob_ref, wa_ref, wb_ref, g1_ref, b1_ref,
                       rwt_ref, rb_ref, h1_ref, idx_ref, gate_ref):
    h = _layer_norm(x_ref[...], gi_ref[...], bi_ref[...])
    mix = (jnp.dot(oa_ref[...], wa_ref[...], preferred_element_type=F32)
           + jnp.dot(ob_ref[...], wb_ref[...], preferred_element_type=F32))
    h1 = _layer_norm(DEEPNORM_ALPHA * h + mix, g1_ref[...], b1_ref[...])
    h1_ref[...] = h1
    lt = lax.dot_general(rwt_ref[...], h1, (((1,), (1,)), ((), ())),
                         precision=lax.Precision.HIGHEST, preferred_element_type=F32)
    lt = lt + rb_ref[...]
    eidx = lax.broadcasted_iota(jnp.int32, lt.shape, 0)
    vals, idxs = [], []
    for _ in range(TOP_K):
        mx = jnp.max(lt, axis=0, keepdims=True)
        am = jnp.min(jnp.where(lt == mx, eidx, N_EXPERTS), axis=0, keepdims=True)
        vals.append(mx)
        idxs.append(am)
        lt = jnp.where(eidx == am, -jnp.inf, lt)
    ex = [jnp.exp(v - vals[0]) for v in vals]
    den = ex[0] + ex[1] + ex[2] + ex[3]
    idx_ref[...] = jnp.concatenate(idxs, axis=0)
    gate_ref[...] = jnp.concatenate([e / den for e in ex], axis=0)


def _out_router(x2, gi, bi, o_a, o_b, wa, wb, g1, b1, rwt, rb):
    t, d = x2.shape
    row = lambda i: (i, 0)
    fixed = lambda i: (0, 0)
    return pl.pallas_call(
        _out_router_kernel,
        grid=(t // ROW_TILE,),
        in_specs=[
            pl.BlockSpec((ROW_TILE, d), row),
            pl.BlockSpec((1, d), fixed),
            pl.BlockSpec((1, d), fixed),
            pl.BlockSpec((ROW_TILE, A_WIDTH), row),
            pl.BlockSpec((ROW_TILE, B_WIDTH), row),
            pl.BlockSpec((A_WIDTH, d), fixed),
            pl.BlockSpec((B_WIDTH, d), fixed),
            pl.BlockSpec((1, d), fixed),
            pl.BlockSpec((1, d), fixed),
            pl.BlockSpec((N_EXPERTS, d), fixed),
            pl.BlockSpec((N_EXPERTS, 1), fixed),
        ],
        out_specs=[
            pl.BlockSpec((ROW_TILE, d), row),
            pl.BlockSpec((TOP_K, ROW_TILE), lambda i: (0, i)),
            pl.BlockSpec((TOP_K, ROW_TILE), lambda i: (0, i)),
        ],
        out_shape=[
            jax.ShapeDtypeStruct((t, d), F32),
            jax.ShapeDtypeStruct((TOP_K, t), jnp.int32),
            jax.ShapeDtypeStruct((TOP_K, t), F32),
        ],
        compiler_params=pltpu.CompilerParams(
            dimension_semantics=("arbitrary",), vmem_limit_bytes=VMEM_LIMIT),
        name="out_router",
    )(x2, gi, bi, o_a, o_b, wa, wb, g1, b1, rwt, rb)


def _split_w1_kernel(w_ref, g_ref, l_ref):
    rows = w_ref.shape[1]
    even = (lax.broadcasted_iota(jnp.int32, (rows, LANES), 1) & 1) == 0
    for g in range(w_ref.shape[2] // (2 * LANES)):
        v0 = w_ref[0, :, 2 * g * LANES:(2 * g + 1) * LANES]
        v1 = w_ref[0, :, (2 * g + 1) * LANES:(2 * g + 2) * LANES]
        glu = jnp.where(even, v0, pltpu.roll(v1, 1, axis=1))
        lin = jnp.where(even, pltpu.roll(v0, LANES - 1, axis=1), v1)
        g_ref[0, :, g * LANES:(g + 1) * LANES] = glu.astype(BF16)
        l_ref[0, :, g * LANES:(g + 1) * LANES] = lin.astype(BF16)


def _unit_order(v, axis):
    shape = v.shape
    f = shape[axis]
    half = LANES // 2
    v = v.reshape(shape[:axis] + (f // LANES, 2, half) + shape[axis + 1:])
    v = jnp.swapaxes(v, axis + 1, axis + 2)
    return v.reshape(shape)


def _split_w1(w1e):
    e, d, f2 = w1e.shape
    rows = 256
    return pl.pallas_call(
        _split_w1_kernel,
        grid=(e, d // rows),
        in_specs=[pl.BlockSpec((1, rows, f2), lambda i, j: (i, j, 0))],
        out_specs=[pl.BlockSpec((1, rows, f2 // 2), lambda i, j: (i, j, 0)),
                   pl.BlockSpec((1, rows, f2 // 2), lambda i, j: (i, j, 0))],
        out_shape=[jax.ShapeDtypeStruct((e, d, f2 // 2), BF16),
                   jax.ShapeDtypeStruct((e, d, f2 // 2), BF16)],
        compiler_params=pltpu.CompilerParams(
            dimension_semantics=("arbitrary", "arbitrary"), vmem_limit_bytes=VMEM_LIMIT),
        name="split_w1",
    )(w1e)


def _moe_kernel(be_ref, nused_ref, nvalid_ref,
                tok_ref, tokn_ref, dst_ref, gate_ref,
                w1g_ref, w1l_ref, b1g_ref, b1l_ref, w2_ref, b2_ref, h_hbm,
                buf_hbm, xbuf, ybuf, gsem, ssem):
    del be_ref
    i = pl.program_id(0)
    nused = nused_ref[0]
    d_model = w1g_ref.shape[1]
    nchunk = d_model // LANES
    blk_rows = MOE_BLOCK * nchunk

    def row_slice(r):
        return pl.ds(pl.multiple_of(r * nchunk, nchunk), nchunk)

    def issue_gather(idx_ref, slot):
        def body(r, carry):
            pltpu.make_async_copy(h_hbm.at[row_slice(idx_ref[0, 0, r]), :],
                                  xbuf.at[slot, row_slice(r), :], gsem.at[slot]).start()
            return carry
        lax.fori_loop(0, MOE_BLOCK, body, 0)

    def wait_gather(slot):
        pltpu.make_async_copy(h_hbm.at[pl.ds(0, blk_rows), :], xbuf.at[slot], gsem.at[slot]).wait()

    def issue_scatter(n):
        def body(r, carry):
            pltpu.make_async_copy(ybuf.at[row_slice(r), :],
                                  buf_hbm.at[row_slice(dst_ref[0, 0, r]), :], ssem.at[0]).start()
            return carry
        lax.fori_loop(0, n, body, 0)

    def wait_scatter(n):
        rows = pl.ds(0, pl.multiple_of(n * nchunk, nchunk))
        pltpu.make_async_copy(ybuf.at[rows, :], buf_hbm.at[rows, :], ssem.at[0]).wait()

    slot = i % 2

    @pl.when(i == 0)
    def _():
        issue_gather(tok_ref, 0)

    @pl.when(i + 1 < nused)
    def _():
        issue_gather(tokn_ref, 1 - slot)

    @pl.when(i < nused)
    def _():
        wait_gather(slot)
        x = jnp.concatenate(
            [xbuf[slot, pl.ds(c, MOE_BLOCK, stride=nchunk), :] for c in range(nchunk)],
            axis=1).astype(BF16)
        hg = jnp.dot(x, w1g_ref[0], preferred_element_type=F32) + b1g_ref[0]
        hl = jnp.dot(x, w1l_ref[0], preferred_element_type=F32) + b1l_ref[0]
        xg = jnp.minimum(hg, SWIGLU_LIMIT)
        xl = jnp.clip(hl, -SWIGLU_LIMIT, SWIGLU_LIMIT)
        act = xg * jax.nn.sigmoid(SWIGLU_ALPHA * xg) * (xl + 1.0)
        y = jnp.dot(act.astype(BF16), w2_ref[0], preferred_element_type=F32) + b2_ref[0]
        y = y * gate_ref[0]

        @pl.when(i > 0)
        def _():
            wait_scatter(nvalid_ref[jnp.maximum(i - 1, 0)])

        for c in range(nchunk):
            ybuf[pl.ds(c, MOE_BLOCK, stride=nchunk), :] = y[:, c * LANES:(c + 1) * LANES]
        issue_scatter(nvalid_ref[i])

        @pl.when(i == nused - 1)
        def _():
            wait_scatter(nvalid_ref[i])


def _moe(h1, block_expert, nused, nvalid, slot_token, slot_dst, slot_gate, w1g, w1l, b1g, b1l, w2, b2,
         n_rows_out):
    t, d = h1.shape
    f = w1g.shape[2]
    nchunk = d // LANES
    n_blocks = block_expert.shape[0]
    h_tiles = h1.reshape(t * nchunk, LANES)
    tok3 = slot_token.reshape(n_blocks, 1, MOE_BLOCK)
    dst3 = slot_dst.reshape(n_blocks, 1, MOE_BLOCK)
    gate3 = slot_gate.reshape(n_blocks, MOE_BLOCK, 1)
    smem_blk = lambda fn: pl.BlockSpec((1, 1, MOE_BLOCK), fn, memory_space=pltpu.SMEM)
    grid_spec = pltpu.PrefetchScalarGridSpec(
        num_scalar_prefetch=3,
        grid=(n_blocks,),
        in_specs=[
            smem_blk(lambda i, be, nu, nv: (i, 0, 0)),
            smem_blk(lambda i, be, nu, nv: (jnp.minimum(i + 1, n_blocks - 1), 0, 0)),
            smem_blk(lambda i, be, nu, nv: (i, 0, 0)),
            pl.BlockSpec((1, MOE_BLOCK, 1), lambda i, be, nu, nv: (i, 0, 0)),
            pl.BlockSpec((1, d, f), lambda i, be, nu, nv: (be[i], 0, 0)),
            pl.BlockSpec((1, d, f), lambda i, be, nu, nv: (be[i], 0, 0)),
            pl.BlockSpec((1, 1, f), lambda i, be, nu, nv: (be[i], 0, 0)),
            pl.BlockSpec((1, 1, f), lambda i, be, nu, nv: (be[i], 0, 0)),
            pl.BlockSpec((1, f, d), lambda i, be, nu, nv: (be[i], 0, 0)),
            pl.BlockSpec((1, 1, d), lambda i, be, nu, nv: (be[i], 0, 0)),
            pl.BlockSpec(memory_space=pl.ANY),
        ],
        out_specs=pl.BlockSpec(memory_space=pl.ANY),
        scratch_shapes=[
            pltpu.VMEM((2, MOE_BLOCK * nchunk, LANES), F32),
            pltpu.VMEM((MOE_BLOCK * nchunk, LANES), F32),
            pltpu.SemaphoreType.DMA((2,)),
            pltpu.SemaphoreType.DMA((1,)),
        ],
    )
    buf = pl.pallas_call(
        _moe_kernel,
        grid_spec=grid_spec,
        out_shape=jax.ShapeDtypeStruct((n_rows_out * nchunk, LANES), F32),
        compiler_params=pltpu.CompilerParams(
            dimension_semantics=("arbitrary",), vmem_limit_bytes=VMEM_LIMIT),
        name="moe",
    )(block_expert, nused, nvalid, tok3, tok3, dst3, gate3, w1g, w1l, b1g, b1l, w2, b2, h_tiles)
    return buf.reshape(n_rows_out, d)


def _combine_kernel(h1_ref, y0_ref, y1_ref, y2_ref, y3_ref, g_ref, b_ref, o_ref):
    ffn = (y0_ref[...] + y1_ref[...]) + (y2_ref[...] + y3_ref[...])
    o_ref[...] = _layer_norm(DEEPNORM_ALPHA * h1_ref[...] + ffn, g_ref[...], b_ref[...])


def _combine(h1, buf, g2, b2):
    t, d = h1.shape
    nt = t // ROW_TILE
    ysp = lambda k: pl.BlockSpec((ROW_TILE, d), lambda i: (k * nt + i, 0))
    return pl.pallas_call(
        _combine_kernel,
        grid=(nt,),
        in_specs=[pl.BlockSpec((ROW_TILE, d), lambda i: (i, 0)),
                  ysp(0), ysp(1), ysp(2), ysp(3),
                  pl.BlockSpec((1, d), lambda i: (0, 0)),
                  pl.BlockSpec((1, d), lambda i: (0, 0))],
        out_specs=pl.BlockSpec((ROW_TILE, d), lambda i: (i, 0)),
        out_shape=jax.ShapeDtypeStruct((t, d), F32),
        compiler_params=pltpu.CompilerParams(
            dimension_semantics=("arbitrary",), vmem_limit_bytes=VMEM_LIMIT),
        name="combine",
    )(h1, buf, buf, buf, buf, g2, b2)


def _routing_tables(top_idx, gates, n_tok):
    m = n_tok * TOP_K
    e_tk = top_idx.T
    oh = jnp.sum((e_tk[:, :, None] == jnp.arange(N_EXPERTS)[None, None, :]).astype(jnp.int32), axis=1)
    csum = jnp.cumsum(oh, axis=0)
    rank = csum - oh
    counts = csum[-1]
    padded = (counts + MOE_BLOCK - 1) // MOE_BLOCK * MOE_BLOCK
    pend = jnp.cumsum(padded)
    pstart = pend - padded
    dest = (pstart[e_tk] + jnp.take_along_axis(rank, e_tk, axis=1)).reshape(m)
    n_blocks = -(-m // MOE_BLOCK) + N_EXPERTS
    n_slots = n_blocks * MOE_BLOCK
    tok = jnp.repeat(jnp.arange(n_tok, dtype=jnp.int32), TOP_K)
    out_row = (jnp.arange(TOP_K, dtype=jnp.int32)[None, :] * n_tok
               + jnp.arange(n_tok, dtype=jnp.int32)[:, None]).reshape(m)
    slot_token = jnp.zeros((n_slots,), jnp.int32).at[dest].set(tok)
    slot_dst = jnp.zeros((n_slots,), jnp.int32).at[dest].set(out_row)
    slot_gate = jnp.zeros((n_slots,), F32).at[dest].set(gates.T.reshape(m))
    block_start = jnp.arange(n_blocks, dtype=jnp.int32) * MOE_BLOCK
    block_expert = jnp.minimum(
        jnp.searchsorted(pend, block_start, side='right'), N_EXPERTS - 1).astype(jnp.int32)
    nused = (pend[-1] // MOE_BLOCK).astype(jnp.int32).reshape(1)
    real_end = (pstart + counts)[block_expert]
    nvalid = jnp.where(block_start < pend[-1],
                       jnp.clip(real_end - block_start, 0, MOE_BLOCK), 0).astype(jnp.int32)
    return block_expert, nused, nvalid, slot_token, slot_dst, slot_gate


def kernel(x, ln_in_g, ln_in_b, w_in, lambda_q1, lambda_k1, lambda_q2, lambda_k2, subln_g, rel_bias,
           w_out, ln1_g, ln1_b, router_w, router_b, w1, b1, w2, b2, ln2_g, ln2_b):
    bsz, seq, d = x.shape
    t = bsz * seq
    x2 = x.reshape(t, d)
    row = lambda v: v.reshape(1, -1).astype(F32)

    qs = A_HEAD_DIM ** -0.5 * LOG2E
    col_scale = jnp.concatenate([
        jnp.full((A_WIDTH,), qs, F32), jnp.ones((2 * A_WIDTH,), F32),
        jnp.full((B_WIDTH,), B_HEAD_DIM ** -0.5 * LOG2E, F32), jnp.ones((2 * B_WIDTH,), F32)]).reshape(1, -1)
    proj = _ln_qkv(x2, row(ln_in_g), row(ln_in_b), w_in[0].astype(BF16), col_scale)
    proj3 = proj.reshape(bsz, seq, -1)

    lam4 = jnp.stack([lambda_q1[0], lambda_k1[0], lambda_q2[0], lambda_k2[0]]).astype(F32)
    o_a = _diff_attn(proj3, lam4, row(subln_g[0]), tq=128, tk=min(512, seq))
    o_b = _band_attn(proj3, rel_bias[0], tq=128)

    w_o = w_out[0].astype(BF16)
    h1, top_idx, gates = _out_router(
        x2, row(ln_in_g), row(ln_in_b), o_a.reshape(t, A_WIDTH), o_b.reshape(t, B_WIDTH),
        w_o[:A_WIDTH], w_o[A_WIDTH:], row(ln1_g[0]), row(ln1_b[0]),
        router_w[0].T.astype(F32), router_b[0].reshape(-1, 1).astype(F32))

    block_expert, nused, nvalid, slot_token, slot_dst, slot_gate = _routing_tables(top_idx, gates, t)
    w1g, w1l = _split_w1(w1[0])
    b1e = b1[0].astype(F32)[:, None, :]
    buf = _moe(h1, block_expert, nused, nvalid, slot_token, slot_dst, slot_gate,
               w1g, w1l, _unit_order(b1e[:, :, 0::2], 2), _unit_order(b1e[:, :, 1::2], 2),
               _unit_order(w2[0], 1).astype(BF16), b2[0][:, None, :].astype(F32), t * TOP_K)
    out = _combine(h1, buf, row(ln2_g[0]), row(ln2_b[0]))
    return out.reshape(bsz, seq, d)
```

```python
import functools
import math

import jax
import jax.numpy as jnp
from jax import lax
from jax.experimental import pallas as pl
from jax.experimental.pallas import tpu as pltpu

F32 = jnp.float32
BF16 = jnp.bfloat16

CHUNK = 64
A_HEADS = 4
A_HEAD_DIM = 64
A_WIDTH = A_HEADS * 2 * A_HEAD_DIM
B_HEADS = 8
B_HEAD_DIM = 64
B_WIDTH = B_HEADS * B_HEAD_DIM
B_PAST_CHUNKS = 8
REL_CLIP = 256
N_EXPERTS = 32
TOP_K = 4
SWIGLU_ALPHA = 1.702
SWIGLU_LIMIT = 7.0
MOE_BLOCK = 512
LN_EPS = 1e-5
RMS_EPS = 1e-5
DEPTH = 1
DEEPNORM_ALPHA = (2 * DEPTH) ** 0.25
LAM_INIT = 0.8 - 0.6 * math.exp(-0.3 * 0)

LOG2E = 1.4426950408889634
NEG = -1e30
LANES = 128
SUBLANES = 8
ROW_TILE = 512
VMEM_LIMIT = 48 * 1024 * 1024
MOE_TILE = 4096
MOE_VMEM_LIMIT = 60 * 1024 * 1024
SMALL_BLOCK = 128
FF_CHUNK = 512
SORT_UNROLL = 8
LIST_PAD = 128


def _layer_norm(x, g, b):
    mu = jnp.mean(x, axis=-1, keepdims=True)
    xc = x - mu
    var = jnp.mean(xc * xc, axis=-1, keepdims=True)
    return xc * lax.rsqrt(var + LN_EPS) * g + b


def _ln_qkv_kernel(x_ref, g_ref, b_ref, w_ref, cs_ref, o_ref):
    h = _layer_norm(x_ref[...], g_ref[...], b_ref[...])
    hb = h.astype(BF16)
    n_out = w_ref.shape[1]
    for c in range(n_out // ROW_TILE):
        sl = slice(c * ROW_TILE, (c + 1) * ROW_TILE)
        acc = jnp.dot(hb, w_ref[:, sl], preferred_element_type=F32)
        o_ref[:, sl] = (acc * cs_ref[:, sl]).astype(BF16)


def _ln_qkv(x2, g, b, w_bf, col_scale):
    t, d = x2.shape
    n_out = w_bf.shape[1]
    return pl.pallas_call(
        _ln_qkv_kernel,
        grid=(t // ROW_TILE,),
        in_specs=[
            pl.BlockSpec((ROW_TILE, d), lambda i: (i, 0)),
            pl.BlockSpec((1, d), lambda i: (0, 0)),
            pl.BlockSpec((1, d), lambda i: (0, 0)),
            pl.BlockSpec((d, n_out), lambda i: (0, 0)),
            pl.BlockSpec((1, n_out), lambda i: (0, 0)),
        ],
        out_specs=pl.BlockSpec((ROW_TILE, n_out), lambda i: (i, 0)),
        out_shape=jax.ShapeDtypeStruct((t, n_out), BF16),
        compiler_params=pltpu.CompilerParams(
            dimension_semantics=("arbitrary",), vmem_limit_bytes=VMEM_LIMIT),
        name="ln_qkv",
    )(x2, g, b, w_bf, col_scale)


def _diff_attn_kernel(lam_ref, g_ref, cbase_ref, cslope_ref, q_ref, k_ref, v_ref, o_ref, *, tq, tk):
    i = pl.program_id(2)
    q = q_ref[0]
    lane = lax.broadcasted_iota(jnp.int32, q.shape, 1)
    zero = jnp.zeros_like(q)
    qq = jnp.concatenate([jnp.where(lane < A_HEAD_DIM, q, zero),
                          jnp.where(lane >= A_HEAD_DIM, q, zero)], axis=0)
    q0 = i * tq
    jd = q0 // tk
    cbase = cbase_ref[0]
    cslope = cslope_ref[0]

    def step(j, carry, diag):
        m, l, acc = carry
        start = pl.multiple_of(j * tk, tk)
        kb = k_ref[0, pl.ds(start, tk), :]
        vb = v_ref[0, pl.ds(start, tk), :]
        s = lax.dot_general(qq, kb, (((1,), (1,)), ((), ())), preferred_element_type=F32)
        off = j * tk - q0
        if diag:
            rowp = lax.broadcasted_iota(jnp.int32, s.shape, 0)
            rowp = jnp.where(rowp >= tq, rowp - tq, rowp)
            colp = lax.broadcasted_iota(jnp.int32, s.shape, 1) + off
            allowed = (colp >> 6) <= (rowp >> 6)
            bias = cslope * (rowp - jnp.abs(rowp - colp)).astype(F32)
            s = jnp.where(allowed, s + bias, NEG)
        else:
            s = s + (cbase + cslope * off.astype(F32))
        m_new = jnp.maximum(m, jnp.max(s, axis=1, keepdims=True))
        alpha = jnp.exp2(m - m_new)
        p = jnp.exp2(s - m_new)
        l = alpha * l + jnp.sum(p, axis=1, keepdims=True)
        acc = alpha * acc + jnp.dot(p.astype(BF16), vb, preferred_element_type=F32)
        return m_new, l, acc

    init = (jnp.full((2 * tq, 1), NEG, F32), jnp.zeros((2 * tq, 1), F32),
            jnp.zeros((2 * tq, LANES), F32))
    carry = lax.fori_loop(0, jd, lambda j, c: step(j, c, False), init)
    _, l, acc = step(jd, carry, True)
    o_all = acc / l
    lv = lam_ref[...]
    lam = (jnp.exp(jnp.sum(lv[0:1] * lv[1:2], axis=1, keepdims=True))
           - jnp.exp(jnp.sum(lv[2:3] * lv[3:4], axis=1, keepdims=True)) + LAM_INIT)
    o = o_all[:tq] - lam * o_all[tq:]
    ms = jnp.mean(o * o, axis=-1, keepdims=True)
    o = o * lax.rsqrt(ms + RMS_EPS) * g_ref[...] * (1.0 - LAM_INIT)
    o_ref[0] = o.astype(BF16)


def _diff_attn(proj3, lam4, subln_g, tq, tk):
    bsz, seq, _ = proj3.shape
    slopes = [2.0 ** (-8.0 * (h + 1) / A_HEADS) for h in range(A_HEADS)]
    c = jnp.asarray(slopes, F32)[:, None, None] * LOG2E
    cslope = jnp.broadcast_to(c, (A_HEADS, 1, tk))
    cbase = c * jnp.arange(tk, dtype=F32)[None, None, :]
    kblk = A_WIDTH // LANES
    return pl.pallas_call(
        functools.partial(_diff_attn_kernel, tq=tq, tk=tk),
        grid=(bsz, A_HEADS, seq // tq),
        in_specs=[
            pl.BlockSpec((4, A_HEAD_DIM), lambda b, h, i: (0, 0)),
            pl.BlockSpec((1, LANES), lambda b, h, i: (0, 0)),
            pl.BlockSpec((1, 1, tk), lambda b, h, i: (h, 0, 0)),
            pl.BlockSpec((1, 1, tk), lambda b, h, i: (h, 0, 0)),
            pl.BlockSpec((1, tq, LANES), lambda b, h, i: (b, i, h)),
            pl.BlockSpec((1, seq, LANES), lambda b, h, i: (b, 0, kblk + h)),
            pl.BlockSpec((1, seq, LANES), lambda b, h, i: (b, 0, 2 * kblk + h)),
        ],
        out_specs=pl.BlockSpec((1, tq, LANES), lambda b, h, i: (b, i, h)),
        out_shape=jax.ShapeDtypeStruct((bsz, seq, A_WIDTH), BF16),
        compiler_params=pltpu.CompilerParams(
            dimension_semantics=("arbitrary", "arbitrary", "arbitrary"),
            vmem_limit_bytes=VMEM_LIMIT),
        name="diff_attn",
    )(lam4, subln_g, cbase, cslope, proj3, proj3, proj3)


def _band_attn_kernel(bias_ref, q_ref, k_ref, v_ref, o_ref, kpad, vpad, *, tq, band, pad):
    i = pl.program_id(2)
    seq = k_ref.shape[1]

    @pl.when(i == 0)
    def _():
        kpad[0:pad, :] = jnp.zeros((pad, LANES), BF16)
        vpad[0:pad, :] = jnp.zeros((pad, LANES), BF16)
        kpad[pad:pad + seq, :] = k_ref[0]
        vpad[pad:pad + seq, :] = v_ref[0]

    q = q_ref[0]
    lane = lax.broadcasted_iota(jnp.int32, q.shape, 1)
    zero = jnp.zeros_like(q)
    qq = jnp.concatenate([jnp.where(lane < B_HEAD_DIM, q, zero),
                          jnp.where(lane >= B_HEAD_DIM, q, zero)], axis=0)
    ks = pl.multiple_of(i * tq, tq)
    kb = kpad[pl.ds(ks, band), :]
    vb = vpad[pl.ds(ks, band), :]
    s = lax.dot_general(qq, kb, (((1,), (1,)), ((), ())), preferred_element_type=F32)
    s = s + bias_ref[0]
    col = lax.broadcasted_iota(jnp.int32, (1, band), 1) + ks
    s = s + jnp.where(col < pad, NEG, 0.0)
    m = jnp.max(s, axis=1, keepdims=True)
    p = jnp.exp2(s - m)
    l = jnp.sum(p, axis=1, keepdims=True)
    o = jnp.dot(p.astype(BF16), vb, preferred_element_type=F32) / l
    o_ref[0] = jnp.where(lane < B_HEAD_DIM, o[:tq], o[tq:]).astype(BF16)


def _band_bias(rel_bias, tq):
    band = tq + B_PAST_CHUNKS * CHUNK
    qi = jnp.arange(tq)
    kj = jnp.arange(band)
    rel = qi[:, None] + B_PAST_CHUNKS * CHUNK - kj[None, :]
    rel_idx = jnp.clip(rel, -REL_CLIP, REL_CLIP) + REL_CLIP
    cq = qi[:, None] // CHUNK
    ck = kj[None, :] // CHUNK
    allowed = (ck >= cq) & (ck <= cq + B_PAST_CHUNKS)
    bias = rel_bias.astype(F32)[:, rel_idx] * LOG2E
    bias = jnp.where(allowed[None], bias, NEG)
    return bias.reshape(B_HEADS // 2, 2 * tq, band)


def _band_attn(proj3, rel_bias, tq):
    bsz, seq, _ = proj3.shape
    pad = B_PAST_CHUNKS * CHUNK
    band = tq + pad
    bias = _band_bias(rel_bias, tq)
    groups = B_HEADS // 2
    qblk = 3 * A_WIDTH // LANES
    kblk = qblk + B_WIDTH // LANES
    vblk = kblk + B_WIDTH // LANES
    return pl.pallas_call(
        functools.partial(_band_attn_kernel, tq=tq, band=band, pad=pad),
        grid=(bsz, groups, seq // tq),
        in_specs=[
            pl.BlockSpec((1, 2 * tq, band), lambda b, g, i: (g, 0, 0)),
            pl.BlockSpec((1, tq, LANES), lambda b, g, i: (b, i, qblk + g)),
            pl.BlockSpec((1, seq, LANES), lambda b, g, i: (b, 0, kblk + g)),
            pl.BlockSpec((1, seq, LANES), lambda b, g, i: (b, 0, vblk + g)),
        ],
        out_specs=pl.BlockSpec((1, tq, LANES), lambda b, g, i: (b, i, g)),
        out_shape=jax.ShapeDtypeStruct((bsz, seq, B_WIDTH), BF16),
        scratch_shapes=[pltpu.VMEM((seq + pad, LANES), BF16),
                        pltpu.VMEM((seq + pad, LANES), BF16)],
        compiler_params=pltpu.CompilerParams(
            dimension_semantics=("arbitrary", "arbitrary", "arbitrary"),
            vmem_limit_bytes=VMEM_LIMIT),
        name="band_attn",
    )(bias, proj3, proj3, proj3)


def _out_router_kernel(x_ref, gi_ref, bi_ref, oa_ref, ob_ref, wa_ref, wb_ref, g1_ref, b1_ref,
                       rwt_ref, rb_ref, tri_ref, h1_ref, idx_ref, gate_ref, rank_ref, cnt_ref,
                       carry_ref, *, steps_per_tile):
    h = _layer_norm(x_ref[...], gi_ref[...], bi_ref[...])
    mix = (jnp.dot(oa_ref[...], wa_ref[...], preferred_element_type=F32)
           + jnp.dot(ob_ref[...], wb_ref[...], preferred_element_type=F32))
    h1 = _layer_norm(DEEPNORM_ALPHA * h + mix, g1_ref[...], b1_ref[...])
    nchunk = h1.shape[1] // LANES
    for c in range(nchunk):
        h1_ref[pl.ds(c, ROW_TILE, stride=nchunk), :] = h1[:, c * LANES:(c + 1) * LANES]
    lt = lax.dot_general(rwt_ref[...], h1, (((1,), (1,)), ((), ())),
                         precision=lax.Precision.HIGHEST, preferred_element_type=F32)
    lt = lt + rb_ref[...]
    eidx = lax.broadcasted_iota(jnp.int32, lt.shape, 0)
    vals, idxs, hots = [], [], []
    for _ in range(TOP_K):
        mx = jnp.max(lt, axis=0, keepdims=True)
        am = jnp.min(jnp.where(lt == mx, eidx, N_EXPERTS), axis=0, keepdims=True)
        hit = eidx == am
        vals.append(mx)
        idxs.append(am)
        hots.append(jnp.where(hit, 1.0, 0.0))
        lt = jnp.where(hit, -jnp.inf, lt)
    ex = [jnp.exp(v - vals[0]) for v in vals]
    den = ex[0] + ex[1] + ex[2] + ex[3]
    idx_ref[...] = jnp.concatenate(idxs, axis=0)
    gate_ref[...] = jnp.concatenate([e / den for e in ex], axis=0)

    @pl.when(pl.program_id(0) % steps_per_tile == 0)
    def _():
        carry_ref[...] = jnp.zeros_like(carry_ref)

    hot = (hots[0] + hots[1]) + (hots[2] + hots[3])
    before = jnp.dot(hot.astype(BF16), tri_ref[...], preferred_element_type=F32) + carry_ref[...]
    rank_ref[...] = jnp.concatenate(
        [jnp.sum(hk * before, axis=0, keepdims=True) for hk in hots], axis=0).astype(jnp.int32)
    total = carry_ref[...] + jnp.sum(hot, axis=1, keepdims=True)
    carry_ref[...] = total
    cnt_ref[0] = total.astype(jnp.int32)


def _out_router(x2, gi, bi, o_a, o_b, wa, wb, g1, b1, rwt, rb, moe_tile):
    t, d = x2.shape
    nchunk = d // LANES
    row = lambda i: (i, 0)
    fixed = lambda i: (0, 0)
    tri = jnp.triu(jnp.ones((ROW_TILE, ROW_TILE), BF16), k=1)
    return pl.pallas_call(
        functools.partial(_out_router_kernel, steps_per_tile=moe_tile // ROW_TILE),
        grid=(t // ROW_TILE,),
        in_specs=[
            pl.BlockSpec((ROW_TILE, d), row),
            pl.BlockSpec((1, d), fixed),
            pl.BlockSpec((1, d), fixed),
            pl.BlockSpec((ROW_TILE, A_WIDTH), row),
            pl.BlockSpec((ROW_TILE, B_WIDTH), row),
            pl.BlockSpec((A_WIDTH, d), fixed),
            pl.BlockSpec((B_WIDTH, d), fixed),
            pl.BlockSpec((1, d), fixed),
            pl.BlockSpec((1, d), fixed),
            pl.BlockSpec((N_EXPERTS, d), fixed),
            pl.BlockSpec((N_EXPERTS, 1), fixed),
            pl.BlockSpec((ROW_TILE, ROW_TILE), fixed),
        ],
        out_specs=[
            pl.BlockSpec((ROW_TILE * nchunk, LANES), row),
            pl.BlockSpec((TOP_K, ROW_TILE), lambda i: (0, i)),
            pl.BlockSpec((TOP_K, ROW_TILE), lambda i: (0, i)),
            pl.BlockSpec((TOP_K, ROW_TILE), lambda i: (0, i)),
            pl.BlockSpec((1, N_EXPERTS, 1), lambda i: (i, 0, 0)),
        ],
        out_shape=[
            jax.ShapeDtypeStruct((t * nchunk, LANES), F32),
            jax.ShapeDtypeStruct((TOP_K, t), jnp.int32),
            jax.ShapeDtypeStruct((TOP_K, t), F32),
            jax.ShapeDtypeStruct((TOP_K, t), jnp.int32),
            jax.ShapeDtypeStruct((t // ROW_TILE, N_EXPERTS, 1), jnp.int32),
        ],
        scratch_shapes=[pltpu.VMEM((N_EXPERTS, 1), F32)],
        compiler_params=pltpu.CompilerParams(
            dimension_semantics=("arbitrary",), vmem_limit_bytes=VMEM_LIMIT),
        name="out_router",
    )(x2, gi, bi, o_a, o_b, wa, wb, g1, b1, rwt, rb, tri)


def _split_w1_kernel(w_ref, g_ref, l_ref):
    rows = w_ref.shape[1]
    even = (lax.broadcasted_iota(jnp.int32, (rows, LANES), 1) & 1) == 0
    for g in range(w_ref.shape[2] // (2 * LANES)):
        v0 = w_ref[0, :, 2 * g * LANES:(2 * g + 1) * LANES]
        v1 = w_ref[0, :, (2 * g + 1) * LANES:(2 * g + 2) * LANES]
        glu = jnp.where(even, v0, pltpu.roll(v1, 1, axis=1))
        lin = jnp.where(even, pltpu.roll(v0, LANES - 1, axis=1), v1)
        g_ref[0, :, g * LANES:(g + 1) * LANES] = glu.astype(BF16)
        l_ref[0, :, g * LANES:(g + 1) * LANES] = lin.astype(BF16)


def _unit_order(v, axis):
    shape = v.shape
    f = shape[axis]
    half = LANES // 2
    v = v.reshape(shape[:axis] + (f // LANES, 2, half) + shape[axis + 1:])
    v = jnp.swapaxes(v, axis + 1, axis + 2)
    return v.reshape(shape)


def _split_w1(w1e):
    e, d, f2 = w1e.shape
    rows = 256
    return pl.pallas_call(
        _split_w1_kernel,
        grid=(e, d // rows),
        in_specs=[pl.BlockSpec((1, rows, f2), lambda i, j: (i, j, 0))],
        out_specs=[pl.BlockSpec((1, rows, f2 // 2), lambda i, j: (i, j, 0)),
                   pl.BlockSpec((1, rows, f2 // 2), lambda i, j: (i, j, 0))],
        out_shape=[jax.ShapeDtypeStruct((e, d, f2 // 2), BF16),
                   jax.ShapeDtypeStruct((e, d, f2 // 2), BF16)],
        compiler_params=pltpu.CompilerParams(
            dimension_semantics=("arbitrary", "arbitrary"), vmem_limit_bytes=VMEM_LIMIT),
        name="split_w1",
    )(w1e)


def _moe_kernel(cnt_ref, offs_ref,
                idx_ref, rank_ref, gate_ref,
                w1g_ref, w1l_ref, b1g_ref, b1l_ref, w2_ref, b2_ref, h_hbm,
                y_hbm, hbuf, ybuf, stage, list_tok, list_gate, sem, *, tile):
    b = pl.program_id(0)
    e = pl.program_id(1)
    n_exp = pl.num_programs(1)
    d_model = w1g_ref.shape[1]
    d_ff = w1g_ref.shape[2]
    nchunk = d_model // LANES
    tile_rows = tile * nchunk
    pairs = tile * TOP_K

    def tok_rows(r):
        return pl.ds(pl.multiple_of(r * nchunk, nchunk), nchunk)

    @pl.when(jnp.logical_and(b == 0, e == 0))
    def _():
        stage[...] = jnp.zeros_like(stage)

        def pad_body(j, c):
            list_tok[pairs + j] = 0
            return c
        lax.fori_loop(0, LIST_PAD, pad_body, 0)

    @pl.when(e == 0)
    def _():
        src = h_hbm.at[pl.ds(pl.multiple_of(b * tile_rows, tile_rows), tile_rows), :]
        load = pltpu.make_async_copy(src, hbuf, sem.at[0])
        load.start()
        ybuf[...] = jnp.zeros_like(ybuf)

        def sort_body(jo, c):
            for u in range(SORT_UNROLL):
                j = jo * SORT_UNROLL + u
                p = offs_ref[b * n_exp + idx_ref[j]] + rank_ref[j]
                list_tok[p] = j & (tile - 1)
                list_gate[p] = gate_ref[j]
            return c
        lax.fori_loop(0, pairs // SORT_UNROLL, sort_body, 0)
        load.wait()

    n = cnt_ref[b * n_exp + e]
    base = offs_ref[b * n_exp + e]

    def run_block(m_rows, p0, nrows):
        def gather_body(ci, c):
            for u in range(SUBLANES):
                r = ci * SUBLANES + u
                stage[tok_rows(r), :] = hbuf[tok_rows(list_tok[p0 + r]), :]
            return c
        lax.fori_loop(0, (nrows + SUBLANES - 1) // SUBLANES, gather_body, 0)

        x = jnp.concatenate(
            [stage[pl.ds(c, m_rows, stride=nchunk), :] for c in range(nchunk)],
            axis=1).astype(BF16)
        y = b2_ref[0]
        for hh in range(d_ff // FF_CHUNK):
            sl = slice(hh * FF_CHUNK, (hh + 1) * FF_CHUNK)
            hg = jnp.dot(x, w1g_ref[0, :, sl], preferred_element_type=F32) + b1g_ref[0, :, sl]
            hl = jnp.dot(x, w1l_ref[0, :, sl], preferred_element_type=F32) + b1l_ref[0, :, sl]
            xg = jnp.minimum(hg, SWIGLU_LIMIT)
            xl = jnp.clip(hl, -SWIGLU_LIMIT, SWIGLU_LIMIT)
            act = xg * jax.nn.sigmoid(SWIGLU_ALPHA * xg) * (xl + 1.0)
            y = y + jnp.dot(act.astype(BF16), w2_ref[0, sl, :], preferred_element_type=F32)
        for c in range(nchunk):
            stage[pl.ds(c, m_rows, stride=nchunk), :] = y[:, c * LANES:(c + 1) * LANES]

        def add_body(ci, c):
            rows, vals = [], []
            for u in range(SUBLANES):
                r = ci * SUBLANES + u
                dst = tok_rows(list_tok[p0 + r])
                rows.append(dst)
                vals.append(ybuf[dst, :] + list_gate[p0 + r] * stage[tok_rows(r), :])
            for dst, val in zip(rows, vals):
                ybuf[dst, :] = val
            return c
        n_full = nrows // SUBLANES
        lax.fori_loop(0, n_full, add_body, 0)

        def add_tail(r, c):
            dst = tok_rows(list_tok[p0 + r])
            ybuf[dst, :] = ybuf[dst, :] + list_gate[p0 + r] * stage[tok_rows(r), :]
            return c
        lax.fori_loop(n_full * SUBLANES, nrows, add_tail, 0)

    n_big = n // MOE_BLOCK
    rem = n - n_big * MOE_BLOCK
    rem_is_big = rem > MOE_BLOCK - SMALL_BLOCK
    n_big_blocks = n_big + rem_is_big.astype(jnp.int32)
    n_small_blocks = jnp.where(rem_is_big, 0, (rem + SMALL_BLOCK - 1) // SMALL_BLOCK)

    def big_body(s, c):
        run_block(MOE_BLOCK, base + s * MOE_BLOCK, jnp.minimum(MOE_BLOCK, n - s * MOE_BLOCK))
        return c
    lax.fori_loop(0, n_big_blocks, big_body, 0)

    def small_body(s, c):
        run_block(SMALL_BLOCK, base + n_big * MOE_BLOCK + s * SMALL_BLOCK,
                  jnp.minimum(SMALL_BLOCK, rem - s * SMALL_BLOCK))
        return c
    lax.fori_loop(0, n_small_blocks, small_body, 0)

    @pl.when(e == n_exp - 1)
    def _():
        dst = y_hbm.at[pl.ds(pl.multiple_of(b * tile_rows, tile_rows), tile_rows), :]
        store = pltpu.make_async_copy(ybuf, dst, sem.at[1])
        store.start()
        store.wait()


def _moe(h1t, cnt, offs, idx_flat, rank_flat, gate_flat, w1g, w1l, b1g, b1l, w2, b2, tile):
    n_exp, d, f = w1g.shape
    nchunk = d // LANES
    n_tiles = h1t.shape[0] // (tile * nchunk)
    pairs = tile * TOP_K
    smem_vec = pl.BlockSpec((pairs,), lambda b, e, cnt, offs: (b,), memory_space=pltpu.SMEM)
    expert = lambda b, e, cnt, offs: (e, 0, 0)
    grid_spec = pltpu.PrefetchScalarGridSpec(
        num_scalar_prefetch=2,
        grid=(n_tiles, n_exp),
        in_specs=[
            smem_vec, smem_vec, smem_vec,
            pl.BlockSpec((1, d, f), expert),
            pl.BlockSpec((1, d, f), expert),
            pl.BlockSpec((1, 1, f), expert),
            pl.BlockSpec((1, 1, f), expert),
            pl.BlockSpec((1, f, d), expert),
            pl.BlockSpec((1, 1, d), expert),
            pl.BlockSpec(memory_space=pl.ANY),
        ],
        out_specs=pl.BlockSpec(memory_space=pl.ANY),
        scratch_shapes=[
            pltpu.VMEM((tile * nchunk, LANES), F32),
            pltpu.VMEM((tile * nchunk, LANES), F32),
            pltpu.VMEM((MOE_BLOCK * nchunk, LANES), F32),
            pltpu.SMEM((pairs + LIST_PAD,), jnp.int32),
            pltpu.SMEM((pairs + LIST_PAD,), F32),
            pltpu.SemaphoreType.DMA((2,)),
        ],
    )
    return pl.pallas_call(
        functools.partial(_moe_kernel, tile=tile),
        grid_spec=grid_spec,
        out_shape=jax.ShapeDtypeStruct(h1t.shape, F32),
        compiler_params=pltpu.CompilerParams(
            dimension_semantics=("arbitrary", "arbitrary"), vmem_limit_bytes=MOE_VMEM_LIMIT),
        name="moe",
    )(cnt, offs, idx_flat, rank_flat, gate_flat, w1g, w1l, b1g, b1l, w2, b2, h1t)


def _combine_kernel(h1_ref, y_ref, g_ref, b_ref, o_ref):
    nchunk = o_ref.shape[1] // LANES
    pieces = [DEEPNORM_ALPHA * h1_ref[pl.ds(c, ROW_TILE, stride=nchunk), :]
              + y_ref[pl.ds(c, ROW_TILE, stride=nchunk), :] for c in range(nchunk)]
    o_ref[...] = _layer_norm(jnp.concatenate(pieces, axis=1), g_ref[...], b_ref[...])


def _combine(h1t, yt, g2, b2):
    d = g2.shape[1]
    nchunk = d // LANES
    t = h1t.shape[0] // nchunk
    tiles = pl.BlockSpec((ROW_TILE * nchunk, LANES), lambda i: (i, 0))
    return pl.pallas_call(
        _combine_kernel,
        grid=(t // ROW_TILE,),
        in_specs=[tiles, tiles,
                  pl.BlockSpec((1, d), lambda i: (0, 0)),
                  pl.BlockSpec((1, d), lambda i: (0, 0))],
        out_specs=pl.BlockSpec((ROW_TILE, d), lambda i: (i, 0)),
        out_shape=jax.ShapeDtypeStruct((t, d), F32),
        compiler_params=pltpu.CompilerParams(
            dimension_semantics=("arbitrary",), vmem_limit_bytes=VMEM_LIMIT),
        name="combine",
    )(h1t, yt, g2, b2)


def kernel(x, ln_in_g, ln_in_b, w_in, lambda_q1, lambda_k1, lambda_q2, lambda_k2, subln_g, rel_bias,
           w_out, ln1_g, ln1_b, router_w, router_b, w1, b1, w2, b2, ln2_g, ln2_b):
    bsz, seq, d = x.shape
    t = bsz * seq
    x2 = x.reshape(t, d)
    row = lambda v: v.reshape(1, -1).astype(F32)

    qs = A_HEAD_DIM ** -0.5 * LOG2E
    col_scale = jnp.concatenate([
        jnp.full((A_WIDTH,), qs, F32), jnp.ones((2 * A_WIDTH,), F32),
        jnp.full((B_WIDTH,), B_HEAD_DIM ** -0.5 * LOG2E, F32), jnp.ones((2 * B_WIDTH,), F32)]).reshape(1, -1)
    proj = _ln_qkv(x2, row(ln_in_g), row(ln_in_b), w_in[0].astype(BF16), col_scale)
    proj3 = proj.reshape(bsz, seq, -1)

    lam4 = jnp.stack([lambda_q1[0], lambda_k1[0], lambda_q2[0], lambda_k2[0]]).astype(F32)
    o_a = _diff_attn(proj3, lam4, row(subln_g[0]), tq=128, tk=min(512, seq))
    o_b = _band_attn(proj3, rel_bias[0], tq=128)

    w_o = w_out[0].astype(BF16)
    tile = min(MOE_TILE, t)
    n_tiles = t // tile
    h1t, top_idx, gates, rank, run_cnt = _out_router(
        x2, row(ln_in_g), row(ln_in_b), o_a.reshape(t, A_WIDTH), o_b.reshape(t, B_WIDTH),
        w_o[:A_WIDTH], w_o[A_WIDTH:], row(ln1_g[0]), row(ln1_b[0]),
        router_w[0].T.astype(F32), router_b[0].reshape(-1, 1).astype(F32), tile)

    steps = tile // ROW_TILE
    cnt = run_cnt[steps - 1::steps, :, 0]
    offs = jnp.cumsum(cnt, axis=1) - cnt
    per_tile = lambda a: a.reshape(TOP_K, n_tiles, tile).transpose(1, 0, 2).reshape(-1)
    w1g, w1l = _split_w1(w1[0])
    b1e = b1[0].astype(F32)[:, None, :]
    yt = _moe(h1t, cnt.reshape(-1), offs.reshape(-1), per_tile(top_idx), per_tile(rank), per_tile(gates),
              w1g, w1l, _unit_order(b1e[:, :, 0::2], 2), _unit_order(b1e[:, :, 1::2], 2),
              _unit_order(w2[0], 1).astype(BF16), b2[0][:, None, :].astype(F32), tile)
    out = _combine(h1t, yt, row(ln2_g[0]), row(ln2_b[0]))
    return out.reshape(bsz, seq, d)
```

```python
import functools
import math

import jax
import jax.numpy as jnp
from jax import lax
from jax.experimental import pallas as pl
from jax.experimental.pallas import tpu as pltpu

F32 = jnp.float32
BF16 = jnp.bfloat16

CHUNK = 64
A_HEADS = 4
A_HEAD_DIM = 64
A_WIDTH = A_HEADS * 2 * A_HEAD_DIM
B_HEADS = 8
B_HEAD_DIM = 64
B_WIDTH = B_HEADS * B_HEAD_DIM
B_PAST_CHUNKS = 8
REL_CLIP = 256
N_EXPERTS = 32
TOP_K = 4
SWIGLU_ALPHA = 1.702
SWIGLU_LIMIT = 7.0
MOE_BLOCK = 512
LN_EPS = 1e-5
RMS_EPS = 1e-5
DEPTH = 1
DEEPNORM_ALPHA = (2 * DEPTH) ** 0.25
LAM_INIT = 0.8 - 0.6 * math.exp(-0.3 * 0)

LOG2E = 1.4426950408889634
NEG = -1e30
LANES = 128
SUBLANES = 8
ROW_TILE = 512
VMEM_LIMIT = 48 * 1024 * 1024
MOE_TILE = 4096
MOE_VMEM_LIMIT = 60 * 1024 * 1024
SMALL_BLOCK = 128
FF_CHUNK = 512
SORT_UNROLL = 8
LIST_PAD = 128


def _layer_norm(x, g, b):
    mu = jnp.mean(x, axis=-1, keepdims=True)
    xc = x - mu
    var = jnp.mean(xc * xc, axis=-1, keepdims=True)
    return xc * lax.rsqrt(var + LN_EPS) * g + b


def _ln_qkv_kernel(x_ref, g_ref, b_ref, w_ref, cs_ref, o_ref):
    h = _layer_norm(x_ref[...], g_ref[...], b_ref[...])
    hb = h.astype(BF16)
    n_out = w_ref.shape[1]
    for c in range(n_out // ROW_TILE):
        sl = slice(c * ROW_TILE, (c + 1) * ROW_TILE)
        acc = jnp.dot(hb, w_ref[:, sl], preferred_element_type=F32)
        o_ref[:, sl] = (acc * cs_ref[:, sl]).astype(BF16)


def _ln_qkv(x2, g, b, w_bf, col_scale):
    t, d = x2.shape
    n_out = w_bf.shape[1]
    return pl.pallas_call(
        _ln_qkv_kernel,
        grid=(t // ROW_TILE,),
        in_specs=[
            pl.BlockSpec((ROW_TILE, d), lambda i: (i, 0)),
            pl.BlockSpec((1, d), lambda i: (0, 0)),
            pl.BlockSpec((1, d), lambda i: (0, 0)),
            pl.BlockSpec((d, n_out), lambda i: (0, 0)),
            pl.BlockSpec((1, n_out), lambda i: (0, 0)),
        ],
        out_specs=pl.BlockSpec((ROW_TILE, n_out), lambda i: (i, 0)),
        out_shape=jax.ShapeDtypeStruct((t, n_out), BF16),
        compiler_params=pltpu.CompilerParams(
            dimension_semantics=("arbitrary",), vmem_limit_bytes=VMEM_LIMIT),
        name="ln_qkv",
    )(x2, g, b, w_bf, col_scale)


def _diff_attn_kernel(lam_ref, g_ref, cbase_ref, cslope_ref, q_ref, k_ref, v_ref, o_ref, *, tq, tk):
    i = pl.program_id(2)
    q = q_ref[0]
    lane = lax.broadcasted_iota(jnp.int32, q.shape, 1)
    zero = jnp.zeros_like(q)
    qq = jnp.concatenate([jnp.where(lane < A_HEAD_DIM, q, zero),
                          jnp.where(lane >= A_HEAD_DIM, q, zero)], axis=0)
    q0 = i * tq
    jd = q0 // tk
    cbase = cbase_ref[0]
    cslope = cslope_ref[0]

    def step(j, carry, diag):
        m, l, acc = carry
        start = pl.multiple_of(j * tk, tk)
        kb = k_ref[0, pl.ds(start, tk), :]
        vb = v_ref[0, pl.ds(start, tk), :]
        s = lax.dot_general(qq, kb, (((1,), (1,)), ((), ())), preferred_element_type=F32)
        off = j * tk - q0
        if diag:
            rowp = lax.broadcasted_iota(jnp.int32, s.shape, 0)
            rowp = jnp.where(rowp >= tq, rowp - tq, rowp)
            colp = lax.broadcasted_iota(jnp.int32, s.shape, 1) + off
            allowed = (colp >> 6) <= (rowp >> 6)
            bias = cslope * (rowp - jnp.abs(rowp - colp)).astype(F32)
            s = jnp.where(allowed, s + bias, NEG)
        else:
            s = s + (cbase + cslope * off.astype(F32))
        m_new = jnp.maximum(m, jnp.max(s, axis=1, keepdims=True))
        alpha = jnp.exp2(m - m_new)
        p = jnp.exp2(s - m_new)
        l = alpha * l + jnp.sum(p, axis=1, keepdims=True)
        acc = alpha * acc + jnp.dot(p.astype(BF16), vb, preferred_element_type=F32)
        return m_new, l, acc

    init = (jnp.full((2 * tq, 1), NEG, F32), jnp.zeros((2 * tq, 1), F32),
            jnp.zeros((2 * tq, LANES), F32))
    carry = lax.fori_loop(0, jd, lambda j, c: step(j, c, False), init)
    _, l, acc = step(jd, carry, True)
    o_all = acc / l
    lv = lam_ref[...]
    lam = (jnp.exp(jnp.sum(lv[0:1] * lv[1:2], axis=1, keepdims=True))
           - jnp.exp(jnp.sum(lv[2:3] * lv[3:4], axis=1, keepdims=True)) + LAM_INIT)
    o = o_all[:tq] - lam * o_all[tq:]
    ms = jnp.mean(o * o, axis=-1, keepdims=True)
    o = o * lax.rsqrt(ms + RMS_EPS) * g_ref[...] * (1.0 - LAM_INIT)
    o_ref[0] = o.astype(BF16)


def _diff_attn(proj3, lam4, subln_g, tq, tk):
    bsz, seq, _ = proj3.shape
    slopes = [2.0 ** (-8.0 * (h + 1) / A_HEADS) for h in range(A_HEADS)]
    c = jnp.asarray(slopes, F32)[:, None, None] * LOG2E
    cslope = jnp.broadcast_to(c, (A_HEADS, 1, tk))
    cbase = c * jnp.arange(tk, dtype=F32)[None, None, :]
    kblk = A_WIDTH // LANES
    return pl.pallas_call(
        functools.partial(_diff_attn_kernel, tq=tq, tk=tk),
        grid=(bsz, A_HEADS, seq // tq),
        in_specs=[
            pl.BlockSpec((4, A_HEAD_DIM), lambda b, h, i: (0, 0)),
            pl.BlockSpec((1, LANES), lambda b, h, i: (0, 0)),
            pl.BlockSpec((1, 1, tk), lambda b, h, i: (h, 0, 0)),
            pl.BlockSpec((1, 1, tk), lambda b, h, i: (h, 0, 0)),
            pl.BlockSpec((1, tq, LANES), lambda b, h, i: (b, i, h)),
            pl.BlockSpec((1, seq, LANES), lambda b, h, i: (b, 0, kblk + h)),
            pl.BlockSpec((1, seq, LANES), lambda b, h, i: (b, 0, 2 * kblk + h)),
        ],
        out_specs=pl.BlockSpec((1, tq, LANES), lambda b, h, i: (b, i, h)),
        out_shape=jax.ShapeDtypeStruct((bsz, seq, A_WIDTH), BF16),
        compiler_params=pltpu.CompilerParams(
            dimension_semantics=("arbitrary", "arbitrary", "arbitrary"),
            vmem_limit_bytes=VMEM_LIMIT),
        name="diff_attn",
    )(lam4, subln_g, cbase, cslope, proj3, proj3, proj3)


def _band_attn_kernel(bias_ref, q_ref, k_ref, v_ref, o_ref, kpad, vpad, *, tq, band, pad):
    i = pl.program_id(2)
    seq = k_ref.shape[1]

    @pl.when(i == 0)
    def _():
        kpad[0:pad, :] = jnp.zeros((pad, LANES), BF16)
        vpad[0:pad, :] = jnp.zeros((pad, LANES), BF16)
        kpad[pad:pad + seq, :] = k_ref[0]
        vpad[pad:pad + seq, :] = v_ref[0]

    q = q_ref[0]
    lane = lax.broadcasted_iota(jnp.int32, q.shape, 1)
    zero = jnp.zeros_like(q)
    qq = jnp.concatenate([jnp.where(lane < B_HEAD_DIM, q, zero),
                          jnp.where(lane >= B_HEAD_DIM, q, zero)], axis=0)
    ks = pl.multiple_of(i * tq, tq)
    kb = kpad[pl.ds(ks, band), :]
    vb = vpad[pl.ds(ks, band), :]
    s = lax.dot_general(qq, kb, (((1,), (1,)), ((), ())), preferred_element_type=F32)
    s = s + bias_ref[0]
    col = lax.broadcasted_iota(jnp.int32, (1, band), 1) + ks
    s = s + jnp.where(col < pad, NEG, 0.0)
    m = jnp.max(s, axis=1, keepdims=True)
    p = jnp.exp2(s - m)
    l = jnp.sum(p, axis=1, keepdims=True)
    o = jnp.dot(p.astype(BF16), vb, preferred_element_type=F32) / l
    o_ref[0] = jnp.where(lane < B_HEAD_DIM, o[:tq], o[tq:]).astype(BF16)


def _band_bias(rel_bias, tq):
    past = B_PAST_CHUNKS * CHUNK
    band = tq + past
    assert tq - 1 <= REL_CLIP <= past
    qi = jnp.arange(tq)
    kj = jnp.arange(band)
    cq = qi[:, None] // CHUNK
    ck = kj[None, :] // CHUNK
    allowed = (ck >= cq) & (ck <= cq + B_PAST_CHUNKS)
    tab = rel_bias.astype(F32) * LOG2E
    n_diag = band + tq - 1
    n_unclipped = REL_CLIP + tq
    w = jnp.concatenate([tab[:, REL_CLIP - (tq - 1):],
                         jnp.broadcast_to(tab[:, -1:], (B_HEADS, n_diag - n_unclipped))], axis=1)
    shifted = jnp.tile(w, (1, tq + 1))[:, :tq * (n_diag + 1)].reshape(B_HEADS, tq, n_diag + 1)
    bias = jnp.flip(shifted[:, :, :band], axis=2)
    bias = jnp.where(allowed[None], bias, NEG)
    return bias.reshape(B_HEADS // 2, 2 * tq, band)


def _band_attn(proj3, rel_bias, tq):
    bsz, seq, _ = proj3.shape
    pad = B_PAST_CHUNKS * CHUNK
    band = tq + pad
    bias = _band_bias(rel_bias, tq)
    groups = B_HEADS // 2
    qblk = 3 * A_WIDTH // LANES
    kblk = qblk + B_WIDTH // LANES
    vblk = kblk + B_WIDTH // LANES
    return pl.pallas_call(
        functools.partial(_band_attn_kernel, tq=tq, band=band, pad=pad),
        grid=(bsz, groups, seq // tq),
        in_specs=[
            pl.BlockSpec((1, 2 * tq, band), lambda b, g, i: (g, 0, 0)),
            pl.BlockSpec((1, tq, LANES), lambda b, g, i: (b, i, qblk + g)),
            pl.BlockSpec((1, seq, LANES), lambda b, g, i: (b, 0, kblk + g)),
            pl.BlockSpec((1, seq, LANES), lambda b, g, i: (b, 0, vblk + g)),
        ],
        out_specs=pl.BlockSpec((1, tq, LANES), lambda b, g, i: (b, i, g)),
        out_shape=jax.ShapeDtypeStruct((bsz, seq, B_WIDTH), BF16),
        scratch_shapes=[pltpu.VMEM((seq + pad, LANES), BF16),
                        pltpu.VMEM((seq + pad, LANES), BF16)],
        compiler_params=pltpu.CompilerParams(
            dimension_semantics=("arbitrary", "arbitrary", "arbitrary"),
            vmem_limit_bytes=VMEM_LIMIT),
        name="band_attn",
    )(bias, proj3, proj3, proj3)


def _out_router_kernel(x_ref, gi_ref, bi_ref, oa_ref, ob_ref, wa_ref, wb_ref, g1_ref, b1_ref,
                       rwt_ref, rb_ref, tri_ref, h1_ref, idx_ref, gate_ref, rank_ref, cnt_ref,
                       carry_ref, *, steps_per_tile):
    h = _layer_norm(x_ref[...], gi_ref[...], bi_ref[...])
    mix = (jnp.dot(oa_ref[...], wa_ref[...], preferred_element_type=F32)
           + jnp.dot(ob_ref[...], wb_ref[...], preferred_element_type=F32))
    h1 = _layer_norm(DEEPNORM_ALPHA * h + mix, g1_ref[...], b1_ref[...])
    nchunk = h1.shape[1] // LANES
    for c in range(nchunk):
        h1_ref[pl.ds(c, ROW_TILE, stride=nchunk), :] = h1[:, c * LANES:(c + 1) * LANES]
    lt = lax.dot_general(rwt_ref[...], h1, (((1,), (1,)), ((), ())),
                         precision=lax.Precision.HIGHEST, preferred_element_type=F32)
    lt = lt + rb_ref[...]
    eidx = lax.broadcasted_iota(jnp.int32, lt.shape, 0)
    vals, idxs, hots = [], [], []
    for _ in range(TOP_K):
        mx = jnp.max(lt, axis=0, keepdims=True)
        am = jnp.min(jnp.where(lt == mx, eidx, N_EXPERTS), axis=0, keepdims=True)
        hit = eidx == am
        vals.append(mx)
        idxs.append(am)
        hots.append(jnp.where(hit, 1.0, 0.0))
        lt = jnp.where(hit, -jnp.inf, lt)
    ex = [jnp.exp(v - vals[0]) for v in vals]
    den = ex[0] + ex[1] + ex[2] + ex[3]
    idx_ref[...] = jnp.concatenate(idxs, axis=0)
    gate_ref[...] = jnp.concatenate([e / den for e in ex], axis=0)

    @pl.when(pl.program_id(0) % steps_per_tile == 0)
    def _():
        carry_ref[...] = jnp.zeros_like(carry_ref)

    hot = (hots[0] + hots[1]) + (hots[2] + hots[3])
    before = jnp.dot(hot.astype(BF16), tri_ref[...], preferred_element_type=F32) + carry_ref[...]
    rank_ref[...] = jnp.concatenate(
        [jnp.sum(hk * before, axis=0, keepdims=True) for hk in hots], axis=0).astype(jnp.int32)
    total = carry_ref[...] + jnp.sum(hot, axis=1, keepdims=True)
    carry_ref[...] = total
    cnt_ref[0] = total.astype(jnp.int32)


def _out_router(x2, gi, bi, o_a, o_b, wa, wb, g1, b1, rwt, rb, moe_tile):
    t, d = x2.shape
    nchunk = d // LANES
    row = lambda i: (i, 0)
    fixed = lambda i: (0, 0)
    tri = jnp.triu(jnp.ones((ROW_TILE, ROW_TILE), BF16), k=1)
    return pl.pallas_call(
        functools.partial(_out_router_kernel, steps_per_tile=moe_tile // ROW_TILE),
        grid=(t // ROW_TILE,),
        in_specs=[
            pl.BlockSpec((ROW_TILE, d), row),
            pl.BlockSpec((1, d), fixed),
            pl.BlockSpec((1, d), fixed),
            pl.BlockSpec((ROW_TILE, A_WIDTH), row),
            pl.BlockSpec((ROW_TILE, B_WIDTH), row),
            pl.BlockSpec((A_WIDTH, d), fixed),
            pl.BlockSpec((B_WIDTH, d), fixed),
            pl.BlockSpec((1, d), fixed),
            pl.BlockSpec((1, d), fixed),
            pl.BlockSpec((N_EXPERTS, d), fixed),
            pl.BlockSpec((N_EXPERTS, 1), fixed),
            pl.BlockSpec((ROW_TILE, ROW_TILE), fixed),
        ],
        out_specs=[
            pl.BlockSpec((ROW_TILE * nchunk, LANES), row),
            pl.BlockSpec((TOP_K, ROW_TILE), lambda i: (0, i)),
            pl.BlockSpec((TOP_K, ROW_TILE), lambda i: (0, i)),
            pl.BlockSpec((TOP_K, ROW_TILE), lambda i: (0, i)),
            pl.BlockSpec((1, N_EXPERTS, 1), lambda i: (i, 0, 0)),
        ],
        out_shape=[
            jax.ShapeDtypeStruct((t * nchunk, LANES), F32),
            jax.ShapeDtypeStruct((TOP_K, t), jnp.int32),
            jax.ShapeDtypeStruct((TOP_K, t), F32),
            jax.ShapeDtypeStruct((TOP_K, t), jnp.int32),
            jax.ShapeDtypeStruct((t // ROW_TILE, N_EXPERTS, 1), jnp.int32),
        ],
        scratch_shapes=[pltpu.VMEM((N_EXPERTS, 1), F32)],
        compiler_params=pltpu.CompilerParams(
            dimension_semantics=("arbitrary",), vmem_limit_bytes=VMEM_LIMIT),
        name="out_router",
    )(x2, gi, bi, o_a, o_b, wa, wb, g1, b1, rwt, rb, tri)


def _split_w1_kernel(w_ref, g_ref, l_ref):
    rows = w_ref.shape[1]
    even = (lax.broadcasted_iota(jnp.int32, (rows, LANES), 1) & 1) == 0
    for g in range(w_ref.shape[2] // (2 * LANES)):
        v0 = w_ref[0, :, 2 * g * LANES:(2 * g + 1) * LANES]
        v1 = w_ref[0, :, (2 * g + 1) * LANES:(2 * g + 2) * LANES]
        glu = jnp.where(even, v0, pltpu.roll(v1, 1, axis=1))
        lin = jnp.where(even, pltpu.roll(v0, LANES - 1, axis=1), v1)
        g_ref[0, :, g * LANES:(g + 1) * LANES] = glu.astype(BF16)
        l_ref[0, :, g * LANES:(g + 1) * LANES] = lin.astype(BF16)


def _unit_order(v, axis):
    shape = v.shape
    f = shape[axis]
    half = LANES // 2
    v = v.reshape(shape[:axis] + (f // LANES, 2, half) + shape[axis + 1:])
    v = jnp.swapaxes(v, axis + 1, axis + 2)
    return v.reshape(shape)


def _split_w1(w1e):
    e, d, f2 = w1e.shape
    rows = 256
    return pl.pallas_call(
        _split_w1_kernel,
        grid=(e, d // rows),
        in_specs=[pl.BlockSpec((1, rows, f2), lambda i, j: (i, j, 0))],
        out_specs=[pl.BlockSpec((1, rows, f2 // 2), lambda i, j: (i, j, 0)),
                   pl.BlockSpec((1, rows, f2 // 2), lambda i, j: (i, j, 0))],
        out_shape=[jax.ShapeDtypeStruct((e, d, f2 // 2), BF16),
                   jax.ShapeDtypeStruct((e, d, f2 // 2), BF16)],
        compiler_params=pltpu.CompilerParams(
            dimension_semantics=("arbitrary", "arbitrary"), vmem_limit_bytes=VMEM_LIMIT),
        name="split_w1",
    )(w1e)


def _moe_kernel(cnt_ref, offs_ref,
                pos_ref, gate_ref,
                w1g_ref, w1l_ref, b1g_ref, b1l_ref, w2_ref, b2_ref, h_hbm,
                y_hbm, hbuf, ybuf, stage, list_tok, list_gate, sem, *, tile):
    b = pl.program_id(0)
    e = pl.program_id(1)
    n_exp = pl.num_programs(1)
    d_model = w1g_ref.shape[1]
    d_ff = w1g_ref.shape[2]
    nchunk = d_model // LANES
    tile_rows = tile * nchunk
    pairs = tile * TOP_K

    def stage_rows(r):
        return pl.ds(pl.multiple_of(r * nchunk, nchunk), nchunk)

    def listed_rows(p):
        return pl.ds(pl.multiple_of(list_tok[p], nchunk), nchunk)

    @pl.when(jnp.logical_and(b == 0, e == 0))
    def _():
        stage[...] = jnp.zeros_like(stage)

        def pad_body(j, c):
            list_tok[pairs + j] = 0
            return c
        lax.fori_loop(0, LIST_PAD, pad_body, 0)

    @pl.when(e == 0)
    def _():
        src = h_hbm.at[pl.ds(pl.multiple_of(b * tile_rows, tile_rows), tile_rows), :]
        load = pltpu.make_async_copy(src, hbuf, sem.at[0])
        load.start()
        ybuf[...] = jnp.zeros_like(ybuf)

        def sort_body(jo, c):
            for u in range(SORT_UNROLL):
                j = jo * SORT_UNROLL + u
                p = pos_ref[j]
                list_tok[p] = (j & (tile - 1)) * nchunk
                list_gate[p] = gate_ref[j]
            return c
        lax.fori_loop(0, pairs // SORT_UNROLL, sort_body, 0)
        load.wait()

    n = cnt_ref[b * n_exp + e]
    base = offs_ref[b * n_exp + e]

    def run_block(m_rows, p0, nrows):
        def gather_body(ci, c):
            for u in range(SUBLANES):
                r = ci * SUBLANES + u
                stage[stage_rows(r), :] = hbuf[listed_rows(p0 + r), :]
            return c
        lax.fori_loop(0, (nrows + SUBLANES - 1) // SUBLANES, gather_body, 0)

        x = jnp.concatenate(
            [stage[pl.ds(c, m_rows, stride=nchunk), :] for c in range(nchunk)],
            axis=1).astype(BF16)
        y = b2_ref[0]
        for hh in range(d_ff // FF_CHUNK):
            sl = slice(hh * FF_CHUNK, (hh + 1) * FF_CHUNK)
            hg = jnp.dot(x, w1g_ref[0, :, sl], preferred_element_type=F32) + b1g_ref[0, :, sl]
            hl = jnp.dot(x, w1l_ref[0, :, sl], preferred_element_type=F32) + b1l_ref[0, :, sl]
            xg = jnp.minimum(hg, SWIGLU_LIMIT)
            xl = jnp.clip(hl, -SWIGLU_LIMIT, SWIGLU_LIMIT)
            act = xg * jax.nn.sigmoid(SWIGLU_ALPHA * xg) * (xl + 1.0)
            y = y + jnp.dot(act.astype(BF16), w2_ref[0, sl, :], preferred_element_type=F32)
        for c in range(nchunk):
            stage[pl.ds(c, m_rows, stride=nchunk), :] = y[:, c * LANES:(c + 1) * LANES]

        def add_body(ci, c):
            rows, vals = [], []
            for u in range(SUBLANES):
                r = ci * SUBLANES + u
                dst = listed_rows(p0 + r)
                rows.append(dst)
                vals.append(ybuf[dst, :] + list_gate[p0 + r] * stage[stage_rows(r), :])
            for dst, val in zip(rows, vals):
                ybuf[dst, :] = val
            return c
        n_full = nrows // SUBLANES
        lax.fori_loop(0, n_full, add_body, 0)

        def add_tail(r, c):
            dst = listed_rows(p0 + r)
            ybuf[dst, :] = ybuf[dst, :] + list_gate[p0 + r] * stage[stage_rows(r), :]
            return c
        lax.fori_loop(n_full * SUBLANES, nrows, add_tail, 0)

    n_big = n // MOE_BLOCK
    rem = n - n_big * MOE_BLOCK
    rem_is_big = rem > MOE_BLOCK - SMALL_BLOCK
    n_big_blocks = n_big + rem_is_big.astype(jnp.int32)
    n_small_blocks = jnp.where(rem_is_big, 0, (rem + SMALL_BLOCK - 1) // SMALL_BLOCK)

    def big_body(s, c):
        run_block(MOE_BLOCK, base + s * MOE_BLOCK, jnp.minimum(MOE_BLOCK, n - s * MOE_BLOCK))
        return c
    lax.fori_loop(0, n_big_blocks, big_body, 0)

    def small_body(s, c):
        run_block(SMALL_BLOCK, base + n_big * MOE_BLOCK + s * SMALL_BLOCK,
                  jnp.minimum(SMALL_BLOCK, rem - s * SMALL_BLOCK))
        return c
    lax.fori_loop(0, n_small_blocks, small_body, 0)

    @pl.when(e == n_exp - 1)
    def _():
        dst = y_hbm.at[pl.ds(pl.multiple_of(b * tile_rows, tile_rows), tile_rows), :]
        store = pltpu.make_async_copy(ybuf, dst, sem.at[1])
        store.start()
        store.wait()


def _moe(h1t, cnt, offs, pos_flat, gate_flat, w1g, w1l, b1g, b1l, w2, b2, tile):
    n_exp, d, f = w1g.shape
    nchunk = d // LANES
    n_tiles = h1t.shape[0] // (tile * nchunk)
    pairs = tile * TOP_K
    smem_vec = pl.BlockSpec((pairs,), lambda b, e, cnt, offs: (b,), memory_space=pltpu.SMEM)
    expert = lambda b, e, cnt, offs: (e, 0, 0)
    grid_spec = pltpu.PrefetchScalarGridSpec(
        num_scalar_prefetch=2,
        grid=(n_tiles, n_exp),
        in_specs=[
            smem_vec, smem_vec,
            pl.BlockSpec((1, d, f), expert),
            pl.BlockSpec((1, d, f), expert),
            pl.BlockSpec((1, 1, f), expert),
            pl.BlockSpec((1, 1, f), expert),
            pl.BlockSpec((1, f, d), expert),
            pl.BlockSpec((1, 1, d), expert),
            pl.BlockSpec(memory_space=pl.ANY),
        ],
        out_specs=pl.BlockSpec(memory_space=pl.ANY),
        scratch_shapes=[
            pltpu.VMEM((tile * nchunk, LANES), F32),
            pltpu.VMEM((tile * nchunk, LANES), F32),
            pltpu.VMEM((MOE_BLOCK * nchunk, LANES), F32),
            pltpu.SMEM((pairs + LIST_PAD,), jnp.int32),
            pltpu.SMEM((pairs + LIST_PAD,), F32),
            pltpu.SemaphoreType.DMA((2,)),
        ],
    )
    return pl.pallas_call(
        functools.partial(_moe_kernel, tile=tile),
        grid_spec=grid_spec,
        out_shape=jax.ShapeDtypeStruct(h1t.shape, F32),
        compiler_params=pltpu.CompilerParams(
            dimension_semantics=("arbitrary", "arbitrary"), vmem_limit_bytes=MOE_VMEM_LIMIT),
        name="moe",
    )(cnt, offs, pos_flat, gate_flat, w1g, w1l, b1g, b1l, w2, b2, h1t)


def _combine_kernel(h1_ref, y_ref, g_ref, b_ref, o_ref):
    nchunk = o_ref.shape[1] // LANES
    pieces = [DEEPNORM_ALPHA * h1_ref[pl.ds(c, ROW_TILE, stride=nchunk), :]
              + y_ref[pl.ds(c, ROW_TILE, stride=nchunk), :] for c in range(nchunk)]
    o_ref[...] = _layer_norm(jnp.concatenate(pieces, axis=1), g_ref[...], b_ref[...])


def _combine(h1t, yt, g2, b2):
    d = g2.shape[1]
    nchunk = d // LANES
    t = h1t.shape[0] // nchunk
    tiles = pl.BlockSpec((ROW_TILE * nchunk, LANES), lambda i: (i, 0))
    return pl.pallas_call(
        _combine_kernel,
        grid=(t // ROW_TILE,),
        in_specs=[tiles, tiles,
                  pl.BlockSpec((1, d), lambda i: (0, 0)),
                  pl.BlockSpec((1, d), lambda i: (0, 0))],
        out_specs=pl.BlockSpec((ROW_TILE, d), lambda i: (i, 0)),
        out_shape=jax.ShapeDtypeStruct((t, d), F32),
        compiler_params=pltpu.CompilerParams(
            dimension_semantics=("arbitrary",), vmem_limit_bytes=VMEM_LIMIT),
        name="combine",
    )(h1t, yt, g2, b2)


def kernel(x, ln_in_g, ln_in_b, w_in, lambda_q1, lambda_k1, lambda_q2, lambda_k2, subln_g, rel_bias,
           w_out, ln1_g, ln1_b, router_w, router_b, w1, b1, w2, b2, ln2_g, ln2_b):
    bsz, seq, d = x.shape
    t = bsz * seq
    x2 = x.reshape(t, d)
    row = lambda v: v.reshape(1, -1).astype(F32)

    qs = A_HEAD_DIM ** -0.5 * LOG2E
    col_scale = jnp.concatenate([
        jnp.full((A_WIDTH,), qs, F32), jnp.ones((2 * A_WIDTH,), F32),
        jnp.full((B_WIDTH,), B_HEAD_DIM ** -0.5 * LOG2E, F32), jnp.ones((2 * B_WIDTH,), F32)]).reshape(1, -1)
    proj = _ln_qkv(x2, row(ln_in_g), row(ln_in_b), w_in[0].astype(BF16), col_scale)
    proj3 = proj.reshape(bsz, seq, -1)

    lam4 = jnp.stack([lambda_q1[0], lambda_k1[0], lambda_q2[0], lambda_k2[0]]).astype(F32)
    o_a = _diff_attn(proj3, lam4, row(subln_g[0]), tq=128, tk=min(512, seq))
    o_b = _band_attn(proj3, rel_bias[0], tq=128)

    w_o = w_out[0].astype(BF16)
    tile = min(MOE_TILE, t)
    n_tiles = t // tile
    h1t, top_idx, gates, rank, run_cnt = _out_router(
        x2, row(ln_in_g), row(ln_in_b), o_a.reshape(t, A_WIDTH), o_b.reshape(t, B_WIDTH),
        w_o[:A_WIDTH], w_o[A_WIDTH:], row(ln1_g[0]), row(ln1_b[0]),
        router_w[0].T.astype(F32), router_b[0].reshape(-1, 1).astype(F32), tile)

    steps = tile // ROW_TILE
    cnt = run_cnt[steps - 1::steps, :, 0]
    offs = jnp.cumsum(cnt, axis=1) - cnt
    hot = top_idx.reshape(TOP_K, n_tiles, tile, 1) == jnp.arange(N_EXPERTS, dtype=jnp.int32)
    pos = rank + jnp.sum(jnp.where(hot, offs[None, :, None, :], 0), axis=-1).reshape(TOP_K, t)
    per_tile = lambda a: a.reshape(TOP_K, n_tiles, tile).transpose(1, 0, 2).reshape(-1)
    w1g, w1l = _split_w1(w1[0])
    b1e = b1[0].astype(F32)[:, None, :]
    yt = _moe(h1t, cnt.reshape(-1), offs.reshape(-1), per_tile(pos), per_tile(gates),
              w1g, w1l, _unit_order(b1e[:, :, 0::2], 2), _unit_order(b1e[:, :, 1::2], 2),
              _unit_order(w2[0], 1).astype(BF16), b2[0][:, None, :].astype(F32), tile)
    out = _combine(h1t, yt, row(ln2_g[0]), row(ln2_b[0]))
    return out.reshape(bsz, seq, d)
```

```python
import functools
import math

import jax
import jax.numpy as jnp
from jax import lax
from jax.experimental import pallas as pl
from jax.experimental.pallas import tpu as pltpu

F32 = jnp.float32
BF16 = jnp.bfloat16

CHUNK = 64
A_HEADS = 4
A_HEAD_DIM = 64
A_WIDTH = A_HEADS * 2 * A_HEAD_DIM
B_HEADS = 8
B_HEAD_DIM = 64
B_WIDTH = B_HEADS * B_HEAD_DIM
B_PAST_CHUNKS = 8
REL_CLIP = 256
N_EXPERTS = 32
TOP_K = 4
SWIGLU_ALPHA = 1.702
SWIGLU_LIMIT = 7.0
MOE_BLOCK = 512
LN_EPS = 1e-5
RMS_EPS = 1e-5
DEPTH = 1
DEEPNORM_ALPHA = (2 * DEPTH) ** 0.25
LAM_INIT = 0.8 - 0.6 * math.exp(-0.3 * 0)

LOG2E = 1.4426950408889634
NEG = -1e30
LANES = 128
SUBLANES = 8
ROW_TILE = 512
VMEM_LIMIT = 48 * 1024 * 1024
DIFF_TILE = 256
MOE_TILE = 4096
MOE_VMEM_LIMIT = 60 * 1024 * 1024
SMALL_BLOCK = 128
FF_CHUNK = 512
SORT_UNROLL = 8
LIST_PAD = 128


def _layer_norm(x, g, b):
    mu = jnp.mean(x, axis=-1, keepdims=True)
    xc = x - mu
    var = jnp.mean(xc * xc, axis=-1, keepdims=True)
    return xc * lax.rsqrt(var + LN_EPS) * g + b


def _ln_qkv_kernel(x_ref, g_ref, b_ref, w_ref, cs_ref, o_ref, vt_ref):
    h = _layer_norm(x_ref[...], g_ref[...], b_ref[...])
    hb = h.astype(BF16)
    n_out = w_ref.shape[1]
    for c in range(n_out // ROW_TILE):
        sl = slice(c * ROW_TILE, (c + 1) * ROW_TILE)
        val = jnp.dot(hb, w_ref[:, sl], preferred_element_type=F32) * cs_ref[:, sl]
        o_ref[:, sl] = val.astype(BF16)
        if sl.start == 2 * A_WIDTH:
            for kt in range(ROW_TILE // DIFF_TILE):
                vt_ref[kt] = val[kt * DIFF_TILE:(kt + 1) * DIFF_TILE, :].T.astype(BF16)


def _ln_qkv(x2, g, b, w_bf, col_scale):
    assert A_WIDTH == ROW_TILE
    t, d = x2.shape
    n_out = w_bf.shape[1]
    kt_per_step = ROW_TILE // DIFF_TILE
    return pl.pallas_call(
        _ln_qkv_kernel,
        grid=(t // ROW_TILE,),
        in_specs=[
            pl.BlockSpec((ROW_TILE, d), lambda i: (i, 0)),
            pl.BlockSpec((1, d), lambda i: (0, 0)),
            pl.BlockSpec((1, d), lambda i: (0, 0)),
            pl.BlockSpec((d, n_out), lambda i: (0, 0)),
            pl.BlockSpec((1, n_out), lambda i: (0, 0)),
        ],
        out_specs=[pl.BlockSpec((ROW_TILE, n_out), lambda i: (i, 0)),
                   pl.BlockSpec((kt_per_step, A_WIDTH, DIFF_TILE), lambda i: (i, 0, 0))],
        out_shape=[jax.ShapeDtypeStruct((t, n_out), BF16),
                   jax.ShapeDtypeStruct((t // DIFF_TILE, A_WIDTH, DIFF_TILE), BF16)],
        compiler_params=pltpu.CompilerParams(
            dimension_semantics=("arbitrary",), vmem_limit_bytes=VMEM_LIMIT),
        name="ln_qkv",
    )(x2, g, b, w_bf, col_scale)


def _diff_attn_kernel(c_ref, lam_ref, g_ref, boff_ref, bdiag_ref, q_ref, k_ref, vt_ref, o_ref, s_scr):
    tq = q_ref.shape[1]
    tk = vt_ref.shape[2]
    h = pl.program_id(1)
    i = pl.program_id(2)
    c = c_ref[h]
    q = q_ref[0]
    lane = lax.broadcasted_iota(jnp.int32, q.shape, 1)
    zero = jnp.zeros_like(q)
    qq = jnp.concatenate([jnp.where(lane < A_HEAD_DIM, q, zero),
                          jnp.where(lane >= A_HEAD_DIM, q, zero)], axis=0)
    qqt = qq.astype(F32).T.astype(BF16)
    n_before = i * (tq // tk)

    def scores(j):
        kb = k_ref[0, pl.ds(pl.multiple_of(j * tk, tk), tk), :]
        return jnp.dot(kb, qqt, preferred_element_type=F32)

    def update(slot, table_ref, j, carry):
        m, l, acc = carry
        s = s_scr[slot] + table_ref[0]
        shift = c * (j * tk).astype(F32)
        m_new = jnp.maximum(m, jnp.max(s, axis=0, keepdims=True) + shift)
        alpha = jnp.exp2(m - m_new)
        p = jnp.exp2(s - (m_new - shift))
        l = alpha * l + jnp.sum(p, axis=0, keepdims=True)
        acc = alpha * acc + jnp.dot(vt_ref[j], p.astype(BF16), preferred_element_type=F32)
        return m_new, l, acc

    s_scr[0] = scores(0)

    def pair(t, carry):
        j = 2 * t
        s_scr[1] = scores(j + 1)
        carry = update(0, boff_ref, j, carry)
        s_scr[0] = scores(j + 2)
        return update(1, boff_ref, j + 1, carry)

    init = (jnp.full((1, 2 * tq), NEG, F32), jnp.zeros((1, 2 * tq), F32),
            jnp.zeros((LANES, 2 * tq), F32))
    carry = lax.fori_loop(0, n_before // 2, pair, init)

    def odd_tail(carry):
        s_scr[1] = scores(n_before)
        carry = update(0, boff_ref, n_before - 1, carry)
        return update(1, bdiag_ref, n_before, carry)

    def even_tail(carry):
        return update(0, bdiag_ref, n_before, carry)

    _, l, acc = lax.cond(n_before % 2 == 1, odd_tail, even_tail, carry)

    o_all = acc / l
    lv = lam_ref[...]
    lam = (jnp.exp(jnp.sum(lv[0:1] * lv[1:2], axis=1, keepdims=True))
           - jnp.exp(jnp.sum(lv[2:3] * lv[3:4], axis=1, keepdims=True)) + LAM_INIT)
    o = o_all[:, :tq] - lam * o_all[:, tq:]
    ms = jnp.mean(o * o, axis=0, keepdims=True)
    o = o * lax.rsqrt(ms + RMS_EPS) * (g_ref[...] * (1.0 - LAM_INIT))
    o_ref[0] = o.T.astype(BF16)


def _diff_attn(proj3, vt3, lam4, subln_g):
    bsz, seq, _ = proj3.shape
    tq = tk = DIFF_TILE
    c = jnp.asarray([2.0 ** (-8.0 * (h + 1) / A_HEADS) for h in range(A_HEADS)], F32) * LOG2E
    r = jnp.arange(tk, dtype=jnp.int32)[:, None]
    qrel = jnp.arange(2 * tq, dtype=jnp.int32)[None, :] % tq
    boff = c[:, None, None] * jnp.broadcast_to(r, (tk, 2 * tq)).astype(F32)
    allowed = (r // CHUNK) <= (qrel // CHUNK)
    bdiag = jnp.where(allowed, c[:, None, None] * (qrel - jnp.abs(qrel - r)).astype(F32), NEG)
    kblk = A_WIDTH // LANES
    return pl.pallas_call(
        _diff_attn_kernel,
        grid=(bsz, A_HEADS, seq // tq),
        in_specs=[
            pl.BlockSpec(memory_space=pltpu.SMEM),
            pl.BlockSpec((4, A_HEAD_DIM), lambda b, h, i: (0, 0)),
            pl.BlockSpec((LANES, 1), lambda b, h, i: (0, 0)),
            pl.BlockSpec((1, tk, 2 * tq), lambda b, h, i: (h, 0, 0)),
            pl.BlockSpec((1, tk, 2 * tq), lambda b, h, i: (h, 0, 0)),
            pl.BlockSpec((1, tq, LANES), lambda b, h, i: (b, i, h)),
            pl.BlockSpec((1, seq, LANES), lambda b, h, i: (b, 0, kblk + h)),
            pl.BlockSpec((seq // tk, LANES, tk), lambda b, h, i: (b, h, 0)),
        ],
        out_specs=pl.BlockSpec((1, tq, LANES), lambda b, h, i: (b, i, h)),
        out_shape=jax.ShapeDtypeStruct((bsz, seq, A_WIDTH), BF16),
        scratch_shapes=[pltpu.VMEM((2, tk, 2 * tq), F32)],
        compiler_params=pltpu.CompilerParams(
            dimension_semantics=("arbitrary", "arbitrary", "arbitrary"),
            vmem_limit_bytes=VMEM_LIMIT),
        name="diff_attn",
    )(c, lam4, subln_g.reshape(LANES, 1), boff, bdiag, proj3, proj3, vt3)


def _band_attn_kernel(bias_ref, q_ref, k_ref, v_ref, o_ref, kpad, vpad, *, tq, band, pad):
    i = pl.program_id(2)
    seq = k_ref.shape[1]

    @pl.when(i == 0)
    def _():
        kpad[0:pad, :] = jnp.zeros((pad, LANES), BF16)
        vpad[0:pad, :] = jnp.zeros((pad, LANES), BF16)
        kpad[pad:pad + seq, :] = k_ref[0]
        vpad[pad:pad + seq, :] = v_ref[0]

    q = q_ref[0]
    lane = lax.broadcasted_iota(jnp.int32, q.shape, 1)
    zero = jnp.zeros_like(q)
    qq = jnp.concatenate([jnp.where(lane < B_HEAD_DIM, q, zero),
                          jnp.where(lane >= B_HEAD_DIM, q, zero)], axis=0)
    ks = pl.multiple_of(i * tq, tq)
    kb = kpad[pl.ds(ks, band), :]
    vb = vpad[pl.ds(ks, band), :]
    s = lax.dot_general(qq, kb, (((1,), (1,)), ((), ())), preferred_element_type=F32)
    s = s + bias_ref[0]
    col = lax.broadcasted_iota(jnp.int32, (1, band), 1) + ks
    s = s + jnp.where(col < pad, NEG, 0.0)
    m = jnp.max(s, axis=1, keepdims=True)
    p = jnp.exp2(s - m)
    l = jnp.sum(p, axis=1, keepdims=True)
    o = jnp.dot(p.astype(BF16), vb, preferred_element_type=F32) / l
    o_ref[0] = jnp.where(lane < B_HEAD_DIM, o[:tq], o[tq:]).astype(BF16)


def _band_bias(rel_bias, tq):
    past = B_PAST_CHUNKS * CHUNK
    band = tq + past
    assert tq - 1 <= REL_CLIP <= past
    qi = jnp.arange(tq)
    kj = jnp.arange(band)
    cq = qi[:, None] // CHUNK
    ck = kj[None, :] // CHUNK
    allowed = (ck >= cq) & (ck <= cq + B_PAST_CHUNKS)
    tab = rel_bias.astype(F32) * LOG2E
    n_diag = band + tq - 1
    n_unclipped = REL_CLIP + tq
    w = jnp.concatenate([tab[:, REL_CLIP - (tq - 1):],
                         jnp.broadcast_to(tab[:, -1:], (B_HEADS, n_diag - n_unclipped))], axis=1)
    shifted = jnp.tile(w, (1, tq + 1))[:, :tq * (n_diag + 1)].reshape(B_HEADS, tq, n_diag + 1)
    bias = jnp.flip(shifted[:, :, :band], axis=2)
    bias = jnp.where(allowed[None], bias, NEG)
    return bias.reshape(B_HEADS // 2, 2 * tq, band)


def _band_attn(proj3, rel_bias, tq):
    bsz, seq, _ = proj3.shape
    pad = B_PAST_CHUNKS * CHUNK
    band = tq + pad
    bias = _band_bias(rel_bias, tq)
    groups = B_HEADS // 2
    qblk = 3 * A_WIDTH // LANES
    kblk = qblk + B_WIDTH // LANES
    vblk = kblk + B_WIDTH // LANES
    return pl.pallas_call(
        functools.partial(_band_attn_kernel, tq=tq, band=band, pad=pad),
        grid=(bsz, groups, seq // tq),
        in_specs=[
            pl.BlockSpec((1, 2 * tq, band), lambda b, g, i: (g, 0, 0)),
            pl.BlockSpec((1, tq, LANES), lambda b, g, i: (b, i, qblk + g)),
            pl.BlockSpec((1, seq, LANES), lambda b, g, i: (b, 0, kblk + g)),
            pl.BlockSpec((1, seq, LANES), lambda b, g, i: (b, 0, vblk + g)),
        ],
        out_specs=pl.BlockSpec((1, tq, LANES), lambda b, g, i: (b, i, g)),
        out_shape=jax.ShapeDtypeStruct((bsz, seq, B_WIDTH), BF16),
        scratch_shapes=[pltpu.VMEM((seq + pad, LANES), BF16),
                        pltpu.VMEM((seq + pad, LANES), BF16)],
        compiler_params=pltpu.CompilerParams(
            dimension_semantics=("arbitrary", "arbitrary", "arbitrary"),
            vmem_limit_bytes=VMEM_LIMIT),
        name="band_attn",
    )(bias, proj3, proj3, proj3)


def _out_router_kernel(x_ref, gi_ref, bi_ref, oa_ref, ob_ref, wa_ref, wb_ref, g1_ref, b1_ref,
                       rwt_ref, rb_ref, tri_ref, h1_ref, idx_ref, gate_ref, rank_ref, cnt_ref,
                       carry_ref, *, steps_per_tile):
    h = _layer_norm(x_ref[...], gi_ref[...], bi_ref[...])
    mix = (jnp.dot(oa_ref[...], wa_ref[...], preferred_element_type=F32)
           + jnp.dot(ob_ref[...], wb_ref[...], preferred_element_type=F32))
    h1 = _layer_norm(DEEPNORM_ALPHA * h + mix, g1_ref[...], b1_ref[...])
    nchunk = h1.shape[1] // LANES
    for c in range(nchunk):
        h1_ref[pl.ds(c, ROW_TILE, stride=nchunk), :] = h1[:, c * LANES:(c + 1) * LANES]
    lt = lax.dot_general(rwt_ref[...], h1, (((1,), (1,)), ((), ())),
                         precision=lax.Precision.HIGHEST, preferred_element_type=F32)
    lt = lt + rb_ref[...]
    eidx = lax.broadcasted_iota(jnp.int32, lt.shape, 0)
    vals, idxs, hots = [], [], []
    for _ in range(TOP_K):
        mx = jnp.max(lt, axis=0, keepdims=True)
        am = jnp.min(jnp.where(lt == mx, eidx, N_EXPERTS), axis=0, keepdims=True)
        hit = eidx == am
        vals.append(mx)
        idxs.append(am)
        hots.append(jnp.where(hit, 1.0, 0.0))
        lt = jnp.where(hit, -jnp.inf, lt)
    ex = [jnp.exp(v - vals[0]) for v in vals]
    den = ex[0] + ex[1] + ex[2] + ex[3]
    idx_ref[...] = jnp.concatenate(idxs, axis=0)
    gate_ref[...] = jnp.concatenate([e / den for e in ex], axis=0)

    @pl.when(pl.program_id(0) % steps_per_tile == 0)
    def _():
        carry_ref[...] = jnp.zeros_like(carry_ref)

    hot = (hots[0] + hots[1]) + (hots[2] + hots[3])
    before = jnp.dot(hot.astype(BF16), tri_ref[...], preferred_element_type=F32) + carry_ref[...]
    rank_ref[...] = jnp.concatenate(
        [jnp.sum(hk * before, axis=0, keepdims=True) for hk in hots], axis=0).astype(jnp.int32)
    total = carry_ref[...] + jnp.sum(hot, axis=1, keepdims=True)
    carry_ref[...] = total
    cnt_ref[0] = total.astype(jnp.int32)


def _out_router(x2, gi, bi, o_a, o_b, wa, wb, g1, b1, rwt, rb, moe_tile):
    t, d = x2.shape
    nchunk = d // LANES
    row = lambda i: (i, 0)
    fixed = lambda i: (0, 0)
    tri = jnp.triu(jnp.ones((ROW_TILE, ROW_TILE), BF16), k=1)
    return pl.pallas_call(
        functools.partial(_out_router_kernel, steps_per_tile=moe_tile // ROW_TILE),
        grid=(t // ROW_TILE,),
        in_specs=[
            pl.BlockSpec((ROW_TILE, d), row),
            pl.BlockSpec((1, d), fixed),
            pl.BlockSpec((1, d), fixed),
            pl.BlockSpec((ROW_TILE, A_WIDTH), row),
            pl.BlockSpec((ROW_TILE, B_WIDTH), row),
            pl.BlockSpec((A_WIDTH, d), fixed),
            pl.BlockSpec((B_WIDTH, d), fixed),
            pl.BlockSpec((1, d), fixed),
            pl.BlockSpec((1, d), fixed),
            pl.BlockSpec((N_EXPERTS, d), fixed),
            pl.BlockSpec((N_EXPERTS, 1), fixed),
            pl.BlockSpec((ROW_TILE, ROW_TILE), fixed),
        ],
        out_specs=[
            pl.BlockSpec((ROW_TILE * nchunk, LANES), row),
            pl.BlockSpec((TOP_K, ROW_TILE), lambda i: (0, i)),
            pl.BlockSpec((TOP_K, ROW_TILE), lambda i: (0, i)),
            pl.BlockSpec((TOP_K, ROW_TILE), lambda i: (0, i)),
            pl.BlockSpec((1, N_EXPERTS, 1), lambda i: (i, 0, 0)),
        ],
        out_shape=[
            jax.ShapeDtypeStruct((t * nchunk, LANES), F32),
            jax.ShapeDtypeStruct((TOP_K, t), jnp.int32),
            jax.ShapeDtypeStruct((TOP_K, t), F32),
            jax.ShapeDtypeStruct((TOP_K, t), jnp.int32),
            jax.ShapeDtypeStruct((t // ROW_TILE, N_EXPERTS, 1), jnp.int32),
        ],
        scratch_shapes=[pltpu.VMEM((N_EXPERTS, 1), F32)],
        compiler_params=pltpu.CompilerParams(
            dimension_semantics=("arbitrary",), vmem_limit_bytes=VMEM_LIMIT),
        name="out_router",
    )(x2, gi, bi, o_a, o_b, wa, wb, g1, b1, rwt, rb, tri)


def _split_w1_kernel(w_ref, g_ref, l_ref):
    rows = w_ref.shape[1]
    even = (lax.broadcasted_iota(jnp.int32, (rows, LANES), 1) & 1) == 0
    for g in range(w_ref.shape[2] // (2 * LANES)):
        v0 = w_ref[0, :, 2 * g * LANES:(2 * g + 1) * LANES]
        v1 = w_ref[0, :, (2 * g + 1) * LANES:(2 * g + 2) * LANES]
        glu = jnp.where(even, v0, pltpu.roll(v1, 1, axis=1))
        lin = jnp.where(even, pltpu.roll(v0, LANES - 1, axis=1), v1)
        g_ref[0, :, g * LANES:(g + 1) * LANES] = glu.astype(BF16)
        l_ref[0, :, g * LANES:(g + 1) * LANES] = lin.astype(BF16)


def _unit_order(v, axis):
    shape = v.shape
    f = shape[axis]
    half = LANES // 2
    v = v.reshape(shape[:axis] + (f // LANES, 2, half) + shape[axis + 1:])
    v = jnp.swapaxes(v, axis + 1, axis + 2)
    return v.reshape(shape)


def _split_w1(w1e):
    e, d, f2 = w1e.shape
    rows = 256
    return pl.pallas_call(
        _split_w1_kernel,
        grid=(e, d // rows),
        in_specs=[pl.BlockSpec((1, rows, f2), lambda i, j: (i, j, 0))],
        out_specs=[pl.BlockSpec((1, rows, f2 // 2), lambda i, j: (i, j, 0)),
                   pl.BlockSpec((1, rows, f2 // 2), lambda i, j: (i, j, 0))],
        out_shape=[jax.ShapeDtypeStruct((e, d, f2 // 2), BF16),
                   jax.ShapeDtypeStruct((e, d, f2 // 2), BF16)],
        compiler_params=pltpu.CompilerParams(
            dimension_semantics=("arbitrary", "arbitrary"), vmem_limit_bytes=VMEM_LIMIT),
        name="split_w1",
    )(w1e)


def _moe_kernel(cnt_ref, offs_ref,
                pos_ref, gate_ref,
                w1g_ref, w1l_ref, b1g_ref, b1l_ref, w2_ref, b2_ref, h_hbm,
                y_hbm, hbuf, ybuf, stage, list_tok, list_gate, sem, *, tile):
    b = pl.program_id(0)
    e = pl.program_id(1)
    n_exp = pl.num_programs(1)
    d_model = w1g_ref.shape[1]
    d_ff = w1g_ref.shape[2]
    nchunk = d_model // LANES
    tile_rows = tile * nchunk
    pairs = tile * TOP_K

    def stage_rows(r):
        return pl.ds(pl.multiple_of(r * nchunk, nchunk), nchunk)

    def listed_rows(p):
        return pl.ds(pl.multiple_of(list_tok[p], nchunk), nchunk)

    @pl.when(jnp.logical_and(b == 0, e == 0))
    def _():
        stage[...] = jnp.zeros_like(stage)

        def pad_body(j, c):
            list_tok[pairs + j] = 0
            return c
        lax.fori_loop(0, LIST_PAD, pad_body, 0)

    @pl.when(e == 0)
    def _():
        src = h_hbm.at[pl.ds(pl.multiple_of(b * tile_rows, tile_rows), tile_rows), :]
        load = pltpu.make_async_copy(src, hbuf, sem.at[0])
        load.start()
        ybuf[...] = jnp.zeros_like(ybuf)

        def sort_body(jo, c):
            for u in range(SORT_UNROLL):
                j = jo * SORT_UNROLL + u
                p = pos_ref[j]
                list_tok[p] = (j & (tile - 1)) * nchunk
                list_gate[p] = gate_ref[j]
            return c
        lax.fori_loop(0, pairs // SORT_UNROLL, sort_body, 0)
        load.wait()

    n = cnt_ref[b * n_exp + e]
    base = offs_ref[b * n_exp + e]

    def run_block(m_rows, p0, nrows):
        def gather_body(ci, c):
            for u in range(SUBLANES):
                r = ci * SUBLANES + u
                stage[stage_rows(r), :] = hbuf[listed_rows(p0 + r), :]
            return c
        lax.fori_loop(0, (nrows + SUBLANES - 1) // SUBLANES, gather_body, 0)

        x = jnp.concatenate(
            [stage[pl.ds(c, m_rows, stride=nchunk), :] for c in range(nchunk)],
            axis=1).astype(BF16)
        y = b2_ref[0]
        for hh in range(d_ff // FF_CHUNK):
            sl = slice(hh * FF_CHUNK, (hh + 1) * FF_CHUNK)
            hg = jnp.dot(x, w1g_ref[0, :, sl], preferred_element_type=F32) + b1g_ref[0, :, sl]
            hl = jnp.dot(x, w1l_ref[0, :, sl], preferred_element_type=F32) + b1l_ref[0, :, sl]
            xg = jnp.minimum(hg, SWIGLU_LIMIT)
            xl = jnp.clip(hl, -SWIGLU_LIMIT, SWIGLU_LIMIT)
            act = xg * jax.nn.sigmoid(SWIGLU_ALPHA * xg) * (xl + 1.0)
            y = y + jnp.dot(act.astype(BF16), w2_ref[0, sl, :], preferred_element_type=F32)
        for c in range(nchunk):
            stage[pl.ds(c, m_rows, stride=nchunk), :] = y[:, c * LANES:(c + 1) * LANES]

        def add_body(ci, c):
            rows, vals = [], []
            for u in range(SUBLANES):
                r = ci * SUBLANES + u
                dst = listed_rows(p0 + r)
                rows.append(dst)
                vals.append(ybuf[dst, :] + list_gate[p0 + r] * stage[stage_rows(r), :])
            for dst, val in zip(rows, vals):
                ybuf[dst, :] = val
            return c
        n_full = nrows // SUBLANES
        lax.fori_loop(0, n_full, add_body, 0)

        def add_tail(r, c):
            dst = listed_rows(p0 + r)
            ybuf[dst, :] = ybuf[dst, :] + list_gate[p0 + r] * stage[stage_rows(r), :]
            return c
        lax.fori_loop(n_full * SUBLANES, nrows, add_tail, 0)

    n_big = n // MOE_BLOCK
    rem = n - n_big * MOE_BLOCK
    rem_is_big = rem > MOE_BLOCK - SMALL_BLOCK
    n_big_blocks = n_big + rem_is_big.astype(jnp.int32)
    n_small_blocks = jnp.where(rem_is_big, 0, (rem + SMALL_BLOCK - 1) // SMALL_BLOCK)

    def big_body(s, c):
        run_block(MOE_BLOCK, base + s * MOE_BLOCK, jnp.minimum(MOE_BLOCK, n - s * MOE_BLOCK))
        return c
    lax.fori_loop(0, n_big_blocks, big_body, 0)

    def small_body(s, c):
        run_block(SMALL_BLOCK, base + n_big * MOE_BLOCK + s * SMALL_BLOCK,
                  jnp.minimum(SMALL_BLOCK, rem - s * SMALL_BLOCK))
        return c
    lax.fori_loop(0, n_small_blocks, small_body, 0)

    @pl.when(e == n_exp - 1)
    def _():
        dst = y_hbm.at[pl.ds(pl.multiple_of(b * tile_rows, tile_rows), tile_rows), :]
        store = pltpu.make_async_copy(ybuf, dst, sem.at[1])
        store.start()
        store.wait()


def _moe(h1t, cnt, offs, pos_flat, gate_flat, w1g, w1l, b1g, b1l, w2, b2, tile):
    n_exp, d, f = w1g.shape
    nchunk = d // LANES
    n_tiles = h1t.shape[0] // (tile * nchunk)
    pairs = tile * TOP_K
    smem_vec = pl.BlockSpec((pairs,), lambda b, e, cnt, offs: (b,), memory_space=pltpu.SMEM)
    expert = lambda b, e, cnt, offs: (e, 0, 0)
    grid_spec = pltpu.PrefetchScalarGridSpec(
        num_scalar_prefetch=2,
        grid=(n_tiles, n_exp),
        in_specs=[
            smem_vec, smem_vec,
            pl.BlockSpec((1, d, f), expert),
            pl.BlockSpec((1, d, f), expert),
            pl.BlockSpec((1, 1, f), expert),
            pl.BlockSpec((1, 1, f), expert),
            pl.BlockSpec((1, f, d), expert),
            pl.BlockSpec((1, 1, d), expert),
            pl.BlockSpec(memory_space=pl.ANY),
        ],
        out_specs=pl.BlockSpec(memory_space=pl.ANY),
        scratch_shapes=[
            pltpu.VMEM((tile * nchunk, LANES), F32),
            pltpu.VMEM((tile * nchunk, LANES), F32),
            pltpu.VMEM((MOE_BLOCK * nchunk, LANES), F32),
            pltpu.SMEM((pairs + LIST_PAD,), jnp.int32),
            pltpu.SMEM((pairs + LIST_PAD,), F32),
            pltpu.SemaphoreType.DMA((2,)),
        ],
    )
    return pl.pallas_call(
        functools.partial(_moe_kernel, tile=tile),
        grid_spec=grid_spec,
        out_shape=jax.ShapeDtypeStruct(h1t.shape, F32),
        compiler_params=pltpu.CompilerParams(
            dimension_semantics=("arbitrary", "arbitrary"), vmem_limit_bytes=MOE_VMEM_LIMIT),
        name="moe",
    )(cnt, offs, pos_flat, gate_flat, w1g, w1l, b1g, b1l, w2, b2, h1t)


def _combine_kernel(h1_ref, y_ref, g_ref, b_ref, o_ref):
    nchunk = o_ref.shape[1] // LANES
    pieces = [DEEPNORM_ALPHA * h1_ref[pl.ds(c, ROW_TILE, stride=nchunk), :]
              + y_ref[pl.ds(c, ROW_TILE, stride=nchunk), :] for c in range(nchunk)]
    o_ref[...] = _layer_norm(jnp.concatenate(pieces, axis=1), g_ref[...], b_ref[...])


def _combine(h1t, yt, g2, b2):
    d = g2.shape[1]
    nchunk = d // LANES
    t = h1t.shape[0] // nchunk
    tiles = pl.BlockSpec((ROW_TILE * nchunk, LANES), lambda i: (i, 0))
    return pl.pallas_call(
        _combine_kernel,
        grid=(t // ROW_TILE,),
        in_specs=[tiles, tiles,
                  pl.BlockSpec((1, d), lambda i: (0, 0)),
                  pl.BlockSpec((1, d), lambda i: (0, 0))],
        out_specs=pl.BlockSpec((ROW_TILE, d), lambda i: (i, 0)),
        out_shape=jax.ShapeDtypeStruct((t, d), F32),
        compiler_params=pltpu.CompilerParams(
            dimension_semantics=("arbitrary",), vmem_limit_bytes=VMEM_LIMIT),
        name="combine",
    )(h1t, yt, g2, b2)


def kernel(x, ln_in_g, ln_in_b, w_in, lambda_q1, lambda_k1, lambda_q2, lambda_k2, subln_g, rel_bias,
           w_out, ln1_g, ln1_b, router_w, router_b, w1, b1, w2, b2, ln2_g, ln2_b):
    bsz, seq, d = x.shape
    t = bsz * seq
    x2 = x.reshape(t, d)
    row = lambda v: v.reshape(1, -1).astype(F32)

    qs = A_HEAD_DIM ** -0.5 * LOG2E
    col_scale = jnp.concatenate([
        jnp.full((A_WIDTH,), qs, F32), jnp.ones((2 * A_WIDTH,), F32),
        jnp.full((B_WIDTH,), B_HEAD_DIM ** -0.5 * LOG2E, F32), jnp.ones((2 * B_WIDTH,), F32)]).reshape(1, -1)
    proj, vt3 = _ln_qkv(x2, row(ln_in_g), row(ln_in_b), w_in[0].astype(BF16), col_scale)
    proj3 = proj.reshape(bsz, seq, -1)

    lam4 = jnp.stack([lambda_q1[0], lambda_k1[0], lambda_q2[0], lambda_k2[0]]).astype(F32)
    o_a = _diff_attn(proj3, vt3, lam4, subln_g[0].astype(F32))
    o_b = _band_attn(proj3, rel_bias[0], tq=128)

    w_o = w_out[0].astype(BF16)
    tile = min(MOE_TILE, t)
    n_tiles = t // tile
    h1t, top_idx, gates, rank, run_cnt = _out_router(
        x2, row(ln_in_g), row(ln_in_b), o_a.reshape(t, A_WIDTH), o_b.reshape(t, B_WIDTH),
        w_o[:A_WIDTH], w_o[A_WIDTH:], row(ln1_g[0]), row(ln1_b[0]),
        router_w[0].T.astype(F32), router_b[0].reshape(-1, 1).astype(F32), tile)

    steps = tile // ROW_TILE
    cnt = run_cnt[steps - 1::steps, :, 0]
    offs = jnp.cumsum(cnt, axis=1) - cnt
    hot = top_idx.reshape(TOP_K, n_tiles, tile, 1) == jnp.arange(N_EXPERTS, dtype=jnp.int32)
    pos = rank + jnp.sum(jnp.where(hot, offs[None, :, None, :], 0), axis=-1).reshape(TOP_K, t)
    per_tile = lambda a: a.reshape(TOP_K, n_tiles, tile).transpose(1, 0, 2).reshape(-1)
    w1g, w1l = _split_w1(w1[0])
    b1e = b1[0].astype(F32)[:, None, :]
    yt = _moe(h1t, cnt.reshape(-1), offs.reshape(-1), per_tile(pos), per_tile(gates),
              w1g, w1l, _unit_order(b1e[:, :, 0::2], 2), _unit_order(b1e[:, :, 1::2], 2),
              _unit_order(w2[0], 1).astype(BF16), b2[0][:, None, :].astype(F32), tile)
    out = _combine(h1t, yt, row(ln2_g[0]), row(ln2_b[0]))
    return out.reshape(bsz, seq, d)
```

```python
import functools
import math

import jax
import jax.numpy as jnp
from jax import lax
from jax.experimental import pallas as pl
from jax.experimental.pallas import tpu as pltpu

F32 = jnp.float32
BF16 = jnp.bfloat16

CHUNK = 64
A_HEADS = 4
A_HEAD_DIM = 64
A_WIDTH = A_HEADS * 2 * A_HEAD_DIM
B_HEADS = 8
B_HEAD_DIM = 64
B_WIDTH = B_HEADS * B_HEAD_DIM
B_PAST_CHUNKS = 8
REL_CLIP = 256
N_EXPERTS = 32
TOP_K = 4
SWIGLU_ALPHA = 1.702
SWIGLU_LIMIT = 7.0
MOE_BLOCK = 512
LN_EPS = 1e-5
RMS_EPS = 1e-5
DEPTH = 1
DEEPNORM_ALPHA = (2 * DEPTH) ** 0.25
LAM_INIT = 0.8 - 0.6 * math.exp(-0.3 * 0)

LOG2E = 1.4426950408889634
NEG = -1e30
LANES = 128
SUBLANES = 8
ROW_TILE = 512
VMEM_LIMIT = 48 * 1024 * 1024
DIFF_TILE = 256
MOE_TILE = 4096
MOE_VMEM_LIMIT = 60 * 1024 * 1024
SMALL_BLOCK = 128
FF_CHUNK = 512
SORT_UNROLL = 8
ROW_UNROLL = 16
LIST_PAD = 128


def _layer_norm(x, g, b):
    mu = jnp.mean(x, axis=-1, keepdims=True)
    xc = x - mu
    var = jnp.mean(xc * xc, axis=-1, keepdims=True)
    return xc * lax.rsqrt(var + LN_EPS) * g + b


def _ln_qkv_kernel(x_ref, g_ref, b_ref, w_ref, cs_ref, o_ref, vt_ref):
    h = _layer_norm(x_ref[...], g_ref[...], b_ref[...])
    hb = h.astype(BF16)
    n_out = w_ref.shape[1]
    for c in range(n_out // ROW_TILE):
        sl = slice(c * ROW_TILE, (c + 1) * ROW_TILE)
        val = jnp.dot(hb, w_ref[:, sl], preferred_element_type=F32) * cs_ref[:, sl]
        o_ref[0, :, sl] = val.astype(BF16)
        if sl.start == 2 * A_WIDTH:
            for kt in range(ROW_TILE // DIFF_TILE):
                vt_ref[kt] = val[kt * DIFF_TILE:(kt + 1) * DIFF_TILE, :].T.astype(BF16)


def _ln_qkv(x2, g, b, w_bf, col_scale, seq):
    assert A_WIDTH == ROW_TILE
    t, d = x2.shape
    n_out = w_bf.shape[1]
    kt_per_step = ROW_TILE // DIFF_TILE
    steps_per_seq = seq // ROW_TILE
    return pl.pallas_call(
        _ln_qkv_kernel,
        grid=(t // ROW_TILE,),
        in_specs=[
            pl.BlockSpec((ROW_TILE, d), lambda i: (i, 0)),
            pl.BlockSpec((1, d), lambda i: (0, 0)),
            pl.BlockSpec((1, d), lambda i: (0, 0)),
            pl.BlockSpec((d, n_out), lambda i: (0, 0)),
            pl.BlockSpec((1, n_out), lambda i: (0, 0)),
        ],
        out_specs=[pl.BlockSpec((1, ROW_TILE, n_out),
                                lambda i: (i // steps_per_seq, i % steps_per_seq, 0)),
                   pl.BlockSpec((kt_per_step, A_WIDTH, DIFF_TILE), lambda i: (i, 0, 0))],
        out_shape=[jax.ShapeDtypeStruct((t // seq, seq, n_out), BF16),
                   jax.ShapeDtypeStruct((t // DIFF_TILE, A_WIDTH, DIFF_TILE), BF16)],
        compiler_params=pltpu.CompilerParams(
            dimension_semantics=("arbitrary",), vmem_limit_bytes=VMEM_LIMIT),
        name="ln_qkv",
    )(x2, g, b, w_bf, col_scale)


def _diff_attn_kernel(c_ref, lam_ref, g_ref, boff_ref, bdiag_ref, q_ref, k_ref, vt_ref, o_ref, s_scr):
    tq = q_ref.shape[1]
    tk = vt_ref.shape[2]
    h = pl.program_id(1)
    i = pl.program_id(2)
    c = c_ref[h]
    q = q_ref[0]
    lane = lax.broadcasted_iota(jnp.int32, q.shape, 1)
    zero = jnp.zeros_like(q)
    qq = jnp.concatenate([jnp.where(lane < A_HEAD_DIM, q, zero),
                          jnp.where(lane >= A_HEAD_DIM, q, zero)], axis=0)
    qqt = qq.astype(F32).T.astype(BF16)
    n_before = i * (tq // tk)

    def scores(j):
        kb = k_ref[0, pl.ds(pl.multiple_of(j * tk, tk), tk), :]
        return jnp.dot(kb, qqt, preferred_element_type=F32)

    def update(slot, table_ref, j, carry):
        m, l, acc = carry
        s = s_scr[slot] + table_ref[0]
        shift = c * (j * tk).astype(F32)
        m_new = jnp.maximum(m, jnp.max(s, axis=0, keepdims=True) + shift)
        alpha = jnp.exp2(m - m_new)
        p = jnp.exp2(s - (m_new - shift))
        l = alpha * l + jnp.sum(p, axis=0, keepdims=True)
        acc = alpha * acc + jnp.dot(vt_ref[j], p.astype(BF16), preferred_element_type=F32)
        return m_new, l, acc

    s_scr[0] = scores(0)

    def pair(t, carry):
        j = 2 * t
        s_scr[1] = scores(j + 1)
        carry = update(0, boff_ref, j, carry)
        s_scr[0] = scores(j + 2)
        return update(1, boff_ref, j + 1, carry)

    init = (jnp.full((1, 2 * tq), NEG, F32), jnp.zeros((1, 2 * tq), F32),
            jnp.zeros((LANES, 2 * tq), F32))
    carry = lax.fori_loop(0, n_before // 2, pair, init)

    def odd_tail(carry):
        s_scr[1] = scores(n_before)
        carry = update(0, boff_ref, n_before - 1, carry)
        return update(1, bdiag_ref, n_before, carry)

    def even_tail(carry):
        return update(0, bdiag_ref, n_before, carry)

    _, l, acc = lax.cond(n_before % 2 == 1, odd_tail, even_tail, carry)

    o_all = acc / l
    lv = lam_ref[...]
    lam = (jnp.exp(jnp.sum(lv[0:1] * lv[1:2], axis=1, keepdims=True))
           - jnp.exp(jnp.sum(lv[2:3] * lv[3:4], axis=1, keepdims=True)) + LAM_INIT)
    o = o_all[:, :tq] - lam * o_all[:, tq:]
    ms = jnp.mean(o * o, axis=0, keepdims=True)
    o = o * lax.rsqrt(ms + RMS_EPS) * (g_ref[...] * (1.0 - LAM_INIT))
    o_ref[0] = o.T.astype(BF16)


def _diff_attn(proj3, vt3, lam4, subln_g):
    bsz, seq, _ = proj3.shape
    tq = tk = DIFF_TILE
    c = jnp.asarray([2.0 ** (-8.0 * (h + 1) / A_HEADS) for h in range(A_HEADS)], F32) * LOG2E
    r = jnp.arange(tk, dtype=jnp.int32)[:, None]
    qrel = jnp.arange(2 * tq, dtype=jnp.int32)[None, :] % tq
    boff = c[:, None, None] * jnp.broadcast_to(r, (tk, 2 * tq)).astype(F32)
    allowed = (r // CHUNK) <= (qrel // CHUNK)
    bdiag = jnp.where(allowed, c[:, None, None] * (qrel - jnp.abs(qrel - r)).astype(F32), NEG)
    kblk = A_WIDTH // LANES
    return pl.pallas_call(
        _diff_attn_kernel,
        grid=(bsz, A_HEADS, seq // tq),
        in_specs=[
            pl.BlockSpec(memory_space=pltpu.SMEM),
            pl.BlockSpec((4, A_HEAD_DIM), lambda b, h, i: (0, 0)),
            pl.BlockSpec((LANES, 1), lambda b, h, i: (0, 0)),
            pl.BlockSpec((1, tk, 2 * tq), lambda b, h, i: (h, 0, 0)),
            pl.BlockSpec((1, tk, 2 * tq), lambda b, h, i: (h, 0, 0)),
            pl.BlockSpec((1, tq, LANES), lambda b, h, i: (b, i, h)),
            pl.BlockSpec((1, seq, LANES), lambda b, h, i: (b, 0, kblk + h)),
            pl.BlockSpec((seq // tk, LANES, tk), lambda b, h, i: (b, h, 0)),
        ],
        out_specs=pl.BlockSpec((1, tq, LANES), lambda b, h, i: (b, i, h)),
        out_shape=jax.ShapeDtypeStruct((bsz, seq, A_WIDTH), BF16),
        scratch_shapes=[pltpu.VMEM((2, tk, 2 * tq), F32)],
        compiler_params=pltpu.CompilerParams(
            dimension_semantics=("arbitrary", "arbitrary", "arbitrary"),
            vmem_limit_bytes=VMEM_LIMIT),
        name="diff_attn",
    )(c, lam4, subln_g.reshape(LANES, 1), boff, bdiag, proj3, proj3, vt3)


def _band_attn_kernel(bias_ref, q_ref, k_ref, v_ref, o_ref, kpad, vpad, *, tq, band, pad):
    i = pl.program_id(2)
    seq = k_ref.shape[1]

    @pl.when(i == 0)
    def _():
        kpad[0:pad, :] = jnp.zeros((pad, LANES), BF16)
        vpad[0:pad, :] = jnp.zeros((pad, LANES), BF16)
        kpad[pad:pad + seq, :] = k_ref[0]
        vpad[pad:pad + seq, :] = v_ref[0]

    q = q_ref[0]
    lane = lax.broadcasted_iota(jnp.int32, q.shape, 1)
    zero = jnp.zeros_like(q)
    qq = jnp.concatenate([jnp.where(lane < B_HEAD_DIM, q, zero),
                          jnp.where(lane >= B_HEAD_DIM, q, zero)], axis=0)
    ks = pl.multiple_of(i * tq, tq)
    kb = kpad[pl.ds(ks, band), :]
    vb = vpad[pl.ds(ks, band), :]
    s = lax.dot_general(qq, kb, (((1,), (1,)), ((), ())), preferred_element_type=F32)
    s = s + bias_ref[0]
    col = lax.broadcasted_iota(jnp.int32, (1, band), 1) + ks
    s = s + jnp.where(col < pad, NEG, 0.0)
    m = jnp.max(s, axis=1, keepdims=True)
    p = jnp.exp2(s - m)
    l = jnp.sum(p, axis=1, keepdims=True)
    o = jnp.dot(p.astype(BF16), vb, preferred_element_type=F32) / l
    o_ref[0] = jnp.where(lane < B_HEAD_DIM, o[:tq], o[tq:]).astype(BF16)


def _band_bias(rel_bias, tq):
    past = B_PAST_CHUNKS * CHUNK
    band = tq + past
    assert tq - 1 <= REL_CLIP <= past
    qi = jnp.arange(tq)
    kj = jnp.arange(band)
    cq = qi[:, None] // CHUNK
    ck = kj[None, :] // CHUNK
    allowed = (ck >= cq) & (ck <= cq + B_PAST_CHUNKS)
    tab = rel_bias.astype(F32) * LOG2E
    n_diag = band + tq - 1
    n_unclipped = REL_CLIP + tq
    w = jnp.concatenate([tab[:, REL_CLIP - (tq - 1):],
                         jnp.broadcast_to(tab[:, -1:], (B_HEADS, n_diag - n_unclipped))], axis=1)
    shifted = jnp.tile(w, (1, tq + 1))[:, :tq * (n_diag + 1)].reshape(B_HEADS, tq, n_diag + 1)
    bias = jnp.flip(shifted[:, :, :band], axis=2)
    bias = jnp.where(allowed[None], bias, NEG)
    return bias.reshape(B_HEADS // 2, 2 * tq, band)


def _band_attn(proj3, rel_bias, tq):
    bsz, seq, _ = proj3.shape
    pad = B_PAST_CHUNKS * CHUNK
    band = tq + pad
    bias = _band_bias(rel_bias, tq)
    groups = B_HEADS // 2
    qblk = 3 * A_WIDTH // LANES
    kblk = qblk + B_WIDTH // LANES
    vblk = kblk + B_WIDTH // LANES
    return pl.pallas_call(
        functools.partial(_band_attn_kernel, tq=tq, band=band, pad=pad),
        grid=(bsz, groups, seq // tq),
        in_specs=[
            pl.BlockSpec((1, 2 * tq, band), lambda b, g, i: (g, 0, 0)),
            pl.BlockSpec((1, tq, LANES), lambda b, g, i: (b, i, qblk + g)),
            pl.BlockSpec((1, seq, LANES), lambda b, g, i: (b, 0, kblk + g)),
            pl.BlockSpec((1, seq, LANES), lambda b, g, i: (b, 0, vblk + g)),
        ],
        out_specs=pl.BlockSpec((1, tq, LANES), lambda b, g, i: (b, i, g)),
        out_shape=jax.ShapeDtypeStruct((bsz, seq, B_WIDTH), BF16),
        scratch_shapes=[pltpu.VMEM((seq + pad, LANES), BF16),
                        pltpu.VMEM((seq + pad, LANES), BF16)],
        compiler_params=pltpu.CompilerParams(
            dimension_semantics=("arbitrary", "arbitrary", "arbitrary"),
            vmem_limit_bytes=VMEM_LIMIT),
        name="band_attn",
    )(bias, proj3, proj3, proj3)


def _out_router_kernel(x_ref, gi_ref, bi_ref, oa_ref, ob_ref, wa_ref, wb_ref, g1_ref, b1_ref,
                       rwt_ref, rb_ref, tri_ref, h1_ref, idx_ref, gate_ref, rank_ref, cnt_ref,
                       carry_ref, *, steps_per_tile):
    h = _layer_norm(x_ref[...], gi_ref[...], bi_ref[...])
    mix = (jnp.dot(oa_ref[0], wa_ref[...], preferred_element_type=F32)
           + jnp.dot(ob_ref[0], wb_ref[...], preferred_element_type=F32))
    h1 = _layer_norm(DEEPNORM_ALPHA * h + mix, g1_ref[...], b1_ref[...])
    nchunk = h1.shape[1] // LANES
    for c in range(nchunk):
        h1_ref[pl.ds(c, ROW_TILE, stride=nchunk), :] = h1[:, c * LANES:(c + 1) * LANES]
    lt = lax.dot_general(rwt_ref[...], h1, (((1,), (1,)), ((), ())),
                         precision=lax.Precision.HIGHEST, preferred_element_type=F32)
    lt = lt + rb_ref[...]
    eidx = lax.broadcasted_iota(jnp.int32, lt.shape, 0)
    vals, idxs, hots = [], [], []
    for _ in range(TOP_K):
        mx = jnp.max(lt, axis=0, keepdims=True)
        am = jnp.min(jnp.where(lt == mx, eidx, N_EXPERTS), axis=0, keepdims=True)
        hit = eidx == am
        vals.append(mx)
        idxs.append(am)
        hots.append(jnp.where(hit, 1.0, 0.0))
        lt = jnp.where(hit, -jnp.inf, lt)
    ex = [jnp.exp(v - vals[0]) for v in vals]
    den = ex[0] + ex[1] + ex[2] + ex[3]
    idx_ref[...] = jnp.concatenate(idxs, axis=0)
    gate_ref[...] = jnp.concatenate([e / den for e in ex], axis=0)

    @pl.when(pl.program_id(0) % steps_per_tile == 0)
    def _():
        carry_ref[...] = jnp.zeros_like(carry_ref)

    hot = (hots[0] + hots[1]) + (hots[2] + hots[3])
    before = jnp.dot(hot.astype(BF16), tri_ref[...], preferred_element_type=F32) + carry_ref[...]
    rank_ref[...] = jnp.concatenate(
        [jnp.sum(hk * before, axis=0, keepdims=True) for hk in hots], axis=0).astype(jnp.int32)
    total = carry_ref[...] + jnp.sum(hot, axis=1, keepdims=True)
    carry_ref[...] = total
    cnt_ref[0] = total.astype(jnp.int32)


def _out_router(x2, gi, bi, o_a, o_b, wa, wb, g1, b1, rwt, rb, moe_tile):
    t, d = x2.shape
    nchunk = d // LANES
    row = lambda i: (i, 0)
    fixed = lambda i: (0, 0)
    steps_per_seq = o_a.shape[1] // ROW_TILE
    seq_row = lambda i: (i // steps_per_seq, i % steps_per_seq, 0)
    tri = jnp.triu(jnp.ones((ROW_TILE, ROW_TILE), BF16), k=1)
    return pl.pallas_call(
        functools.partial(_out_router_kernel, steps_per_tile=moe_tile // ROW_TILE),
        grid=(t // ROW_TILE,),
        in_specs=[
            pl.BlockSpec((ROW_TILE, d), row),
            pl.BlockSpec((1, d), fixed),
            pl.BlockSpec((1, d), fixed),
            pl.BlockSpec((1, ROW_TILE, A_WIDTH), seq_row),
            pl.BlockSpec((1, ROW_TILE, B_WIDTH), seq_row),
            pl.BlockSpec((A_WIDTH, d), fixed),
            pl.BlockSpec((B_WIDTH, d), fixed),
            pl.BlockSpec((1, d), fixed),
            pl.BlockSpec((1, d), fixed),
            pl.BlockSpec((N_EXPERTS, d), fixed),
            pl.BlockSpec((N_EXPERTS, 1), fixed),
            pl.BlockSpec((ROW_TILE, ROW_TILE), fixed),
        ],
        out_specs=[
            pl.BlockSpec((ROW_TILE * nchunk, LANES), row),
            pl.BlockSpec((TOP_K, ROW_TILE), lambda i: (0, i)),
            pl.BlockSpec((TOP_K, ROW_TILE), lambda i: (0, i)),
            pl.BlockSpec((TOP_K, ROW_TILE), lambda i: (0, i)),
            pl.BlockSpec((1, N_EXPERTS, 1), lambda i: (i, 0, 0)),
        ],
        out_shape=[
            jax.ShapeDtypeStruct((t * nchunk, LANES), F32),
            jax.ShapeDtypeStruct((TOP_K, t), jnp.int32),
            jax.ShapeDtypeStruct((TOP_K, t), F32),
            jax.ShapeDtypeStruct((TOP_K, t), jnp.int32),
            jax.ShapeDtypeStruct((t // ROW_TILE, N_EXPERTS, 1), jnp.int32),
        ],
        scratch_shapes=[pltpu.VMEM((N_EXPERTS, 1), F32)],
        compiler_params=pltpu.CompilerParams(
            dimension_semantics=("arbitrary",), vmem_limit_bytes=VMEM_LIMIT),
        name="out_router",
    )(x2, gi, bi, o_a, o_b, wa, wb, g1, b1, rwt, rb, tri)


def _split_w1_kernel(w_ref, perm_ref, g_ref, l_ref):
    rows = w_ref.shape[1]
    even = (lax.broadcasted_iota(jnp.int32, (rows, LANES), 1) & 1) == 0
    perm = perm_ref[...]
    for g in range(w_ref.shape[2] // (2 * LANES)):
        v0 = w_ref[0, :, 2 * g * LANES:(2 * g + 1) * LANES]
        v1 = w_ref[0, :, (2 * g + 1) * LANES:(2 * g + 2) * LANES]
        glu = jnp.where(even, v0, pltpu.roll(v1, 1, axis=1)).astype(BF16)
        lin = jnp.where(even, pltpu.roll(v0, LANES - 1, axis=1), v1).astype(BF16)
        g_ref[0, :, g * LANES:(g + 1) * LANES] = jnp.dot(
            glu, perm, preferred_element_type=F32).astype(BF16)
        l_ref[0, :, g * LANES:(g + 1) * LANES] = jnp.dot(
            lin, perm, preferred_element_type=F32).astype(BF16)


def _split_w1(w1e):
    e, d, f2 = w1e.shape
    rows = 256
    half = LANES // 2
    unit = jnp.arange(LANES)
    perm = (jnp.arange(LANES)[:, None] == (2 * (unit % half) + unit // half)[None, :]).astype(BF16)
    return pl.pallas_call(
        _split_w1_kernel,
        grid=(e, d // rows),
        in_specs=[pl.BlockSpec((1, rows, f2), lambda i, j: (i, j, 0)),
                  pl.BlockSpec((LANES, LANES), lambda i, j: (0, 0))],
        out_specs=[pl.BlockSpec((1, rows, f2 // 2), lambda i, j: (i, j, 0)),
                   pl.BlockSpec((1, rows, f2 // 2), lambda i, j: (i, j, 0))],
        out_shape=[jax.ShapeDtypeStruct((e, d, f2 // 2), BF16),
                   jax.ShapeDtypeStruct((e, d, f2 // 2), BF16)],
        compiler_params=pltpu.CompilerParams(
            dimension_semantics=("arbitrary", "arbitrary"), vmem_limit_bytes=VMEM_LIMIT),
        name="split_w1",
    )(w1e, perm)


def _moe_kernel(cnt_ref, offs_ref,
                pos_ref, gate_ref,
                w1g_ref, w1l_ref, b1g_ref, b1l_ref, w2_ref, b2_ref, h_hbm,
                y_hbm, hbuf, ybuf, stage, list_tok, list_gate, sem, *, tile):
    b = pl.program_id(0)
    e = pl.program_id(1)
    n_exp = pl.num_programs(1)
    d_model = w1g_ref.shape[1]
    d_ff = w1g_ref.shape[2]
    nchunk = d_model // LANES
    tile_rows = tile * nchunk
    pairs = tile * TOP_K

    def stage_rows(r):
        return pl.ds(pl.multiple_of(r * nchunk, nchunk), nchunk)

    def listed_rows(p):
        return pl.ds(pl.multiple_of(list_tok[p], nchunk), nchunk)

    @pl.when(jnp.logical_and(b == 0, e == 0))
    def _():
        stage[...] = jnp.zeros_like(stage)

        def pad_body(j, c):
            list_tok[pairs + j] = 0
            return c
        lax.fori_loop(0, LIST_PAD, pad_body, 0)

    @pl.when(e == 0)
    def _():
        src = h_hbm.at[pl.ds(pl.multiple_of(b * tile_rows, tile_rows), tile_rows), :]
        load = pltpu.make_async_copy(src, hbuf, sem.at[0])
        load.start()
        ybuf[...] = jnp.zeros_like(ybuf)

        def sort_body(jo, c):
            for u in range(SORT_UNROLL):
                j = jo * SORT_UNROLL + u
                p = pos_ref[j]
                list_tok[p] = (j & (tile - 1)) * nchunk
                list_gate[p] = gate_ref[j]
            return c
        lax.fori_loop(0, pairs // SORT_UNROLL, sort_body, 0)
        load.wait()

    n = cnt_ref[b * n_exp + e]
    base = offs_ref[b * n_exp + e]

    def run_block(m_rows, p0, nrows):
        def gather_body(ci, c):
            for u in range(ROW_UNROLL):
                r = ci * ROW_UNROLL + u
                stage[stage_rows(r), :] = hbuf[listed_rows(p0 + r), :]
            return c
        lax.fori_loop(0, (nrows + ROW_UNROLL - 1) // ROW_UNROLL, gather_body, 0)

        x = jnp.concatenate(
            [stage[pl.ds(c, m_rows, stride=nchunk), :] for c in range(nchunk)],
            axis=1).astype(BF16)
        y = b2_ref[0]
        for hh in range(d_ff // FF_CHUNK):
            sl = slice(hh * FF_CHUNK, (hh + 1) * FF_CHUNK)
            hg = jnp.dot(x, w1g_ref[0, :, sl], preferred_element_type=F32) + b1g_ref[0, :, sl]
            hl = jnp.dot(x, w1l_ref[0, :, sl], preferred_element_type=F32) + b1l_ref[0, :, sl]
            xg = jnp.minimum(hg, SWIGLU_LIMIT)
            xl = jnp.clip(hl, -SWIGLU_LIMIT, SWIGLU_LIMIT)
            act = xg * jax.nn.sigmoid(SWIGLU_ALPHA * xg) * (xl + 1.0)
            y = y + jnp.dot(act.astype(BF16), w2_ref[0, sl, :], preferred_element_type=F32)
        for c in range(nchunk):
            stage[pl.ds(c, m_rows, stride=nchunk), :] = y[:, c * LANES:(c + 1) * LANES]

        def add_body(ci, c):
            rows, vals = [], []
            for u in range(SUBLANES):
                r = ci * SUBLANES + u
                dst = listed_rows(p0 + r)
                rows.append(dst)
                vals.append(ybuf[dst, :] + list_gate[p0 + r] * stage[stage_rows(r), :])
            for dst, val in zip(rows, vals):
                ybuf[dst, :] = val
            return c
        n_full = nrows // SUBLANES
        lax.fori_loop(0, n_full, add_body, 0)

        def add_tail(r, c):
            dst = listed_rows(p0 + r)
            ybuf[dst, :] = ybuf[dst, :] + list_gate[p0 + r] * stage[stage_rows(r), :]
            return c
        lax.fori_loop(n_full * SUBLANES, nrows, add_tail, 0)

    n_big = n // MOE_BLOCK
    rem = n - n_big * MOE_BLOCK
    rem_is_big = rem > MOE_BLOCK - SMALL_BLOCK
    n_big_blocks = n_big + rem_is_big.astype(jnp.int32)
    n_small_blocks = jnp.where(rem_is_big, 0, (rem + SMALL_BLOCK - 1) // SMALL_BLOCK)

    def big_body(s, c):
        run_block(MOE_BLOCK, base + s * MOE_BLOCK, jnp.minimum(MOE_BLOCK, n - s * MOE_BLOCK))
        return c
    lax.fori_loop(0, n_big_blocks, big_body, 0)

    def small_body(s, c):
        run_block(SMALL_BLOCK, base + n_big * MOE_BLOCK + s * SMALL_BLOCK,
                  jnp.minimum(SMALL_BLOCK, rem - s * SMALL_BLOCK))
        return c
    lax.fori_loop(0, n_small_blocks, small_body, 0)

    @pl.when(e == n_exp - 1)
    def _():
        dst = y_hbm.at[pl.ds(pl.multiple_of(b * tile_rows, tile_rows), tile_rows), :]
        store = pltpu.make_async_copy(ybuf, dst, sem.at[1])
        store.start()
        store.wait()


def _moe(h1t, cnt, offs, pos_flat, gate_flat, w1g, w1l, b1g, b1l, w2, b2, tile):
    n_exp, d, f = w1g.shape
    nchunk = d // LANES
    n_tiles = h1t.shape[0] // (tile * nchunk)
    pairs = tile * TOP_K
    smem_vec = pl.BlockSpec((pairs,), lambda b, e, cnt, offs: (b,), memory_space=pltpu.SMEM)
    expert = lambda b, e, cnt, offs: (e, 0, 0)
    grid_spec = pltpu.PrefetchScalarGridSpec(
        num_scalar_prefetch=2,
        grid=(n_tiles, n_exp),
        in_specs=[
            smem_vec, smem_vec,
            pl.BlockSpec((1, d, f), expert),
            pl.BlockSpec((1, d, f), expert),
            pl.BlockSpec((1, 1, f), expert),
            pl.BlockSpec((1, 1, f), expert),
            pl.BlockSpec((1, f, d), expert),
            pl.BlockSpec((1, 1, d), expert),
            pl.BlockSpec(memory_space=pl.ANY),
        ],
        out_specs=pl.BlockSpec(memory_space=pl.ANY),
        scratch_shapes=[
            pltpu.VMEM((tile * nchunk, LANES), F32),
            pltpu.VMEM((tile * nchunk, LANES), F32),
            pltpu.VMEM((MOE_BLOCK * nchunk, LANES), F32),
            pltpu.SMEM((pairs + LIST_PAD,), jnp.int32),
            pltpu.SMEM((pairs + LIST_PAD,), F32),
            pltpu.SemaphoreType.DMA((2,)),
        ],
    )
    return pl.pallas_call(
        functools.partial(_moe_kernel, tile=tile),
        grid_spec=grid_spec,
        out_shape=jax.ShapeDtypeStruct(h1t.shape, F32),
        compiler_params=pltpu.CompilerParams(
            dimension_semantics=("arbitrary", "arbitrary"), vmem_limit_bytes=MOE_VMEM_LIMIT),
        name="moe",
    )(cnt, offs, pos_flat, gate_flat, w1g, w1l, b1g, b1l, w2, b2, h1t)


def _combine_kernel(h1_ref, y_ref, g_ref, b_ref, o_ref):
    nchunk = o_ref.shape[1] // LANES
    pieces = [DEEPNORM_ALPHA * h1_ref[pl.ds(c, ROW_TILE, stride=nchunk), :]
              + y_ref[pl.ds(c, ROW_TILE, stride=nchunk), :] for c in range(nchunk)]
    o_ref[...] = _layer_norm(jnp.concatenate(pieces, axis=1), g_ref[...], b_ref[...])


def _combine(h1t, yt, g2, b2):
    d = g2.shape[1]
    nchunk = d // LANES
    t = h1t.shape[0] // nchunk
    tiles = pl.BlockSpec((ROW_TILE * nchunk, LANES), lambda i: (i, 0))
    return pl.pallas_call(
        _combine_kernel,
        grid=(t // ROW_TILE,),
        in_specs=[tiles, tiles,
                  pl.BlockSpec((1, d), lambda i: (0, 0)),
                  pl.BlockSpec((1, d), lambda i: (0, 0))],
        out_specs=pl.BlockSpec((ROW_TILE, d), lambda i: (i, 0)),
        out_shape=jax.ShapeDtypeStruct((t, d), F32),
        compiler_params=pltpu.CompilerParams(
            dimension_semantics=("arbitrary",), vmem_limit_bytes=VMEM_LIMIT),
        name="combine",
    )(h1t, yt, g2, b2)


def kernel(x, ln_in_g, ln_in_b, w_in, lambda_q1, lambda_k1, lambda_q2, lambda_k2, subln_g, rel_bias,
           w_out, ln1_g, ln1_b, router_w, router_b, w1, b1, w2, b2, ln2_g, ln2_b):
    bsz, seq, d = x.shape
    t = bsz * seq
    x2 = x.reshape(t, d)
    row = lambda v: v.reshape(1, -1).astype(F32)

    qs = A_HEAD_DIM ** -0.5 * LOG2E
    col_scale = jnp.concatenate([
        jnp.full((A_WIDTH,), qs, F32), jnp.ones((2 * A_WIDTH,), F32),
        jnp.full((B_WIDTH,), B_HEAD_DIM ** -0.5 * LOG2E, F32), jnp.ones((2 * B_WIDTH,), F32)]).reshape(1, -1)
    proj3, vt3 = _ln_qkv(x2, row(ln_in_g), row(ln_in_b), w_in[0].astype(BF16), col_scale, seq)

    lam4 = jnp.stack([lambda_q1[0], lambda_k1[0], lambda_q2[0], lambda_k2[0]]).astype(F32)
    o_a = _diff_attn(proj3, vt3, lam4, subln_g[0].astype(F32))
    o_b = _band_attn(proj3, rel_bias[0], tq=128)

    w_o = w_out[0].astype(BF16)
    tile = min(MOE_TILE, t)
    n_tiles = t // tile
    h1t, top_idx, gates, rank, run_cnt = _out_router(
        x2, row(ln_in_g), row(ln_in_b), o_a, o_b,
        w_o[:A_WIDTH], w_o[A_WIDTH:], row(ln1_g[0]), row(ln1_b[0]),
        router_w[0].T.astype(F32), router_b[0].reshape(-1, 1).astype(F32), tile)

    steps = tile // ROW_TILE
    cnt = run_cnt[steps - 1::steps, :, 0]
    offs = jnp.cumsum(cnt, axis=1) - cnt
    hot = top_idx.reshape(TOP_K, n_tiles, tile, 1) == jnp.arange(N_EXPERTS, dtype=jnp.int32)
    pos = rank + jnp.sum(jnp.where(hot, offs[None, :, None, :], 0), axis=-1).reshape(TOP_K, t)
    per_tile = lambda a: a.reshape(TOP_K, n_tiles, tile).transpose(1, 0, 2).reshape(-1)
    w1g, w1l = _split_w1(w1[0])
    b1e = b1[0].astype(F32)[:, None, :]
    yt = _moe(h1t, cnt.reshape(-1), offs.reshape(-1), per_tile(pos), per_tile(gates),
              w1g, w1l, b1e[:, :, 0::2], b1e[:, :, 1::2],
              w2[0].astype(BF16), b2[0][:, None, :].astype(F32), tile)
    out = _combine(h1t, yt, row(ln2_g[0]), row(ln2_b[0]))
    return out.reshape(bsz, seq, d)
```

```python
import functools
import math

import jax
import jax.numpy as jnp
from jax import lax
from jax.experimental import pallas as pl
from jax.experimental.pallas import tpu as pltpu

F32 = jnp.float32
BF16 = jnp.bfloat16

CHUNK = 64
A_HEADS = 4
A_HEAD_DIM = 64
A_WIDTH = A_HEADS * 2 * A_HEAD_DIM
B_HEADS = 8
B_HEAD_DIM = 64
B_WIDTH = B_HEADS * B_HEAD_DIM
B_PAST_CHUNKS = 8
REL_CLIP = 256
N_EXPERTS = 32
TOP_K = 4
SWIGLU_ALPHA = 1.702
SWIGLU_LIMIT = 7.0
MOE_BLOCK = 512
LN_EPS = 1e-5
RMS_EPS = 1e-5
DEPTH = 1
DEEPNORM_ALPHA = (2 * DEPTH) ** 0.25
LAM_INIT = 0.8 - 0.6 * math.exp(-0.3 * 0)

LOG2E = 1.4426950408889634
NEG = -1e30
LANES = 128
SUBLANES = 8
ROW_TILE = 512
VMEM_LIMIT = 48 * 1024 * 1024
DIFF_TILE = 256
BAND_TILE = 128
BAND_TILES = (BAND_TILE + B_PAST_CHUNKS * CHUNK) // BAND_TILE
BAND_PAIR = 2
BAND_STEP_ROWS = 1024
MOE_TILE = 4096
MOE_VMEM_LIMIT = 60 * 1024 * 1024
SMALL_BLOCK = 128
FF_CHUNK = 512
SORT_UNROLL = 8
ROW_UNROLL = 16
LIST_PAD = 128


def _layer_norm(x, g, b):
    mu = jnp.mean(x, axis=-1, keepdims=True)
    xc = x - mu
    var = jnp.mean(xc * xc, axis=-1, keepdims=True)
    return xc * lax.rsqrt(var + LN_EPS) * g + b


def _ln_qkv_kernel(x_ref, g_ref, b_ref, w_ref, cs_ref, o_ref, vta_ref, vtb_ref):
    h = _layer_norm(x_ref[...], g_ref[...], b_ref[...])
    hb = h.astype(BF16)
    n_out = w_ref.shape[1]
    for c in range(n_out // ROW_TILE):
        sl = slice(c * ROW_TILE, (c + 1) * ROW_TILE)
        val = jnp.dot(hb, w_ref[:, sl], preferred_element_type=F32) * cs_ref[:, sl]
        o_ref[0, :, sl] = val.astype(BF16)
        for start, vt_ref in ((2 * A_WIDTH, vta_ref), (3 * A_WIDTH + 2 * B_WIDTH, vtb_ref)):
            if sl.start == start:
                tile = vt_ref.shape[2]
                for kt in range(ROW_TILE // tile):
                    vt_ref[kt] = val[kt * tile:(kt + 1) * tile, :].T.astype(BF16)


def _ln_qkv(x2, g, b, w_bf, col_scale, seq):
    assert A_WIDTH == ROW_TILE and B_WIDTH == ROW_TILE
    t, d = x2.shape
    n_out = w_bf.shape[1]
    steps_per_seq = seq // ROW_TILE
    vt_spec = lambda tile: pl.BlockSpec((ROW_TILE // tile, ROW_TILE, tile), lambda i: (i, 0, 0))
    vt_shape = lambda tile: jax.ShapeDtypeStruct((t // tile, ROW_TILE, tile), BF16)
    return pl.pallas_call(
        _ln_qkv_kernel,
        grid=(t // ROW_TILE,),
        in_specs=[
            pl.BlockSpec((ROW_TILE, d), lambda i: (i, 0)),
            pl.BlockSpec((1, d), lambda i: (0, 0)),
            pl.BlockSpec((1, d), lambda i: (0, 0)),
            pl.BlockSpec((d, n_out), lambda i: (0, 0)),
            pl.BlockSpec((1, n_out), lambda i: (0, 0)),
        ],
        out_specs=[pl.BlockSpec((1, ROW_TILE, n_out),
                                lambda i: (i // steps_per_seq, i % steps_per_seq, 0)),
                   vt_spec(DIFF_TILE), vt_spec(BAND_TILE)],
        out_shape=[jax.ShapeDtypeStruct((t // seq, seq, n_out), BF16),
                   vt_shape(DIFF_TILE), vt_shape(BAND_TILE)],
        compiler_params=pltpu.CompilerParams(
            dimension_semantics=("arbitrary",), vmem_limit_bytes=VMEM_LIMIT),
        name="ln_qkv",
    )(x2, g, b, w_bf, col_scale)


def _diff_attn_kernel(c_ref, lam_ref, g_ref, boff_ref, bdiag_ref, q_ref, k_ref, vt_ref, o_ref, s_scr):
    tq = q_ref.shape[1]
    tk = vt_ref.shape[2]
    h = pl.program_id(1)
    i = pl.program_id(2)
    c = c_ref[h]
    q = q_ref[0]
    lane = lax.broadcasted_iota(jnp.int32, q.shape, 1)
    zero = jnp.zeros_like(q)
    qq = jnp.concatenate([jnp.where(lane < A_HEAD_DIM, q, zero),
                          jnp.where(lane >= A_HEAD_DIM, q, zero)], axis=0)
    qqt = qq.astype(F32).T.astype(BF16)
    n_before = i * (tq // tk)

    def scores(j):
        kb = k_ref[0, pl.ds(pl.multiple_of(j * tk, tk), tk), :]
        return jnp.dot(kb, qqt, preferred_element_type=F32)

    def update(slot, table_ref, j, carry):
        m, l, acc = carry
        s = s_scr[slot] + table_ref[0]
        shift = c * (j * tk).astype(F32)
        m_new = jnp.maximum(m, jnp.max(s, axis=0, keepdims=True) + shift)
        alpha = jnp.exp2(m - m_new)
        p = jnp.exp2(s - (m_new - shift))
        l = alpha * l + jnp.sum(p, axis=0, keepdims=True)
        acc = alpha * acc + jnp.dot(vt_ref[j], p.astype(BF16), preferred_element_type=F32)
        return m_new, l, acc

    s_scr[0] = scores(0)

    def pair(t, carry):
        j = 2 * t
        s_scr[1] = scores(j + 1)
        carry = update(0, boff_ref, j, carry)
        s_scr[0] = scores(j + 2)
        return update(1, boff_ref, j + 1, carry)

    init = (jnp.full((1, 2 * tq), NEG, F32), jnp.zeros((1, 2 * tq), F32),
            jnp.zeros((LANES, 2 * tq), F32))
    carry = lax.fori_loop(0, n_before // 2, pair, init)

    def odd_tail(carry):
        s_scr[1] = scores(n_before)
        carry = update(0, boff_ref, n_before - 1, carry)
        return update(1, bdiag_ref, n_before, carry)

    def even_tail(carry):
        return update(0, bdiag_ref, n_before, carry)

    _, l, acc = lax.cond(n_before % 2 == 1, odd_tail, even_tail, carry)

    o_all = acc / l
    lv = lam_ref[...]
    lam = (jnp.exp(jnp.sum(lv[0:1] * lv[1:2], axis=1, keepdims=True))
           - jnp.exp(jnp.sum(lv[2:3] * lv[3:4], axis=1, keepdims=True)) + LAM_INIT)
    o = o_all[:, :tq] - lam * o_all[:, tq:]
    ms = jnp.mean(o * o, axis=0, keepdims=True)
    o = o * lax.rsqrt(ms + RMS_EPS) * (g_ref[...] * (1.0 - LAM_INIT))
    o_ref[0] = o.T.astype(BF16)


def _diff_attn(proj3, vt3, lam4, subln_g):
    bsz, seq, _ = proj3.shape
    tq = tk = DIFF_TILE
    c = jnp.asarray([2.0 ** (-8.0 * (h + 1) / A_HEADS) for h in range(A_HEADS)], F32) * LOG2E
    r = jnp.arange(tk, dtype=jnp.int32)[:, None]
    qrel = jnp.arange(2 * tq, dtype=jnp.int32)[None, :] % tq
    boff = c[:, None, None] * jnp.broadcast_to(r, (tk, 2 * tq)).astype(F32)
    allowed = (r // CHUNK) <= (qrel // CHUNK)
    bdiag = jnp.where(allowed, c[:, None, None] * (qrel - jnp.abs(qrel - r)).astype(F32), NEG)
    kblk = A_WIDTH // LANES
    return pl.pallas_call(
        _diff_attn_kernel,
        grid=(bsz, A_HEADS, seq // tq),
        in_specs=[
            pl.BlockSpec(memory_space=pltpu.SMEM),
            pl.BlockSpec((4, A_HEAD_DIM), lambda b, h, i: (0, 0)),
            pl.BlockSpec((LANES, 1), lambda b, h, i: (0, 0)),
            pl.BlockSpec((1, tk, 2 * tq), lambda b, h, i: (h, 0, 0)),
            pl.BlockSpec((1, tk, 2 * tq), lambda b, h, i: (h, 0, 0)),
            pl.BlockSpec((1, tq, LANES), lambda b, h, i: (b, i, h)),
            pl.BlockSpec((1, seq, LANES), lambda b, h, i: (b, 0, kblk + h)),
            pl.BlockSpec((seq // tk, LANES, tk), lambda b, h, i: (b, h, 0)),
        ],
        out_specs=pl.BlockSpec((1, tq, LANES), lambda b, h, i: (b, i, h)),
        out_shape=jax.ShapeDtypeStruct((bsz, seq, A_WIDTH), BF16),
        scratch_shapes=[pltpu.VMEM((2, tk, 2 * tq), F32)],
        compiler_params=pltpu.CompilerParams(
            dimension_semantics=("arbitrary", "arbitrary", "arbitrary"),
            vmem_limit_bytes=VMEM_LIMIT),
        name="diff_attn",
    )(c, lam4, subln_g.reshape(LANES, 1), boff, bdiag, proj3, proj3, vt3)


def _band_attn_kernel(bias_ref, q_ref, k_ref, vt_ref, o_ref, s_scr):
    i = pl.program_id(2)
    tq = BAND_TILE
    units = q_ref.shape[1] // (BAND_PAIR * tq)
    key_tiles = BAND_TILES + BAND_PAIR - 1
    row = lax.broadcasted_iota(jnp.int32, (LANES, tq), 0)

    def key_tile(n, t):
        jt = (i * units + n) * BAND_PAIR - (BAND_TILES - 1) + t
        return jnp.maximum(jt, 0), jt < 0

    def scores(n):
        qs = []
        for a in range(BAND_PAIR):
            q = q_ref[0, (n * BAND_PAIR + a) * tq:(n * BAND_PAIR + a + 1) * tq, :]
            lane = lax.broadcasted_iota(jnp.int32, q.shape, 1)
            zero = jnp.zeros_like(q)
            qs += [jnp.where(lane < B_HEAD_DIM, q, zero), jnp.where(lane >= B_HEAD_DIM, q, zero)]
        qqt = jnp.concatenate(qs, axis=0).astype(F32).T.astype(BF16)
        ks = [k_ref[0, pl.ds(pl.multiple_of(key_tile(n, t)[0] * tq, tq), tq), :] for t in range(key_tiles)]
        return jnp.dot(jnp.concatenate(ks, axis=0), qqt, preferred_element_type=F32)

    def finish(n):
        bias = [bias_ref[0, jnp.where(key_tile(n, t)[1], 1, 0), t] for t in range(key_tiles)]
        s = s_scr[n % 2] + jnp.concatenate(bias, axis=0)
        m = jnp.max(s, axis=0, keepdims=True)
        p = jnp.exp2(s - m)
        l = jnp.sum(p, axis=0, keepdims=True)
        vt = jnp.concatenate([vt_ref[key_tile(n, t)[0]] for t in range(key_tiles)], axis=1)
        o = jnp.dot(vt, p.astype(BF16), preferred_element_type=F32) / l
        for a in range(BAND_PAIR):
            oa = jnp.where(row < B_HEAD_DIM, o[:, 2 * a * tq:(2 * a + 1) * tq],
                           o[:, (2 * a + 1) * tq:(2 * a + 2) * tq])
            o_ref[0, (n * BAND_PAIR + a) * tq:(n * BAND_PAIR + a + 1) * tq, :] = oa.T.astype(BF16)

    s_scr[0] = scores(0)
    for n in range(units):
        if n + 1 < units:
            s_scr[(n + 1) % 2] = scores(n + 1)
        finish(n)


def _band_bias(rel_bias, tq):
    past = B_PAST_CHUNKS * CHUNK
    band = tq + past
    assert tq - 1 <= REL_CLIP <= past
    qi = jnp.arange(tq)
    kj = jnp.arange(band)
    cq = qi[:, None] // CHUNK
    ck = kj[None, :] // CHUNK
    allowed = (ck >= cq) & (ck <= cq + B_PAST_CHUNKS)
    tab = rel_bias.astype(F32) * LOG2E
    n_diag = band + tq - 1
    n_unclipped = REL_CLIP + tq
    w = jnp.concatenate([tab[:, REL_CLIP - (tq - 1):],
                         jnp.broadcast_to(tab[:, -1:], (B_HEADS, n_diag - n_unclipped))], axis=1)
    shifted = jnp.tile(w, (1, tq + 1))[:, :tq * (n_diag + 1)].reshape(B_HEADS, tq, n_diag + 1)
    bias = jnp.flip(shifted[:, :, :band], axis=2)
    return jnp.where(allowed[None], bias, NEG)


def _band_attn(proj3, vtb3, rel_bias):
    bsz, seq, _ = proj3.shape
    tq = BAND_TILE
    groups = B_HEADS // 2
    key_tiles = BAND_TILES + BAND_PAIR - 1
    cols = BAND_PAIR * 2 * tq
    bias = _band_bias(rel_bias, tq)
    bias = bias.reshape(groups, 2, tq, BAND_TILES, tq)
    masked = jnp.full((groups, 2, tq, 1, tq), NEG, F32)
    per_tile = [jnp.concatenate([masked] * a + [bias] + [masked] * (BAND_PAIR - 1 - a), axis=3)
                for a in range(BAND_PAIR)]
    bias = jnp.stack(per_tile, axis=1)
    bias = bias.transpose(0, 4, 5, 1, 2, 3).reshape(groups, 1, key_tiles, tq, cols)
    bias = jnp.concatenate([bias, jnp.full_like(bias, NEG)], axis=1)
    qblk = 3 * A_WIDTH // LANES
    kblk = qblk + B_WIDTH // LANES
    rows = min(BAND_STEP_ROWS, seq)
    return pl.pallas_call(
        _band_attn_kernel,
        grid=(bsz, groups, seq // rows),
        in_specs=[
            pl.BlockSpec((1, 2, key_tiles, tq, cols), lambda b, g, i: (g, 0, 0, 0, 0)),
            pl.BlockSpec((1, rows, LANES), lambda b, g, i: (b, i, qblk + g)),
            pl.BlockSpec((1, seq, LANES), lambda b, g, i: (b, 0, kblk + g)),
            pl.BlockSpec((seq // tq, LANES, tq), lambda b, g, i: (b, g, 0)),
        ],
        out_specs=pl.BlockSpec((1, rows, LANES), lambda b, g, i: (b, i, g)),
        out_shape=jax.ShapeDtypeStruct((bsz, seq, B_WIDTH), BF16),
        scratch_shapes=[pltpu.VMEM((2, key_tiles * tq, cols), F32)],
        compiler_params=pltpu.CompilerParams(
            dimension_semantics=("arbitrary", "arbitrary", "arbitrary"),
            vmem_limit_bytes=VMEM_LIMIT),
        name="band_attn",
    )(bias, proj3, proj3, vtb3)


def _out_router_kernel(x_ref, gi_ref, bi_ref, oa_ref, ob_ref, wa_ref, wb_ref, g1_ref, b1_ref,
                       rwt_ref, rb_ref, tri_ref, h1_ref, idx_ref, gate_ref, rank_ref, cnt_ref,
                       carry_ref, *, steps_per_tile):
    h = _layer_norm(x_ref[...], gi_ref[...], bi_ref[...])
    mix = (jnp.dot(oa_ref[0], wa_ref[...], preferred_element_type=F32)
           + jnp.dot(ob_ref[0], wb_ref[...], preferred_element_type=F32))
    h1 = _layer_norm(DEEPNORM_ALPHA * h + mix, g1_ref[...], b1_ref[...])
    nchunk = h1.shape[1] // LANES
    for c in range(nchunk):
        h1_ref[pl.ds(c, ROW_TILE, stride=nchunk), :] = h1[:, c * LANES:(c + 1) * LANES]
    lt = lax.dot_general(rwt_ref[...], h1, (((1,), (1,)), ((), ())),
                         precision=lax.Precision.HIGHEST, preferred_element_type=F32)
    lt = lt + rb_ref[...]
    eidx = lax.broadcasted_iota(jnp.int32, lt.shape, 0)
    vals, idxs, hots = [], [], []
    for _ in range(TOP_K):
        mx = jnp.max(lt, axis=0, keepdims=True)
        am = jnp.min(jnp.where(lt == mx, eidx, N_EXPERTS), axis=0, keepdims=True)
        hit = eidx == am
        vals.append(mx)
        idxs.append(am)
        hots.append(jnp.where(hit, 1.0, 0.0))
        lt = jnp.where(hit, -jnp.inf, lt)
    ex = [jnp.exp(v - vals[0]) for v in vals]
    den = ex[0] + ex[1] + ex[2] + ex[3]
    idx_ref[...] = jnp.concatenate(idxs, axis=0)
    gate_ref[...] = jnp.concatenate([e / den for e in ex], axis=0)

    @pl.when(pl.program_id(0) % steps_per_tile == 0)
    def _():
        carry_ref[...] = jnp.zeros_like(carry_ref)

    hot = (hots[0] + hots[1]) + (hots[2] + hots[3])
    before = jnp.dot(hot.astype(BF16), tri_ref[...], preferred_element_type=F32) + carry_ref[...]
    rank_ref[...] = jnp.concatenate(
        [jnp.sum(hk * before, axis=0, keepdims=True) for hk in hots], axis=0).astype(jnp.int32)
    total = carry_ref[...] + jnp.sum(hot, axis=1, keepdims=True)
    carry_ref[...] = total
    cnt_ref[0] = total.astype(jnp.int32)


def _out_router(x2, gi, bi, o_a, o_b, wa, wb, g1, b1, rwt, rb, moe_tile):
    t, d = x2.shape
    nchunk = d // LANES
    row = lambda i: (i, 0)
    fixed = lambda i: (0, 0)
    steps_per_seq = o_a.shape[1] // ROW_TILE
    seq_row = lambda i: (i // steps_per_seq, i % steps_per_seq, 0)
    tri = jnp.triu(jnp.ones((ROW_TILE, ROW_TILE), BF16), k=1)
    return pl.pallas_call(
        functools.partial(_out_router_kernel, steps_per_tile=moe_tile // ROW_TILE),
        grid=(t // ROW_TILE,),
        in_specs=[
            pl.BlockSpec((ROW_TILE, d), row),
            pl.BlockSpec((1, d), fixed),
            pl.BlockSpec((1, d), fixed),
            pl.BlockSpec((1, ROW_TILE, A_WIDTH), seq_row),
            pl.BlockSpec((1, ROW_TILE, B_WIDTH), seq_row),
            pl.BlockSpec((A_WIDTH, d), fixed),
            pl.BlockSpec((B_WIDTH, d), fixed),
            pl.BlockSpec((1, d), fixed),
            pl.BlockSpec((1, d), fixed),
            pl.BlockSpec((N_EXPERTS, d), fixed),
            pl.BlockSpec((N_EXPERTS, 1), fixed),
            pl.BlockSpec((ROW_TILE, ROW_TILE), fixed),
        ],
        out_specs=[
            pl.BlockSpec((ROW_TILE * nchunk, LANES), row),
            pl.BlockSpec((TOP_K, ROW_TILE), lambda i: (0, i)),
            pl.BlockSpec((TOP_K, ROW_TILE), lambda i: (0, i)),
            pl.BlockSpec((TOP_K, ROW_TILE), lambda i: (0, i)),
            pl.BlockSpec((1, N_EXPERTS, 1), lambda i: (i, 0, 0)),
        ],
        out_shape=[
            jax.ShapeDtypeStruct((t * nchunk, LANES), F32),
            jax.ShapeDtypeStruct((TOP_K, t), jnp.int32),
            jax.ShapeDtypeStruct((TOP_K, t), F32),
            jax.ShapeDtypeStruct((TOP_K, t), jnp.int32),
            jax.ShapeDtypeStruct((t // ROW_TILE, N_EXPERTS, 1), jnp.int32),
        ],
        scratch_shapes=[pltpu.VMEM((N_EXPERTS, 1), F32)],
        compiler_params=pltpu.CompilerParams(
            dimension_semantics=("arbitrary",), vmem_limit_bytes=VMEM_LIMIT),
        name="out_router",
    )(x2, gi, bi, o_a, o_b, wa, wb, g1, b1, rwt, rb, tri)


def _split_w1_kernel(w_ref, perm_ref, g_ref, l_ref):
    rows = w_ref.shape[1]
    even = (lax.broadcasted_iota(jnp.int32, (rows, LANES), 1) & 1) == 0
    perm = perm_ref[...]
    for g in range(w_ref.shape[2] // (2 * LANES)):
        v0 = w_ref[0, :, 2 * g * LANES:(2 * g + 1) * LANES]
        v1 = w_ref[0, :, (2 * g + 1) * LANES:(2 * g + 2) * LANES]
        glu = jnp.where(even, v0, pltpu.roll(v1, 1, axis=1)).astype(BF16)
        lin = jnp.where(even, pltpu.roll(v0, LANES - 1, axis=1), v1).astype(BF16)
        g_ref[0, :, g * LANES:(g + 1) * LANES] = jnp.dot(
            glu, perm, preferred_element_type=F32).astype(BF16)
        l_ref[0, :, g * LANES:(g + 1) * LANES] = jnp.dot(
            lin, perm, preferred_element_type=F32).astype(BF16)


def _split_w1(w1e):
    e, d, f2 = w1e.shape
    rows = 256
    half = LANES // 2
    unit = jnp.arange(LANES)
    perm = (jnp.arange(LANES)[:, None] == (2 * (unit % half) + unit // half)[None, :]).astype(BF16)
    return pl.pallas_call(
        _split_w1_kernel,
        grid=(e, d // rows),
        in_specs=[pl.BlockSpec((1, rows, f2), lambda i, j: (i, j, 0)),
                  pl.BlockSpec((LANES, LANES), lambda i, j: (0, 0))],
        out_specs=[pl.BlockSpec((1, rows, f2 // 2), lambda i, j: (i, j, 0)),
                   pl.BlockSpec((1, rows, f2 // 2), lambda i, j: (i, j, 0))],
        out_shape=[jax.ShapeDtypeStruct((e, d, f2 // 2), BF16),
                   jax.ShapeDtypeStruct((e, d, f2 // 2), BF16)],
        compiler_params=pltpu.CompilerParams(
            dimension_semantics=("arbitrary", "arbitrary"), vmem_limit_bytes=VMEM_LIMIT),
        name="split_w1",
    )(w1e, perm)


def _moe_kernel(cnt_ref, offs_ref,
                pos_ref, gate_ref,
                w1g_ref, w1l_ref, b1g_ref, b1l_ref, w2_ref, b2_ref, h_hbm,
                y_hbm, hbuf, ybuf, stage, list_tok, list_gate, sem, *, tile):
    b = pl.program_id(0)
    e = pl.program_id(1)
    n_exp = pl.num_programs(1)
    d_model = w1g_ref.shape[1]
    d_ff = w1g_ref.shape[2]
    nchunk = d_model // LANES
    tile_rows = tile * nchunk
    pairs = tile * TOP_K

    def stage_rows(r):
        return pl.ds(pl.multiple_of(r * nchunk, nchunk), nchunk)

    def listed_rows(p):
        return pl.ds(pl.multiple_of(list_tok[p], nchunk), nchunk)

    @pl.when(jnp.logical_and(b == 0, e == 0))
    def _():
        stage[...] = jnp.zeros_like(stage)

        def pad_body(j, c):
            list_tok[pairs + j] = 0
            return c
        lax.fori_loop(0, LIST_PAD, pad_body, 0)

    @pl.when(e == 0)
    def _():
        src = h_hbm.at[pl.ds(pl.multiple_of(b * tile_rows, tile_rows), tile_rows), :]
        load = pltpu.make_async_copy(src, hbuf, sem.at[0])
        load.start()
        ybuf[...] = jnp.zeros_like(ybuf)

        def sort_body(jo, c):
            for u in range(SORT_UNROLL):
                j = jo * SORT_UNROLL + u
                p = pos_ref[j]
                list_tok[p] = (j & (tile - 1)) * nchunk
                list_gate[p] = gate_ref[j]
            return c
        lax.fori_loop(0, pairs // SORT_UNROLL, sort_body, 0)
        load.wait()

    n = cnt_ref[b * n_exp + e]
    base = offs_ref[b * n_exp + e]

    def run_block(m_rows, p0, nrows):
        def gather_body(ci, c):
            for u in range(ROW_UNROLL):
                r = ci * ROW_UNROLL + u
                stage[stage_rows(r), :] = hbuf[listed_rows(p0 + r), :]
            return c
        lax.fori_loop(0, (nrows + ROW_UNROLL - 1) // ROW_UNROLL, gather_body, 0)

        x = jnp.concatenate(
            [stage[pl.ds(c, m_rows, stride=nchunk), :] for c in range(nchunk)],
            axis=1).astype(BF16)
        y = b2_ref[0]
        for hh in range(d_ff // FF_CHUNK):
            sl = slice(hh * FF_CHUNK, (hh + 1) * FF_CHUNK)
            hg = jnp.dot(x, w1g_ref[0, :, sl], preferred_element_type=F32) + b1g_ref[0, :, sl]
            hl = jnp.dot(x, w1l_ref[0, :, sl], preferred_element_type=F32) + b1l_ref[0, :, sl]
            xg = jnp.minimum(hg, SWIGLU_LIMIT)
            xl = jnp.clip(hl, -SWIGLU_LIMIT, SWIGLU_LIMIT)
            act = xg * jax.nn.sigmoid(SWIGLU_ALPHA * xg) * (xl + 1.0)
            y = y + jnp.dot(act.astype(BF16), w2_ref[0, sl, :], preferred_element_type=F32)
        for c in range(nchunk):
            stage[pl.ds(c, m_rows, stride=nchunk), :] = y[:, c * LANES:(c + 1) * LANES]

        def add_body(ci, c):
            rows, vals = [], []
            for u in range(SUBLANES):
                r = ci * SUBLANES + u
                dst = listed_rows(p0 + r)
                rows.append(dst)
                vals.append(ybuf[dst, :] + list_gate[p0 + r] * stage[stage_rows(r), :])
            for dst, val in zip(rows, vals):
                ybuf[dst, :] = val
            return c
        n_full = nrows // SUBLANES
        lax.fori_loop(0, n_full, add_body, 0)

        def add_tail(r, c):
            dst = listed_rows(p0 + r)
            ybuf[dst, :] = ybuf[dst, :] + list_gate[p0 + r] * stage[stage_rows(r), :]
            return c
        lax.fori_loop(n_full * SUBLANES, nrows, add_tail, 0)

    n_big = n // MOE_BLOCK
    rem = n - n_big * MOE_BLOCK
    rem_is_big = rem > MOE_BLOCK - SMALL_BLOCK
    n_big_blocks = n_big + rem_is_big.astype(jnp.int32)
    n_small_blocks = jnp.where(rem_is_big, 0, (rem + SMALL_BLOCK - 1) // SMALL_BLOCK)

    def big_body(s, c):
        run_block(MOE_BLOCK, base + s * MOE_BLOCK, jnp.minimum(MOE_BLOCK, n - s * MOE_BLOCK))
        return c
    lax.fori_loop(0, n_big_blocks, big_body, 0)

    def small_body(s, c):
        run_block(SMALL_BLOCK, base + n_big * MOE_BLOCK + s * SMALL_BLOCK,
                  jnp.minimum(SMALL_BLOCK, rem - s * SMALL_BLOCK))
        return c
    lax.fori_loop(0, n_small_blocks, small_body, 0)

    @pl.when(e == n_exp - 1)
    def _():
        dst = y_hbm.at[pl.ds(pl.multiple_of(b * tile_rows, tile_rows), tile_rows), :]
        store = pltpu.make_async_copy(ybuf, dst, sem.at[1])
        store.start()
        store.wait()


def _moe(h1t, cnt, offs, pos_flat, gate_flat, w1g, w1l, b1g, b1l, w2, b2, tile):
    n_exp, d, f = w1g.shape
    nchunk = d // LANES
    n_tiles = h1t.shape[0] // (tile * nchunk)
    pairs = tile * TOP_K
    smem_vec = pl.BlockSpec((pairs,), lambda b, e, cnt, offs: (b,), memory_space=pltpu.SMEM)
    expert = lambda b, e, cnt, offs: (e, 0, 0)
    grid_spec = pltpu.PrefetchScalarGridSpec(
        num_scalar_prefetch=2,
        grid=(n_tiles, n_exp),
        in_specs=[
            smem_vec, smem_vec,
            pl.BlockSpec((1, d, f), expert),
            pl.BlockSpec((1, d, f), expert),
            pl.BlockSpec((1, 1, f), expert),
            pl.BlockSpec((1, 1, f), expert),
            pl.BlockSpec((1, f, d), expert),
            pl.BlockSpec((1, 1, d), expert),
            pl.BlockSpec(memory_space=pl.ANY),
        ],
        out_specs=pl.BlockSpec(memory_space=pl.ANY),
        scratch_shapes=[
            pltpu.VMEM((tile * nchunk, LANES), F32),
            pltpu.VMEM((tile * nchunk, LANES), F32),
            pltpu.VMEM((MOE_BLOCK * nchunk, LANES), F32),
            pltpu.SMEM((pairs + LIST_PAD,), jnp.int32),
            pltpu.SMEM((pairs + LIST_PAD,), F32),
            pltpu.SemaphoreType.DMA((2,)),
        ],
    )
    return pl.pallas_call(
        functools.partial(_moe_kernel, tile=tile),
        grid_spec=grid_spec,
        out_shape=jax.ShapeDtypeStruct(h1t.shape, F32),
        compiler_params=pltpu.CompilerParams(
            dimension_semantics=("arbitrary", "arbitrary"), vmem_limit_bytes=MOE_VMEM_LIMIT),
        name="moe",
    )(cnt, offs, pos_flat, gate_flat, w1g, w1l, b1g, b1l, w2, b2, h1t)


def _combine_kernel(h1_ref, y_ref, g_ref, b_ref, o_ref):
    nchunk = o_ref.shape[1] // LANES
    pieces = [DEEPNORM_ALPHA * h1_ref[pl.ds(c, ROW_TILE, stride=nchunk), :]
              + y_ref[pl.ds(c, ROW_TILE, stride=nchunk), :] for c in range(nchunk)]
    o_ref[...] = _layer_norm(jnp.concatenate(pieces, axis=1), g_ref[...], b_ref[...])


def _combine(h1t, yt, g2, b2):
    d = g2.shape[1]
    nchunk = d // LANES
    t = h1t.shape[0] // nchunk
    tiles = pl.BlockSpec((ROW_TILE * nchunk, LANES), lambda i: (i, 0))
    return pl.pallas_call(
        _combine_kernel,
        grid=(t // ROW_TILE,),
        in_specs=[tiles, tiles,
                  pl.BlockSpec((1, d), lambda i: (0, 0)),
                  pl.BlockSpec((1, d), lambda i: (0, 0))],
        out_specs=pl.BlockSpec((ROW_TILE, d), lambda i: (i, 0)),
        out_shape=jax.ShapeDtypeStruct((t, d), F32),
        compiler_params=pltpu.CompilerParams(
            dimension_semantics=("arbitrary",), vmem_limit_bytes=VMEM_LIMIT),
        name="combine",
    )(h1t, yt, g2, b2)


def kernel(x, ln_in_g, ln_in_b, w_in, lambda_q1, lambda_k1, lambda_q2, lambda_k2, subln_g, rel_bias,
           w_out, ln1_g, ln1_b, router_w, router_b, w1, b1, w2, b2, ln2_g, ln2_b):
    bsz, seq, d = x.shape
    t = bsz * seq
    x2 = x.reshape(t, d)
    row = lambda v: v.reshape(1, -1).astype(F32)

    qs = A_HEAD_DIM ** -0.5 * LOG2E
    col_scale = jnp.concatenate([
        jnp.full((A_WIDTH,), qs, F32), jnp.ones((2 * A_WIDTH,), F32),
        jnp.full((B_WIDTH,), B_HEAD_DIM ** -0.5 * LOG2E, F32), jnp.ones((2 * B_WIDTH,), F32)]).reshape(1, -1)
    proj3, vt3, vtb3 = _ln_qkv(x2, row(ln_in_g), row(ln_in_b), w_in[0].astype(BF16), col_scale, seq)

    lam4 = jnp.stack([lambda_q1[0], lambda_k1[0], lambda_q2[0], lambda_k2[0]]).astype(F32)
    o_a = _diff_attn(proj3, vt3, lam4, subln_g[0].astype(F32))
    o_b = _band_attn(proj3, vtb3, rel_bias[0])

    w_o = w_out[0].astype(BF16)
    tile = min(MOE_TILE, t)
    n_tiles = t // tile
    h1t, top_idx, gates, rank, run_cnt = _out_router(
        x2, row(ln_in_g), row(ln_in_b), o_a, o_b,
        w_o[:A_WIDTH], w_o[A_WIDTH:], row(ln1_g[0]), row(ln1_b[0]),
        router_w[0].T.astype(F32), router_b[0].reshape(-1, 1).astype(F32), tile)

    steps = tile // ROW_TILE
    cnt = run_cnt[steps - 1::steps, :, 0]
    offs = jnp.cumsum(cnt, axis=1) - cnt
    hot = top_idx.reshape(TOP_K, n_tiles, tile, 1) == jnp.arange(N_EXPERTS, dtype=jnp.int32)
    pos = rank + jnp.sum(jnp.where(hot, offs[None, :, None, :], 0), axis=-1).reshape(TOP_K, t)
    per_tile = lambda a: a.reshape(TOP_K, n_tiles, tile).transpose(1, 0, 2).reshape(-1)
    w1g, w1l = _split_w1(w1[0])
    b1e = b1[0].astype(F32)[:, None, :]
    yt = _moe(h1t, cnt.reshape(-1), offs.reshape(-1), per_tile(pos), per_tile(gates),
              w1g, w1l, b1e[:, :, 0::2], b1e[:, :, 1::2],
              w2[0].astype(BF16), b2[0][:, None, :].astype(F32), tile)
    out = _combine(h1t, yt, row(ln2_g[0]), row(ln2_b[0]))
    return out.reshape(bsz, seq, d)
```

```python
import functools
import math

import jax
import jax.numpy as jnp
from jax import lax
from jax.experimental import pallas as pl
from jax.experimental.pallas import tpu as pltpu

F32 = jnp.float32
BF16 = jnp.bfloat16

CHUNK = 64
A_HEADS = 4
A_HEAD_DIM = 64
A_WIDTH = A_HEADS * 2 * A_HEAD_DIM
B_HEADS = 8
B_HEAD_DIM = 64
B_WIDTH = B_HEADS * B_HEAD_DIM
B_PAST_CHUNKS = 8
REL_CLIP = 256
N_EXPERTS = 32
TOP_K = 4
SWIGLU_ALPHA = 1.702
SWIGLU_LIMIT = 7.0
MOE_BLOCK = 512
LN_EPS = 1e-5
RMS_EPS = 1e-5
DEPTH = 1
DEEPNORM_ALPHA = (2 * DEPTH) ** 0.25
LAM_INIT = 0.8 - 0.6 * math.exp(-0.3 * 0)

LOG2E = 1.4426950408889634
NEG = -1e30
LANES = 128
SUBLANES = 8
ROW_TILE = 512
VMEM_LIMIT = 48 * 1024 * 1024
DIFF_TILE = 256
BAND_TILE = 128
BAND_TILES = (BAND_TILE + B_PAST_CHUNKS * CHUNK) // BAND_TILE
BAND_PAIR = 2
BAND_STEP_ROWS = 1024
MOE_TILE = 4096
MOE_VMEM_LIMIT = 60 * 1024 * 1024
SMALL_BLOCK = 128
FF_CHUNK = 512
SORT_UNROLL = 8
ROW_UNROLL = 16
LIST_PAD = 128


def _layer_norm(x, g, b):
    mu = jnp.mean(x, axis=-1, keepdims=True)
    xc = x - mu
    var = jnp.mean(xc * xc, axis=-1, keepdims=True)
    return xc * lax.rsqrt(var + LN_EPS) * g + b


def _ln_qkv_kernel(x_ref, g_ref, b_ref, w_ref, cs_ref, o_ref, vta_ref, vtb_ref):
    h = _layer_norm(x_ref[...], g_ref[...], b_ref[...])
    hb = h.astype(BF16)
    n_out = w_ref.shape[1]
    for c in range(n_out // ROW_TILE):
        sl = slice(c * ROW_TILE, (c + 1) * ROW_TILE)
        val = jnp.dot(hb, w_ref[:, sl], preferred_element_type=F32) * cs_ref[:, sl]
        o_ref[0, :, sl] = val.astype(BF16)
        for start, vt_ref in ((2 * A_WIDTH, vta_ref), (3 * A_WIDTH + 2 * B_WIDTH, vtb_ref)):
            if sl.start == start:
                tile = vt_ref.shape[2]
                for kt in range(ROW_TILE // tile):
                    vt_ref[kt] = val[kt * tile:(kt + 1) * tile, :].T.astype(BF16)


def _ln_qkv(x2, g, b, w_bf, col_scale, seq):
    assert A_WIDTH == ROW_TILE and B_WIDTH == ROW_TILE
    t, d = x2.shape
    n_out = w_bf.shape[1]
    steps_per_seq = seq // ROW_TILE
    vt_spec = lambda tile: pl.BlockSpec((ROW_TILE // tile, ROW_TILE, tile), lambda i: (i, 0, 0))
    vt_shape = lambda tile: jax.ShapeDtypeStruct((t // tile, ROW_TILE, tile), BF16)
    return pl.pallas_call(
        _ln_qkv_kernel,
        grid=(t // ROW_TILE,),
        in_specs=[
            pl.BlockSpec((ROW_TILE, d), lambda i: (i, 0)),
            pl.BlockSpec((1, d), lambda i: (0, 0)),
            pl.BlockSpec((1, d), lambda i: (0, 0)),
            pl.BlockSpec((d, n_out), lambda i: (0, 0)),
            pl.BlockSpec((1, n_out), lambda i: (0, 0)),
        ],
        out_specs=[pl.BlockSpec((1, ROW_TILE, n_out),
                                lambda i: (i // steps_per_seq, i % steps_per_seq, 0)),
                   vt_spec(DIFF_TILE), vt_spec(BAND_TILE)],
        out_shape=[jax.ShapeDtypeStruct((t // seq, seq, n_out), BF16),
                   vt_shape(DIFF_TILE), vt_shape(BAND_TILE)],
        compiler_params=pltpu.CompilerParams(
            dimension_semantics=("arbitrary",), vmem_limit_bytes=VMEM_LIMIT),
        name="ln_qkv",
    )(x2, g, b, w_bf, col_scale)


def _diff_attn_kernel(c_ref, lam_ref, g_ref, boff_ref, bdiag_ref, q_ref, k_ref, vt_ref, o_ref, s_scr):
    tq = q_ref.shape[1]
    tk = vt_ref.shape[2]
    h = pl.program_id(1)
    i = pl.program_id(2)
    c = c_ref[h]
    q = q_ref[0]
    lane = lax.broadcasted_iota(jnp.int32, q.shape, 1)
    zero = jnp.zeros_like(q)
    qq = jnp.concatenate([jnp.where(lane < A_HEAD_DIM, q, zero),
                          jnp.where(lane >= A_HEAD_DIM, q, zero)], axis=0)
    qqt = qq.astype(F32).T.astype(BF16)
    n_before = i * (tq // tk)

    def scores(j):
        kb = k_ref[0, pl.ds(pl.multiple_of(j * tk, tk), tk), :]
        return jnp.dot(kb, qqt, preferred_element_type=F32)

    def update(slot, table_ref, j, carry):
        m, l, acc = carry
        s = s_scr[slot] + table_ref[0]
        shift = c * (j * tk).astype(F32)
        m_new = jnp.maximum(m, jnp.max(s, axis=0, keepdims=True) + shift)
        alpha = jnp.exp2(m - m_new)
        p = jnp.exp2(s - (m_new - shift))
        l = alpha * l + jnp.sum(p, axis=0, keepdims=True)
        acc = alpha * acc + jnp.dot(vt_ref[j], p.astype(BF16), preferred_element_type=F32)
        return m_new, l, acc

    s_scr[0] = scores(0)

    def pair(t, carry):
        j = 2 * t
        s_scr[1] = scores(j + 1)
        carry = update(0, boff_ref, j, carry)
        s_scr[0] = scores(j + 2)
        return update(1, boff_ref, j + 1, carry)

    def quad(t, carry):
        return pair(2 * t + 1, pair(2 * t, carry))

    init = (jnp.full((1, 2 * tq), NEG, F32), jnp.zeros((1, 2 * tq), F32),
            jnp.zeros((LANES, 2 * tq), F32))
    carry = lax.fori_loop(0, n_before // 4, quad, init)
    carry = lax.fori_loop(n_before // 4 * 2, n_before // 2, pair, carry)

    def odd_tail(carry):
        s_scr[1] = scores(n_before)
        carry = update(0, boff_ref, n_before - 1, carry)
        return update(1, bdiag_ref, n_before, carry)

    def even_tail(carry):
        return update(0, bdiag_ref, n_before, carry)

    _, l, acc = lax.cond(n_before % 2 == 1, odd_tail, even_tail, carry)

    o_all = acc / l
    lv = lam_ref[...]
    lam = (jnp.exp(jnp.sum(lv[0:1] * lv[1:2], axis=1, keepdims=True))
           - jnp.exp(jnp.sum(lv[2:3] * lv[3:4], axis=1, keepdims=True)) + LAM_INIT)
    o = o_all[:, :tq] - lam * o_all[:, tq:]
    ms = jnp.mean(o * o, axis=0, keepdims=True)
    o = o * lax.rsqrt(ms + RMS_EPS) * (g_ref[...] * (1.0 - LAM_INIT))
    o_ref[0] = o.T.astype(BF16)


def _diff_attn(proj3, vt3, lam4, subln_g):
    bsz, seq, _ = proj3.shape
    tq = tk = DIFF_TILE
    c = jnp.asarray([2.0 ** (-8.0 * (h + 1) / A_HEADS) for h in range(A_HEADS)], F32) * LOG2E
    r = jnp.arange(tk, dtype=jnp.int32)[:, None]
    qrel = jnp.arange(2 * tq, dtype=jnp.int32)[None, :] % tq
    boff = c[:, None, None] * jnp.broadcast_to(r, (tk, 2 * tq)).astype(F32)
    allowed = (r // CHUNK) <= (qrel // CHUNK)
    bdiag = jnp.where(allowed, c[:, None, None] * (qrel - jnp.abs(qrel - r)).astype(F32), NEG)
    kblk = A_WIDTH // LANES
    return pl.pallas_call(
        _diff_attn_kernel,
        grid=(bsz, A_HEADS, seq // tq),
        in_specs=[
            pl.BlockSpec(memory_space=pltpu.SMEM),
            pl.BlockSpec((4, A_HEAD_DIM), lambda b, h, i: (0, 0)),
            pl.BlockSpec((LANES, 1), lambda b, h, i: (0, 0)),
            pl.BlockSpec((1, tk, 2 * tq), lambda b, h, i: (h, 0, 0)),
            pl.BlockSpec((1, tk, 2 * tq), lambda b, h, i: (h, 0, 0)),
            pl.BlockSpec((1, tq, LANES), lambda b, h, i: (b, i, h)),
            pl.BlockSpec((1, seq, LANES), lambda b, h, i: (b, 0, kblk + h)),
            pl.BlockSpec((seq // tk, LANES, tk), lambda b, h, i: (b, h, 0)),
        ],
        out_specs=pl.BlockSpec((1, tq, LANES), lambda b, h, i: (b, i, h)),
        out_shape=jax.ShapeDtypeStruct((bsz, seq, A_WIDTH), BF16),
        scratch_shapes=[pltpu.VMEM((2, tk, 2 * tq), F32)],
        compiler_params=pltpu.CompilerParams(
            dimension_semantics=("arbitrary", "arbitrary", "arbitrary"),
            vmem_limit_bytes=VMEM_LIMIT),
        name="diff_attn",
    )(c, lam4, subln_g.reshape(LANES, 1), boff, bdiag, proj3, proj3, vt3)


def _band_attn_kernel(bias_ref, q_ref, k_ref, vt_ref, o_ref, s_scr):
    i = pl.program_id(2)
    tq = BAND_TILE
    units = q_ref.shape[1] // (BAND_PAIR * tq)
    key_tiles = BAND_TILES + BAND_PAIR - 1
    row = lax.broadcasted_iota(jnp.int32, (LANES, tq), 0)

    def key_tile(n, t):
        jt = (i * units + n) * BAND_PAIR - (BAND_TILES - 1) + t
        return jnp.maximum(jt, 0), jt < 0

    def scores(n):
        qs = []
        for a in range(BAND_PAIR):
            q = q_ref[0, (n * BAND_PAIR + a) * tq:(n * BAND_PAIR + a + 1) * tq, :]
            lane = lax.broadcasted_iota(jnp.int32, q.shape, 1)
            zero = jnp.zeros_like(q)
            qs += [jnp.where(lane < B_HEAD_DIM, q, zero), jnp.where(lane >= B_HEAD_DIM, q, zero)]
        qqt = jnp.concatenate(qs, axis=0).astype(F32).T.astype(BF16)
        ks = [k_ref[0, pl.ds(pl.multiple_of(key_tile(n, t)[0] * tq, tq), tq), :] for t in range(key_tiles)]
        return jnp.dot(jnp.concatenate(ks, axis=0), qqt, preferred_element_type=F32)

    def finish(n):
        bias = [bias_ref[0, jnp.where(key_tile(n, t)[1], 1, 0), t] for t in range(key_tiles)]
        s = s_scr[n % 2] + jnp.concatenate(bias, axis=0)
        m = jnp.max(s, axis=0, keepdims=True)
        p = jnp.exp2(s - m)
        l = jnp.sum(p, axis=0, keepdims=True)
        vt = jnp.concatenate([vt_ref[key_tile(n, t)[0]] for t in range(key_tiles)], axis=1)
        o = jnp.dot(vt, p.astype(BF16), preferred_element_type=F32) / l
        for a in range(BAND_PAIR):
            oa = jnp.where(row < B_HEAD_DIM, o[:, 2 * a * tq:(2 * a + 1) * tq],
                           o[:, (2 * a + 1) * tq:(2 * a + 2) * tq])
            o_ref[0, (n * BAND_PAIR + a) * tq:(n * BAND_PAIR + a + 1) * tq, :] = oa.T.astype(BF16)

    s_scr[0] = scores(0)
    for n in range(units):
        if n + 1 < units:
            s_scr[(n + 1) % 2] = scores(n + 1)
        finish(n)


def _band_bias(rel_bias, tq):
    past = B_PAST_CHUNKS * CHUNK
    band = tq + past
    assert tq - 1 <= REL_CLIP <= past
    qi = jnp.arange(tq)
    kj = jnp.arange(band)
    cq = qi[:, None] // CHUNK
    ck = kj[None, :] // CHUNK
    allowed = (ck >= cq) & (ck <= cq + B_PAST_CHUNKS)
    tab = rel_bias.astype(F32) * LOG2E
    n_diag = band + tq - 1
    n_unclipped = REL_CLIP + tq
    w = jnp.concatenate([tab[:, REL_CLIP - (tq - 1):],
                         jnp.broadcast_to(tab[:, -1:], (B_HEADS, n_diag - n_unclipped))], axis=1)
    shifted = jnp.tile(w, (1, tq + 1))[:, :tq * (n_diag + 1)].reshape(B_HEADS, tq, n_diag + 1)
    bias = jnp.flip(shifted[:, :, :band], axis=2)
    return jnp.where(allowed[None], bias, NEG)


def _band_attn(proj3, vtb3, rel_bias):
    bsz, seq, _ = proj3.shape
    tq = BAND_TILE
    groups = B_HEADS // 2
    key_tiles = BAND_TILES + BAND_PAIR - 1
    cols = BAND_PAIR * 2 * tq
    bias = _band_bias(rel_bias, tq)
    bias = bias.reshape(groups, 2, tq, BAND_TILES, tq)
    masked = jnp.full((groups, 2, tq, 1, tq), NEG, F32)
    per_tile = [jnp.concatenate([masked] * a + [bias] + [masked] * (BAND_PAIR - 1 - a), axis=3)
                for a in range(BAND_PAIR)]
    bias = jnp.stack(per_tile, axis=1)
    bias = bias.transpose(0, 4, 5, 1, 2, 3).reshape(groups, 1, key_tiles, tq, cols)
    bias = jnp.concatenate([bias, jnp.full_like(bias, NEG)], axis=1)
    qblk = 3 * A_WIDTH // LANES
    kblk = qblk + B_WIDTH // LANES
    rows = min(BAND_STEP_ROWS, seq)
    return pl.pallas_call(
        _band_attn_kernel,
        grid=(bsz, groups, seq // rows),
        in_specs=[
            pl.BlockSpec((1, 2, key_tiles, tq, cols), lambda b, g, i: (g, 0, 0, 0, 0)),
            pl.BlockSpec((1, rows, LANES), lambda b, g, i: (b, i, qblk + g)),
            pl.BlockSpec((1, seq, LANES), lambda b, g, i: (b, 0, kblk + g)),
            pl.BlockSpec((seq // tq, LANES, tq), lambda b, g, i: (b, g, 0)),
        ],
        out_specs=pl.BlockSpec((1, rows, LANES), lambda b, g, i: (b, i, g)),
        out_shape=jax.ShapeDtypeStruct((bsz, seq, B_WIDTH), BF16),
        scratch_shapes=[pltpu.VMEM((2, key_tiles * tq, cols), F32)],
        compiler_params=pltpu.CompilerParams(
            dimension_semantics=("arbitrary", "arbitrary", "arbitrary"),
            vmem_limit_bytes=VMEM_LIMIT),
        name="band_attn",
    )(bias, proj3, proj3, vtb3)


def _out_router_kernel(x_ref, gi_ref, bi_ref, oa_ref, ob_ref, wa_ref, wb_ref, g1_ref, b1_ref,
                       rwt_ref, rb_ref, tri_ref, h1_ref, idx_ref, gate_ref, rank_ref, cnt_ref,
                       carry_ref, *, steps_per_tile):
    h = _layer_norm(x_ref[...], gi_ref[...], bi_ref[...])
    mix = (jnp.dot(oa_ref[0], wa_ref[...], preferred_element_type=F32)
           + jnp.dot(ob_ref[0], wb_ref[...], preferred_element_type=F32))
    h1 = _layer_norm(DEEPNORM_ALPHA * h + mix, g1_ref[...], b1_ref[...])
    nchunk = h1.shape[1] // LANES
    for c in range(nchunk):
        h1_ref[pl.ds(c, ROW_TILE, stride=nchunk), :] = h1[:, c * LANES:(c + 1) * LANES]
    lt = lax.dot_general(rwt_ref[...], h1, (((1,), (1,)), ((), ())),
                         precision=lax.Precision.HIGHEST, preferred_element_type=F32)
    lt = lt + rb_ref[...]
    eidx = lax.broadcasted_iota(jnp.int32, lt.shape, 0)
    vals, idxs, hots = [], [], []
    for _ in range(TOP_K):
        mx = jnp.max(lt, axis=0, keepdims=True)
        am = jnp.min(jnp.where(lt == mx, eidx, N_EXPERTS), axis=0, keepdims=True)
        hit = eidx == am
        vals.append(mx)
        idxs.append(am)
        hots.append(jnp.where(hit, 1.0, 0.0))
        lt = jnp.where(hit, -jnp.inf, lt)
    ex = [jnp.exp(v - vals[0]) for v in vals]
    den = ex[0] + ex[1] + ex[2] + ex[3]
    idx_ref[...] = jnp.concatenate(idxs, axis=0)
    gate_ref[...] = jnp.concatenate([e / den for e in ex], axis=0)

    @pl.when(pl.program_id(0) % steps_per_tile == 0)
    def _():
        carry_ref[...] = jnp.zeros_like(carry_ref)

    hot = (hots[0] + hots[1]) + (hots[2] + hots[3])
    before = jnp.dot(hot.astype(BF16), tri_ref[...], preferred_element_type=F32) + carry_ref[...]
    rank_ref[...] = jnp.concatenate(
        [jnp.sum(hk * before, axis=0, keepdims=True) for hk in hots], axis=0).astype(jnp.int32)
    total = carry_ref[...] + jnp.sum(hot, axis=1, keepdims=True)
    carry_ref[...] = total
    cnt_ref[0] = total.astype(jnp.int32)


def _out_router(x2, gi, bi, o_a, o_b, wa, wb, g1, b1, rwt, rb, moe_tile):
    t, d = x2.shape
    nchunk = d // LANES
    row = lambda i: (i, 0)
    fixed = lambda i: (0, 0)
    steps_per_seq = o_a.shape[1] // ROW_TILE
    seq_row = lambda i: (i // steps_per_seq, i % steps_per_seq, 0)
    tri = jnp.triu(jnp.ones((ROW_TILE, ROW_TILE), BF16), k=1)
    return pl.pallas_call(
        functools.partial(_out_router_kernel, steps_per_tile=moe_tile // ROW_TILE),
        grid=(t // ROW_TILE,),
        in_specs=[
            pl.BlockSpec((ROW_TILE, d), row),
            pl.BlockSpec((1, d), fixed),
            pl.BlockSpec((1, d), fixed),
            pl.BlockSpec((1, ROW_TILE, A_WIDTH), seq_row),
            pl.BlockSpec((1, ROW_TILE, B_WIDTH), seq_row),
            pl.BlockSpec((A_WIDTH, d), fixed),
            pl.BlockSpec((B_WIDTH, d), fixed),
            pl.BlockSpec((1, d), fixed),
            pl.BlockSpec((1, d), fixed),
            pl.BlockSpec((N_EXPERTS, d), fixed),
            pl.BlockSpec((N_EXPERTS, 1), fixed),
            pl.BlockSpec((ROW_TILE, ROW_TILE), fixed),
        ],
        out_specs=[
            pl.BlockSpec((ROW_TILE * nchunk, LANES), row),
            pl.BlockSpec((TOP_K, ROW_TILE), lambda i: (0, i)),
            pl.BlockSpec((TOP_K, ROW_TILE), lambda i: (0, i)),
            pl.BlockSpec((TOP_K, ROW_TILE), lambda i: (0, i)),
            pl.BlockSpec((1, N_EXPERTS, 1), lambda i: (i, 0, 0)),
        ],
        out_shape=[
            jax.ShapeDtypeStruct((t * nchunk, LANES), F32),
            jax.ShapeDtypeStruct((TOP_K, t), jnp.int32),
            jax.ShapeDtypeStruct((TOP_K, t), F32),
            jax.ShapeDtypeStruct((TOP_K, t), jnp.int32),
            jax.ShapeDtypeStruct((t // ROW_TILE, N_EXPERTS, 1), jnp.int32),
        ],
        scratch_shapes=[pltpu.VMEM((N_EXPERTS, 1), F32)],
        compiler_params=pltpu.CompilerParams(
            dimension_semantics=("arbitrary",), vmem_limit_bytes=VMEM_LIMIT),
        name="out_router",
    )(x2, gi, bi, o_a, o_b, wa, wb, g1, b1, rwt, rb, tri)


def _split_w1_kernel(w_ref, perm_ref, g_ref, l_ref):
    rows = w_ref.shape[1]
    even = (lax.broadcasted_iota(jnp.int32, (rows, LANES), 1) & 1) == 0
    perm = perm_ref[...]
    for g in range(w_ref.shape[2] // (2 * LANES)):
        v0 = w_ref[0, :, 2 * g * LANES:(2 * g + 1) * LANES]
        v1 = w_ref[0, :, (2 * g + 1) * LANES:(2 * g + 2) * LANES]
        glu = jnp.where(even, v0, pltpu.roll(v1, 1, axis=1)).astype(BF16)
        lin = jnp.where(even, pltpu.roll(v0, LANES - 1, axis=1), v1).astype(BF16)
        g_ref[0, :, g * LANES:(g + 1) * LANES] = jnp.dot(
            glu, perm, preferred_element_type=F32).astype(BF16)
        l_ref[0, :, g * LANES:(g + 1) * LANES] = jnp.dot(
            lin, perm, preferred_element_type=F32).astype(BF16)


def _split_w1(w1e):
    e, d, f2 = w1e.shape
    rows = 256
    half = LANES // 2
    unit = jnp.arange(LANES)
    perm = (jnp.arange(LANES)[:, None] == (2 * (unit % half) + unit // half)[None, :]).astype(BF16)
    return pl.pallas_call(
        _split_w1_kernel,
        grid=(e, d // rows),
        in_specs=[pl.BlockSpec((1, rows, f2), lambda i, j: (i, j, 0)),
                  pl.BlockSpec((LANES, LANES), lambda i, j: (0, 0))],
        out_specs=[pl.BlockSpec((1, rows, f2 // 2), lambda i, j: (i, j, 0)),
                   pl.BlockSpec((1, rows, f2 // 2), lambda i, j: (i, j, 0))],
        out_shape=[jax.ShapeDtypeStruct((e, d, f2 // 2), BF16),
                   jax.ShapeDtypeStruct((e, d, f2 // 2), BF16)],
        compiler_params=pltpu.CompilerParams(
            dimension_semantics=("arbitrary", "arbitrary"), vmem_limit_bytes=VMEM_LIMIT),
        name="split_w1",
    )(w1e, perm)


def _moe_kernel(cnt_ref, offs_ref,
                pos_ref, gate_ref,
                w1g_ref, w1l_ref, b1g_ref, b1l_ref, w2_ref, b2_ref, h_hbm,
                y_hbm, hbuf, ybuf, stage, list_tok, list_gate, sem, *, tile):
    b = pl.program_id(0)
    e = pl.program_id(1)
    n_exp = pl.num_programs(1)
    d_model = w1g_ref.shape[1]
    d_ff = w1g_ref.shape[2]
    nchunk = d_model // LANES
    tile_rows = tile * nchunk
    pairs = tile * TOP_K

    def stage_rows(r):
        return pl.ds(pl.multiple_of(r * nchunk, nchunk), nchunk)

    def listed_rows(p):
        return pl.ds(pl.multiple_of(list_tok[p], nchunk), nchunk)

    @pl.when(jnp.logical_and(b == 0, e == 0))
    def _():
        stage[...] = jnp.zeros_like(stage)

        def pad_body(j, c):
            list_tok[pairs + j] = 0
            return c
        lax.fori_loop(0, LIST_PAD, pad_body, 0)

    @pl.when(e == 0)
    def _():
        src = h_hbm.at[pl.ds(pl.multiple_of(b * tile_rows, tile_rows), tile_rows), :]
        load = pltpu.make_async_copy(src, hbuf, sem.at[0])
        load.start()
        ybuf[...] = jnp.zeros_like(ybuf)

        def sort_body(jo, c):
            for u in range(SORT_UNROLL):
                j = jo * SORT_UNROLL + u
                p = pos_ref[j]
                list_tok[p] = (j & (tile - 1)) * nchunk
                list_gate[p] = gate_ref[j]
            return c
        lax.fori_loop(0, pairs // SORT_UNROLL, sort_body, 0)
        load.wait()

    n = cnt_ref[b * n_exp + e]
    base = offs_ref[b * n_exp + e]

    def run_block(m_rows, p0, nrows):
        def gather_body(ci, c):
            for u in range(ROW_UNROLL):
                r = ci * ROW_UNROLL + u
                stage[stage_rows(r), :] = hbuf[listed_rows(p0 + r), :]
            return c
        lax.fori_loop(0, (nrows + ROW_UNROLL - 1) // ROW_UNROLL, gather_body, 0)

        x = jnp.concatenate(
            [stage[pl.ds(c, m_rows, stride=nchunk), :] for c in range(nchunk)],
            axis=1).astype(BF16)
        y = b2_ref[0]
        for hh in range(d_ff // FF_CHUNK):
            sl = slice(hh * FF_CHUNK, (hh + 1) * FF_CHUNK)
            hg = jnp.dot(x, w1g_ref[0, :, sl], preferred_element_type=F32) + b1g_ref[0, :, sl]
            hl = jnp.dot(x, w1l_ref[0, :, sl], preferred_element_type=F32) + b1l_ref[0, :, sl]
            xg = jnp.minimum(hg, SWIGLU_LIMIT)
            xl = jnp.clip(hl, -SWIGLU_LIMIT, SWIGLU_LIMIT)
            act = xg * jax.nn.sigmoid(SWIGLU_ALPHA * xg) * (xl + 1.0)
            y = y + jnp.dot(act.astype(BF16), w2_ref[0, sl, :], preferred_element_type=F32)
        for c in range(nchunk):
            stage[pl.ds(c, m_rows, stride=nchunk), :] = y[:, c * LANES:(c + 1) * LANES]

        def add_body(ci, c):
            rows, vals = [], []
            for u in range(SUBLANES):
                r = ci * SUBLANES + u
                dst = listed_rows(p0 + r)
                rows.append(dst)
                vals.append(ybuf[dst, :] + list_gate[p0 + r] * stage[stage_rows(r), :])
            for dst, val in zip(rows, vals):
                ybuf[dst, :] = val
            return c
        n_full = nrows // SUBLANES
        lax.fori_loop(0, n_full, add_body, 0)

        def add_tail(r, c):
            dst = listed_rows(p0 + r)
            ybuf[dst, :] = ybuf[dst, :] + list_gate[p0 + r] * stage[stage_rows(r), :]
            return c
        lax.fori_loop(n_full * SUBLANES, nrows, add_tail, 0)

    n_big = n // MOE_BLOCK
    rem = n - n_big * MOE_BLOCK
    rem_is_big = rem > MOE_BLOCK - SMALL_BLOCK
    n_big_blocks = n_big + rem_is_big.astype(jnp.int32)
    n_small_blocks = jnp.where(rem_is_big, 0, (rem + SMALL_BLOCK - 1) // SMALL_BLOCK)

    def big_body(s, c):
        run_block(MOE_BLOCK, base + s * MOE_BLOCK, jnp.minimum(MOE_BLOCK, n - s * MOE_BLOCK))
        return c
    lax.fori_loop(0, n_big_blocks, big_body, 0)

    def small_body(s, c):
        run_block(SMALL_BLOCK, base + n_big * MOE_BLOCK + s * SMALL_BLOCK,
                  jnp.minimum(SMALL_BLOCK, rem - s * SMALL_BLOCK))
        return c
    lax.fori_loop(0, n_small_blocks, small_body, 0)

    @pl.when(e == n_exp - 1)
    def _():
        dst = y_hbm.at[pl.ds(pl.multiple_of(b * tile_rows, tile_rows), tile_rows), :]
        store = pltpu.make_async_copy(ybuf, dst, sem.at[1])
        store.start()
        store.wait()


def _moe(h1t, cnt, offs, pos_flat, gate_flat, w1g, w1l, b1g, b1l, w2, b2, tile):
    n_exp, d, f = w1g.shape
    nchunk = d // LANES
    n_tiles = h1t.shape[0] // (tile * nchunk)
    pairs = tile * TOP_K
    smem_vec = pl.BlockSpec((pairs,), lambda b, e, cnt, offs: (b,), memory_space=pltpu.SMEM)
    expert = lambda b, e, cnt, offs: (e, 0, 0)
    grid_spec = pltpu.PrefetchScalarGridSpec(
        num_scalar_prefetch=2,
        grid=(n_tiles, n_exp),
        in_specs=[
            smem_vec, smem_vec,
            pl.BlockSpec((1, d, f), expert),
            pl.BlockSpec((1, d, f), expert),
            pl.BlockSpec((1, 1, f), expert),
            pl.BlockSpec((1, 1, f), expert),
            pl.BlockSpec((1, f, d), expert),
            pl.BlockSpec((1, 1, d), expert),
            pl.BlockSpec(memory_space=pl.ANY),
        ],
        out_specs=pl.BlockSpec(memory_space=pl.ANY),
        scratch_shapes=[
            pltpu.VMEM((tile * nchunk, LANES), F32),
            pltpu.VMEM((tile * nchunk, LANES), F32),
            pltpu.VMEM((MOE_BLOCK * nchunk, LANES), F32),
            pltpu.SMEM((pairs + LIST_PAD,), jnp.int32),
            pltpu.SMEM((pairs + LIST_PAD,), F32),
            pltpu.SemaphoreType.DMA((2,)),
        ],
    )
    return pl.pallas_call(
        functools.partial(_moe_kernel, tile=tile),
        grid_spec=grid_spec,
        out_shape=jax.ShapeDtypeStruct(h1t.shape, F32),
        compiler_params=pltpu.CompilerParams(
            dimension_semantics=("arbitrary", "arbitrary"), vmem_limit_bytes=MOE_VMEM_LIMIT),
        name="moe",
    )(cnt, offs, pos_flat, gate_flat, w1g, w1l, b1g, b1l, w2, b2, h1t)


def _combine_kernel(h1_ref, y_ref, g_ref, b_ref, o_ref):
    nchunk = o_ref.shape[1] // LANES
    pieces = [DEEPNORM_ALPHA * h1_ref[pl.ds(c, ROW_TILE, stride=nchunk), :]
              + y_ref[pl.ds(c, ROW_TILE, stride=nchunk), :] for c in range(nchunk)]
    o_ref[...] = _layer_norm(jnp.concatenate(pieces, axis=1), g_ref[...], b_ref[...])


def _combine(h1t, yt, g2, b2):
    d = g2.shape[1]
    nchunk = d // LANES
    t = h1t.shape[0] // nchunk
    tiles = pl.BlockSpec((ROW_TILE * nchunk, LANES), lambda i: (i, 0))
    return pl.pallas_call(
        _combine_kernel,
        grid=(t // ROW_TILE,),
        in_specs=[tiles, tiles,
                  pl.BlockSpec((1, d), lambda i: (0, 0)),
                  pl.BlockSpec((1, d), lambda i: (0, 0))],
        out_specs=pl.BlockSpec((ROW_TILE, d), lambda i: (i, 0)),
        out_shape=jax.ShapeDtypeStruct((t, d), F32),
        compiler_params=pltpu.CompilerParams(
            dimension_semantics=("arbitrary",), vmem_limit_bytes=VMEM_LIMIT),
        name="combine",
    )(h1t, yt, g2, b2)


def kernel(x, ln_in_g, ln_in_b, w_in, lambda_q1, lambda_k1, lambda_q2, lambda_k2, subln_g, rel_bias,
           w_out, ln1_g, ln1_b, router_w, router_b, w1, b1, w2, b2, ln2_g, ln2_b):
    bsz, seq, d = x.shape
    t = bsz * seq
    x2 = x.reshape(t, d)
    row = lambda v: v.reshape(1, -1).astype(F32)

    qs = A_HEAD_DIM ** -0.5 * LOG2E
    col_scale = jnp.concatenate([
        jnp.full((A_WIDTH,), qs, F32), jnp.ones((2 * A_WIDTH,), F32),
        jnp.full((B_WIDTH,), B_HEAD_DIM ** -0.5 * LOG2E, F32), jnp.ones((2 * B_WIDTH,), F32)]).reshape(1, -1)
    proj3, vt3, vtb3 = _ln_qkv(x2, row(ln_in_g), row(ln_in_b), w_in[0].astype(BF16), col_scale, seq)

    lam4 = jnp.stack([lambda_q1[0], lambda_k1[0], lambda_q2[0], lambda_k2[0]]).astype(F32)
    o_a = _diff_attn(proj3, vt3, lam4, subln_g[0].astype(F32))
    o_b = _band_attn(proj3, vtb3, rel_bias[0])

    w_o = w_out[0].astype(BF16)
    tile = min(MOE_TILE, t)
    n_tiles = t // tile
    h1t, top_idx, gates, rank, run_cnt = _out_router(
        x2, row(ln_in_g), row(ln_in_b), o_a, o_b,
        w_o[:A_WIDTH], w_o[A_WIDTH:], row(ln1_g[0]), row(ln1_b[0]),
        router_w[0].T.astype(F32), router_b[0].reshape(-1, 1).astype(F32), tile)

    steps = tile // ROW_TILE
    cnt = run_cnt[steps - 1::steps, :, 0]
    offs = jnp.cumsum(cnt, axis=1) - cnt
    hot = top_idx.reshape(TOP_K, n_tiles, tile, 1) == jnp.arange(N_EXPERTS, dtype=jnp.int32)
    pos = rank + jnp.sum(jnp.where(hot, offs[None, :, None, :], 0), axis=-1).reshape(TOP_K, t)
    per_tile = lambda a: a.reshape(TOP_K, n_tiles, tile).transpose(1, 0, 2).reshape(-1)
    w1g, w1l = _split_w1(w1[0])
    b1e = b1[0].astype(F32)[:, None, :]
    yt = _moe(h1t, cnt.reshape(-1), offs.reshape(-1), per_tile(pos), per_tile(gates),
              w1g, w1l, b1e[:, :, 0::2], b1e[:, :, 1::2],
              w2[0].astype(BF16), b2[0][:, None, :].astype(F32), tile)
    out = _combine(h1t, yt, row(ln2_g[0]), row(ln2_b[0]))
    return out.reshape(bsz, seq, d)
```

```python
import functools
import math

import jax
import jax.numpy as jnp
from jax import lax
from jax.experimental import pallas as pl
from jax.experimental.pallas import tpu as pltpu

F32 = jnp.float32
BF16 = jnp.bfloat16

CHUNK = 64
A_HEADS = 4
A_HEAD_DIM = 64
A_WIDTH = A_HEADS * 2 * A_HEAD_DIM
B_HEADS = 8
B_HEAD_DIM = 64
B_WIDTH = B_HEADS * B_HEAD_DIM
B_PAST_CHUNKS = 8
REL_CLIP = 256
N_EXPERTS = 32
TOP_K = 4
SWIGLU_ALPHA = 1.702
SWIGLU_LIMIT = 7.0
MOE_BLOCK = 512
LN_EPS = 1e-5
RMS_EPS = 1e-5
DEPTH = 1
DEEPNORM_ALPHA = (2 * DEPTH) ** 0.25
LAM_INIT = 0.8 - 0.6 * math.exp(-0.3 * 0)

LOG2E = 1.4426950408889634
NEG = -1e30
LANES = 128
SUBLANES = 8
ROW_TILE = 512
VMEM_LIMIT = 48 * 1024 * 1024
DIFF_TILE = 256
BAND_TILE = 128
BAND_TILES = (BAND_TILE + B_PAST_CHUNKS * CHUNK) // BAND_TILE
BAND_PAIR = 2
BAND_STEP_ROWS = 1024
MOE_TILE = 4096
MOE_VMEM_LIMIT = 60 * 1024 * 1024
SMALL_BLOCK = 128
FF_CHUNK = 512
SORT_UNROLL = 8
SORT_CHUNK = 512
ROW_UNROLL = 16
LIST_PAD = 128


def _layer_norm(x, g, b):
    mu = jnp.mean(x, axis=-1, keepdims=True)
    xc = x - mu
    var = jnp.mean(xc * xc, axis=-1, keepdims=True)
    return xc * lax.rsqrt(var + LN_EPS) * g + b


def _ln_qkv_kernel(x_ref, g_ref, b_ref, w_ref, cs_ref, o_ref, vta_ref, vtb_ref):
    h = _layer_norm(x_ref[...], g_ref[...], b_ref[...])
    hb = h.astype(BF16)
    n_out = w_ref.shape[1]
    for c in range(n_out // ROW_TILE):
        sl = slice(c * ROW_TILE, (c + 1) * ROW_TILE)
        val = jnp.dot(hb, w_ref[:, sl], preferred_element_type=F32) * cs_ref[:, sl]
        o_ref[0, :, sl] = val.astype(BF16)
        for start, vt_ref in ((2 * A_WIDTH, vta_ref), (3 * A_WIDTH + 2 * B_WIDTH, vtb_ref)):
            if sl.start == start:
                tile = vt_ref.shape[2]
                for kt in range(ROW_TILE // tile):
                    vt_ref[kt] = val[kt * tile:(kt + 1) * tile, :].T.astype(BF16)


def _ln_qkv(x2, g, b, w_bf, col_scale, seq):
    assert A_WIDTH == ROW_TILE and B_WIDTH == ROW_TILE
    t, d = x2.shape
    n_out = w_bf.shape[1]
    steps_per_seq = seq // ROW_TILE
    vt_spec = lambda tile: pl.BlockSpec((ROW_TILE // tile, ROW_TILE, tile), lambda i: (i, 0, 0))
    vt_shape = lambda tile: jax.ShapeDtypeStruct((t // tile, ROW_TILE, tile), BF16)
    return pl.pallas_call(
        _ln_qkv_kernel,
        grid=(t // ROW_TILE,),
        in_specs=[
            pl.BlockSpec((ROW_TILE, d), lambda i: (i, 0)),
            pl.BlockSpec((1, d), lambda i: (0, 0)),
            pl.BlockSpec((1, d), lambda i: (0, 0)),
            pl.BlockSpec((d, n_out), lambda i: (0, 0)),
            pl.BlockSpec((1, n_out), lambda i: (0, 0)),
        ],
        out_specs=[pl.BlockSpec((1, ROW_TILE, n_out),
                                lambda i: (i // steps_per_seq, i % steps_per_seq, 0)),
                   vt_spec(DIFF_TILE), vt_spec(BAND_TILE)],
        out_shape=[jax.ShapeDtypeStruct((t // seq, seq, n_out), BF16),
                   vt_shape(DIFF_TILE), vt_shape(BAND_TILE)],
        compiler_params=pltpu.CompilerParams(
            dimension_semantics=("arbitrary",), vmem_limit_bytes=VMEM_LIMIT),
        name="ln_qkv",
    )(x2, g, b, w_bf, col_scale)


def _diff_attn_kernel(c_ref, lam_ref, g_ref, boff_ref, bdiag_ref, q_ref, k_ref, vt_ref, o_ref, s_scr):
    tq = q_ref.shape[1]
    tk = vt_ref.shape[2]
    h = pl.program_id(1)
    i = pl.program_id(2)
    c = c_ref[h]
    q = q_ref[0]
    lane = lax.broadcasted_iota(jnp.int32, q.shape, 1)
    zero = jnp.zeros_like(q)
    qq = jnp.concatenate([jnp.where(lane < A_HEAD_DIM, q, zero),
                          jnp.where(lane >= A_HEAD_DIM, q, zero)], axis=0)
    qqt = qq.astype(F32).T.astype(BF16)
    n_before = i * (tq // tk)

    def scores(j):
        kb = k_ref[0, pl.ds(pl.multiple_of(j * tk, tk), tk), :]
        return jnp.dot(kb, qqt, preferred_element_type=F32)

    def update(slot, table_ref, j, carry):
        m, l, acc = carry
        s = s_scr[slot] + table_ref[0]
        shift = c * (j * tk).astype(F32)
        m_new = jnp.maximum(m, jnp.max(s, axis=0, keepdims=True) + shift)
        alpha = jnp.exp2(m - m_new)
        p = jnp.exp2(s - (m_new - shift))
        l = alpha * l + jnp.sum(p, axis=0, keepdims=True)
        acc = alpha * acc + jnp.dot(vt_ref[j], p.astype(BF16), preferred_element_type=F32)
        return m_new, l, acc

    s_scr[0] = scores(0)

    def pair(t, carry):
        j = 2 * t
        s_scr[1] = scores(j + 1)
        carry = update(0, boff_ref, j, carry)
        s_scr[0] = scores(j + 2)
        return update(1, boff_ref, j + 1, carry)

    def quad(t, carry):
        return pair(2 * t + 1, pair(2 * t, carry))

    init = (jnp.full((1, 2 * tq), NEG, F32), jnp.zeros((1, 2 * tq), F32),
            jnp.zeros((LANES, 2 * tq), F32))
    carry = lax.fori_loop(0, n_before // 4, quad, init)
    carry = lax.fori_loop(n_before // 4 * 2, n_before // 2, pair, carry)

    def odd_tail(carry):
        s_scr[1] = scores(n_before)
        carry = update(0, boff_ref, n_before - 1, carry)
        return update(1, bdiag_ref, n_before, carry)

    def even_tail(carry):
        return update(0, bdiag_ref, n_before, carry)

    _, l, acc = lax.cond(n_before % 2 == 1, odd_tail, even_tail, carry)

    o_all = acc / l
    lv = lam_ref[...]
    lam = (jnp.exp(jnp.sum(lv[0:1] * lv[1:2], axis=1, keepdims=True))
           - jnp.exp(jnp.sum(lv[2:3] * lv[3:4], axis=1, keepdims=True)) + LAM_INIT)
    o = o_all[:, :tq] - lam * o_all[:, tq:]
    ms = jnp.mean(o * o, axis=0, keepdims=True)
    o = o * lax.rsqrt(ms + RMS_EPS) * (g_ref[...] * (1.0 - LAM_INIT))
    o_ref[0] = o.T.astype(BF16)


def _diff_attn(proj3, vt3, lam4, subln_g):
    bsz, seq, _ = proj3.shape
    tq = tk = DIFF_TILE
    c = jnp.asarray([2.0 ** (-8.0 * (h + 1) / A_HEADS) for h in range(A_HEADS)], F32) * LOG2E
    r = jnp.arange(tk, dtype=jnp.int32)[:, None]
    qrel = jnp.arange(2 * tq, dtype=jnp.int32)[None, :] % tq
    boff = c[:, None, None] * jnp.broadcast_to(r, (tk, 2 * tq)).astype(F32)
    allowed = (r // CHUNK) <= (qrel // CHUNK)
    bdiag = jnp.where(allowed, c[:, None, None] * (qrel - jnp.abs(qrel - r)).astype(F32), NEG)
    kblk = A_WIDTH // LANES
    return pl.pallas_call(
        _diff_attn_kernel,
        grid=(bsz, A_HEADS, seq // tq),
        in_specs=[
            pl.BlockSpec(memory_space=pltpu.SMEM),
            pl.BlockSpec((4, A_HEAD_DIM), lambda b, h, i: (0, 0)),
            pl.BlockSpec((LANES, 1), lambda b, h, i: (0, 0)),
            pl.BlockSpec((1, tk, 2 * tq), lambda b, h, i: (h, 0, 0)),
            pl.BlockSpec((1, tk, 2 * tq), lambda b, h, i: (h, 0, 0)),
            pl.BlockSpec((1, tq, LANES), lambda b, h, i: (b, i, h)),
            pl.BlockSpec((1, seq, LANES), lambda b, h, i: (b, 0, kblk + h)),
            pl.BlockSpec((seq // tk, LANES, tk), lambda b, h, i: (b, h, 0)),
        ],
        out_specs=pl.BlockSpec((1, tq, LANES), lambda b, h, i: (b, i, h)),
        out_shape=jax.ShapeDtypeStruct((bsz, seq, A_WIDTH), BF16),
        scratch_shapes=[pltpu.VMEM((2, tk, 2 * tq), F32)],
        compiler_params=pltpu.CompilerParams(
            dimension_semantics=("arbitrary", "arbitrary", "arbitrary"),
            vmem_limit_bytes=VMEM_LIMIT),
        name="diff_attn",
    )(c, lam4, subln_g.reshape(LANES, 1), boff, bdiag, proj3, proj3, vt3)


def _band_attn_kernel(bias_ref, q_ref, k_ref, vt_ref, o_ref, s_scr):
    i = pl.program_id(2)
    tq = BAND_TILE
    units = q_ref.shape[1] // (BAND_PAIR * tq)
    key_tiles = BAND_TILES + BAND_PAIR - 1
    row = lax.broadcasted_iota(jnp.int32, (LANES, tq), 0)

    def key_tile(n, t):
        jt = (i * units + n) * BAND_PAIR - (BAND_TILES - 1) + t
        return jnp.maximum(jt, 0), jt < 0

    def scores(n):
        qs = []
        for a in range(BAND_PAIR):
            q = q_ref[0, (n * BAND_PAIR + a) * tq:(n * BAND_PAIR + a + 1) * tq, :]
            lane = lax.broadcasted_iota(jnp.int32, q.shape, 1)
            zero = jnp.zeros_like(q)
            qs += [jnp.where(lane < B_HEAD_DIM, q, zero), jnp.where(lane >= B_HEAD_DIM, q, zero)]
        qqt = jnp.concatenate(qs, axis=0).astype(F32).T.astype(BF16)
        ks = [k_ref[0, pl.ds(pl.multiple_of(key_tile(n, t)[0] * tq, tq), tq), :] for t in range(key_tiles)]
        return jnp.dot(jnp.concatenate(ks, axis=0), qqt, preferred_element_type=F32)

    def finish(n):
        bias = [bias_ref[0, jnp.where(key_tile(n, t)[1], 1, 0), t] for t in range(key_tiles)]
        s = s_scr[n % 2] + jnp.concatenate(bias, axis=0)
        m = jnp.max(s, axis=0, keepdims=True)
        p = jnp.exp2(s - m)
        l = jnp.sum(p, axis=0, keepdims=True)
        vt = jnp.concatenate([vt_ref[key_tile(n, t)[0]] for t in range(key_tiles)], axis=1)
        o = jnp.dot(vt, p.astype(BF16), preferred_element_type=F32) / l
        for a in range(BAND_PAIR):
            oa = jnp.where(row < B_HEAD_DIM, o[:, 2 * a * tq:(2 * a + 1) * tq],
                           o[:, (2 * a + 1) * tq:(2 * a + 2) * tq])
            o_ref[0, (n * BAND_PAIR + a) * tq:(n * BAND_PAIR + a + 1) * tq, :] = oa.T.astype(BF16)

    s_scr[0] = scores(0)
    for n in range(units):
        if n + 1 < units:
            s_scr[(n + 1) % 2] = scores(n + 1)
        finish(n)


def _band_bias(rel_bias, tq):
    past = B_PAST_CHUNKS * CHUNK
    band = tq + past
    assert tq - 1 <= REL_CLIP <= past
    qi = jnp.arange(tq)
    kj = jnp.arange(band)
    cq = qi[:, None] // CHUNK
    ck = kj[None, :] // CHUNK
    allowed = (ck >= cq) & (ck <= cq + B_PAST_CHUNKS)
    tab = rel_bias.astype(F32) * LOG2E
    n_diag = band + tq - 1
    n_unclipped = REL_CLIP + tq
    w = jnp.concatenate([tab[:, REL_CLIP - (tq - 1):],
                         jnp.broadcast_to(tab[:, -1:], (B_HEADS, n_diag - n_unclipped))], axis=1)
    shifted = jnp.tile(w, (1, tq + 1))[:, :tq * (n_diag + 1)].reshape(B_HEADS, tq, n_diag + 1)
    bias = jnp.flip(shifted[:, :, :band], axis=2)
    return jnp.where(allowed[None], bias, NEG)


def _band_attn(proj3, vtb3, rel_bias):
    bsz, seq, _ = proj3.shape
    tq = BAND_TILE
    groups = B_HEADS // 2
    key_tiles = BAND_TILES + BAND_PAIR - 1
    cols = BAND_PAIR * 2 * tq
    bias = _band_bias(rel_bias, tq)
    bias = bias.reshape(groups, 2, tq, BAND_TILES, tq)
    masked = jnp.full((groups, 2, tq, 1, tq), NEG, F32)
    per_tile = [jnp.concatenate([masked] * a + [bias] + [masked] * (BAND_PAIR - 1 - a), axis=3)
                for a in range(BAND_PAIR)]
    bias = jnp.stack(per_tile, axis=1)
    bias = bias.transpose(0, 4, 5, 1, 2, 3).reshape(groups, 1, key_tiles, tq, cols)
    bias = jnp.concatenate([bias, jnp.full_like(bias, NEG)], axis=1)
    qblk = 3 * A_WIDTH // LANES
    kblk = qblk + B_WIDTH // LANES
    rows = min(BAND_STEP_ROWS, seq)
    return pl.pallas_call(
        _band_attn_kernel,
        grid=(bsz, groups, seq // rows),
        in_specs=[
            pl.BlockSpec((1, 2, key_tiles, tq, cols), lambda b, g, i: (g, 0, 0, 0, 0)),
            pl.BlockSpec((1, rows, LANES), lambda b, g, i: (b, i, qblk + g)),
            pl.BlockSpec((1, seq, LANES), lambda b, g, i: (b, 0, kblk + g)),
            pl.BlockSpec((seq // tq, LANES, tq), lambda b, g, i: (b, g, 0)),
        ],
        out_specs=pl.BlockSpec((1, rows, LANES), lambda b, g, i: (b, i, g)),
        out_shape=jax.ShapeDtypeStruct((bsz, seq, B_WIDTH), BF16),
        scratch_shapes=[pltpu.VMEM((2, key_tiles * tq, cols), F32)],
        compiler_params=pltpu.CompilerParams(
            dimension_semantics=("arbitrary", "arbitrary", "arbitrary"),
            vmem_limit_bytes=VMEM_LIMIT),
        name="band_attn",
    )(bias, proj3, proj3, vtb3)


def _out_router_kernel(x_ref, gi_ref, bi_ref, oa_ref, ob_ref, wa_ref, wb_ref, g1_ref, b1_ref,
                       rwt_ref, rb_ref, tri_ref, h1_ref, idx_ref, gate_ref, rank_ref, cnt_ref,
                       carry_ref, *, steps_per_tile):
    h = _layer_norm(x_ref[...], gi_ref[...], bi_ref[...])
    mix = (jnp.dot(oa_ref[0], wa_ref[...], preferred_element_type=F32)
           + jnp.dot(ob_ref[0], wb_ref[...], preferred_element_type=F32))
    h1 = _layer_norm(DEEPNORM_ALPHA * h + mix, g1_ref[...], b1_ref[...])
    nchunk = h1.shape[1] // LANES
    for c in range(nchunk):
        h1_ref[pl.ds(c, ROW_TILE, stride=nchunk), :] = h1[:, c * LANES:(c + 1) * LANES]
    lt = lax.dot_general(rwt_ref[...], h1, (((1,), (1,)), ((), ())),
                         precision=lax.Precision.HIGHEST, preferred_element_type=F32)
    lt = lt + rb_ref[...]
    eidx = lax.broadcasted_iota(jnp.int32, lt.shape, 0)
    vals, idxs, hots = [], [], []
    for _ in range(TOP_K):
        mx = jnp.max(lt, axis=0, keepdims=True)
        am = jnp.min(jnp.where(lt == mx, eidx, N_EXPERTS), axis=0, keepdims=True)
        hit = eidx == am
        vals.append(mx)
        idxs.append(am)
        hots.append(jnp.where(hit, 1.0, 0.0))
        lt = jnp.where(hit, -jnp.inf, lt)
    ex = [jnp.exp(v - vals[0]) for v in vals]
    den = ex[0] + ex[1] + ex[2] + ex[3]
    idx_ref[...] = jnp.concatenate(idxs, axis=0)
    gate_ref[...] = jnp.concatenate([e / den for e in ex], axis=0)

    @pl.when(pl.program_id(0) % steps_per_tile == 0)
    def _():
        carry_ref[...] = jnp.zeros_like(carry_ref)

    hot = (hots[0] + hots[1]) + (hots[2] + hots[3])
    before = jnp.dot(hot.astype(BF16), tri_ref[...], preferred_element_type=F32) + carry_ref[...]
    rank_ref[...] = jnp.concatenate(
        [jnp.sum(hk * before, axis=0, keepdims=True) for hk in hots], axis=0).astype(jnp.int32)
    total = carry_ref[...] + jnp.sum(hot, axis=1, keepdims=True)
    carry_ref[...] = total
    cnt_ref[0] = total.astype(jnp.int32)


def _out_router(x2, gi, bi, o_a, o_b, wa, wb, g1, b1, rwt, rb, moe_tile):
    t, d = x2.shape
    nchunk = d // LANES
    row = lambda i: (i, 0)
    fixed = lambda i: (0, 0)
    steps_per_seq = o_a.shape[1] // ROW_TILE
    seq_row = lambda i: (i // steps_per_seq, i % steps_per_seq, 0)
    tri = jnp.triu(jnp.ones((ROW_TILE, ROW_TILE), BF16), k=1)
    return pl.pallas_call(
        functools.partial(_out_router_kernel, steps_per_tile=moe_tile // ROW_TILE),
        grid=(t // ROW_TILE,),
        in_specs=[
            pl.BlockSpec((ROW_TILE, d), row),
            pl.BlockSpec((1, d), fixed),
            pl.BlockSpec((1, d), fixed),
            pl.BlockSpec((1, ROW_TILE, A_WIDTH), seq_row),
            pl.BlockSpec((1, ROW_TILE, B_WIDTH), seq_row),
            pl.BlockSpec((A_WIDTH, d), fixed),
            pl.BlockSpec((B_WIDTH, d), fixed),
            pl.BlockSpec((1, d), fixed),
            pl.BlockSpec((1, d), fixed),
            pl.BlockSpec((N_EXPERTS, d), fixed),
            pl.BlockSpec((N_EXPERTS, 1), fixed),
            pl.BlockSpec((ROW_TILE, ROW_TILE), fixed),
        ],
        out_specs=[
            pl.BlockSpec((ROW_TILE * nchunk, LANES), row),
            pl.BlockSpec((TOP_K, ROW_TILE), lambda i: (0, i)),
            pl.BlockSpec((TOP_K, ROW_TILE), lambda i: (0, i)),
            pl.BlockSpec((TOP_K, ROW_TILE), lambda i: (0, i)),
            pl.BlockSpec((1, N_EXPERTS, 1), lambda i: (i, 0, 0)),
        ],
        out_shape=[
            jax.ShapeDtypeStruct((t * nchunk, LANES), F32),
            jax.ShapeDtypeStruct((TOP_K, t), jnp.int32),
            jax.ShapeDtypeStruct((TOP_K, t), F32),
            jax.ShapeDtypeStruct((TOP_K, t), jnp.int32),
            jax.ShapeDtypeStruct((t // ROW_TILE, N_EXPERTS, 1), jnp.int32),
        ],
        scratch_shapes=[pltpu.VMEM((N_EXPERTS, 1), F32)],
        compiler_params=pltpu.CompilerParams(
            dimension_semantics=("arbitrary",), vmem_limit_bytes=VMEM_LIMIT),
        name="out_router",
    )(x2, gi, bi, o_a, o_b, wa, wb, g1, b1, rwt, rb, tri)


def _split_w1_kernel(w_ref, perm_ref, g_ref, l_ref):
    rows = w_ref.shape[1]
    even = (lax.broadcasted_iota(jnp.int32, (rows, LANES), 1) & 1) == 0
    perm = perm_ref[...]
    for g in range(w_ref.shape[2] // (2 * LANES)):
        v0 = w_ref[0, :, 2 * g * LANES:(2 * g + 1) * LANES]
        v1 = w_ref[0, :, (2 * g + 1) * LANES:(2 * g + 2) * LANES]
        glu = jnp.where(even, v0, pltpu.roll(v1, 1, axis=1)).astype(BF16)
        lin = jnp.where(even, pltpu.roll(v0, LANES - 1, axis=1), v1).astype(BF16)
        g_ref[0, :, g * LANES:(g + 1) * LANES] = jnp.dot(
            glu, perm, preferred_element_type=F32).astype(BF16)
        l_ref[0, :, g * LANES:(g + 1) * LANES] = jnp.dot(
            lin, perm, preferred_element_type=F32).astype(BF16)


def _split_w1(w1e):
    e, d, f2 = w1e.shape
    rows = 256
    half = LANES // 2
    unit = jnp.arange(LANES)
    perm = (jnp.arange(LANES)[:, None] == (2 * (unit % half) + unit // half)[None, :]).astype(BF16)
    return pl.pallas_call(
        _split_w1_kernel,
        grid=(e, d // rows),
        in_specs=[pl.BlockSpec((1, rows, f2), lambda i, j: (i, j, 0)),
                  pl.BlockSpec((LANES, LANES), lambda i, j: (0, 0))],
        out_specs=[pl.BlockSpec((1, rows, f2 // 2), lambda i, j: (i, j, 0)),
                   pl.BlockSpec((1, rows, f2 // 2), lambda i, j: (i, j, 0))],
        out_shape=[jax.ShapeDtypeStruct((e, d, f2 // 2), BF16),
                   jax.ShapeDtypeStruct((e, d, f2 // 2), BF16)],
        compiler_params=pltpu.CompilerParams(
            dimension_semantics=("arbitrary", "arbitrary"), vmem_limit_bytes=VMEM_LIMIT),
        name="split_w1",
    )(w1e, perm)


def _moe_kernel(cnt_ref, offs_ref,
                w1g_ref, w1l_ref, b1g_ref, b1l_ref, w2_ref, b2_ref, pos_hbm, gate_hbm, h_hbm,
                y_hbm, hbuf, ybuf, stage, list_tok, list_gate, pos_in, gate_in, n_sorted, sem, *, tile):
    b = pl.program_id(0)
    e = pl.program_id(1)
    n_tiles = pl.num_programs(0)
    n_exp = pl.num_programs(1)
    d_model = w1g_ref.shape[1]
    d_ff = w1g_ref.shape[2]
    nchunk = d_model // LANES
    tile_rows = tile * nchunk
    pairs = tile * TOP_K
    n_sort_chunks = pairs // SORT_CHUNK
    half = (b % 2) * pairs
    other_half = pairs - half

    def stage_rows(r):
        return pl.ds(pl.multiple_of(r * nchunk, nchunk), nchunk)

    def listed_rows(p):
        return pl.ds(pl.multiple_of(list_tok[p], nchunk), nchunk)

    def sort_pair(j, first_row, dst_half):
        p = pos_in[j] + dst_half
        list_tok[p] = first_row
        list_gate[p] = gate_in[j]

    def sort_chunks(lo, hi, dst_half):
        def body(jo, c):
            for u in range(SORT_UNROLL):
                j = jo * SORT_UNROLL + u
                sort_pair(j, (j & (tile - 1)) * nchunk, dst_half)
            return c
        lax.fori_loop(lo * (SORT_CHUNK // SORT_UNROLL), hi * (SORT_CHUNK // SORT_UNROLL), body, 0)

    def load_pairs(t):
        rows = pl.ds(pl.multiple_of(t * pairs, pairs), pairs)
        copies = [pltpu.make_async_copy(pos_hbm.at[rows], pos_in, sem.at[2]),
                  pltpu.make_async_copy(gate_hbm.at[rows], gate_in, sem.at[3])]
        for cp in copies:
            cp.start()
        for cp in copies:
            cp.wait()

    @pl.when(jnp.logical_and(b == 0, e == 0))
    def _():
        stage[...] = jnp.zeros_like(stage)

        def pad_body(j, c):
            list_tok[pairs + j] = 0
            list_tok[2 * pairs + j] = 0
            return c
        lax.fori_loop(0, LIST_PAD, pad_body, 0)
        load_pairs(0)
        n_sorted[0] = 0

    @pl.when(e == 0)
    def _():
        src = h_hbm.at[pl.ds(pl.multiple_of(b * tile_rows, tile_rows), tile_rows), :]
        load = pltpu.make_async_copy(src, hbuf, sem.at[0])
        load.start()
        ybuf[...] = jnp.zeros_like(ybuf)
        sort_chunks(jnp.minimum(n_sorted[0], n_sort_chunks), n_sort_chunks, half)
        n_sorted[0] = 0

        @pl.when(b + 1 < n_tiles)
        def _():
            load_pairs(b + 1)
        load.wait()

    n = cnt_ref[b * n_exp + e]
    base = offs_ref[b * n_exp + e] + half

    def run_block(m_rows, p0, nrows):
        def gather_body(ci, c):
            for u in range(ROW_UNROLL):
                r = ci * ROW_UNROLL + u
                stage[stage_rows(r), :] = hbuf[listed_rows(p0 + r), :]
            return c
        lax.fori_loop(0, (nrows + ROW_UNROLL - 1) // ROW_UNROLL, gather_body, 0)

        if m_rows == MOE_BLOCK:
            chunk = jnp.minimum(n_sorted[0], n_sort_chunks - 1)
            n_sorted[0] = n_sorted[0] + 1
            j0 = chunk * SORT_CHUNK
            row0 = (j0 & (tile - 1)) * nchunk
            for u in range(SORT_CHUNK):
                sort_pair(j0 + u, row0 + u * nchunk, other_half)

        x = jnp.concatenate(
            [stage[pl.ds(c, m_rows, stride=nchunk), :] for c in range(nchunk)],
            axis=1).astype(BF16)
        y = b2_ref[0]
        for hh in range(d_ff // FF_CHUNK):
            sl = slice(hh * FF_CHUNK, (hh + 1) * FF_CHUNK)
            hg = jnp.dot(x, w1g_ref[0, :, sl], preferred_element_type=F32) + b1g_ref[0, :, sl]
            hl = jnp.dot(x, w1l_ref[0, :, sl], preferred_element_type=F32) + b1l_ref[0, :, sl]
            xg = jnp.minimum(hg, SWIGLU_LIMIT)
            xl = jnp.clip(hl, -SWIGLU_LIMIT, SWIGLU_LIMIT)
            act = xg * jax.nn.sigmoid(SWIGLU_ALPHA * xg) * (xl + 1.0)
            y = y + jnp.dot(act.astype(BF16), w2_ref[0, sl, :], preferred_element_type=F32)
        for c in range(nchunk):
            stage[pl.ds(c, m_rows, stride=nchunk), :] = y[:, c * LANES:(c + 1) * LANES]

        def add_body(ci, c):
            rows, vals = [], []
            for u in range(SUBLANES):
                r = ci * SUBLANES + u
                dst = listed_rows(p0 + r)
                rows.append(dst)
                vals.append(ybuf[dst, :] + list_gate[p0 + r] * stage[stage_rows(r), :])
            for dst, val in zip(rows, vals):
                ybuf[dst, :] = val
            return c
        n_full = nrows // SUBLANES
        lax.fori_loop(0, n_full, add_body, 0)

        def add_tail(r, c):
            dst = listed_rows(p0 + r)
            ybuf[dst, :] = ybuf[dst, :] + list_gate[p0 + r] * stage[stage_rows(r), :]
            return c
        lax.fori_loop(n_full * SUBLANES, nrows, add_tail, 0)

    n_big = n // MOE_BLOCK
    rem = n - n_big * MOE_BLOCK
    rem_is_big = rem > MOE_BLOCK - SMALL_BLOCK
    n_big_blocks = n_big + rem_is_big.astype(jnp.int32)
    n_small_blocks = jnp.where(rem_is_big, 0, (rem + SMALL_BLOCK - 1) // SMALL_BLOCK)

    def big_body(s, c):
        run_block(MOE_BLOCK, base + s * MOE_BLOCK, jnp.minimum(MOE_BLOCK, n - s * MOE_BLOCK))
        return c
    lax.fori_loop(0, n_big_blocks, big_body, 0)

    def small_body(s, c):
        run_block(SMALL_BLOCK, base + n_big * MOE_BLOCK + s * SMALL_BLOCK,
                  jnp.minimum(SMALL_BLOCK, rem - s * SMALL_BLOCK))
        return c
    lax.fori_loop(0, n_small_blocks, small_body, 0)

    @pl.when(e == n_exp - 1)
    def _():
        dst = y_hbm.at[pl.ds(pl.multiple_of(b * tile_rows, tile_rows), tile_rows), :]
        store = pltpu.make_async_copy(ybuf, dst, sem.at[1])
        store.start()
        store.wait()


def _moe(h1t, cnt, offs, pos_flat, gate_flat, w1g, w1l, b1g, b1l, w2, b2, tile):
    n_exp, d, f = w1g.shape
    nchunk = d // LANES
    n_tiles = h1t.shape[0] // (tile * nchunk)
    pairs = tile * TOP_K
    assert tile % SORT_CHUNK == 0
    expert = lambda b, e, cnt, offs: (e, 0, 0)
    grid_spec = pltpu.PrefetchScalarGridSpec(
        num_scalar_prefetch=2,
        grid=(n_tiles, n_exp),
        in_specs=[
            pl.BlockSpec((1, d, f), expert),
            pl.BlockSpec((1, d, f), expert),
            pl.BlockSpec((1, 1, f), expert),
            pl.BlockSpec((1, 1, f), expert),
            pl.BlockSpec((1, f, d), expert),
            pl.BlockSpec((1, 1, d), expert),
            pl.BlockSpec(memory_space=pl.ANY),
            pl.BlockSpec(memory_space=pl.ANY),
            pl.BlockSpec(memory_space=pl.ANY),
        ],
        out_specs=pl.BlockSpec(memory_space=pl.ANY),
        scratch_shapes=[
            pltpu.VMEM((tile * nchunk, LANES), F32),
            pltpu.VMEM((tile * nchunk, LANES), F32),
            pltpu.VMEM((MOE_BLOCK * nchunk, LANES), F32),
            pltpu.SMEM((2 * pairs + LIST_PAD,), jnp.int32),
            pltpu.SMEM((2 * pairs + LIST_PAD,), F32),
            pltpu.SMEM((pairs,), jnp.int32),
            pltpu.SMEM((pairs,), F32),
            pltpu.SMEM((1,), jnp.int32),
            pltpu.SemaphoreType.DMA((4,)),
        ],
    )
    return pl.pallas_call(
        functools.partial(_moe_kernel, tile=tile),
        grid_spec=grid_spec,
        out_shape=jax.ShapeDtypeStruct(h1t.shape, F32),
        compiler_params=pltpu.CompilerParams(
            dimension_semantics=("arbitrary", "arbitrary"), vmem_limit_bytes=MOE_VMEM_LIMIT),
        name="moe",
    )(cnt, offs, w1g, w1l, b1g, b1l, w2, b2, pos_flat, gate_flat, h1t)


def _combine_kernel(h1_ref, y_ref, g_ref, b_ref, o_ref):
    nchunk = o_ref.shape[1] // LANES
    pieces = [DEEPNORM_ALPHA * h1_ref[pl.ds(c, ROW_TILE, stride=nchunk), :]
              + y_ref[pl.ds(c, ROW_TILE, stride=nchunk), :] for c in range(nchunk)]
    o_ref[...] = _layer_norm(jnp.concatenate(pieces, axis=1), g_ref[...], b_ref[...])


def _combine(h1t, yt, g2, b2):
    d = g2.shape[1]
    nchunk = d // LANES
    t = h1t.shape[0] // nchunk
    tiles = pl.BlockSpec((ROW_TILE * nchunk, LANES), lambda i: (i, 0))
    return pl.pallas_call(
        _combine_kernel,
        grid=(t // ROW_TILE,),
        in_specs=[tiles, tiles,
                  pl.BlockSpec((1, d), lambda i: (0, 0)),
                  pl.BlockSpec((1, d), lambda i: (0, 0))],
        out_specs=pl.BlockSpec((ROW_TILE, d), lambda i: (i, 0)),
        out_shape=jax.ShapeDtypeStruct((t, d), F32),
        compiler_params=pltpu.CompilerParams(
            dimension_semantics=("arbitrary",), vmem_limit_bytes=VMEM_LIMIT),
        name="combine",
    )(h1t, yt, g2, b2)


def kernel(x, ln_in_g, ln_in_b, w_in, lambda_q1, lambda_k1, lambda_q2, lambda_k2, subln_g, rel_bias,
           w_out, ln1_g, ln1_b, router_w, router_b, w1, b1, w2, b2, ln2_g, ln2_b):
    bsz, seq, d = x.shape
    t = bsz * seq
    x2 = x.reshape(t, d)
    row = lambda v: v.reshape(1, -1).astype(F32)

    qs = A_HEAD_DIM ** -0.5 * LOG2E
    col_scale = jnp.concatenate([
        jnp.full((A_WIDTH,), qs, F32), jnp.ones((2 * A_WIDTH,), F32),
        jnp.full((B_WIDTH,), B_HEAD_DIM ** -0.5 * LOG2E, F32), jnp.ones((2 * B_WIDTH,), F32)]).reshape(1, -1)
    proj3, vt3, vtb3 = _ln_qkv(x2, row(ln_in_g), row(ln_in_b), w_in[0].astype(BF16), col_scale, seq)

    lam4 = jnp.stack([lambda_q1[0], lambda_k1[0], lambda_q2[0], lambda_k2[0]]).astype(F32)
    o_a = _diff_attn(proj3, vt3, lam4, subln_g[0].astype(F32))
    o_b = _band_attn(proj3, vtb3, rel_bias[0])

    w_o = w_out[0].astype(BF16)
    tile = min(MOE_TILE, t)
    n_tiles = t // tile
    h1t, top_idx, gates, rank, run_cnt = _out_router(
        x2, row(ln_in_g), row(ln_in_b), o_a, o_b,
        w_o[:A_WIDTH], w_o[A_WIDTH:], row(ln1_g[0]), row(ln1_b[0]),
        router_w[0].T.astype(F32), router_b[0].reshape(-1, 1).astype(F32), tile)

    steps = tile // ROW_TILE
    cnt = run_cnt[steps - 1::steps, :, 0]
    offs = jnp.cumsum(cnt, axis=1) - cnt
    hot = top_idx.reshape(TOP_K, n_tiles, tile, 1) == jnp.arange(N_EXPERTS, dtype=jnp.int32)
    pos = rank + jnp.sum(jnp.where(hot, offs[None, :, None, :], 0), axis=-1).reshape(TOP_K, t)
    per_tile = lambda a: a.reshape(TOP_K, n_tiles, tile).transpose(1, 0, 2).reshape(-1)
    w1g, w1l = _split_w1(w1[0])
    b1e = b1[0].astype(F32)[:, None, :]
    yt = _moe(h1t, cnt.reshape(-1), offs.reshape(-1), per_tile(pos), per_tile(gates),
              w1g, w1l, b1e[:, :, 0::2], b1e[:, :, 1::2],
              w2[0].astype(BF16), b2[0][:, None, :].astype(F32), tile)
    out = _combine(h1t, yt, row(ln2_g[0]), row(ln2_b[0]))
    return out.reshape(bsz, seq, d)
```

```python
import functools
import math

import jax
import jax.numpy as jnp
from jax import lax
from jax.experimental import pallas as pl
from jax.experimental.pallas import tpu as pltpu

F32 = jnp.float32
BF16 = jnp.bfloat16

CHUNK = 64
A_HEADS = 4
A_HEAD_DIM = 64
A_WIDTH = A_HEADS * 2 * A_HEAD_DIM
B_HEADS = 8
B_HEAD_DIM = 64
B_WIDTH = B_HEADS * B_HEAD_DIM
B_PAST_CHUNKS = 8
REL_CLIP = 256
N_EXPERTS = 32
TOP_K = 4
SWIGLU_ALPHA = 1.702
SWIGLU_LIMIT = 7.0
MOE_BLOCK = 544
LN_EPS = 1e-5
RMS_EPS = 1e-5
DEPTH = 1
DEEPNORM_ALPHA = (2 * DEPTH) ** 0.25
LAM_INIT = 0.8 - 0.6 * math.exp(-0.3 * 0)

LOG2E = 1.4426950408889634
NEG = -1e30
LANES = 128
SUBLANES = 8
ROW_TILE = 512
VMEM_LIMIT = 48 * 1024 * 1024
DIFF_TILE = 256
BAND_TILE = 128
BAND_TILES = (BAND_TILE + B_PAST_CHUNKS * CHUNK) // BAND_TILE
BAND_PAIR = 2
BAND_STEP_ROWS = 1024
MOE_TILE = 4096
MOE_VMEM_LIMIT = 60 * 1024 * 1024
SMALL_BLOCK = 128
FF_CHUNK = 512
SORT_UNROLL = 8
SORT_CHUNK = 512
ROW_UNROLL = 16
LIST_PAD = 128


def _layer_norm(x, g, b):
    mu = jnp.mean(x, axis=-1, keepdims=True)
    xc = x - mu
    var = jnp.mean(xc * xc, axis=-1, keepdims=True)
    return xc * lax.rsqrt(var + LN_EPS) * g + b


def _ln_qkv_kernel(x_ref, g_ref, b_ref, w_ref, cs_ref, o_ref, vta_ref, vtb_ref):
    h = _layer_norm(x_ref[...], g_ref[...], b_ref[...])
    hb = h.astype(BF16)
    n_out = w_ref.shape[1]
    for c in range(n_out // ROW_TILE):
        sl = slice(c * ROW_TILE, (c + 1) * ROW_TILE)
        val = jnp.dot(hb, w_ref[:, sl], preferred_element_type=F32) * cs_ref[:, sl]
        o_ref[0, :, sl] = val.astype(BF16)
        for start, vt_ref in ((2 * A_WIDTH, vta_ref), (3 * A_WIDTH + 2 * B_WIDTH, vtb_ref)):
            if sl.start == start:
                tile = vt_ref.shape[2]
                for kt in range(ROW_TILE // tile):
                    vt_ref[kt] = val[kt * tile:(kt + 1) * tile, :].T.astype(BF16)


def _ln_qkv(x2, g, b, w_bf, col_scale, seq):
    assert A_WIDTH == ROW_TILE and B_WIDTH == ROW_TILE
    t, d = x2.shape
    n_out = w_bf.shape[1]
    steps_per_seq = seq // ROW_TILE
    vt_spec = lambda tile: pl.BlockSpec((ROW_TILE // tile, ROW_TILE, tile), lambda i: (i, 0, 0))
    vt_shape = lambda tile: jax.ShapeDtypeStruct((t // tile, ROW_TILE, tile), BF16)
    return pl.pallas_call(
        _ln_qkv_kernel,
        grid=(t // ROW_TILE,),
        in_specs=[
            pl.BlockSpec((ROW_TILE, d), lambda i: (i, 0)),
            pl.BlockSpec((1, d), lambda i: (0, 0)),
            pl.BlockSpec((1, d), lambda i: (0, 0)),
            pl.BlockSpec((d, n_out), lambda i: (0, 0)),
            pl.BlockSpec((1, n_out), lambda i: (0, 0)),
        ],
        out_specs=[pl.BlockSpec((1, ROW_TILE, n_out),
                                lambda i: (i // steps_per_seq, i % steps_per_seq, 0)),
                   vt_spec(DIFF_TILE), vt_spec(BAND_TILE)],
        out_shape=[jax.ShapeDtypeStruct((t // seq, seq, n_out), BF16),
                   vt_shape(DIFF_TILE), vt_shape(BAND_TILE)],
        compiler_params=pltpu.CompilerParams(
            dimension_semantics=("arbitrary",), vmem_limit_bytes=VMEM_LIMIT),
        name="ln_qkv",
    )(x2, g, b, w_bf, col_scale)


def _diff_attn_kernel(c_ref, lam_ref, g_ref, boff_ref, bdiag_ref, q_ref, k_ref, vt_ref, o_ref, s_scr):
    tq = q_ref.shape[1]
    tk = vt_ref.shape[2]
    h = pl.program_id(1)
    i = pl.program_id(2)
    c = c_ref[h]
    q = q_ref[0]
    lane = lax.broadcasted_iota(jnp.int32, q.shape, 1)
    zero = jnp.zeros_like(q)
    qq = jnp.concatenate([jnp.where(lane < A_HEAD_DIM, q, zero),
                          jnp.where(lane >= A_HEAD_DIM, q, zero)], axis=0)
    qqt = qq.astype(F32).T.astype(BF16)
    n_before = i * (tq // tk)

    def scores(j):
        kb = k_ref[0, pl.ds(pl.multiple_of(j * tk, tk), tk), :]
        return jnp.dot(kb, qqt, preferred_element_type=F32)

    def update(slot, table_ref, j, carry):
        m, l, acc = carry
        s = s_scr[slot] + table_ref[0]
        shift = c * (j * tk).astype(F32)
        m_new = jnp.maximum(m, jnp.max(s, axis=0, keepdims=True) + shift)
        alpha = jnp.exp2(m - m_new)
        p = jnp.exp2(s - (m_new - shift))
        l = alpha * l + jnp.sum(p, axis=0, keepdims=True)
        acc = alpha * acc + jnp.dot(vt_ref[j], p.astype(BF16), preferred_element_type=F32)
        return m_new, l, acc

    s_scr[0] = scores(0)

    def pair(t, carry):
        j = 2 * t
        s_scr[1] = scores(j + 1)
        carry = update(0, boff_ref, j, carry)
        s_scr[0] = scores(j + 2)
        return update(1, boff_ref, j + 1, carry)

    def quad(t, carry):
        return pair(2 * t + 1, pair(2 * t, carry))

    init = (jnp.full((1, 2 * tq), NEG, F32), jnp.zeros((1, 2 * tq), F32),
            jnp.zeros((LANES, 2 * tq), F32))
    carry = lax.fori_loop(0, n_before // 4, quad, init)
    carry = lax.fori_loop(n_before // 4 * 2, n_before // 2, pair, carry)

    def odd_tail(carry):
        s_scr[1] = scores(n_before)
        carry = update(0, boff_ref, n_before - 1, carry)
        return update(1, bdiag_ref, n_before, carry)

    def even_tail(carry):
        return update(0, bdiag_ref, n_before, carry)

    _, l, acc = lax.cond(n_before % 2 == 1, odd_tail, even_tail, carry)

    o_all = acc / l
    lv = lam_ref[...]
    lam = (jnp.exp(jnp.sum(lv[0:1] * lv[1:2], axis=1, keepdims=True))
           - jnp.exp(jnp.sum(lv[2:3] * lv[3:4], axis=1, keepdims=True)) + LAM_INIT)
    o = o_all[:, :tq] - lam * o_all[:, tq:]
    ms = jnp.mean(o * o, axis=0, keepdims=True)
    o = o * lax.rsqrt(ms + RMS_EPS) * (g_ref[...] * (1.0 - LAM_INIT))
    o_ref[0] = o.T.astype(BF16)


def _diff_attn(proj3, vt3, lam4, subln_g):
    bsz, seq, _ = proj3.shape
    tq = tk = DIFF_TILE
    c = jnp.asarray([2.0 ** (-8.0 * (h + 1) / A_HEADS) for h in range(A_HEADS)], F32) * LOG2E
    r = jnp.arange(tk, dtype=jnp.int32)[:, None]
    qrel = jnp.arange(2 * tq, dtype=jnp.int32)[None, :] % tq
    boff = c[:, None, None] * jnp.broadcast_to(r, (tk, 2 * tq)).astype(F32)
    allowed = (r // CHUNK) <= (qrel // CHUNK)
    bdiag = jnp.where(allowed, c[:, None, None] * (qrel - jnp.abs(qrel - r)).astype(F32), NEG)
    kblk = A_WIDTH // LANES
    return pl.pallas_call(
        _diff_attn_kernel,
        grid=(bsz, A_HEADS, seq // tq),
        in_specs=[
            pl.BlockSpec(memory_space=pltpu.SMEM),
            pl.BlockSpec((4, A_HEAD_DIM), lambda b, h, i: (0, 0)),
            pl.BlockSpec((LANES, 1), lambda b, h, i: (0, 0)),
            pl.BlockSpec((1, tk, 2 * tq), lambda b, h, i: (h, 0, 0)),
            pl.BlockSpec((1, tk, 2 * tq), lambda b, h, i: (h, 0, 0)),
            pl.BlockSpec((1, tq, LANES), lambda b, h, i: (b, i, h)),
            pl.BlockSpec((1, seq, LANES), lambda b, h, i: (b, 0, kblk + h)),
            pl.BlockSpec((seq // tk, LANES, tk), lambda b, h, i: (b, h, 0)),
        ],
        out_specs=pl.BlockSpec((1, tq, LANES), lambda b, h, i: (b, i, h)),
        out_shape=jax.ShapeDtypeStruct((bsz, seq, A_WIDTH), BF16),
        scratch_shapes=[pltpu.VMEM((2, tk, 2 * tq), F32)],
        compiler_params=pltpu.CompilerParams(
            dimension_semantics=("arbitrary", "arbitrary", "arbitrary"),
            vmem_limit_bytes=VMEM_LIMIT),
        name="diff_attn",
    )(c, lam4, subln_g.reshape(LANES, 1), boff, bdiag, proj3, proj3, vt3)


def _band_attn_kernel(bias_ref, q_ref, k_ref, vt_ref, o_ref, s_scr):
    i = pl.program_id(2)
    tq = BAND_TILE
    units = q_ref.shape[1] // (BAND_PAIR * tq)
    key_tiles = BAND_TILES + BAND_PAIR - 1
    row = lax.broadcasted_iota(jnp.int32, (LANES, tq), 0)

    def key_tile(n, t):
        jt = (i * units + n) * BAND_PAIR - (BAND_TILES - 1) + t
        return jnp.maximum(jt, 0), jt < 0

    def scores(n):
        qs = []
        for a in range(BAND_PAIR):
            q = q_ref[0, (n * BAND_PAIR + a) * tq:(n * BAND_PAIR + a + 1) * tq, :]
            lane = lax.broadcasted_iota(jnp.int32, q.shape, 1)
            zero = jnp.zeros_like(q)
            qs += [jnp.where(lane < B_HEAD_DIM, q, zero), jnp.where(lane >= B_HEAD_DIM, q, zero)]
        qqt = jnp.concatenate(qs, axis=0).astype(F32).T.astype(BF16)
        ks = [k_ref[0, pl.ds(pl.multiple_of(key_tile(n, t)[0] * tq, tq), tq), :] for t in range(key_tiles)]
        return jnp.dot(jnp.concatenate(ks, axis=0), qqt, preferred_element_type=F32)

    def finish(n):
        bias = [bias_ref[0, jnp.where(key_tile(n, t)[1], 1, 0), t] for t in range(key_tiles)]
        s = s_scr[n % 2] + jnp.concatenate(bias, axis=0)
        m = jnp.max(s, axis=0, keepdims=True)
        p = jnp.exp2(s - m)
        l = jnp.sum(p, axis=0, keepdims=True)
        vt = jnp.concatenate([vt_ref[key_tile(n, t)[0]] for t in range(key_tiles)], axis=1)
        o = jnp.dot(vt, p.astype(BF16), preferred_element_type=F32) / l
        for a in range(BAND_PAIR):
            oa = jnp.where(row < B_HEAD_DIM, o[:, 2 * a * tq:(2 * a + 1) * tq],
                           o[:, (2 * a + 1) * tq:(2 * a + 2) * tq])
            o_ref[0, (n * BAND_PAIR + a) * tq:(n * BAND_PAIR + a + 1) * tq, :] = oa.T.astype(BF16)

    s_scr[0] = scores(0)
    for n in range(units):
        if n + 1 < units:
            s_scr[(n + 1) % 2] = scores(n + 1)
        finish(n)


def _band_bias(rel_bias, tq):
    past = B_PAST_CHUNKS * CHUNK
    band = tq + past
    assert tq - 1 <= REL_CLIP <= past
    qi = jnp.arange(tq)
    kj = jnp.arange(band)
    cq = qi[:, None] // CHUNK
    ck = kj[None, :] // CHUNK
    allowed = (ck >= cq) & (ck <= cq + B_PAST_CHUNKS)
    tab = rel_bias.astype(F32) * LOG2E
    n_diag = band + tq - 1
    n_unclipped = REL_CLIP + tq
    w = jnp.concatenate([tab[:, REL_CLIP - (tq - 1):],
                         jnp.broadcast_to(tab[:, -1:], (B_HEADS, n_diag - n_unclipped))], axis=1)
    shifted = jnp.tile(w, (1, tq + 1))[:, :tq * (n_diag + 1)].reshape(B_HEADS, tq, n_diag + 1)
    bias = jnp.flip(shifted[:, :, :band], axis=2)
    return jnp.where(allowed[None], bias, NEG)


def _band_attn(proj3, vtb3, rel_bias):
    bsz, seq, _ = proj3.shape
    tq = BAND_TILE
    groups = B_HEADS // 2
    key_tiles = BAND_TILES + BAND_PAIR - 1
    cols = BAND_PAIR * 2 * tq
    bias = _band_bias(rel_bias, tq)
    bias = bias.reshape(groups, 2, tq, BAND_TILES, tq)
    masked = jnp.full((groups, 2, tq, 1, tq), NEG, F32)
    per_tile = [jnp.concatenate([masked] * a + [bias] + [masked] * (BAND_PAIR - 1 - a), axis=3)
                for a in range(BAND_PAIR)]
    bias = jnp.stack(per_tile, axis=1)
    bias = bias.transpose(0, 4, 5, 1, 2, 3).reshape(groups, 1, key_tiles, tq, cols)
    bias = jnp.concatenate([bias, jnp.full_like(bias, NEG)], axis=1)
    qblk = 3 * A_WIDTH // LANES
    kblk = qblk + B_WIDTH // LANES
    rows = min(BAND_STEP_ROWS, seq)
    return pl.pallas_call(
        _band_attn_kernel,
        grid=(bsz, groups, seq // rows),
        in_specs=[
            pl.BlockSpec((1, 2, key_tiles, tq, cols), lambda b, g, i: (g, 0, 0, 0, 0)),
            pl.BlockSpec((1, rows, LANES), lambda b, g, i: (b, i, qblk + g)),
            pl.BlockSpec((1, seq, LANES), lambda b, g, i: (b, 0, kblk + g)),
            pl.BlockSpec((seq // tq, LANES, tq), lambda b, g, i: (b, g, 0)),
        ],
        out_specs=pl.BlockSpec((1, rows, LANES), lambda b, g, i: (b, i, g)),
        out_shape=jax.ShapeDtypeStruct((bsz, seq, B_WIDTH), BF16),
        scratch_shapes=[pltpu.VMEM((2, key_tiles * tq, cols), F32)],
        compiler_params=pltpu.CompilerParams(
            dimension_semantics=("arbitrary", "arbitrary", "arbitrary"),
            vmem_limit_bytes=VMEM_LIMIT),
        name="band_attn",
    )(bias, proj3, proj3, vtb3)


def _out_router_kernel(x_ref, gi_ref, bi_ref, oa_ref, ob_ref, wa_ref, wb_ref, g1_ref, b1_ref,
                       rwt_ref, rb_ref, tri_ref, h1_ref, idx_ref, gate_ref, rank_ref, cnt_ref,
                       carry_ref, *, steps_per_tile):
    h = _layer_norm(x_ref[...], gi_ref[...], bi_ref[...])
    mix = (jnp.dot(oa_ref[0], wa_ref[...], preferred_element_type=F32)
           + jnp.dot(ob_ref[0], wb_ref[...], preferred_element_type=F32))
    h1 = _layer_norm(DEEPNORM_ALPHA * h + mix, g1_ref[...], b1_ref[...])
    nchunk = h1.shape[1] // LANES
    for c in range(nchunk):
        h1_ref[pl.ds(c, ROW_TILE, stride=nchunk), :] = h1[:, c * LANES:(c + 1) * LANES]
    lt = lax.dot_general(rwt_ref[...], h1, (((1,), (1,)), ((), ())),
                         precision=lax.Precision.HIGHEST, preferred_element_type=F32)
    lt = lt + rb_ref[...]
    eidx = lax.broadcasted_iota(jnp.int32, lt.shape, 0)
    vals, idxs, hots = [], [], []
    for _ in range(TOP_K):
        mx = jnp.max(lt, axis=0, keepdims=True)
        am = jnp.min(jnp.where(lt == mx, eidx, N_EXPERTS), axis=0, keepdims=True)
        hit = eidx == am
        vals.append(mx)
        idxs.append(am)
        hots.append(jnp.where(hit, 1.0, 0.0))
        lt = jnp.where(hit, -jnp.inf, lt)
    ex = [jnp.exp(v - vals[0]) for v in vals]
    den = ex[0] + ex[1] + ex[2] + ex[3]
    idx_ref[...] = jnp.concatenate(idxs, axis=0)
    gate_ref[...] = jnp.concatenate([e / den for e in ex], axis=0)

    @pl.when(pl.program_id(0) % steps_per_tile == 0)
    def _():
        carry_ref[...] = jnp.zeros_like(carry_ref)

    hot = (hots[0] + hots[1]) + (hots[2] + hots[3])
    before = jnp.dot(hot.astype(BF16), tri_ref[...], preferred_element_type=F32) + carry_ref[...]
    rank_ref[...] = jnp.concatenate(
        [jnp.sum(hk * before, axis=0, keepdims=True) for hk in hots], axis=0).astype(jnp.int32)
    total = carry_ref[...] + jnp.sum(hot, axis=1, keepdims=True)
    carry_ref[...] = total
    cnt_ref[0] = total.astype(jnp.int32)


def _out_router(x2, gi, bi, o_a, o_b, wa, wb, g1, b1, rwt, rb, moe_tile):
    t, d = x2.shape
    nchunk = d // LANES
    row = lambda i: (i, 0)
    fixed = lambda i: (0, 0)
    steps_per_seq = o_a.shape[1] // ROW_TILE
    seq_row = lambda i: (i // steps_per_seq, i % steps_per_seq, 0)
    tri = jnp.triu(jnp.ones((ROW_TILE, ROW_TILE), BF16), k=1)
    return pl.pallas_call(
        functools.partial(_out_router_kernel, steps_per_tile=moe_tile // ROW_TILE),
        grid=(t // ROW_TILE,),
        in_specs=[
            pl.BlockSpec((ROW_TILE, d), row),
            pl.BlockSpec((1, d), fixed),
            pl.BlockSpec((1, d), fixed),
            pl.BlockSpec((1, ROW_TILE, A_WIDTH), seq_row),
            pl.BlockSpec((1, ROW_TILE, B_WIDTH), seq_row),
            pl.BlockSpec((A_WIDTH, d), fixed),
            pl.BlockSpec((B_WIDTH, d), fixed),
            pl.BlockSpec((1, d), fixed),
            pl.BlockSpec((1, d), fixed),
            pl.BlockSpec((N_EXPERTS, d), fixed),
            pl.BlockSpec((N_EXPERTS, 1), fixed),
            pl.BlockSpec((ROW_TILE, ROW_TILE), fixed),
        ],
        out_specs=[
            pl.BlockSpec((ROW_TILE * nchunk, LANES), row),
            pl.BlockSpec((TOP_K, ROW_TILE), lambda i: (0, i)),
            pl.BlockSpec((TOP_K, ROW_TILE), lambda i: (0, i)),
            pl.BlockSpec((TOP_K, ROW_TILE), lambda i: (0, i)),
            pl.BlockSpec((1, N_EXPERTS, 1), lambda i: (i, 0, 0)),
        ],
        out_shape=[
            jax.ShapeDtypeStruct((t * nchunk, LANES), F32),
            jax.ShapeDtypeStruct((TOP_K, t), jnp.int32),
            jax.ShapeDtypeStruct((TOP_K, t), F32),
            jax.ShapeDtypeStruct((TOP_K, t), jnp.int32),
            jax.ShapeDtypeStruct((t // ROW_TILE, N_EXPERTS, 1), jnp.int32),
        ],
        scratch_shapes=[pltpu.VMEM((N_EXPERTS, 1), F32)],
        compiler_params=pltpu.CompilerParams(
            dimension_semantics=("arbitrary",), vmem_limit_bytes=VMEM_LIMIT),
        name="out_router",
    )(x2, gi, bi, o_a, o_b, wa, wb, g1, b1, rwt, rb, tri)


def _split_w1_kernel(w_ref, perm_ref, g_ref, l_ref):
    rows = w_ref.shape[1]
    even = (lax.broadcasted_iota(jnp.int32, (rows, LANES), 1) & 1) == 0
    perm = perm_ref[...]
    for g in range(w_ref.shape[2] // (2 * LANES)):
        v0 = w_ref[0, :, 2 * g * LANES:(2 * g + 1) * LANES]
        v1 = w_ref[0, :, (2 * g + 1) * LANES:(2 * g + 2) * LANES]
        glu = jnp.where(even, v0, pltpu.roll(v1, 1, axis=1)).astype(BF16)
        lin = jnp.where(even, pltpu.roll(v0, LANES - 1, axis=1), v1).astype(BF16)
        g_ref[0, :, g * LANES:(g + 1) * LANES] = jnp.dot(
            glu, perm, preferred_element_type=F32).astype(BF16)
        l_ref[0, :, g * LANES:(g + 1) * LANES] = jnp.dot(
            lin, perm, preferred_element_type=F32).astype(BF16)


def _split_w1(w1e):
    e, d, f2 = w1e.shape
    rows = 256
    half = LANES // 2
    unit = jnp.arange(LANES)
    perm = (jnp.arange(LANES)[:, None] == (2 * (unit % half) + unit // half)[None, :]).astype(BF16)
    return pl.pallas_call(
        _split_w1_kernel,
        grid=(e, d // rows),
        in_specs=[pl.BlockSpec((1, rows, f2), lambda i, j: (i, j, 0)),
                  pl.BlockSpec((LANES, LANES), lambda i, j: (0, 0))],
        out_specs=[pl.BlockSpec((1, rows, f2 // 2), lambda i, j: (i, j, 0)),
                   pl.BlockSpec((1, rows, f2 // 2), lambda i, j: (i, j, 0))],
        out_shape=[jax.ShapeDtypeStruct((e, d, f2 // 2), BF16),
                   jax.ShapeDtypeStruct((e, d, f2 // 2), BF16)],
        compiler_params=pltpu.CompilerParams(
            dimension_semantics=("arbitrary", "arbitrary"), vmem_limit_bytes=VMEM_LIMIT),
        name="split_w1",
    )(w1e, perm)


def _moe_kernel(cnt_ref, offs_ref,
                w1g_ref, w1l_ref, b1g_ref, b1l_ref, w2_ref, b2_ref, pos_hbm, gate_hbm, h_hbm,
                y_hbm, hbuf, ybuf, stage, list_tok, list_gate, pos_in, gate_in, n_sorted, sem, *, tile):
    b = pl.program_id(0)
    e = pl.program_id(1)
    n_tiles = pl.num_programs(0)
    n_exp = pl.num_programs(1)
    d_model = w1g_ref.shape[1]
    d_ff = w1g_ref.shape[2]
    nchunk = d_model // LANES
    tile_rows = tile * nchunk
    pairs = tile * TOP_K
    n_sort_chunks = pairs // SORT_CHUNK
    half = (b % 2) * pairs
    other_half = pairs - half

    def stage_rows(r):
        return pl.ds(pl.multiple_of(r * nchunk, nchunk), nchunk)

    def listed_rows(p):
        return pl.ds(pl.multiple_of(list_tok[p], nchunk), nchunk)

    def sort_pair(j, first_row, dst_half):
        p = pos_in[j] + dst_half
        list_tok[p] = first_row
        list_gate[p] = gate_in[j]

    def sort_chunks(lo, hi, dst_half):
        def body(jo, c):
            for u in range(SORT_UNROLL):
                j = jo * SORT_UNROLL + u
                sort_pair(j, (j & (tile - 1)) * nchunk, dst_half)
            return c
        lax.fori_loop(lo * (SORT_CHUNK // SORT_UNROLL), hi * (SORT_CHUNK // SORT_UNROLL), body, 0)

    def load_pairs(t):
        rows = pl.ds(pl.multiple_of(t * pairs, pairs), pairs)
        copies = [pltpu.make_async_copy(pos_hbm.at[rows], pos_in, sem.at[2]),
                  pltpu.make_async_copy(gate_hbm.at[rows], gate_in, sem.at[3])]
        for cp in copies:
            cp.start()
        for cp in copies:
            cp.wait()

    @pl.when(jnp.logical_and(b == 0, e == 0))
    def _():
        stage[...] = jnp.zeros_like(stage)

        def pad_body(j, c):
            list_tok[pairs + j] = 0
            list_tok[2 * pairs + j] = 0
            return c
        lax.fori_loop(0, LIST_PAD, pad_body, 0)
        load_pairs(0)
        n_sorted[0] = 0

    @pl.when(e == 0)
    def _():
        src = h_hbm.at[pl.ds(pl.multiple_of(b * tile_rows, tile_rows), tile_rows), :]
        load = pltpu.make_async_copy(src, hbuf, sem.at[0])
        load.start()
        ybuf[...] = jnp.zeros_like(ybuf)
        sort_chunks(jnp.minimum(n_sorted[0], n_sort_chunks), n_sort_chunks, half)
        n_sorted[0] = 0

        @pl.when(b + 1 < n_tiles)
        def _():
            load_pairs(b + 1)
        load.wait()

    n = cnt_ref[b * n_exp + e]
    base = offs_ref[b * n_exp + e] + half

    def run_block(m_rows, p0, nrows):
        def gather_body(ci, c):
            for u in range(ROW_UNROLL):
                r = ci * ROW_UNROLL + u
                stage[stage_rows(r), :] = hbuf[listed_rows(p0 + r), :]
            return c
        lax.fori_loop(0, (nrows + ROW_UNROLL - 1) // ROW_UNROLL, gather_body, 0)

        if m_rows == MOE_BLOCK:
            chunk = jnp.minimum(n_sorted[0], n_sort_chunks - 1)
            n_sorted[0] = n_sorted[0] + 1
            j0 = chunk * SORT_CHUNK
            row0 = (j0 & (tile - 1)) * nchunk
            for u in range(SORT_CHUNK):
                sort_pair(j0 + u, row0 + u * nchunk, other_half)

        x = jnp.concatenate(
            [stage[pl.ds(c, m_rows, stride=nchunk), :] for c in range(nchunk)],
            axis=1).astype(BF16)
        y = b2_ref[0]
        for hh in range(d_ff // FF_CHUNK):
            sl = slice(hh * FF_CHUNK, (hh + 1) * FF_CHUNK)
            hg = jnp.dot(x, w1g_ref[0, :, sl], preferred_element_type=F32) + b1g_ref[0, :, sl]
            hl = jnp.dot(x, w1l_ref[0, :, sl], preferred_element_type=F32) + b1l_ref[0, :, sl]
            xg = jnp.minimum(hg, SWIGLU_LIMIT)
            xl = jnp.clip(hl, -SWIGLU_LIMIT, SWIGLU_LIMIT)
            act = xg * jax.nn.sigmoid(SWIGLU_ALPHA * xg) * (xl + 1.0)
            y = y + jnp.dot(act.astype(BF16), w2_ref[0, sl, :], preferred_element_type=F32)
        for c in range(nchunk):
            stage[pl.ds(c, m_rows, stride=nchunk), :] = y[:, c * LANES:(c + 1) * LANES]

        def add_body(ci, c):
            rows, vals = [], []
            for u in range(SUBLANES):
                r = ci * SUBLANES + u
                dst = listed_rows(p0 + r)
                rows.append(dst)
                vals.append(ybuf[dst, :] + list_gate[p0 + r] * stage[stage_rows(r), :])
            for dst, val in zip(rows, vals):
                ybuf[dst, :] = val
            return c
        n_full = nrows // SUBLANES
        lax.fori_loop(0, n_full, add_body, 0)

        def add_tail(r, c):
            dst = listed_rows(p0 + r)
            ybuf[dst, :] = ybuf[dst, :] + list_gate[p0 + r] * stage[stage_rows(r), :]
            return c
        lax.fori_loop(n_full * SUBLANES, nrows, add_tail, 0)

    n_big = n // MOE_BLOCK
    rem = n - n_big * MOE_BLOCK
    rem_is_big = rem > MOE_BLOCK - SMALL_BLOCK
    n_big_blocks = n_big + rem_is_big.astype(jnp.int32)
    n_small_blocks = jnp.where(rem_is_big, 0, (rem + SMALL_BLOCK - 1) // SMALL_BLOCK)

    def big_body(s, c):
        run_block(MOE_BLOCK, base + s * MOE_BLOCK, jnp.minimum(MOE_BLOCK, n - s * MOE_BLOCK))
        return c
    lax.fori_loop(0, n_big_blocks, big_body, 0)

    def small_body(s, c):
        run_block(SMALL_BLOCK, base + n_big * MOE_BLOCK + s * SMALL_BLOCK,
                  jnp.minimum(SMALL_BLOCK, rem - s * SMALL_BLOCK))
        return c
    lax.fori_loop(0, n_small_blocks, small_body, 0)

    @pl.when(e == n_exp - 1)
    def _():
        dst = y_hbm.at[pl.ds(pl.multiple_of(b * tile_rows, tile_rows), tile_rows), :]
        store = pltpu.make_async_copy(ybuf, dst, sem.at[1])
        store.start()
        store.wait()


def _moe(h1t, cnt, offs, pos_flat, gate_flat, w1g, w1l, b1g, b1l, w2, b2, tile):
    n_exp, d, f = w1g.shape
    nchunk = d // LANES
    n_tiles = h1t.shape[0] // (tile * nchunk)
    pairs = tile * TOP_K
    assert tile % SORT_CHUNK == 0
    expert = lambda b, e, cnt, offs: (e, 0, 0)
    grid_spec = pltpu.PrefetchScalarGridSpec(
        num_scalar_prefetch=2,
        grid=(n_tiles, n_exp),
        in_specs=[
            pl.BlockSpec((1, d, f), expert),
            pl.BlockSpec((1, d, f), expert),
            pl.BlockSpec((1, 1, f), expert),
            pl.BlockSpec((1, 1, f), expert),
            pl.BlockSpec((1, f, d), expert),
            pl.BlockSpec((1, 1, d), expert),
            pl.BlockSpec(memory_space=pl.ANY),
            pl.BlockSpec(memory_space=pl.ANY),
            pl.BlockSpec(memory_space=pl.ANY),
        ],
        out_specs=pl.BlockSpec(memory_space=pl.ANY),
        scratch_shapes=[
            pltpu.VMEM((tile * nchunk, LANES), F32),
            pltpu.VMEM((tile * nchunk, LANES), F32),
            pltpu.VMEM((MOE_BLOCK * nchunk, LANES), F32),
            pltpu.SMEM((2 * pairs + LIST_PAD,), jnp.int32),
            pltpu.SMEM((2 * pairs + LIST_PAD,), F32),
            pltpu.SMEM((pairs,), jnp.int32),
            pltpu.SMEM((pairs,), F32),
            pltpu.SMEM((1,), jnp.int32),
            pltpu.SemaphoreType.DMA((4,)),
        ],
    )
    return pl.pallas_call(
        functools.partial(_moe_kernel, tile=tile),
        grid_spec=grid_spec,
        out_shape=jax.ShapeDtypeStruct(h1t.shape, F32),
        compiler_params=pltpu.CompilerParams(
            dimension_semantics=("arbitrary", "arbitrary"), vmem_limit_bytes=MOE_VMEM_LIMIT),
        name="moe",
    )(cnt, offs, w1g, w1l, b1g, b1l, w2, b2, pos_flat, gate_flat, h1t)


def _combine_kernel(h1_ref, y_ref, g_ref, b_ref, o_ref):
    nchunk = o_ref.shape[1] // LANES
    pieces = [DEEPNORM_ALPHA * h1_ref[pl.ds(c, ROW_TILE, stride=nchunk), :]
              + y_ref[pl.ds(c, ROW_TILE, stride=nchunk), :] for c in range(nchunk)]
    o_ref[...] = _layer_norm(jnp.concatenate(pieces, axis=1), g_ref[...], b_ref[...])


def _combine(h1t, yt, g2, b2):
    d = g2.shape[1]
    nchunk = d // LANES
    t = h1t.shape[0] // nchunk
    tiles = pl.BlockSpec((ROW_TILE * nchunk, LANES), lambda i: (i, 0))
    return pl.pallas_call(
        _combine_kernel,
        grid=(t // ROW_TILE,),
        in_specs=[tiles, tiles,
                  pl.BlockSpec((1, d), lambda i: (0, 0)),
                  pl.BlockSpec((1, d), lambda i: (0, 0))],
        out_specs=pl.BlockSpec((ROW_TILE, d), lambda i: (i, 0)),
        out_shape=jax.ShapeDtypeStruct((t, d), F32),
        compiler_params=pltpu.CompilerParams(
            dimension_semantics=("arbitrary",), vmem_limit_bytes=VMEM_LIMIT),
        name="combine",
    )(h1t, yt, g2, b2)


def kernel(x, ln_in_g, ln_in_b, w_in, lambda_q1, lambda_k1, lambda_q2, lambda_k2, subln_g, rel_bias,
           w_out, ln1_g, ln1_b, router_w, router_b, w1, b1, w2, b2, ln2_g, ln2_b):
    bsz, seq, d = x.shape
    t = bsz * seq
    x2 = x.reshape(t, d)
    row = lambda v: v.reshape(1, -1).astype(F32)

    qs = A_HEAD_DIM ** -0.5 * LOG2E
    col_scale = jnp.concatenate([
        jnp.full((A_WIDTH,), qs, F32), jnp.ones((2 * A_WIDTH,), F32),
        jnp.full((B_WIDTH,), B_HEAD_DIM ** -0.5 * LOG2E, F32), jnp.ones((2 * B_WIDTH,), F32)]).reshape(1, -1)
    proj3, vt3, vtb3 = _ln_qkv(x2, row(ln_in_g), row(ln_in_b), w_in[0].astype(BF16), col_scale, seq)

    lam4 = jnp.stack([lambda_q1[0], lambda_k1[0], lambda_q2[0], lambda_k2[0]]).astype(F32)
    o_a = _diff_attn(proj3, vt3, lam4, subln_g[0].astype(F32))
    o_b = _band_attn(proj3, vtb3, rel_bias[0])

    w_o = w_out[0].astype(BF16)
    tile = min(MOE_TILE, t)
    n_tiles = t // tile
    h1t, top_idx, gates, rank, run_cnt = _out_router(
        x2, row(ln_in_g), row(ln_in_b), o_a, o_b,
        w_o[:A_WIDTH], w_o[A_WIDTH:], row(ln1_g[0]), row(ln1_b[0]),
        router_w[0].T.astype(F32), router_b[0].reshape(-1, 1).astype(F32), tile)

    steps = tile // ROW_TILE
    cnt = run_cnt[steps - 1::steps, :, 0]
    offs = jnp.cumsum(cnt, axis=1) - cnt
    hot = top_idx.reshape(TOP_K, n_tiles, tile, 1) == jnp.arange(N_EXPERTS, dtype=jnp.int32)
    pos = rank + jnp.sum(jnp.where(hot, offs[None, :, None, :], 0), axis=-1).reshape(TOP_K, t)
    per_tile = lambda a: a.reshape(TOP_K, n_tiles, tile).transpose(1, 0, 2).reshape(-1)
    w1g, w1l = _split_w1(w1[0])
    b1e = b1[0].astype(F32)[:, None, :]
    yt = _moe(h1t, cnt.reshape(-1), offs.reshape(-1), per_tile(pos), per_tile(gates),
              w1g, w1l, b1e[:, :, 0::2], b1e[:, :, 1::2],
              w2[0].astype(BF16), b2[0][:, None, :].astype(F32), tile)
    out = _combine(h1t, yt, row(ln2_g[0]), row(ln2_b[0]))
    return out.reshape(bsz, seq, d)
```

```python
import functools
import math

import jax
import jax.numpy as jnp
from jax import lax
from jax.experimental import pallas as pl
from jax.experimental.pallas import tpu as pltpu

F32 = jnp.float32
BF16 = jnp.bfloat16

CHUNK = 64
A_HEADS = 4
A_HEAD_DIM = 64
A_WIDTH = A_HEADS * 2 * A_HEAD_DIM
B_HEADS = 8
B_HEAD_DIM = 64
B_WIDTH = B_HEADS * B_HEAD_DIM
B_PAST_CHUNKS = 8
REL_CLIP = 256
N_EXPERTS = 32
TOP_K = 4
SWIGLU_ALPHA = 1.702
SWIGLU_LIMIT = 7.0
MOE_BLOCK = 512
LN_EPS = 1e-5
RMS_EPS = 1e-5
DEPTH = 1
DEEPNORM_ALPHA = (2 * DEPTH) ** 0.25
LAM_INIT = 0.8 - 0.6 * math.exp(-0.3 * 0)

LOG2E = 1.4426950408889634
NEG = -1e30
LANES = 128
SUBLANES = 8
ROW_TILE = 512
VMEM_LIMIT = 48 * 1024 * 1024
DIFF_TILE = 256
BAND_TILE = 128
BAND_TILES = (BAND_TILE + B_PAST_CHUNKS * CHUNK) // BAND_TILE
BAND_PAIR = 2
BAND_STEP_ROWS = 1024
MOE_TILE = 4096
MOE_VMEM_LIMIT = 60 * 1024 * 1024
SMALL_BLOCK = 128
FF_CHUNK = 512
SORT_UNROLL = 8
SORT_CHUNK = 512
ROW_UNROLL = 16
LIST_PAD = 128


def _layer_norm(x, g, b):
    mu = jnp.mean(x, axis=-1, keepdims=True)
    xc = x - mu
    var = jnp.mean(xc * xc, axis=-1, keepdims=True)
    return xc * lax.rsqrt(var + LN_EPS) * g + b


def _ln_qkv_kernel(x_ref, g_ref, b_ref, w_ref, cs_ref, o_ref, vta_ref, vtb_ref):
    h = _layer_norm(x_ref[...], g_ref[...], b_ref[...])
    hb = h.astype(BF16)
    n_out = w_ref.shape[1]
    for c in range(n_out // ROW_TILE):
        sl = slice(c * ROW_TILE, (c + 1) * ROW_TILE)
        val = jnp.dot(hb, w_ref[:, sl], preferred_element_type=F32) * cs_ref[:, sl]
        o_ref[0, :, sl] = val.astype(BF16)
        for start, vt_ref in ((2 * A_WIDTH, vta_ref), (3 * A_WIDTH + 2 * B_WIDTH, vtb_ref)):
            if sl.start == start:
                tile = vt_ref.shape[2]
                for kt in range(ROW_TILE // tile):
                    vt_ref[kt] = val[kt * tile:(kt + 1) * tile, :].T.astype(BF16)


def _ln_qkv(x2, g, b, w_bf, col_scale, seq):
    assert A_WIDTH == ROW_TILE and B_WIDTH == ROW_TILE
    t, d = x2.shape
    n_out = w_bf.shape[1]
    steps_per_seq = seq // ROW_TILE
    vt_spec = lambda tile: pl.BlockSpec((ROW_TILE // tile, ROW_TILE, tile), lambda i: (i, 0, 0))
    vt_shape = lambda tile: jax.ShapeDtypeStruct((t // tile, ROW_TILE, tile), BF16)
    return pl.pallas_call(
        _ln_qkv_kernel,
        grid=(t // ROW_TILE,),
        in_specs=[
            pl.BlockSpec((ROW_TILE, d), lambda i: (i, 0)),
            pl.BlockSpec((1, d), lambda i: (0, 0)),
            pl.BlockSpec((1, d), lambda i: (0, 0)),
            pl.BlockSpec((d, n_out), lambda i: (0, 0)),
            pl.BlockSpec((1, n_out), lambda i: (0, 0)),
        ],
        out_specs=[pl.BlockSpec((1, ROW_TILE, n_out),
                                lambda i: (i // steps_per_seq, i % steps_per_seq, 0)),
                   vt_spec(DIFF_TILE), vt_spec(BAND_TILE)],
        out_shape=[jax.ShapeDtypeStruct((t // seq, seq, n_out), BF16),
                   vt_shape(DIFF_TILE), vt_shape(BAND_TILE)],
        compiler_params=pltpu.CompilerParams(
            dimension_semantics=("arbitrary",), vmem_limit_bytes=VMEM_LIMIT),
        name="ln_qkv",
    )(x2, g, b, w_bf, col_scale)


def _diff_attn_kernel(c_ref, lam_ref, g_ref, boff_ref, bdiag_ref, q_ref, k_ref, vt_ref, o_ref, s_scr):
    tq = q_ref.shape[1]
    tk = vt_ref.shape[2]
    h = pl.program_id(1)
    i = pl.program_id(2)
    c = c_ref[h]
    q = q_ref[0]
    lane = lax.broadcasted_iota(jnp.int32, q.shape, 1)
    zero = jnp.zeros_like(q)
    qq = jnp.concatenate([jnp.where(lane < A_HEAD_DIM, q, zero),
                          jnp.where(lane >= A_HEAD_DIM, q, zero)], axis=0)
    qqt = qq.astype(F32).T.astype(BF16)
    n_before = i * (tq // tk)

    def scores(j):
        kb = k_ref[0, pl.ds(pl.multiple_of(j * tk, tk), tk), :]
        return jnp.dot(kb, qqt, preferred_element_type=F32)

    def update(slot, table_ref, j, carry):
        m, l, acc = carry
        s = s_scr[slot] + table_ref[0]
        shift = c * (j * tk).astype(F32)
        m_new = jnp.maximum(m, jnp.max(s, axis=0, keepdims=True) + shift)
        alpha = jnp.exp2(m - m_new)
        p = jnp.exp2(s - (m_new - shift))
        l = alpha * l + jnp.sum(p, axis=0, keepdims=True)
        acc = alpha * acc + jnp.dot(vt_ref[j], p.astype(BF16), preferred_element_type=F32)
        return m_new, l, acc

    s_scr[0] = scores(0)

    def pair(t, carry):
        j = 2 * t
        s_scr[1] = scores(j + 1)
        carry = update(0, boff_ref, j, carry)
        s_scr[0] = scores(j + 2)
        return update(1, boff_ref, j + 1, carry)

    def quad(t, carry):
        return pair(2 * t + 1, pair(2 * t, carry))

    init = (jnp.full((1, 2 * tq), NEG, F32), jnp.zeros((1, 2 * tq), F32),
            jnp.zeros((LANES, 2 * tq), F32))
    carry = lax.fori_loop(0, n_before // 4, quad, init)
    carry = lax.fori_loop(n_before // 4 * 2, n_before // 2, pair, carry)

    def odd_tail(carry):
        s_scr[1] = scores(n_before)
        carry = update(0, boff_ref, n_before - 1, carry)
        return update(1, bdiag_ref, n_before, carry)

    def even_tail(carry):
        return update(0, bdiag_ref, n_before, carry)

    _, l, acc = lax.cond(n_before % 2 == 1, odd_tail, even_tail, carry)

    o_all = acc / l
    lv = lam_ref[...]
    lam = (jnp.exp(jnp.sum(lv[0:1] * lv[1:2], axis=1, keepdims=True))
           - jnp.exp(jnp.sum(lv[2:3] * lv[3:4], axis=1, keepdims=True)) + LAM_INIT)
    o = o_all[:, :tq] - lam * o_all[:, tq:]
    ms = jnp.mean(o * o, axis=0, keepdims=True)
    o = o * lax.rsqrt(ms + RMS_EPS) * (g_ref[...] * (1.0 - LAM_INIT))
    o_ref[0] = o.T.astype(BF16)


def _diff_attn(proj3, vt3, lam4, subln_g):
    bsz, seq, _ = proj3.shape
    tq = tk = DIFF_TILE
    c = jnp.asarray([2.0 ** (-8.0 * (h + 1) / A_HEADS) for h in range(A_HEADS)], F32) * LOG2E
    r = jnp.arange(tk, dtype=jnp.int32)[:, None]
    qrel = jnp.arange(2 * tq, dtype=jnp.int32)[None, :] % tq
    boff = c[:, None, None] * jnp.broadcast_to(r, (tk, 2 * tq)).astype(F32)
    allowed = (r // CHUNK) <= (qrel // CHUNK)
    bdiag = jnp.where(allowed, c[:, None, None] * (qrel - jnp.abs(qrel - r)).astype(F32), NEG)
    kblk = A_WIDTH // LANES
    return pl.pallas_call(
        _diff_attn_kernel,
        grid=(bsz, A_HEADS, seq // tq),
        in_specs=[
            pl.BlockSpec(memory_space=pltpu.SMEM),
            pl.BlockSpec((4, A_HEAD_DIM), lambda b, h, i: (0, 0)),
            pl.BlockSpec((LANES, 1), lambda b, h, i: (0, 0)),
            pl.BlockSpec((1, tk, 2 * tq), lambda b, h, i: (h, 0, 0)),
            pl.BlockSpec((1, tk, 2 * tq), lambda b, h, i: (h, 0, 0)),
            pl.BlockSpec((1, tq, LANES), lambda b, h, i: (b, i, h)),
            pl.BlockSpec((1, seq, LANES), lambda b, h, i: (b, 0, kblk + h)),
            pl.BlockSpec((seq // tk, LANES, tk), lambda b, h, i: (b, h, 0)),
        ],
        out_specs=pl.BlockSpec((1, tq, LANES), lambda b, h, i: (b, i, h)),
        out_shape=jax.ShapeDtypeStruct((bsz, seq, A_WIDTH), BF16),
        scratch_shapes=[pltpu.VMEM((2, tk, 2 * tq), F32)],
        compiler_params=pltpu.CompilerParams(
            dimension_semantics=("arbitrary", "arbitrary", "arbitrary"),
            vmem_limit_bytes=VMEM_LIMIT),
        name="diff_attn",
    )(c, lam4, subln_g.reshape(LANES, 1), boff, bdiag, proj3, proj3, vt3)


def _band_attn_kernel(bias_ref, q_ref, k_ref, vt_ref, o_ref, s_scr):
    i = pl.program_id(2)
    tq = BAND_TILE
    units = q_ref.shape[1] // (BAND_PAIR * tq)
    key_tiles = BAND_TILES + BAND_PAIR - 1
    row = lax.broadcasted_iota(jnp.int32, (LANES, tq), 0)

    def key_tile(n, t):
        jt = (i * units + n) * BAND_PAIR - (BAND_TILES - 1) + t
        return jnp.maximum(jt, 0), jt < 0

    def scores(n):
        qs = []
        for a in range(BAND_PAIR):
            q = q_ref[0, (n * BAND_PAIR + a) * tq:(n * BAND_PAIR + a + 1) * tq, :]
            lane = lax.broadcasted_iota(jnp.int32, q.shape, 1)
            zero = jnp.zeros_like(q)
            qs += [jnp.where(lane < B_HEAD_DIM, q, zero), jnp.where(lane >= B_HEAD_DIM, q, zero)]
        qqt = jnp.concatenate(qs, axis=0).astype(F32).T.astype(BF16)
        ks = [k_ref[0, pl.ds(pl.multiple_of(key_tile(n, t)[0] * tq, tq), tq), :] for t in range(key_tiles)]
        return jnp.dot(jnp.concatenate(ks, axis=0), qqt, preferred_element_type=F32)

    def finish(n):
        bias = [bias_ref[0, jnp.where(key_tile(n, t)[1], 1, 0), t] for t in range(key_tiles)]
        s = s_scr[n % 2] + jnp.concatenate(bias, axis=0)
        m = jnp.max(s, axis=0, keepdims=True)
        p = jnp.exp2(s - m)
        l = jnp.sum(p, axis=0, keepdims=True)
        vt = jnp.concatenate([vt_ref[key_tile(n, t)[0]] for t in range(key_tiles)], axis=1)
        o = jnp.dot(vt, p.astype(BF16), preferred_element_type=F32) / l
        for a in range(BAND_PAIR):
            oa = jnp.where(row < B_HEAD_DIM, o[:, 2 * a * tq:(2 * a + 1) * tq],
                           o[:, (2 * a + 1) * tq:(2 * a + 2) * tq])
            o_ref[0, (n * BAND_PAIR + a) * tq:(n * BAND_PAIR + a + 1) * tq, :] = oa.T.astype(BF16)

    s_scr[0] = scores(0)
    for n in range(units):
        if n + 1 < units:
            s_scr[(n + 1) % 2] = scores(n + 1)
        finish(n)


def _band_bias(rel_bias, tq):
    past = B_PAST_CHUNKS * CHUNK
    band = tq + past
    assert tq - 1 <= REL_CLIP <= past
    qi = jnp.arange(tq)
    kj = jnp.arange(band)
    cq = qi[:, None] // CHUNK
    ck = kj[None, :] // CHUNK
    allowed = (ck >= cq) & (ck <= cq + B_PAST_CHUNKS)
    tab = rel_bias.astype(F32) * LOG2E
    n_diag = band + tq - 1
    n_unclipped = REL_CLIP + tq
    w = jnp.concatenate([tab[:, REL_CLIP - (tq - 1):],
                         jnp.broadcast_to(tab[:, -1:], (B_HEADS, n_diag - n_unclipped))], axis=1)
    shifted = jnp.tile(w, (1, tq + 1))[:, :tq * (n_diag + 1)].reshape(B_HEADS, tq, n_diag + 1)
    bias = jnp.flip(shifted[:, :, :band], axis=2)
    return jnp.where(allowed[None], bias, NEG)


def _band_attn(proj3, vtb3, rel_bias):
    bsz, seq, _ = proj3.shape
    tq = BAND_TILE
    groups = B_HEADS // 2
    key_tiles = BAND_TILES + BAND_PAIR - 1
    cols = BAND_PAIR * 2 * tq
    bias = _band_bias(rel_bias, tq)
    bias = bias.reshape(groups, 2, tq, BAND_TILES, tq)
    masked = jnp.full((groups, 2, tq, 1, tq), NEG, F32)
    per_tile = [jnp.concatenate([masked] * a + [bias] + [masked] * (BAND_PAIR - 1 - a), axis=3)
                for a in range(BAND_PAIR)]
    bias = jnp.stack(per_tile, axis=1)
    bias = bias.transpose(0, 4, 5, 1, 2, 3).reshape(groups, 1, key_tiles, tq, cols)
    bias = jnp.concatenate([bias, jnp.full_like(bias, NEG)], axis=1)
    qblk = 3 * A_WIDTH // LANES
    kblk = qblk + B_WIDTH // LANES
    rows = min(BAND_STEP_ROWS, seq)
    return pl.pallas_call(
        _band_attn_kernel,
        grid=(bsz, groups, seq // rows),
        in_specs=[
            pl.BlockSpec((1, 2, key_tiles, tq, cols), lambda b, g, i: (g, 0, 0, 0, 0)),
            pl.BlockSpec((1, rows, LANES), lambda b, g, i: (b, i, qblk + g)),
            pl.BlockSpec((1, seq, LANES), lambda b, g, i: (b, 0, kblk + g)),
            pl.BlockSpec((seq // tq, LANES, tq), lambda b, g, i: (b, g, 0)),
        ],
        out_specs=pl.BlockSpec((1, rows, LANES), lambda b, g, i: (b, i, g)),
        out_shape=jax.ShapeDtypeStruct((bsz, seq, B_WIDTH), BF16),
        scratch_shapes=[pltpu.VMEM((2, key_tiles * tq, cols), F32)],
        compiler_params=pltpu.CompilerParams(
            dimension_semantics=("arbitrary", "arbitrary", "arbitrary"),
            vmem_limit_bytes=VMEM_LIMIT),
        name="band_attn",
    )(bias, proj3, proj3, vtb3)


def _out_router_kernel(x_ref, gi_ref, bi_ref, oa_ref, ob_ref, wa_ref, wb_ref, g1_ref, b1_ref,
                       rwt_ref, rb_ref, tri_ref, h1_ref, idx_ref, gate_ref, rank_ref, cnt_ref,
                       carry_ref, *, steps_per_tile):
    h = _layer_norm(x_ref[...], gi_ref[...], bi_ref[...])
    mix = (jnp.dot(oa_ref[0], wa_ref[...], preferred_element_type=F32)
           + jnp.dot(ob_ref[0], wb_ref[...], preferred_element_type=F32))
    h1 = _layer_norm(DEEPNORM_ALPHA * h + mix, g1_ref[...], b1_ref[...])
    nchunk = h1.shape[1] // LANES
    for c in range(nchunk):
        h1_ref[pl.ds(c, ROW_TILE, stride=nchunk), :] = h1[:, c * LANES:(c + 1) * LANES]
    lt = lax.dot_general(rwt_ref[...], h1, (((1,), (1,)), ((), ())),
                         precision=lax.Precision.HIGHEST, preferred_element_type=F32)
    lt = lt + rb_ref[...]
    eidx = lax.broadcasted_iota(jnp.int32, lt.shape, 0)
    vals, idxs, hots = [], [], []
    for _ in range(TOP_K):
        mx = jnp.max(lt, axis=0, keepdims=True)
        am = jnp.min(jnp.where(lt == mx, eidx, N_EXPERTS), axis=0, keepdims=True)
        hit = eidx == am
        vals.append(mx)
        idxs.append(am)
        hots.append(jnp.where(hit, 1.0, 0.0))
        lt = jnp.where(hit, -jnp.inf, lt)
    ex = [jnp.exp(v - vals[0]) for v in vals]
    den = ex[0] + ex[1] + ex[2] + ex[3]
    idx_ref[...] = jnp.concatenate(idxs, axis=0)
    gate_ref[...] = jnp.concatenate([e / den for e in ex], axis=0)

    @pl.when(pl.program_id(0) % steps_per_tile == 0)
    def _():
        carry_ref[...] = jnp.zeros_like(carry_ref)

    hot = (hots[0] + hots[1]) + (hots[2] + hots[3])
    before = jnp.dot(hot.astype(BF16), tri_ref[...], preferred_element_type=F32) + carry_ref[...]
    rank_ref[...] = jnp.concatenate(
        [jnp.sum(hk * before, axis=0, keepdims=True) for hk in hots], axis=0).astype(jnp.int32)
    total = carry_ref[...] + jnp.sum(hot, axis=1, keepdims=True)
    carry_ref[...] = total
    cnt_ref[0] = total.astype(jnp.int32)


def _out_router(x2, gi, bi, o_a, o_b, wa, wb, g1, b1, rwt, rb, moe_tile):
    t, d = x2.shape
    nchunk = d // LANES
    row = lambda i: (i, 0)
    fixed = lambda i: (0, 0)
    steps_per_seq = o_a.shape[1] // ROW_TILE
    seq_row = lambda i: (i // steps_per_seq, i % steps_per_seq, 0)
    tri = jnp.triu(jnp.ones((ROW_TILE, ROW_TILE), BF16), k=1)
    return pl.pallas_call(
        functools.partial(_out_router_kernel, steps_per_tile=moe_tile // ROW_TILE),
        grid=(t // ROW_TILE,),
        in_specs=[
            pl.BlockSpec((ROW_TILE, d), row),
            pl.BlockSpec((1, d), fixed),
            pl.BlockSpec((1, d), fixed),
            pl.BlockSpec((1, ROW_TILE, A_WIDTH), seq_row),
            pl.BlockSpec((1, ROW_TILE, B_WIDTH), seq_row),
            pl.BlockSpec((A_WIDTH, d), fixed),
            pl.BlockSpec((B_WIDTH, d), fixed),
            pl.BlockSpec((1, d), fixed),
            pl.BlockSpec((1, d), fixed),
            pl.BlockSpec((N_EXPERTS, d), fixed),
            pl.BlockSpec((N_EXPERTS, 1), fixed),
            pl.BlockSpec((ROW_TILE, ROW_TILE), fixed),
        ],
        out_specs=[
            pl.BlockSpec((ROW_TILE * nchunk, LANES), row),
            pl.BlockSpec((TOP_K, ROW_TILE), lambda i: (0, i)),
            pl.BlockSpec((TOP_K, ROW_TILE), lambda i: (0, i)),
            pl.BlockSpec((TOP_K, ROW_TILE), lambda i: (0, i)),
            pl.BlockSpec((1, N_EXPERTS, 1), lambda i: (i, 0, 0)),
        ],
        out_shape=[
            jax.ShapeDtypeStruct((t * nchunk, LANES), F32),
            jax.ShapeDtypeStruct((TOP_K, t), jnp.int32),
            jax.ShapeDtypeStruct((TOP_K, t), F32),
            jax.ShapeDtypeStruct((TOP_K, t), jnp.int32),
            jax.ShapeDtypeStruct((t // ROW_TILE, N_EXPERTS, 1), jnp.int32),
        ],
        scratch_shapes=[pltpu.VMEM((N_EXPERTS, 1), F32)],
        compiler_params=pltpu.CompilerParams(
            dimension_semantics=("arbitrary",), vmem_limit_bytes=VMEM_LIMIT),
        name="out_router",
    )(x2, gi, bi, o_a, o_b, wa, wb, g1, b1, rwt, rb, tri)


def _split_w1_kernel(w_ref, perm_ref, g_ref, l_ref):
    rows = w_ref.shape[1]
    even = (lax.broadcasted_iota(jnp.int32, (rows, LANES), 1) & 1) == 0
    perm = perm_ref[...]
    for g in range(w_ref.shape[2] // (2 * LANES)):
        v0 = w_ref[0, :, 2 * g * LANES:(2 * g + 1) * LANES]
        v1 = w_ref[0, :, (2 * g + 1) * LANES:(2 * g + 2) * LANES]
        glu = jnp.where(even, v0, pltpu.roll(v1, 1, axis=1)).astype(BF16)
        lin = jnp.where(even, pltpu.roll(v0, LANES - 1, axis=1), v1).astype(BF16)
        g_ref[0, :, g * LANES:(g + 1) * LANES] = jnp.dot(
            glu, perm, preferred_element_type=F32).astype(BF16)
        l_ref[0, :, g * LANES:(g + 1) * LANES] = jnp.dot(
            lin, perm, preferred_element_type=F32).astype(BF16)


def _split_w1(w1e):
    e, d, f2 = w1e.shape
    rows = 256
    half = LANES // 2
    unit = jnp.arange(LANES)
    perm = (jnp.arange(LANES)[:, None] == (2 * (unit % half) + unit // half)[None, :]).astype(BF16)
    return pl.pallas_call(
        _split_w1_kernel,
        grid=(e, d // rows),
        in_specs=[pl.BlockSpec((1, rows, f2), lambda i, j: (i, j, 0)),
                  pl.BlockSpec((LANES, LANES), lambda i, j: (0, 0))],
        out_specs=[pl.BlockSpec((1, rows, f2 // 2), lambda i, j: (i, j, 0)),
                   pl.BlockSpec((1, rows, f2 // 2), lambda i, j: (i, j, 0))],
        out_shape=[jax.ShapeDtypeStruct((e, d, f2 // 2), BF16),
                   jax.ShapeDtypeStruct((e, d, f2 // 2), BF16)],
        compiler_params=pltpu.CompilerParams(
            dimension_semantics=("arbitrary", "arbitrary"), vmem_limit_bytes=VMEM_LIMIT),
        name="split_w1",
    )(w1e, perm)


def _moe_kernel(cnt_ref, offs_ref,
                w1g_ref, w1l_ref, b1g_ref, b1l_ref, w2_ref, b2_ref, g2_ref, bb2_ref,
                pos_hbm, gate_hbm, h_hbm,
                out_hbm, hbuf, ybuf, stage, out_buf, list_tok, list_gate, pos_in, gate_in, n_sorted, sem,
                *, tile):
    b = pl.program_id(0)
    e = pl.program_id(1)
    n_tiles = pl.num_programs(0)
    n_exp = pl.num_programs(1)
    d_model = w1g_ref.shape[1]
    d_ff = w1g_ref.shape[2]
    nchunk = d_model // LANES
    tile_rows = tile * nchunk
    pairs = tile * TOP_K
    n_sort_chunks = pairs // SORT_CHUNK
    half = (b % 2) * pairs
    other_half = pairs - half

    def stage_rows(r):
        return pl.ds(pl.multiple_of(r * nchunk, nchunk), nchunk)

    def listed_rows(p):
        return pl.ds(pl.multiple_of(list_tok[p], nchunk), nchunk)

    def sort_pair(j, first_row, dst_half):
        p = pos_in[j] + dst_half
        list_tok[p] = first_row
        list_gate[p] = gate_in[j]

    def sort_chunks(lo, hi, dst_half):
        def body(jo, c):
            for u in range(SORT_UNROLL):
                j = jo * SORT_UNROLL + u
                sort_pair(j, (j & (tile - 1)) * nchunk, dst_half)
            return c
        lax.fori_loop(lo * (SORT_CHUNK // SORT_UNROLL), hi * (SORT_CHUNK // SORT_UNROLL), body, 0)

    def load_pairs(t):
        rows = pl.ds(pl.multiple_of(t * pairs, pairs), pairs)
        copies = [pltpu.make_async_copy(pos_hbm.at[rows], pos_in, sem.at[2]),
                  pltpu.make_async_copy(gate_hbm.at[rows], gate_in, sem.at[3])]
        for cp in copies:
            cp.start()
        for cp in copies:
            cp.wait()

    @pl.when(jnp.logical_and(b == 0, e == 0))
    def _():
        stage[...] = jnp.zeros_like(stage)

        def pad_body(j, c):
            list_tok[pairs + j] = 0
            list_tok[2 * pairs + j] = 0
            return c
        lax.fori_loop(0, LIST_PAD, pad_body, 0)
        load_pairs(0)
        n_sorted[0] = 0

    @pl.when(e == 0)
    def _():
        src = h_hbm.at[pl.ds(pl.multiple_of(b * tile_rows, tile_rows), tile_rows), :]
        load = pltpu.make_async_copy(src, hbuf, sem.at[0])
        load.start()
        ybuf[...] = jnp.zeros_like(ybuf)
        sort_chunks(jnp.minimum(n_sorted[0], n_sort_chunks), n_sort_chunks, half)
        n_sorted[0] = 0

        @pl.when(b + 1 < n_tiles)
        def _():
            load_pairs(b + 1)
        load.wait()

    n = cnt_ref[b * n_exp + e]
    base = offs_ref[b * n_exp + e] + half

    def run_block(m_rows, p0, nrows):
        def gather_body(ci, c):
            for u in range(ROW_UNROLL):
                r = ci * ROW_UNROLL + u
                stage[stage_rows(r), :] = hbuf[listed_rows(p0 + r), :]
            return c
        lax.fori_loop(0, (nrows + ROW_UNROLL - 1) // ROW_UNROLL, gather_body, 0)

        if m_rows == MOE_BLOCK:
            chunk = jnp.minimum(n_sorted[0], n_sort_chunks - 1)
            n_sorted[0] = n_sorted[0] + 1
            j0 = chunk * SORT_CHUNK
            row0 = (j0 & (tile - 1)) * nchunk
            for u in range(SORT_CHUNK):
                sort_pair(j0 + u, row0 + u * nchunk, other_half)

        x = jnp.concatenate(
            [stage[pl.ds(c, m_rows, stride=nchunk), :] for c in range(nchunk)],
            axis=1).astype(BF16)
        y = b2_ref[0]
        for hh in range(d_ff // FF_CHUNK):
            sl = slice(hh * FF_CHUNK, (hh + 1) * FF_CHUNK)
            hg = jnp.dot(x, w1g_ref[0, :, sl], preferred_element_type=F32) + b1g_ref[0, :, sl]
            hl = jnp.dot(x, w1l_ref[0, :, sl], preferred_element_type=F32) + b1l_ref[0, :, sl]
            xg = jnp.minimum(hg, SWIGLU_LIMIT)
            xl = jnp.clip(hl, -SWIGLU_LIMIT, SWIGLU_LIMIT)
            act = xg * jax.nn.sigmoid(SWIGLU_ALPHA * xg) * (xl + 1.0)
            y = y + jnp.dot(act.astype(BF16), w2_ref[0, sl, :], preferred_element_type=F32)
        for c in range(nchunk):
            stage[pl.ds(c, m_rows, stride=nchunk), :] = y[:, c * LANES:(c + 1) * LANES]

        def add_body(ci, c):
            rows, vals = [], []
            for u in range(SUBLANES):
                r = ci * SUBLANES + u
                dst = listed_rows(p0 + r)
                rows.append(dst)
                vals.append(ybuf[dst, :] + list_gate[p0 + r] * stage[stage_rows(r), :])
            for dst, val in zip(rows, vals):
                ybuf[dst, :] = val
            return c
        n_full = nrows // SUBLANES
        lax.fori_loop(0, n_full, add_body, 0)

        def add_tail(r, c):
            dst = listed_rows(p0 + r)
            ybuf[dst, :] = ybuf[dst, :] + list_gate[p0 + r] * stage[stage_rows(r), :]
            return c
        lax.fori_loop(n_full * SUBLANES, nrows, add_tail, 0)

    n_big = n // MOE_BLOCK
    rem = n - n_big * MOE_BLOCK
    rem_is_big = rem > MOE_BLOCK - SMALL_BLOCK
    n_big_blocks = n_big + rem_is_big.astype(jnp.int32)
    n_small_blocks = jnp.where(rem_is_big, 0, (rem + SMALL_BLOCK - 1) // SMALL_BLOCK)

    def big_body(s, c):
        run_block(MOE_BLOCK, base + s * MOE_BLOCK, jnp.minimum(MOE_BLOCK, n - s * MOE_BLOCK))
        return c
    lax.fori_loop(0, n_big_blocks, big_body, 0)

    def small_body(s, c):
        run_block(SMALL_BLOCK, base + n_big * MOE_BLOCK + s * SMALL_BLOCK,
                  jnp.minimum(SMALL_BLOCK, rem - s * SMALL_BLOCK))
        return c
    lax.fori_loop(0, n_small_blocks, small_body, 0)

    @pl.when(e == n_exp - 1)
    def _():
        def out_copy(c, slot):
            rows = pl.ds(pl.multiple_of(b * tile + c * ROW_TILE, ROW_TILE), ROW_TILE)
            return pltpu.make_async_copy(out_buf.at[slot], out_hbm.at[rows, :], sem.at[4 + slot])

        def chunk_body(c, carry):
            slot = c % 2

            @pl.when(c >= 2)
            def _():
                out_copy(c - 2, slot).wait()
            first = c * (ROW_TILE * nchunk)
            pieces = [DEEPNORM_ALPHA * hbuf[pl.ds(first + k, ROW_TILE, stride=nchunk), :]
                      + ybuf[pl.ds(first + k, ROW_TILE, stride=nchunk), :] for k in range(nchunk)]
            out_buf[slot] = _layer_norm(jnp.concatenate(pieces, axis=1), g2_ref[...], bb2_ref[...])
            out_copy(c, slot).start()
            return carry
        n_out = tile // ROW_TILE
        lax.fori_loop(0, n_out, chunk_body, 0)
        for c in range(max(n_out - 2, 0), n_out):
            out_copy(c, c % 2).wait()


def _moe(h1t, cnt, offs, pos_flat, gate_flat, w1g, w1l, b1g, b1l, w2, b2, g2, bb2, tile):
    n_exp, d, f = w1g.shape
    nchunk = d // LANES
    n_tiles = h1t.shape[0] // (tile * nchunk)
    pairs = tile * TOP_K
    assert tile % SORT_CHUNK == 0
    expert = lambda b, e, cnt, offs: (e, 0, 0)
    grid_spec = pltpu.PrefetchScalarGridSpec(
        num_scalar_prefetch=2,
        grid=(n_tiles, n_exp),
        in_specs=[
            pl.BlockSpec((1, d, f), expert),
            pl.BlockSpec((1, d, f), expert),
            pl.BlockSpec((1, 1, f), expert),
            pl.BlockSpec((1, 1, f), expert),
            pl.BlockSpec((1, f, d), expert),
            pl.BlockSpec((1, 1, d), expert),
            pl.BlockSpec((1, d), lambda b, e, cnt, offs: (0, 0)),
            pl.BlockSpec((1, d), lambda b, e, cnt, offs: (0, 0)),
            pl.BlockSpec(memory_space=pl.ANY),
            pl.BlockSpec(memory_space=pl.ANY),
            pl.BlockSpec(memory_space=pl.ANY),
        ],
        out_specs=pl.BlockSpec(memory_space=pl.ANY),
        scratch_shapes=[
            pltpu.VMEM((tile * nchunk, LANES), F32),
            pltpu.VMEM((tile * nchunk, LANES), F32),
            pltpu.VMEM((MOE_BLOCK * nchunk, LANES), F32),
            pltpu.VMEM((2, ROW_TILE, d), F32),
            pltpu.SMEM((2 * pairs + LIST_PAD,), jnp.int32),
            pltpu.SMEM((2 * pairs + LIST_PAD,), F32),
            pltpu.SMEM((pairs,), jnp.int32),
            pltpu.SMEM((pairs,), F32),
            pltpu.SMEM((1,), jnp.int32),
            pltpu.SemaphoreType.DMA((6,)),
        ],
    )
    return pl.pallas_call(
        functools.partial(_moe_kernel, tile=tile),
        grid_spec=grid_spec,
        out_shape=jax.ShapeDtypeStruct((n_tiles * tile, d), F32),
        compiler_params=pltpu.CompilerParams(
            dimension_semantics=("arbitrary", "arbitrary"), vmem_limit_bytes=MOE_VMEM_LIMIT),
        name="moe",
    )(cnt, offs, w1g, w1l, b1g, b1l, w2, b2, g2, bb2, pos_flat, gate_flat, h1t)


def kernel(x, ln_in_g, ln_in_b, w_in, lambda_q1, lambda_k1, lambda_q2, lambda_k2, subln_g, rel_bias,
           w_out, ln1_g, ln1_b, router_w, router_b, w1, b1, w2, b2, ln2_g, ln2_b):
    bsz, seq, d = x.shape
    t = bsz * seq
    x2 = x.reshape(t, d)
    row = lambda v: v.reshape(1, -1).astype(F32)

    qs = A_HEAD_DIM ** -0.5 * LOG2E
    col_scale = jnp.concatenate([
        jnp.full((A_WIDTH,), qs, F32), jnp.ones((2 * A_WIDTH,), F32),
        jnp.full((B_WIDTH,), B_HEAD_DIM ** -0.5 * LOG2E, F32), jnp.ones((2 * B_WIDTH,), F32)]).reshape(1, -1)
    proj3, vt3, vtb3 = _ln_qkv(x2, row(ln_in_g), row(ln_in_b), w_in[0].astype(BF16), col_scale, seq)

    lam4 = jnp.stack([lambda_q1[0], lambda_k1[0], lambda_q2[0], lambda_k2[0]]).astype(F32)
    o_a = _diff_attn(proj3, vt3, lam4, subln_g[0].astype(F32))
    o_b = _band_attn(proj3, vtb3, rel_bias[0])

    w_o = w_out[0].astype(BF16)
    tile = min(MOE_TILE, t)
    n_tiles = t // tile
    h1t, top_idx, gates, rank, run_cnt = _out_router(
        x2, row(ln_in_g), row(ln_in_b), o_a, o_b,
        w_o[:A_WIDTH], w_o[A_WIDTH:], row(ln1_g[0]), row(ln1_b[0]),
        router_w[0].T.astype(F32), router_b[0].reshape(-1, 1).astype(F32), tile)

    steps = tile // ROW_TILE
    cnt = run_cnt[steps - 1::steps, :, 0]
    offs = jnp.cumsum(cnt, axis=1) - cnt
    hot = top_idx.reshape(TOP_K, n_tiles, tile, 1) == jnp.arange(N_EXPERTS, dtype=jnp.int32)
    pos = rank + jnp.sum(jnp.where(hot, offs[None, :, None, :], 0), axis=-1).reshape(TOP_K, t)
    per_tile = lambda a: a.reshape(TOP_K, n_tiles, tile).transpose(1, 0, 2).reshape(-1)
    w1g, w1l = _split_w1(w1[0])
    b1e = b1[0].astype(F32)[:, None, :]
    out = _moe(h1t, cnt.reshape(-1), offs.reshape(-1), per_tile(pos), per_tile(gates),
               w1g, w1l, b1e[:, :, 0::2], b1e[:, :, 1::2],
               w2[0].astype(BF16), b2[0][:, None, :].astype(F32), row(ln2_g[0]), row(ln2_b[0]), tile)
    return out.reshape(bsz, seq, d)
```

```python
import functools
import math

import jax
import jax.numpy as jnp
from jax import lax
from jax.experimental import pallas as pl
from jax.experimental.pallas import tpu as pltpu

F32 = jnp.float32
BF16 = jnp.bfloat16

CHUNK = 64
A_HEADS = 4
A_HEAD_DIM = 64
A_WIDTH = A_HEADS * 2 * A_HEAD_DIM
B_HEADS = 8
B_HEAD_DIM = 64
B_WIDTH = B_HEADS * B_HEAD_DIM
B_PAST_CHUNKS = 8
REL_CLIP = 256
N_EXPERTS = 32
TOP_K = 4
SWIGLU_ALPHA = 1.702
SWIGLU_LIMIT = 7.0
MOE_BLOCK = 512
LN_EPS = 1e-5
RMS_EPS = 1e-5
DEPTH = 1
DEEPNORM_ALPHA = (2 * DEPTH) ** 0.25
LAM_INIT = 0.8 - 0.6 * math.exp(-0.3 * 0)

LOG2E = 1.4426950408889634
NEG = -1e30
LANES = 128
SUBLANES = 8
ROW_TILE = 512
VMEM_LIMIT = 48 * 1024 * 1024
DIFF_TILE = 256
BAND_TILE = 128
BAND_TILES = (BAND_TILE + B_PAST_CHUNKS * CHUNK) // BAND_TILE
BAND_PAIR = 2
BAND_STEP_ROWS = 1024
MOE_TILE = 4096
MOE_VMEM_LIMIT = 60 * 1024 * 1024
SMALL_BLOCK = 128
FF_CHUNK = 512
SORT_UNROLL = 8
SORT_CHUNK = 512
ROW_UNROLL = 16
LIST_PAD = 1024


def _layer_norm(x, g, b):
    mu = jnp.mean(x, axis=-1, keepdims=True)
    xc = x - mu
    var = jnp.mean(xc * xc, axis=-1, keepdims=True)
    return xc * lax.rsqrt(var + LN_EPS) * g + b


def _ln_qkv_kernel(x_ref, g_ref, b_ref, w_ref, cs_ref, o_ref, vta_ref, vtb_ref):
    h = _layer_norm(x_ref[...], g_ref[...], b_ref[...])
    hb = h.astype(BF16)
    n_out = w_ref.shape[1]
    for c in range(n_out // ROW_TILE):
        sl = slice(c * ROW_TILE, (c + 1) * ROW_TILE)
        val = jnp.dot(hb, w_ref[:, sl], preferred_element_type=F32) * cs_ref[:, sl]
        o_ref[0, :, sl] = val.astype(BF16)
        for start, vt_ref in ((2 * A_WIDTH, vta_ref), (3 * A_WIDTH + 2 * B_WIDTH, vtb_ref)):
            if sl.start == start:
                tile = vt_ref.shape[2]
                for kt in range(ROW_TILE // tile):
                    vt_ref[kt] = val[kt * tile:(kt + 1) * tile, :].T.astype(BF16)


def _ln_qkv(x2, g, b, w_bf, col_scale, seq):
    assert A_WIDTH == ROW_TILE and B_WIDTH == ROW_TILE
    t, d = x2.shape
    n_out = w_bf.shape[1]
    steps_per_seq = seq // ROW_TILE
    vt_spec = lambda tile: pl.BlockSpec((ROW_TILE // tile, ROW_TILE, tile), lambda i: (i, 0, 0))
    vt_shape = lambda tile: jax.ShapeDtypeStruct((t // tile, ROW_TILE, tile), BF16)
    return pl.pallas_call(
        _ln_qkv_kernel,
        grid=(t // ROW_TILE,),
        in_specs=[
            pl.BlockSpec((ROW_TILE, d), lambda i: (i, 0)),
            pl.BlockSpec((1, d), lambda i: (0, 0)),
            pl.BlockSpec((1, d), lambda i: (0, 0)),
            pl.BlockSpec((d, n_out), lambda i: (0, 0)),
            pl.BlockSpec((1, n_out), lambda i: (0, 0)),
        ],
        out_specs=[pl.BlockSpec((1, ROW_TILE, n_out),
                                lambda i: (i // steps_per_seq, i % steps_per_seq, 0)),
                   vt_spec(DIFF_TILE), vt_spec(BAND_TILE)],
        out_shape=[jax.ShapeDtypeStruct((t // seq, seq, n_out), BF16),
                   vt_shape(DIFF_TILE), vt_shape(BAND_TILE)],
        compiler_params=pltpu.CompilerParams(
            dimension_semantics=("arbitrary",), vmem_limit_bytes=VMEM_LIMIT),
        name="ln_qkv",
    )(x2, g, b, w_bf, col_scale)


def _diff_attn_kernel(c_ref, lam_ref, g_ref, boff_ref, bdiag_ref, q_ref, k_ref, vt_ref, o_ref, s_scr):
    tq = q_ref.shape[1]
    tk = vt_ref.shape[2]
    h = pl.program_id(1)
    i = pl.program_id(2)
    c = c_ref[h]
    q = q_ref[0]
    lane = lax.broadcasted_iota(jnp.int32, q.shape, 1)
    zero = jnp.zeros_like(q)
    qq = jnp.concatenate([jnp.where(lane < A_HEAD_DIM, q, zero),
                          jnp.where(lane >= A_HEAD_DIM, q, zero)], axis=0)
    qqt = qq.astype(F32).T.astype(BF16)
    n_before = i * (tq // tk)

    def scores(j):
        kb = k_ref[0, pl.ds(pl.multiple_of(j * tk, tk), tk), :]
        return jnp.dot(kb, qqt, preferred_element_type=F32)

    def update(slot, table_ref, j, carry):
        m, l, acc = carry
        s = s_scr[slot] + table_ref[0]
        shift = c * (j * tk).astype(F32)
        m_new = jnp.maximum(m, jnp.max(s, axis=0, keepdims=True) + shift)
        alpha = jnp.exp2(m - m_new)
        p = jnp.exp2(s - (m_new - shift))
        l = alpha * l + jnp.sum(p, axis=0, keepdims=True)
        acc = alpha * acc + jnp.dot(vt_ref[j], p.astype(BF16), preferred_element_type=F32)
        return m_new, l, acc

    s_scr[0] = scores(0)

    def pair(t, carry):
        j = 2 * t
        s_scr[1] = scores(j + 1)
        carry = update(0, boff_ref, j, carry)
        s_scr[0] = scores(j + 2)
        return update(1, boff_ref, j + 1, carry)

    def quad(t, carry):
        return pair(2 * t + 1, pair(2 * t, carry))

    init = (jnp.full((1, 2 * tq), NEG, F32), jnp.zeros((1, 2 * tq), F32),
            jnp.zeros((LANES, 2 * tq), F32))
    carry = lax.fori_loop(0, n_before // 4, quad, init)
    carry = lax.fori_loop(n_before // 4 * 2, n_before // 2, pair, carry)

    def odd_tail(carry):
        s_scr[1] = scores(n_before)
        carry = update(0, boff_ref, n_before - 1, carry)
        return update(1, bdiag_ref, n_before, carry)

    def even_tail(carry):
        return update(0, bdiag_ref, n_before, carry)

    _, l, acc = lax.cond(n_before % 2 == 1, odd_tail, even_tail, carry)

    o_all = acc / l
    lv = lam_ref[...]
    lam = (jnp.exp(jnp.sum(lv[0:1] * lv[1:2], axis=1, keepdims=True))
           - jnp.exp(jnp.sum(lv[2:3] * lv[3:4], axis=1, keepdims=True)) + LAM_INIT)
    o = o_all[:, :tq] - lam * o_all[:, tq:]
    ms = jnp.mean(o * o, axis=0, keepdims=True)
    o = o * lax.rsqrt(ms + RMS_EPS) * (g_ref[...] * (1.0 - LAM_INIT))
    o_ref[0] = o.T.astype(BF16)


def _diff_attn(proj3, vt3, lam4, subln_g):
    bsz, seq, _ = proj3.shape
    tq = tk = DIFF_TILE
    c = jnp.asarray([2.0 ** (-8.0 * (h + 1) / A_HEADS) for h in range(A_HEADS)], F32) * LOG2E
    r = jnp.arange(tk, dtype=jnp.int32)[:, None]
    qrel = jnp.arange(2 * tq, dtype=jnp.int32)[None, :] % tq
    boff = c[:, None, None] * jnp.broadcast_to(r, (tk, 2 * tq)).astype(F32)
    allowed = (r // CHUNK) <= (qrel // CHUNK)
    bdiag = jnp.where(allowed, c[:, None, None] * (qrel - jnp.abs(qrel - r)).astype(F32), NEG)
    kblk = A_WIDTH // LANES
    return pl.pallas_call(
        _diff_attn_kernel,
        grid=(bsz, A_HEADS, seq // tq),
        in_specs=[
            pl.BlockSpec(memory_space=pltpu.SMEM),
            pl.BlockSpec((4, A_HEAD_DIM), lambda b, h, i: (0, 0)),
            pl.BlockSpec((LANES, 1), lambda b, h, i: (0, 0)),
            pl.BlockSpec((1, tk, 2 * tq), lambda b, h, i: (h, 0, 0)),
            pl.BlockSpec((1, tk, 2 * tq), lambda b, h, i: (h, 0, 0)),
            pl.BlockSpec((1, tq, LANES), lambda b, h, i: (b, i, h)),
            pl.BlockSpec((1, seq, LANES), lambda b, h, i: (b, 0, kblk + h)),
            pl.BlockSpec((seq // tk, LANES, tk), lambda b, h, i: (b, h, 0)),
        ],
        out_specs=pl.BlockSpec((1, tq, LANES), lambda b, h, i: (b, i, h)),
        out_shape=jax.ShapeDtypeStruct((bsz, seq, A_WIDTH), BF16),
        scratch_shapes=[pltpu.VMEM((2, tk, 2 * tq), F32)],
        compiler_params=pltpu.CompilerParams(
            dimension_semantics=("arbitrary", "arbitrary", "arbitrary"),
            vmem_limit_bytes=VMEM_LIMIT),
        name="diff_attn",
    )(c, lam4, subln_g.reshape(LANES, 1), boff, bdiag, proj3, proj3, vt3)


def _band_attn_kernel(bias_ref, q_ref, k_ref, vt_ref, o_ref, s_scr):
    i = pl.program_id(2)
    tq = BAND_TILE
    units = q_ref.shape[1] // (BAND_PAIR * tq)
    key_tiles = BAND_TILES + BAND_PAIR - 1
    row = lax.broadcasted_iota(jnp.int32, (LANES, tq), 0)

    def key_tile(n, t):
        jt = (i * units + n) * BAND_PAIR - (BAND_TILES - 1) + t
        return jnp.maximum(jt, 0), jt < 0

    def scores(n):
        qs = []
        for a in range(BAND_PAIR):
            q = q_ref[0, (n * BAND_PAIR + a) * tq:(n * BAND_PAIR + a + 1) * tq, :]
            lane = lax.broadcasted_iota(jnp.int32, q.shape, 1)
            zero = jnp.zeros_like(q)
            qs += [jnp.where(lane < B_HEAD_DIM, q, zero), jnp.where(lane >= B_HEAD_DIM, q, zero)]
        qqt = jnp.concatenate(qs, axis=0).astype(F32).T.astype(BF16)
        ks = [k_ref[0, pl.ds(pl.multiple_of(key_tile(n, t)[0] * tq, tq), tq), :] for t in range(key_tiles)]
        return jnp.dot(jnp.concatenate(ks, axis=0), qqt, preferred_element_type=F32)

    def finish(n):
        bias = [bias_ref[0, jnp.where(key_tile(n, t)[1], 1, 0), t] for t in range(key_tiles)]
        s = s_scr[n % 2] + jnp.concatenate(bias, axis=0)
        m = jnp.max(s, axis=0, keepdims=True)
        p = jnp.exp2(s - m)
        l = jnp.sum(p, axis=0, keepdims=True)
        vt = jnp.concatenate([vt_ref[key_tile(n, t)[0]] for t in range(key_tiles)], axis=1)
        o = jnp.dot(vt, p.astype(BF16), preferred_element_type=F32) / l
        for a in range(BAND_PAIR):
            oa = jnp.where(row < B_HEAD_DIM, o[:, 2 * a * tq:(2 * a + 1) * tq],
                           o[:, (2 * a + 1) * tq:(2 * a + 2) * tq])
            o_ref[0, (n * BAND_PAIR + a) * tq:(n * BAND_PAIR + a + 1) * tq, :] = oa.T.astype(BF16)

    s_scr[0] = scores(0)
    for n in range(units):
        if n + 1 < units:
            s_scr[(n + 1) % 2] = scores(n + 1)
        finish(n)


def _band_bias(rel_bias, tq):
    past = B_PAST_CHUNKS * CHUNK
    band = tq + past
    assert tq - 1 <= REL_CLIP <= past
    qi = jnp.arange(tq)
    kj = jnp.arange(band)
    cq = qi[:, None] // CHUNK
    ck = kj[None, :] // CHUNK
    allowed = (ck >= cq) & (ck <= cq + B_PAST_CHUNKS)
    tab = rel_bias.astype(F32) * LOG2E
    n_diag = band + tq - 1
    n_unclipped = REL_CLIP + tq
    w = jnp.concatenate([tab[:, REL_CLIP - (tq - 1):],
                         jnp.broadcast_to(tab[:, -1:], (B_HEADS, n_diag - n_unclipped))], axis=1)
    shifted = jnp.tile(w, (1, tq + 1))[:, :tq * (n_diag + 1)].reshape(B_HEADS, tq, n_diag + 1)
    bias = jnp.flip(shifted[:, :, :band], axis=2)
    return jnp.where(allowed[None], bias, NEG)


def _band_attn(proj3, vtb3, rel_bias):
    bsz, seq, _ = proj3.shape
    tq = BAND_TILE
    groups = B_HEADS // 2
    key_tiles = BAND_TILES + BAND_PAIR - 1
    cols = BAND_PAIR * 2 * tq
    bias = _band_bias(rel_bias, tq)
    bias = bias.reshape(groups, 2, tq, BAND_TILES, tq)
    masked = jnp.full((groups, 2, tq, 1, tq), NEG, F32)
    per_tile = [jnp.concatenate([masked] * a + [bias] + [masked] * (BAND_PAIR - 1 - a), axis=3)
                for a in range(BAND_PAIR)]
    bias = jnp.stack(per_tile, axis=1)
    bias = bias.transpose(0, 4, 5, 1, 2, 3).reshape(groups, 1, key_tiles, tq, cols)
    bias = jnp.concatenate([bias, jnp.full_like(bias, NEG)], axis=1)
    qblk = 3 * A_WIDTH // LANES
    kblk = qblk + B_WIDTH // LANES
    rows = min(BAND_STEP_ROWS, seq)
    return pl.pallas_call(
        _band_attn_kernel,
        grid=(bsz, groups, seq // rows),
        in_specs=[
            pl.BlockSpec((1, 2, key_tiles, tq, cols), lambda b, g, i: (g, 0, 0, 0, 0)),
            pl.BlockSpec((1, rows, LANES), lambda b, g, i: (b, i, qblk + g)),
            pl.BlockSpec((1, seq, LANES), lambda b, g, i: (b, 0, kblk + g)),
            pl.BlockSpec((seq // tq, LANES, tq), lambda b, g, i: (b, g, 0)),
        ],
        out_specs=pl.BlockSpec((1, rows, LANES), lambda b, g, i: (b, i, g)),
        out_shape=jax.ShapeDtypeStruct((bsz, seq, B_WIDTH), BF16),
        scratch_shapes=[pltpu.VMEM((2, key_tiles * tq, cols), F32)],
        compiler_params=pltpu.CompilerParams(
            dimension_semantics=("arbitrary", "arbitrary", "arbitrary"),
            vmem_limit_bytes=VMEM_LIMIT),
        name="band_attn",
    )(bias, proj3, proj3, vtb3)


def _out_router_kernel(x_ref, gi_ref, bi_ref, oa_ref, ob_ref, wa_ref, wb_ref, g1_ref, b1_ref,
                       rwt_ref, rb_ref, tri_ref, h1_ref, idx_ref, gate_ref, rank_ref, cnt_ref,
                       carry_ref, *, steps_per_tile):
    h = _layer_norm(x_ref[...], gi_ref[...], bi_ref[...])
    mix = (jnp.dot(oa_ref[0], wa_ref[...], preferred_element_type=F32)
           + jnp.dot(ob_ref[0], wb_ref[...], preferred_element_type=F32))
    h1 = _layer_norm(DEEPNORM_ALPHA * h + mix, g1_ref[...], b1_ref[...])
    nchunk = h1.shape[1] // LANES
    for c in range(nchunk):
        h1_ref[pl.ds(c, ROW_TILE, stride=nchunk), :] = h1[:, c * LANES:(c + 1) * LANES]
    lt = lax.dot_general(rwt_ref[...], h1, (((1,), (1,)), ((), ())),
                         precision=lax.Precision.HIGHEST, preferred_element_type=F32)
    lt = lt + rb_ref[...]
    eidx = lax.broadcasted_iota(jnp.int32, lt.shape, 0)
    vals, idxs, hots = [], [], []
    for _ in range(TOP_K):
        mx = jnp.max(lt, axis=0, keepdims=True)
        am = jnp.min(jnp.where(lt == mx, eidx, N_EXPERTS), axis=0, keepdims=True)
        hit = eidx == am
        vals.append(mx)
        idxs.append(am)
        hots.append(jnp.where(hit, 1.0, 0.0))
        lt = jnp.where(hit, -jnp.inf, lt)
    ex = [jnp.exp(v - vals[0]) for v in vals]
    den = ex[0] + ex[1] + ex[2] + ex[3]
    idx_ref[...] = jnp.concatenate(idxs, axis=0)
    gate_ref[...] = jnp.concatenate([e / den for e in ex], axis=0)

    @pl.when(pl.program_id(0) % steps_per_tile == 0)
    def _():
        carry_ref[...] = jnp.zeros_like(carry_ref)

    hot = (hots[0] + hots[1]) + (hots[2] + hots[3])
    before = jnp.dot(hot.astype(BF16), tri_ref[...], preferred_element_type=F32) + carry_ref[...]
    rank_ref[...] = jnp.concatenate(
        [jnp.sum(hk * before, axis=0, keepdims=True) for hk in hots], axis=0).astype(jnp.int32)
    total = carry_ref[...] + jnp.sum(hot, axis=1, keepdims=True)
    carry_ref[...] = total
    cnt_ref[0] = total.astype(jnp.int32)


def _out_router(x2, gi, bi, o_a, o_b, wa, wb, g1, b1, rwt, rb, moe_tile):
    t, d = x2.shape
    nchunk = d // LANES
    row = lambda i: (i, 0)
    fixed = lambda i: (0, 0)
    steps_per_seq = o_a.shape[1] // ROW_TILE
    seq_row = lambda i: (i // steps_per_seq, i % steps_per_seq, 0)
    tri = jnp.triu(jnp.ones((ROW_TILE, ROW_TILE), BF16), k=1)
    return pl.pallas_call(
        functools.partial(_out_router_kernel, steps_per_tile=moe_tile // ROW_TILE),
        grid=(t // ROW_TILE,),
        in_specs=[
            pl.BlockSpec((ROW_TILE, d), row),
            pl.BlockSpec((1, d), fixed),
            pl.BlockSpec((1, d), fixed),
            pl.BlockSpec((1, ROW_TILE, A_WIDTH), seq_row),
            pl.BlockSpec((1, ROW_TILE, B_WIDTH), seq_row),
            pl.BlockSpec((A_WIDTH, d), fixed),
            pl.BlockSpec((B_WIDTH, d), fixed),
            pl.BlockSpec((1, d), fixed),
            pl.BlockSpec((1, d), fixed),
            pl.BlockSpec((N_EXPERTS, d), fixed),
            pl.BlockSpec((N_EXPERTS, 1), fixed),
            pl.BlockSpec((ROW_TILE, ROW_TILE), fixed),
        ],
        out_specs=[
            pl.BlockSpec((ROW_TILE * nchunk, LANES), row),
            pl.BlockSpec((TOP_K, ROW_TILE), lambda i: (0, i)),
            pl.BlockSpec((TOP_K, ROW_TILE), lambda i: (0, i)),
            pl.BlockSpec((TOP_K, ROW_TILE), lambda i: (0, i)),
            pl.BlockSpec((1, N_EXPERTS, 1), lambda i: (i, 0, 0)),
        ],
        out_shape=[
            jax.ShapeDtypeStruct((t * nchunk, LANES), F32),
            jax.ShapeDtypeStruct((TOP_K, t), jnp.int32),
            jax.ShapeDtypeStruct((TOP_K, t), F32),
            jax.ShapeDtypeStruct((TOP_K, t), jnp.int32),
            jax.ShapeDtypeStruct((t // ROW_TILE, N_EXPERTS, 1), jnp.int32),
        ],
        scratch_shapes=[pltpu.VMEM((N_EXPERTS, 1), F32)],
        compiler_params=pltpu.CompilerParams(
            dimension_semantics=("arbitrary",), vmem_limit_bytes=VMEM_LIMIT),
        name="out_router",
    )(x2, gi, bi, o_a, o_b, wa, wb, g1, b1, rwt, rb, tri)


def _split_w1_kernel(w_ref, perm_ref, g_ref, l_ref):
    rows = w_ref.shape[1]
    even = (lax.broadcasted_iota(jnp.int32, (rows, LANES), 1) & 1) == 0
    perm = perm_ref[...]
    for g in range(w_ref.shape[2] // (2 * LANES)):
        v0 = w_ref[0, :, 2 * g * LANES:(2 * g + 1) * LANES]
        v1 = w_ref[0, :, (2 * g + 1) * LANES:(2 * g + 2) * LANES]
        glu = jnp.where(even, v0, pltpu.roll(v1, 1, axis=1)).astype(BF16)
        lin = jnp.where(even, pltpu.roll(v0, LANES - 1, axis=1), v1).astype(BF16)
        g_ref[0, :, g * LANES:(g + 1) * LANES] = jnp.dot(
            glu, perm, preferred_element_type=F32).astype(BF16)
        l_ref[0, :, g * LANES:(g + 1) * LANES] = jnp.dot(
            lin, perm, preferred_element_type=F32).astype(BF16)


def _split_w1(w1e):
    e, d, f2 = w1e.shape
    rows = 256
    half = LANES // 2
    unit = jnp.arange(LANES)
    perm = (jnp.arange(LANES)[:, None] == (2 * (unit % half) + unit // half)[None, :]).astype(BF16)
    return pl.pallas_call(
        _split_w1_kernel,
        grid=(e, d // rows),
        in_specs=[pl.BlockSpec((1, rows, f2), lambda i, j: (i, j, 0)),
                  pl.BlockSpec((LANES, LANES), lambda i, j: (0, 0))],
        out_specs=[pl.BlockSpec((1, rows, f2 // 2), lambda i, j: (i, j, 0)),
                   pl.BlockSpec((1, rows, f2 // 2), lambda i, j: (i, j, 0))],
        out_shape=[jax.ShapeDtypeStruct((e, d, f2 // 2), BF16),
                   jax.ShapeDtypeStruct((e, d, f2 // 2), BF16)],
        compiler_params=pltpu.CompilerParams(
            dimension_semantics=("arbitrary", "arbitrary"), vmem_limit_bytes=VMEM_LIMIT),
        name="split_w1",
    )(w1e, perm)


def _moe_kernel(cnt_ref, offs_ref,
                w1g_ref, w1l_ref, b1g_ref, b1l_ref, w2_ref, b2_ref, g2_ref, bb2_ref,
                pos_hbm, gate_hbm, h_hbm,
                out_hbm, hbuf, ybuf, stage, ystage, out_buf, list_tok, list_gate, pos_in, gate_in, n_sorted,
                ahead, sem, *, tile):
    b = pl.program_id(0)
    e = pl.program_id(1)
    n_tiles = pl.num_programs(0)
    n_exp = pl.num_programs(1)
    d_model = w1g_ref.shape[1]
    d_ff = w1g_ref.shape[2]
    nchunk = d_model // LANES
    tile_rows = tile * nchunk
    pairs = tile * TOP_K
    n_sort_chunks = pairs // SORT_CHUNK
    half = (b % 2) * pairs
    other_half = pairs - half

    def stage_rows(r):
        return pl.ds(pl.multiple_of(r * nchunk, nchunk), nchunk)

    def listed_rows(p):
        return pl.ds(pl.multiple_of(list_tok[p], nchunk), nchunk)

    def sort_pair(j, first_row, dst_half):
        p = pos_in[j] + dst_half
        list_tok[p] = first_row
        list_gate[p] = gate_in[j]

    def sort_chunks(lo, hi, dst_half):
        def body(jo, c):
            for u in range(SORT_UNROLL):
                j = jo * SORT_UNROLL + u
                sort_pair(j, (j & (tile - 1)) * nchunk, dst_half)
            return c
        lax.fori_loop(lo * (SORT_CHUNK // SORT_UNROLL), hi * (SORT_CHUNK // SORT_UNROLL), body, 0)

    def load_pairs(t):
        rows = pl.ds(pl.multiple_of(t * pairs, pairs), pairs)
        copies = [pltpu.make_async_copy(pos_hbm.at[rows], pos_in, sem.at[2]),
                  pltpu.make_async_copy(gate_hbm.at[rows], gate_in, sem.at[3])]
        for cp in copies:
            cp.start()
        for cp in copies:
            cp.wait()

    @pl.when(jnp.logical_and(b == 0, e == 0))
    def _():
        stage[...] = jnp.zeros_like(stage)
        ystage[...] = jnp.zeros_like(ystage)

        def clear_body(j, c):
            list_tok[j] = 0
            return c
        lax.fori_loop(0, 2 * pairs + LIST_PAD, clear_body, 0)
        load_pairs(0)
        n_sorted[0] = 0
        ahead[1] = 0

    @pl.when(e == 0)
    def _():
        src = h_hbm.at[pl.ds(pl.multiple_of(b * tile_rows, tile_rows), tile_rows), :]
        load = pltpu.make_async_copy(src, hbuf, sem.at[0])
        load.start()
        ybuf[...] = jnp.zeros_like(ybuf)
        sort_chunks(jnp.minimum(n_sorted[0], n_sort_chunks), n_sort_chunks, half)
        n_sorted[0] = 0
        ahead[0] = -2 * MOE_BLOCK

        @pl.when(b + 1 < n_tiles)
        def _():
            load_pairs(b + 1)
        load.wait()

    n = cnt_ref[b * n_exp + e]
    base = offs_ref[b * n_exp + e] + half

    def run_block(m_rows, p0, nrows):
        xoff = p0 - ahead[0]
        hit = jnp.logical_and(xoff >= 0, xoff + m_rows <= MOE_BLOCK)
        xslot = jnp.where(hit, ahead[1], 1 - ahead[1])
        xoff = jnp.where(hit, xoff, 0)

        def gather_body(ci, c):
            for u in range(ROW_UNROLL):
                r = ci * ROW_UNROLL + u
                stage[xslot, stage_rows(r), :] = hbuf[listed_rows(p0 + r), :]
            return c
        lax.fori_loop(0, jnp.where(hit, 0, (nrows + ROW_UNROLL - 1) // ROW_UNROLL), gather_body, 0)

        if m_rows == MOE_BLOCK:
            chunk = jnp.minimum(n_sorted[0], n_sort_chunks - 1)
            n_sorted[0] = n_sorted[0] + 1
            j0 = chunk * SORT_CHUNK
            row0 = (j0 & (tile - 1)) * nchunk
            for u in range(SORT_CHUNK):
                sort_pair(j0 + u, row0 + u * nchunk, other_half)

        first = pl.multiple_of(xoff * nchunk, nchunk)
        x = jnp.concatenate(
            [stage[xslot, pl.ds(first + c, m_rows, stride=nchunk), :] for c in range(nchunk)],
            axis=1).astype(BF16)

        if m_rows == MOE_BLOCK:
            nxt = p0 + nrows
            for r in range(MOE_BLOCK):
                stage[1 - xslot, r * nchunk:(r + 1) * nchunk, :] = hbuf[listed_rows(nxt + r), :]
            ahead[0] = nxt
            ahead[1] = 1 - xslot
        y = b2_ref[0]
        for hh in range(d_ff // FF_CHUNK):
            sl = slice(hh * FF_CHUNK, (hh + 1) * FF_CHUNK)
            hg = jnp.dot(x, w1g_ref[0, :, sl], preferred_element_type=F32) + b1g_ref[0, :, sl]
            hl = jnp.dot(x, w1l_ref[0, :, sl], preferred_element_type=F32) + b1l_ref[0, :, sl]
            xg = jnp.minimum(hg, SWIGLU_LIMIT)
            xl = jnp.clip(hl, -SWIGLU_LIMIT, SWIGLU_LIMIT)
            act = xg * jax.nn.sigmoid(SWIGLU_ALPHA * xg) * (xl + 1.0)
            y = y + jnp.dot(act.astype(BF16), w2_ref[0, sl, :], preferred_element_type=F32)
        for c in range(nchunk):
            ystage[pl.ds(c, m_rows, stride=nchunk), :] = y[:, c * LANES:(c + 1) * LANES]

        def add_body(ci, c):
            rows, vals = [], []
            for u in range(SUBLANES):
                r = ci * SUBLANES + u
                dst = listed_rows(p0 + r)
                rows.append(dst)
                vals.append(ybuf[dst, :] + list_gate[p0 + r] * ystage[stage_rows(r), :])
            for dst, val in zip(rows, vals):
                ybuf[dst, :] = val
            return c
        n_full = nrows // SUBLANES
        lax.fori_loop(0, n_full, add_body, 0)

        def add_tail(r, c):
            dst = listed_rows(p0 + r)
            ybuf[dst, :] = ybuf[dst, :] + list_gate[p0 + r] * ystage[stage_rows(r), :]
            return c
        lax.fori_loop(n_full * SUBLANES, nrows, add_tail, 0)

    n_big = n // MOE_BLOCK
    rem = n - n_big * MOE_BLOCK
    rem_is_big = rem > MOE_BLOCK - SMALL_BLOCK
    n_big_blocks = n_big + rem_is_big.astype(jnp.int32)
    n_small_blocks = jnp.where(rem_is_big, 0, (rem + SMALL_BLOCK - 1) // SMALL_BLOCK)

    def big_body(s, c):
        run_block(MOE_BLOCK, base + s * MOE_BLOCK, jnp.minimum(MOE_BLOCK, n - s * MOE_BLOCK))
        return c
    lax.fori_loop(0, n_big_blocks, big_body, 0)

    def small_body(s, c):
        run_block(SMALL_BLOCK, base + n_big * MOE_BLOCK + s * SMALL_BLOCK,
                  jnp.minimum(SMALL_BLOCK, rem - s * SMALL_BLOCK))
        return c
    lax.fori_loop(0, n_small_blocks, small_body, 0)

    @pl.when(e == n_exp - 1)
    def _():
        def out_copy(c, slot):
            rows = pl.ds(pl.multiple_of(b * tile + c * ROW_TILE, ROW_TILE), ROW_TILE)
            return pltpu.make_async_copy(out_buf.at[slot], out_hbm.at[rows, :], sem.at[4 + slot])

        def chunk_body(c, carry):
            slot = c % 2

            @pl.when(c >= 2)
            def _():
                out_copy(c - 2, slot).wait()
            first = c * (ROW_TILE * nchunk)
            pieces = [DEEPNORM_ALPHA * hbuf[pl.ds(first + k, ROW_TILE, stride=nchunk), :]
                      + ybuf[pl.ds(first + k, ROW_TILE, stride=nchunk), :] for k in range(nchunk)]
            out_buf[slot] = _layer_norm(jnp.concatenate(pieces, axis=1), g2_ref[...], bb2_ref[...])
            out_copy(c, slot).start()
            return carry
        n_out = tile // ROW_TILE
        lax.fori_loop(0, n_out, chunk_body, 0)
        for c in range(max(n_out - 2, 0), n_out):
            out_copy(c, c % 2).wait()


def _moe(h1t, cnt, offs, pos_flat, gate_flat, w1g, w1l, b1g, b1l, w2, b2, g2, bb2, tile):
    n_exp, d, f = w1g.shape
    nchunk = d // LANES
    n_tiles = h1t.shape[0] // (tile * nchunk)
    pairs = tile * TOP_K
    assert tile % SORT_CHUNK == 0
    expert = lambda b, e, cnt, offs: (e, 0, 0)
    grid_spec = pltpu.PrefetchScalarGridSpec(
        num_scalar_prefetch=2,
        grid=(n_tiles, n_exp),
        in_specs=[
            pl.BlockSpec((1, d, f), expert),
            pl.BlockSpec((1, d, f), expert),
            pl.BlockSpec((1, 1, f), expert),
            pl.BlockSpec((1, 1, f), expert),
            pl.BlockSpec((1, f, d), expert),
            pl.BlockSpec((1, 1, d), expert),
            pl.BlockSpec((1, d), lambda b, e, cnt, offs: (0, 0)),
            pl.BlockSpec((1, d), lambda b, e, cnt, offs: (0, 0)),
            pl.BlockSpec(memory_space=pl.ANY),
            pl.BlockSpec(memory_space=pl.ANY),
            pl.BlockSpec(memory_space=pl.ANY),
        ],
        out_specs=pl.BlockSpec(memory_space=pl.ANY),
        scratch_shapes=[
            pltpu.VMEM((tile * nchunk, LANES), F32),
            pltpu.VMEM((tile * nchunk, LANES), F32),
            pltpu.VMEM((2, MOE_BLOCK * nchunk, LANES), F32),
            pltpu.VMEM((MOE_BLOCK * nchunk, LANES), F32),
            pltpu.VMEM((2, ROW_TILE, d), F32),
            pltpu.SMEM((2 * pairs + LIST_PAD,), jnp.int32),
            pltpu.SMEM((2 * pairs + LIST_PAD,), F32),
            pltpu.SMEM((pairs,), jnp.int32),
            pltpu.SMEM((pairs,), F32),
            pltpu.SMEM((1,), jnp.int32),
            pltpu.SMEM((2,), jnp.int32),
            pltpu.SemaphoreType.DMA((6,)),
        ],
    )
    return pl.pallas_call(
        functools.partial(_moe_kernel, tile=tile),
        grid_spec=grid_spec,
        out_shape=jax.ShapeDtypeStruct((n_tiles * tile, d), F32),
        compiler_params=pltpu.CompilerParams(
            dimension_semantics=("arbitrary", "arbitrary"), vmem_limit_bytes=MOE_VMEM_LIMIT),
        name="moe",
    )(cnt, offs, w1g, w1l, b1g, b1l, w2, b2, g2, bb2, pos_flat, gate_flat, h1t)


def kernel(x, ln_in_g, ln_in_b, w_in, lambda_q1, lambda_k1, lambda_q2, lambda_k2, subln_g, rel_bias,
           w_out, ln1_g, ln1_b, router_w, router_b, w1, b1, w2, b2, ln2_g, ln2_b):
    bsz, seq, d = x.shape
    t = bsz * seq
    x2 = x.reshape(t, d)
    row = lambda v: v.reshape(1, -1).astype(F32)

    qs = A_HEAD_DIM ** -0.5 * LOG2E
    col_scale = jnp.concatenate([
        jnp.full((A_WIDTH,), qs, F32), jnp.ones((2 * A_WIDTH,), F32),
        jnp.full((B_WIDTH,), B_HEAD_DIM ** -0.5 * LOG2E, F32), jnp.ones((2 * B_WIDTH,), F32)]).reshape(1, -1)
    proj3, vt3, vtb3 = _ln_qkv(x2, row(ln_in_g), row(ln_in_b), w_in[0].astype(BF16), col_scale, seq)

    lam4 = jnp.stack([lambda_q1[0], lambda_k1[0], lambda_q2[0], lambda_k2[0]]).astype(F32)
    o_a = _diff_attn(proj3, vt3, lam4, subln_g[0].astype(F32))
    o_b = _band_attn(proj3, vtb3, rel_bias[0])

    w_o = w_out[0].astype(BF16)
    tile = min(MOE_TILE, t)
    n_tiles = t // tile
    h1t, top_idx, gates, rank, run_cnt = _out_router(
        x2, row(ln_in_g), row(ln_in_b), o_a, o_b,
        w_o[:A_WIDTH], w_o[A_WIDTH:], row(ln1_g[0]), row(ln1_b[0]),
        router_w[0].T.astype(F32), router_b[0].reshape(-1, 1).astype(F32), tile)

    steps = tile // ROW_TILE
    cnt = run_cnt[steps - 1::steps, :, 0]
    offs = jnp.cumsum(cnt, axis=1) - cnt
    hot = top_idx.reshape(TOP_K, n_tiles, tile, 1) == jnp.arange(N_EXPERTS, dtype=jnp.int32)
    pos = rank + jnp.sum(jnp.where(hot, offs[None, :, None, :], 0), axis=-1).reshape(TOP_K, t)
    per_tile = lambda a: a.reshape(TOP_K, n_tiles, tile).transpose(1, 0, 2).reshape(-1)
    w1g, w1l = _split_w1(w1[0])
    b1e = b1[0].astype(F32)[:, None, :]
    out = _moe(h1t, cnt.reshape(-1), offs.reshape(-1), per_tile(pos), per_tile(gates),
               w1g, w1l, b1e[:, :, 0::2], b1e[:, :, 1::2],
               w2[0].astype(BF16), b2[0][:, None, :].astype(F32), row(ln2_g[0]), row(ln2_b[0]), tile)
    return out.reshape(bsz, seq, d)
```

```python
import functools
import math

import jax
import jax.numpy as jnp
from jax import lax
from jax.experimental import pallas as pl
from jax.experimental.pallas import tpu as pltpu

F32 = jnp.float32
BF16 = jnp.bfloat16

CHUNK = 64
A_HEADS = 4
A_HEAD_DIM = 64
A_WIDTH = A_HEADS * 2 * A_HEAD_DIM
B_HEADS = 8
B_HEAD_DIM = 64
B_WIDTH = B_HEADS * B_HEAD_DIM
B_PAST_CHUNKS = 8
REL_CLIP = 256
N_EXPERTS = 32
TOP_K = 4
SWIGLU_ALPHA = 1.702
SWIGLU_LIMIT = 7.0
MOE_BLOCK = 512
LN_EPS = 1e-5
RMS_EPS = 1e-5
DEPTH = 1
DEEPNORM_ALPHA = (2 * DEPTH) ** 0.25
LAM_INIT = 0.8 - 0.6 * math.exp(-0.3 * 0)

LOG2E = 1.4426950408889634
NEG = -1e30
LANES = 128
SUBLANES = 8
ROW_TILE = 512
VMEM_LIMIT = 48 * 1024 * 1024
DIFF_TILE = 256
BAND_TILE = 128
BAND_TILES = (BAND_TILE + B_PAST_CHUNKS * CHUNK) // BAND_TILE
BAND_PAIR = 2
BAND_STEP_ROWS = 1024
MOE_TILE = 4096
MOE_VMEM_LIMIT = 60 * 1024 * 1024
SMALL_BLOCK = 128
FF_CHUNK = 1024
SORT_UNROLL = 8
SORT_CHUNK = 512
ROW_UNROLL = 16
LIST_PAD = 128


def _layer_norm(x, g, b):
    mu = jnp.mean(x, axis=-1, keepdims=True)
    xc = x - mu
    var = jnp.mean(xc * xc, axis=-1, keepdims=True)
    return xc * lax.rsqrt(var + LN_EPS) * g + b


def _ln_qkv_kernel(x_ref, g_ref, b_ref, w_ref, cs_ref, o_ref, vta_ref, vtb_ref):
    h = _layer_norm(x_ref[...], g_ref[...], b_ref[...])
    hb = h.astype(BF16)
    n_out = w_ref.shape[1]
    for c in range(n_out // ROW_TILE):
        sl = slice(c * ROW_TILE, (c + 1) * ROW_TILE)
        val = jnp.dot(hb, w_ref[:, sl], preferred_element_type=F32) * cs_ref[:, sl]
        o_ref[0, :, sl] = val.astype(BF16)
        for start, vt_ref in ((2 * A_WIDTH, vta_ref), (3 * A_WIDTH + 2 * B_WIDTH, vtb_ref)):
            if sl.start == start:
                tile = vt_ref.shape[2]
                for kt in range(ROW_TILE // tile):
                    vt_ref[kt] = val[kt * tile:(kt + 1) * tile, :].T.astype(BF16)


def _ln_qkv(x2, g, b, w_bf, col_scale, seq):
    assert A_WIDTH == ROW_TILE and B_WIDTH == ROW_TILE
    t, d = x2.shape
    n_out = w_bf.shape[1]
    steps_per_seq = seq // ROW_TILE
    vt_spec = lambda tile: pl.BlockSpec((ROW_TILE // tile, ROW_TILE, tile), lambda i: (i, 0, 0))
    vt_shape = lambda tile: jax.ShapeDtypeStruct((t // tile, ROW_TILE, tile), BF16)
    return pl.pallas_call(
        _ln_qkv_kernel,
        grid=(t // ROW_TILE,),
        in_specs=[
            pl.BlockSpec((ROW_TILE, d), lambda i: (i, 0)),
            pl.BlockSpec((1, d), lambda i: (0, 0)),
            pl.BlockSpec((1, d), lambda i: (0, 0)),
            pl.BlockSpec((d, n_out), lambda i: (0, 0)),
            pl.BlockSpec((1, n_out), lambda i: (0, 0)),
        ],
        out_specs=[pl.BlockSpec((1, ROW_TILE, n_out),
                                lambda i: (i // steps_per_seq, i % steps_per_seq, 0)),
                   vt_spec(DIFF_TILE), vt_spec(BAND_TILE)],
        out_shape=[jax.ShapeDtypeStruct((t // seq, seq, n_out), BF16),
                   vt_shape(DIFF_TILE), vt_shape(BAND_TILE)],
        compiler_params=pltpu.CompilerParams(
            dimension_semantics=("arbitrary",), vmem_limit_bytes=VMEM_LIMIT),
        name="ln_qkv",
    )(x2, g, b, w_bf, col_scale)


def _diff_attn_kernel(c_ref, lam_ref, g_ref, boff_ref, bdiag_ref, q_ref, k_ref, vt_ref, o_ref, s_scr):
    tq = q_ref.shape[1]
    tk = vt_ref.shape[2]
    h = pl.program_id(1)
    i = pl.program_id(2)
    c = c_ref[h]
    q = q_ref[0]
    lane = lax.broadcasted_iota(jnp.int32, q.shape, 1)
    zero = jnp.zeros_like(q)
    qq = jnp.concatenate([jnp.where(lane < A_HEAD_DIM, q, zero),
                          jnp.where(lane >= A_HEAD_DIM, q, zero)], axis=0)
    qqt = qq.astype(F32).T.astype(BF16)
    n_before = i * (tq // tk)

    def scores(j):
        kb = k_ref[0, pl.ds(pl.multiple_of(j * tk, tk), tk), :]
        return jnp.dot(kb, qqt, preferred_element_type=F32)

    def update(slot, table_ref, j, carry):
        m, l, acc = carry
        s = s_scr[slot] + table_ref[0]
        shift = c * (j * tk).astype(F32)
        m_new = jnp.maximum(m, jnp.max(s, axis=0, keepdims=True) + shift)
        alpha = jnp.exp2(m - m_new)
        p = jnp.exp2(s - (m_new - shift))
        l = alpha * l + jnp.sum(p, axis=0, keepdims=True)
        acc = alpha * acc + jnp.dot(vt_ref[j], p.astype(BF16), preferred_element_type=F32)
        return m_new, l, acc

    s_scr[0] = scores(0)

    def pair(t, carry):
        j = 2 * t
        s_scr[1] = scores(j + 1)
        carry = update(0, boff_ref, j, carry)
        s_scr[0] = scores(j + 2)
        return update(1, boff_ref, j + 1, carry)

    def quad(t, carry):
        return pair(2 * t + 1, pair(2 * t, carry))

    init = (jnp.full((1, 2 * tq), NEG, F32), jnp.zeros((1, 2 * tq), F32),
            jnp.zeros((LANES, 2 * tq), F32))
    carry = lax.fori_loop(0, n_before // 4, quad, init)
    carry = lax.fori_loop(n_before // 4 * 2, n_before // 2, pair, carry)

    def odd_tail(carry):
        s_scr[1] = scores(n_before)
        carry = update(0, boff_ref, n_before - 1, carry)
        return update(1, bdiag_ref, n_before, carry)

    def even_tail(carry):
        return update(0, bdiag_ref, n_before, carry)

    _, l, acc = lax.cond(n_before % 2 == 1, odd_tail, even_tail, carry)

    o_all = acc / l
    lv = lam_ref[...]
    lam = (jnp.exp(jnp.sum(lv[0:1] * lv[1:2], axis=1, keepdims=True))
           - jnp.exp(jnp.sum(lv[2:3] * lv[3:4], axis=1, keepdims=True)) + LAM_INIT)
    o = o_all[:, :tq] - lam * o_all[:, tq:]
    ms = jnp.mean(o * o, axis=0, keepdims=True)
    o = o * lax.rsqrt(ms + RMS_EPS) * (g_ref[...] * (1.0 - LAM_INIT))
    o_ref[0] = o.T.astype(BF16)


def _diff_attn(proj3, vt3, lam4, subln_g):
    bsz, seq, _ = proj3.shape
    tq = tk = DIFF_TILE
    c = jnp.asarray([2.0 ** (-8.0 * (h + 1) / A_HEADS) for h in range(A_HEADS)], F32) * LOG2E
    r = jnp.arange(tk, dtype=jnp.int32)[:, None]
    qrel = jnp.arange(2 * tq, dtype=jnp.int32)[None, :] % tq
    boff = c[:, None, None] * jnp.broadcast_to(r, (tk, 2 * tq)).astype(F32)
    allowed = (r // CHUNK) <= (qrel // CHUNK)
    bdiag = jnp.where(allowed, c[:, None, None] * (qrel - jnp.abs(qrel - r)).astype(F32), NEG)
    kblk = A_WIDTH // LANES
    return pl.pallas_call(
        _diff_attn_kernel,
        grid=(bsz, A_HEADS, seq // tq),
        in_specs=[
            pl.BlockSpec(memory_space=pltpu.SMEM),
            pl.BlockSpec((4, A_HEAD_DIM), lambda b, h, i: (0, 0)),
            pl.BlockSpec((LANES, 1), lambda b, h, i: (0, 0)),
            pl.BlockSpec((1, tk, 2 * tq), lambda b, h, i: (h, 0, 0)),
            pl.BlockSpec((1, tk, 2 * tq), lambda b, h, i: (h, 0, 0)),
            pl.BlockSpec((1, tq, LANES), lambda b, h, i: (b, i, h)),
            pl.BlockSpec((1, seq, LANES), lambda b, h, i: (b, 0, kblk + h)),
            pl.BlockSpec((seq // tk, LANES, tk), lambda b, h, i: (b, h, 0)),
        ],
        out_specs=pl.BlockSpec((1, tq, LANES), lambda b, h, i: (b, i, h)),
        out_shape=jax.ShapeDtypeStruct((bsz, seq, A_WIDTH), BF16),
        scratch_shapes=[pltpu.VMEM((2, tk, 2 * tq), F32)],
        compiler_params=pltpu.CompilerParams(
            dimension_semantics=("arbitrary", "arbitrary", "arbitrary"),
            vmem_limit_bytes=VMEM_LIMIT),
        name="diff_attn",
    )(c, lam4, subln_g.reshape(LANES, 1), boff, bdiag, proj3, proj3, vt3)


def _band_attn_kernel(bias_ref, q_ref, k_ref, vt_ref, o_ref, s_scr):
    i = pl.program_id(2)
    tq = BAND_TILE
    units = q_ref.shape[1] // (BAND_PAIR * tq)
    key_tiles = BAND_TILES + BAND_PAIR - 1
    row = lax.broadcasted_iota(jnp.int32, (LANES, tq), 0)

    def key_tile(n, t):
        jt = (i * units + n) * BAND_PAIR - (BAND_TILES - 1) + t
        return jnp.maximum(jt, 0), jt < 0

    def scores(n):
        qs = []
        for a in range(BAND_PAIR):
            q = q_ref[0, (n * BAND_PAIR + a) * tq:(n * BAND_PAIR + a + 1) * tq, :]
            lane = lax.broadcasted_iota(jnp.int32, q.shape, 1)
            zero = jnp.zeros_like(q)
            qs += [jnp.where(lane < B_HEAD_DIM, q, zero), jnp.where(lane >= B_HEAD_DIM, q, zero)]
        qqt = jnp.concatenate(qs, axis=0).astype(F32).T.astype(BF16)
        ks = [k_ref[0, pl.ds(pl.multiple_of(key_tile(n, t)[0] * tq, tq), tq), :] for t in range(key_tiles)]
        return jnp.dot(jnp.concatenate(ks, axis=0), qqt, preferred_element_type=F32)

    def finish(n):
        bias = [bias_ref[0, jnp.where(key_tile(n, t)[1], 1, 0), t] for t in range(key_tiles)]
        s = s_scr[n % 2] + jnp.concatenate(bias, axis=0)
        m = jnp.max(s, axis=0, keepdims=True)
        p = jnp.exp2(s - m)
        l = jnp.sum(p, axis=0, keepdims=True)
        vt = jnp.concatenate([vt_ref[key_tile(n, t)[0]] for t in range(key_tiles)], axis=1)
        o = jnp.dot(vt, p.astype(BF16), preferred_element_type=F32) / l
        for a in range(BAND_PAIR):
            oa = jnp.where(row < B_HEAD_DIM, o[:, 2 * a * tq:(2 * a + 1) * tq],
                           o[:, (2 * a + 1) * tq:(2 * a + 2) * tq])
            o_ref[0, (n * BAND_PAIR + a) * tq:(n * BAND_PAIR + a + 1) * tq, :] = oa.T.astype(BF16)

    s_scr[0] = scores(0)
    for n in range(units):
        if n + 1 < units:
            s_scr[(n + 1) % 2] = scores(n + 1)
        finish(n)


def _band_bias(rel_bias, tq):
    past = B_PAST_CHUNKS * CHUNK
    band = tq + past
    assert tq - 1 <= REL_CLIP <= past
    qi = jnp.arange(tq)
    kj = jnp.arange(band)
    cq = qi[:, None] // CHUNK
    ck = kj[None, :] // CHUNK
    allowed = (ck >= cq) & (ck <= cq + B_PAST_CHUNKS)
    tab = rel_bias.astype(F32) * LOG2E
    n_diag = band + tq - 1
    n_unclipped = REL_CLIP + tq
    w = jnp.concatenate([tab[:, REL_CLIP - (tq - 1):],
                         jnp.broadcast_to(tab[:, -1:], (B_HEADS, n_diag - n_unclipped))], axis=1)
    shifted = jnp.tile(w, (1, tq + 1))[:, :tq * (n_diag + 1)].reshape(B_HEADS, tq, n_diag + 1)
    bias = jnp.flip(shifted[:, :, :band], axis=2)
    return jnp.where(allowed[None], bias, NEG)


def _band_attn(proj3, vtb3, rel_bias):
    bsz, seq, _ = proj3.shape
    tq = BAND_TILE
    groups = B_HEADS // 2
    key_tiles = BAND_TILES + BAND_PAIR - 1
    cols = BAND_PAIR * 2 * tq
    bias = _band_bias(rel_bias, tq)
    bias = bias.reshape(groups, 2, tq, BAND_TILES, tq)
    masked = jnp.full((groups, 2, tq, 1, tq), NEG, F32)
    per_tile = [jnp.concatenate([masked] * a + [bias] + [masked] * (BAND_PAIR - 1 - a), axis=3)
                for a in range(BAND_PAIR)]
    bias = jnp.stack(per_tile, axis=1)
    bias = bias.transpose(0, 4, 5, 1, 2, 3).reshape(groups, 1, key_tiles, tq, cols)
    bias = jnp.concatenate([bias, jnp.full_like(bias, NEG)], axis=1)
    qblk = 3 * A_WIDTH // LANES
    kblk = qblk + B_WIDTH // LANES
    rows = min(BAND_STEP_ROWS, seq)
    return pl.pallas_call(
        _band_attn_kernel,
        grid=(bsz, groups, seq // rows),
        in_specs=[
            pl.BlockSpec((1, 2, key_tiles, tq, cols), lambda b, g, i: (g, 0, 0, 0, 0)),
            pl.BlockSpec((1, rows, LANES), lambda b, g, i: (b, i, qblk + g)),
            pl.BlockSpec((1, seq, LANES), lambda b, g, i: (b, 0, kblk + g)),
            pl.BlockSpec((seq // tq, LANES, tq), lambda b, g, i: (b, g, 0)),
        ],
        out_specs=pl.BlockSpec((1, rows, LANES), lambda b, g, i: (b, i, g)),
        out_shape=jax.ShapeDtypeStruct((bsz, seq, B_WIDTH), BF16),
        scratch_shapes=[pltpu.VMEM((2, key_tiles * tq, cols), F32)],
        compiler_params=pltpu.CompilerParams(
            dimension_semantics=("arbitrary", "arbitrary", "arbitrary"),
            vmem_limit_bytes=VMEM_LIMIT),
        name="band_attn",
    )(bias, proj3, proj3, vtb3)


def _out_router_kernel(x_ref, gi_ref, bi_ref, oa_ref, ob_ref, wa_ref, wb_ref, g1_ref, b1_ref,
                       rwt_ref, rb_ref, tri_ref, h1_ref, idx_ref, gate_ref, rank_ref, cnt_ref,
                       carry_ref, *, steps_per_tile):
    h = _layer_norm(x_ref[...], gi_ref[...], bi_ref[...])
    mix = (jnp.dot(oa_ref[0], wa_ref[...], preferred_element_type=F32)
           + jnp.dot(ob_ref[0], wb_ref[...], preferred_element_type=F32))
    h1 = _layer_norm(DEEPNORM_ALPHA * h + mix, g1_ref[...], b1_ref[...])
    nchunk = h1.shape[1] // LANES
    for c in range(nchunk):
        h1_ref[pl.ds(c, ROW_TILE, stride=nchunk), :] = h1[:, c * LANES:(c + 1) * LANES]
    lt = lax.dot_general(rwt_ref[...], h1, (((1,), (1,)), ((), ())),
                         precision=lax.Precision.HIGHEST, preferred_element_type=F32)
    lt = lt + rb_ref[...]
    eidx = lax.broadcasted_iota(jnp.int32, lt.shape, 0)
    vals, idxs, hots = [], [], []
    for _ in range(TOP_K):
        mx = jnp.max(lt, axis=0, keepdims=True)
        am = jnp.min(jnp.where(lt == mx, eidx, N_EXPERTS), axis=0, keepdims=True)
        hit = eidx == am
        vals.append(mx)
        idxs.append(am)
        hots.append(jnp.where(hit, 1.0, 0.0))
        lt = jnp.where(hit, -jnp.inf, lt)
    ex = [jnp.exp(v - vals[0]) for v in vals]
    den = ex[0] + ex[1] + ex[2] + ex[3]
    idx_ref[...] = jnp.concatenate(idxs, axis=0)
    gate_ref[...] = jnp.concatenate([e / den for e in ex], axis=0)

    @pl.when(pl.program_id(0) % steps_per_tile == 0)
    def _():
        carry_ref[...] = jnp.zeros_like(carry_ref)

    hot = (hots[0] + hots[1]) + (hots[2] + hots[3])
    before = jnp.dot(hot.astype(BF16), tri_ref[...], preferred_element_type=F32) + carry_ref[...]
    rank_ref[...] = jnp.concatenate(
        [jnp.sum(hk * before, axis=0, keepdims=True) for hk in hots], axis=0).astype(jnp.int32)
    total = carry_ref[...] + jnp.sum(hot, axis=1, keepdims=True)
    carry_ref[...] = total
    cnt_ref[0] = total.astype(jnp.int32)


def _out_router(x2, gi, bi, o_a, o_b, wa, wb, g1, b1, rwt, rb, moe_tile):
    t, d = x2.shape
    nchunk = d // LANES
    row = lambda i: (i, 0)
    fixed = lambda i: (0, 0)
    steps_per_seq = o_a.shape[1] // ROW_TILE
    seq_row = lambda i: (i // steps_per_seq, i % steps_per_seq, 0)
    tri = jnp.triu(jnp.ones((ROW_TILE, ROW_TILE), BF16), k=1)
    return pl.pallas_call(
        functools.partial(_out_router_kernel, steps_per_tile=moe_tile // ROW_TILE),
        grid=(t // ROW_TILE,),
        in_specs=[
            pl.BlockSpec((ROW_TILE, d), row),
            pl.BlockSpec((1, d), fixed),
            pl.BlockSpec((1, d), fixed),
            pl.BlockSpec((1, ROW_TILE, A_WIDTH), seq_row),
            pl.BlockSpec((1, ROW_TILE, B_WIDTH), seq_row),
            pl.BlockSpec((A_WIDTH, d), fixed),
            pl.BlockSpec((B_WIDTH, d), fixed),
            pl.BlockSpec((1, d), fixed),
            pl.BlockSpec((1, d), fixed),
            pl.BlockSpec((N_EXPERTS, d), fixed),
            pl.BlockSpec((N_EXPERTS, 1), fixed),
            pl.BlockSpec((ROW_TILE, ROW_TILE), fixed),
        ],
        out_specs=[
            pl.BlockSpec((ROW_TILE * nchunk, LANES), row),
            pl.BlockSpec((TOP_K, ROW_TILE), lambda i: (0, i)),
            pl.BlockSpec((TOP_K, ROW_TILE), lambda i: (0, i)),
            pl.BlockSpec((TOP_K, ROW_TILE), lambda i: (0, i)),
            pl.BlockSpec((1, N_EXPERTS, 1), lambda i: (i, 0, 0)),
        ],
        out_shape=[
            jax.ShapeDtypeStruct((t * nchunk, LANES), F32),
            jax.ShapeDtypeStruct((TOP_K, t), jnp.int32),
            jax.ShapeDtypeStruct((TOP_K, t), F32),
            jax.ShapeDtypeStruct((TOP_K, t), jnp.int32),
            jax.ShapeDtypeStruct((t // ROW_TILE, N_EXPERTS, 1), jnp.int32),
        ],
        scratch_shapes=[pltpu.VMEM((N_EXPERTS, 1), F32)],
        compiler_params=pltpu.CompilerParams(
            dimension_semantics=("arbitrary",), vmem_limit_bytes=VMEM_LIMIT),
        name="out_router",
    )(x2, gi, bi, o_a, o_b, wa, wb, g1, b1, rwt, rb, tri)


def _split_w1_kernel(w_ref, perm_ref, g_ref, l_ref):
    rows = w_ref.shape[1]
    even = (lax.broadcasted_iota(jnp.int32, (rows, LANES), 1) & 1) == 0
    perm = perm_ref[...]
    for g in range(w_ref.shape[2] // (2 * LANES)):
        v0 = w_ref[0, :, 2 * g * LANES:(2 * g + 1) * LANES]
        v1 = w_ref[0, :, (2 * g + 1) * LANES:(2 * g + 2) * LANES]
        glu = jnp.where(even, v0, pltpu.roll(v1, 1, axis=1)).astype(BF16)
        lin = jnp.where(even, pltpu.roll(v0, LANES - 1, axis=1), v1).astype(BF16)
        g_ref[0, :, g * LANES:(g + 1) * LANES] = jnp.dot(
            glu, perm, preferred_element_type=F32).astype(BF16)
        l_ref[0, :, g * LANES:(g + 1) * LANES] = jnp.dot(
            lin, perm, preferred_element_type=F32).astype(BF16)


def _split_w1(w1e):
    e, d, f2 = w1e.shape
    rows = 256
    half = LANES // 2
    unit = jnp.arange(LANES)
    perm = (jnp.arange(LANES)[:, None] == (2 * (unit % half) + unit // half)[None, :]).astype(BF16)
    return pl.pallas_call(
        _split_w1_kernel,
        grid=(e, d // rows),
        in_specs=[pl.BlockSpec((1, rows, f2), lambda i, j: (i, j, 0)),
                  pl.BlockSpec((LANES, LANES), lambda i, j: (0, 0))],
        out_specs=[pl.BlockSpec((1, rows, f2 // 2), lambda i, j: (i, j, 0)),
                   pl.BlockSpec((1, rows, f2 // 2), lambda i, j: (i, j, 0))],
        out_shape=[jax.ShapeDtypeStruct((e, d, f2 // 2), BF16),
                   jax.ShapeDtypeStruct((e, d, f2 // 2), BF16)],
        compiler_params=pltpu.CompilerParams(
            dimension_semantics=("arbitrary", "arbitrary"), vmem_limit_bytes=VMEM_LIMIT),
        name="split_w1",
    )(w1e, perm)


def _moe_kernel(cnt_ref, offs_ref,
                w1g_ref, w1l_ref, b1g_ref, b1l_ref, w2_ref, b2_ref, g2_ref, bb2_ref,
                pos_hbm, gate_hbm, h_hbm,
                out_hbm, hbuf, ybuf, stage, out_buf, list_tok, list_gate, pos_in, gate_in, n_sorted, sem,
                *, tile):
    b = pl.program_id(0)
    e = pl.program_id(1)
    n_tiles = pl.num_programs(0)
    n_exp = pl.num_programs(1)
    d_model = w1g_ref.shape[1]
    d_ff = w1g_ref.shape[2]
    nchunk = d_model // LANES
    tile_rows = tile * nchunk
    pairs = tile * TOP_K
    n_sort_chunks = pairs // SORT_CHUNK
    half = (b % 2) * pairs
    other_half = pairs - half

    def stage_rows(r):
        return pl.ds(pl.multiple_of(r * nchunk, nchunk), nchunk)

    def listed_rows(p):
        return pl.ds(pl.multiple_of(list_tok[p], nchunk), nchunk)

    def sort_pair(j, first_row, dst_half):
        p = pos_in[j] + dst_half
        list_tok[p] = first_row
        list_gate[p] = gate_in[j]

    def sort_chunks(lo, hi, dst_half):
        def body(jo, c):
            for u in range(SORT_UNROLL):
                j = jo * SORT_UNROLL + u
                sort_pair(j, (j & (tile - 1)) * nchunk, dst_half)
            return c
        lax.fori_loop(lo * (SORT_CHUNK // SORT_UNROLL), hi * (SORT_CHUNK // SORT_UNROLL), body, 0)

    def load_pairs(t):
        rows = pl.ds(pl.multiple_of(t * pairs, pairs), pairs)
        copies = [pltpu.make_async_copy(pos_hbm.at[rows], pos_in, sem.at[2]),
                  pltpu.make_async_copy(gate_hbm.at[rows], gate_in, sem.at[3])]
        for cp in copies:
            cp.start()
        for cp in copies:
            cp.wait()

    @pl.when(jnp.logical_and(b == 0, e == 0))
    def _():
        stage[...] = jnp.zeros_like(stage)

        def pad_body(j, c):
            list_tok[pairs + j] = 0
            list_tok[2 * pairs + j] = 0
            return c
        lax.fori_loop(0, LIST_PAD, pad_body, 0)
        load_pairs(0)
        n_sorted[0] = 0

    @pl.when(e == 0)
    def _():
        src = h_hbm.at[pl.ds(pl.multiple_of(b * tile_rows, tile_rows), tile_rows), :]
        load = pltpu.make_async_copy(src, hbuf, sem.at[0])
        load.start()
        ybuf[...] = jnp.zeros_like(ybuf)
        sort_chunks(jnp.minimum(n_sorted[0], n_sort_chunks), n_sort_chunks, half)
        n_sorted[0] = 0

        @pl.when(b + 1 < n_tiles)
        def _():
            load_pairs(b + 1)
        load.wait()

    n = cnt_ref[b * n_exp + e]
    base = offs_ref[b * n_exp + e] + half

    def run_block(m_rows, p0, nrows):
        def gather_body(ci, c):
            for u in range(ROW_UNROLL):
                r = ci * ROW_UNROLL + u
                stage[stage_rows(r), :] = hbuf[listed_rows(p0 + r), :]
            return c
        lax.fori_loop(0, (nrows + ROW_UNROLL - 1) // ROW_UNROLL, gather_body, 0)

        if m_rows == MOE_BLOCK:
            chunk = jnp.minimum(n_sorted[0], n_sort_chunks - 1)
            n_sorted[0] = n_sorted[0] + 1
            j0 = chunk * SORT_CHUNK
            row0 = (j0 & (tile - 1)) * nchunk
            for u in range(SORT_CHUNK):
                sort_pair(j0 + u, row0 + u * nchunk, other_half)

        x = jnp.concatenate(
            [stage[pl.ds(c, m_rows, stride=nchunk), :] for c in range(nchunk)],
            axis=1).astype(BF16)
        y = b2_ref[0]
        for hh in range(d_ff // FF_CHUNK):
            sl = slice(hh * FF_CHUNK, (hh + 1) * FF_CHUNK)
            hg = jnp.dot(x, w1g_ref[0, :, sl], preferred_element_type=F32) + b1g_ref[0, :, sl]
            hl = jnp.dot(x, w1l_ref[0, :, sl], preferred_element_type=F32) + b1l_ref[0, :, sl]
            xg = jnp.minimum(hg, SWIGLU_LIMIT)
            xl = jnp.clip(hl, -SWIGLU_LIMIT, SWIGLU_LIMIT)
            act = xg * jax.nn.sigmoid(SWIGLU_ALPHA * xg) * (xl + 1.0)
            y = y + jnp.dot(act.astype(BF16), w2_ref[0, sl, :], preferred_element_type=F32)
        for c in range(nchunk):
            stage[pl.ds(c, m_rows, stride=nchunk), :] = y[:, c * LANES:(c + 1) * LANES]

        def add_body(ci, c):
            rows, vals = [], []
            for u in range(SUBLANES):
                r = ci * SUBLANES + u
                dst = listed_rows(p0 + r)
                rows.append(dst)
                vals.append(ybuf[dst, :] + list_gate[p0 + r] * stage[stage_rows(r), :])
            for dst, val in zip(rows, vals):
                ybuf[dst, :] = val
            return c
        n_full = nrows // SUBLANES
        lax.fori_loop(0, n_full, add_body, 0)

        def add_tail(r, c):
            dst = listed_rows(p0 + r)
            ybuf[dst, :] = ybuf[dst, :] + list_gate[p0 + r] * stage[stage_rows(r), :]
            return c
        lax.fori_loop(n_full * SUBLANES, nrows, add_tail, 0)

    n_big = n // MOE_BLOCK
    rem = n - n_big * MOE_BLOCK
    rem_is_big = rem > MOE_BLOCK - SMALL_BLOCK
    n_big_blocks = n_big + rem_is_big.astype(jnp.int32)
    n_small_blocks = jnp.where(rem_is_big, 0, (rem + SMALL_BLOCK - 1) // SMALL_BLOCK)

    def big_body(s, c):
        run_block(MOE_BLOCK, base + s * MOE_BLOCK, jnp.minimum(MOE_BLOCK, n - s * MOE_BLOCK))
        return c
    lax.fori_loop(0, n_big_blocks, big_body, 0)

    def small_body(s, c):
        run_block(SMALL_BLOCK, base + n_big * MOE_BLOCK + s * SMALL_BLOCK,
                  jnp.minimum(SMALL_BLOCK, rem - s * SMALL_BLOCK))
        return c
    lax.fori_loop(0, n_small_blocks, small_body, 0)

    @pl.when(e == n_exp - 1)
    def _():
        def out_copy(c, slot):
            rows = pl.ds(pl.multiple_of(b * tile + c * ROW_TILE, ROW_TILE), ROW_TILE)
            return pltpu.make_async_copy(out_buf.at[slot], out_hbm.at[rows, :], sem.at[4 + slot])

        def chunk_body(c, carry):
            slot = c % 2

            @pl.when(c >= 2)
            def _():
                out_copy(c - 2, slot).wait()
            first = c * (ROW_TILE * nchunk)
            pieces = [DEEPNORM_ALPHA * hbuf[pl.ds(first + k, ROW_TILE, stride=nchunk), :]
                      + ybuf[pl.ds(first + k, ROW_TILE, stride=nchunk), :] for k in range(nchunk)]
            out_buf[slot] = _layer_norm(jnp.concatenate(pieces, axis=1), g2_ref[...], bb2_ref[...])
            out_copy(c, slot).start()
            return carry
        n_out = tile // ROW_TILE
        lax.fori_loop(0, n_out, chunk_body, 0)
        for c in range(max(n_out - 2, 0), n_out):
            out_copy(c, c % 2).wait()


def _moe(h1t, cnt, offs, pos_flat, gate_flat, w1g, w1l, b1g, b1l, w2, b2, g2, bb2, tile):
    n_exp, d, f = w1g.shape
    nchunk = d // LANES
    n_tiles = h1t.shape[0] // (tile * nchunk)
    pairs = tile * TOP_K
    assert tile % SORT_CHUNK == 0
    expert = lambda b, e, cnt, offs: (e, 0, 0)
    grid_spec = pltpu.PrefetchScalarGridSpec(
        num_scalar_prefetch=2,
        grid=(n_tiles, n_exp),
        in_specs=[
            pl.BlockSpec((1, d, f), expert),
            pl.BlockSpec((1, d, f), expert),
            pl.BlockSpec((1, 1, f), expert),
            pl.BlockSpec((1, 1, f), expert),
            pl.BlockSpec((1, f, d), expert),
            pl.BlockSpec((1, 1, d), expert),
            pl.BlockSpec((1, d), lambda b, e, cnt, offs: (0, 0)),
            pl.BlockSpec((1, d), lambda b, e, cnt, offs: (0, 0)),
            pl.BlockSpec(memory_space=pl.ANY),
            pl.BlockSpec(memory_space=pl.ANY),
            pl.BlockSpec(memory_space=pl.ANY),
        ],
        out_specs=pl.BlockSpec(memory_space=pl.ANY),
        scratch_shapes=[
            pltpu.VMEM((tile * nchunk, LANES), F32),
            pltpu.VMEM((tile * nchunk, LANES), F32),
            pltpu.VMEM((MOE_BLOCK * nchunk, LANES), F32),
            pltpu.VMEM((2, ROW_TILE, d), F32),
            pltpu.SMEM((2 * pairs + LIST_PAD,), jnp.int32),
            pltpu.SMEM((2 * pairs + LIST_PAD,), F32),
            pltpu.SMEM((pairs,), jnp.int32),
            pltpu.SMEM((pairs,), F32),
            pltpu.SMEM((1,), jnp.int32),
            pltpu.SemaphoreType.DMA((6,)),
        ],
    )
    return pl.pallas_call(
        functools.partial(_moe_kernel, tile=tile),
        grid_spec=grid_spec,
        out_shape=jax.ShapeDtypeStruct((n_tiles * tile, d), F32),
        compiler_params=pltpu.CompilerParams(
            dimension_semantics=("arbitrary", "arbitrary"), vmem_limit_bytes=MOE_VMEM_LIMIT),
        name="moe",
    )(cnt, offs, w1g, w1l, b1g, b1l, w2, b2, g2, bb2, pos_flat, gate_flat, h1t)


def kernel(x, ln_in_g, ln_in_b, w_in, lambda_q1, lambda_k1, lambda_q2, lambda_k2, subln_g, rel_bias,
           w_out, ln1_g, ln1_b, router_w, router_b, w1, b1, w2, b2, ln2_g, ln2_b):
    bsz, seq, d = x.shape
    t = bsz * seq
    x2 = x.reshape(t, d)
    row = lambda v: v.reshape(1, -1).astype(F32)

    qs = A_HEAD_DIM ** -0.5 * LOG2E
    col_scale = jnp.concatenate([
        jnp.full((A_WIDTH,), qs, F32), jnp.ones((2 * A_WIDTH,), F32),
        jnp.full((B_WIDTH,), B_HEAD_DIM ** -0.5 * LOG2E, F32), jnp.ones((2 * B_WIDTH,), F32)]).reshape(1, -1)
    proj3, vt3, vtb3 = _ln_qkv(x2, row(ln_in_g), row(ln_in_b), w_in[0].astype(BF16), col_scale, seq)

    lam4 = jnp.stack([lambda_q1[0], lambda_k1[0], lambda_q2[0], lambda_k2[0]]).astype(F32)
    o_a = _diff_attn(proj3, vt3, lam4, subln_g[0].astype(F32))
    o_b = _band_attn(proj3, vtb3, rel_bias[0])

    w_o = w_out[0].astype(BF16)
    tile = min(MOE_TILE, t)
    n_tiles = t // tile
    h1t, top_idx, gates, rank, run_cnt = _out_router(
        x2, row(ln_in_g), row(ln_in_b), o_a, o_b,
        w_o[:A_WIDTH], w_o[A_WIDTH:], row(ln1_g[0]), row(ln1_b[0]),
        router_w[0].T.astype(F32), router_b[0].reshape(-1, 1).astype(F32), tile)

    steps = tile // ROW_TILE
    cnt = run_cnt[steps - 1::steps, :, 0]
    offs = jnp.cumsum(cnt, axis=1) - cnt
    hot = top_idx.reshape(TOP_K, n_tiles, tile, 1) == jnp.arange(N_EXPERTS, dtype=jnp.int32)
    pos = rank + jnp.sum(jnp.where(hot, offs[None, :, None, :], 0), axis=-1).reshape(TOP_K, t)
    per_tile = lambda a: a.reshape(TOP_K, n_tiles, tile).transpose(1, 0, 2).reshape(-1)
    w1g, w1l = _split_w1(w1[0])
    b1e = b1[0].astype(F32)[:, None, :]
    out = _moe(h1t, cnt.reshape(-1), offs.reshape(-1), per_tile(pos), per_tile(gates),
               w1g, w1l, b1e[:, :, 0::2], b1e[:, :, 1::2],
               w2[0].astype(BF16), b2[0][:, None, :].astype(F32), row(ln2_g[0]), row(ln2_b[0]), tile)
    return out.reshape(bsz, seq, d)
```

```python
import functools
import math

import jax
import jax.numpy as jnp
from jax import lax
from jax.experimental import pallas as pl
from jax.experimental.pallas import tpu as pltpu

F32 = jnp.float32
BF16 = jnp.bfloat16

CHUNK = 64
A_HEADS = 4
A_HEAD_DIM = 64
A_WIDTH = A_HEADS * 2 * A_HEAD_DIM
B_HEADS = 8
B_HEAD_DIM = 64
B_WIDTH = B_HEADS * B_HEAD_DIM
B_PAST_CHUNKS = 8
REL_CLIP = 256
N_EXPERTS = 32
TOP_K = 4
SWIGLU_ALPHA = 1.702
SWIGLU_LIMIT = 7.0
MOE_BLOCK = 512
LN_EPS = 1e-5
RMS_EPS = 1e-5
DEPTH = 1
DEEPNORM_ALPHA = (2 * DEPTH) ** 0.25
LAM_INIT = 0.8 - 0.6 * math.exp(-0.3 * 0)

LOG2E = 1.4426950408889634
NEG = -1e30
LANES = 128
SUBLANES = 8
ROW_TILE = 512
VMEM_LIMIT = 48 * 1024 * 1024
DIFF_TILE = 256
BAND_TILE = 128
BAND_TILES = (BAND_TILE + B_PAST_CHUNKS * CHUNK) // BAND_TILE
BAND_PAIR = 2
BAND_STEP_ROWS = 2048
MOE_TILE = 4096
MOE_VMEM_LIMIT = 60 * 1024 * 1024
SMALL_BLOCK = 128
FF_CHUNK = 1024
SORT_UNROLL = 8
SORT_CHUNK = 512
ROW_UNROLL = 32
LIST_PAD = 128


def _layer_norm(x, g, b):
    mu = jnp.mean(x, axis=-1, keepdims=True)
    xc = x - mu
    var = jnp.mean(xc * xc, axis=-1, keepdims=True)
    return xc * lax.rsqrt(var + LN_EPS) * g + b


def _ln_qkv_kernel(x_ref, g_ref, b_ref, w_ref, cs_ref, o_ref, vta_ref, vtb_ref):
    h = _layer_norm(x_ref[...], g_ref[...], b_ref[...])
    hb = h.astype(BF16)
    n_out = w_ref.shape[1]
    for c in range(n_out // ROW_TILE):
        sl = slice(c * ROW_TILE, (c + 1) * ROW_TILE)
        val = jnp.dot(hb, w_ref[:, sl], preferred_element_type=F32) * cs_ref[:, sl]
        o_ref[0, :, sl] = val.astype(BF16)
        for start, vt_ref in ((2 * A_WIDTH, vta_ref), (3 * A_WIDTH + 2 * B_WIDTH, vtb_ref)):
            if sl.start == start:
                tile = vt_ref.shape[2]
                for kt in range(ROW_TILE // tile):
                    vt_ref[kt] = val[kt * tile:(kt + 1) * tile, :].T.astype(BF16)


def _ln_qkv(x2, g, b, w_bf, col_scale, seq):
    assert A_WIDTH == ROW_TILE and B_WIDTH == ROW_TILE
    t, d = x2.shape
    n_out = w_bf.shape[1]
    steps_per_seq = seq // ROW_TILE
    vt_spec = lambda tile: pl.BlockSpec((ROW_TILE // tile, ROW_TILE, tile), lambda i: (i, 0, 0))
    vt_shape = lambda tile: jax.ShapeDtypeStruct((t // tile, ROW_TILE, tile), BF16)
    return pl.pallas_call(
        _ln_qkv_kernel,
        grid=(t // ROW_TILE,),
        in_specs=[
            pl.BlockSpec((ROW_TILE, d), lambda i: (i, 0)),
            pl.BlockSpec((1, d), lambda i: (0, 0)),
            pl.BlockSpec((1, d), lambda i: (0, 0)),
            pl.BlockSpec((d, n_out), lambda i: (0, 0)),
            pl.BlockSpec((1, n_out), lambda i: (0, 0)),
        ],
        out_specs=[pl.BlockSpec((1, ROW_TILE, n_out),
                                lambda i: (i // steps_per_seq, i % steps_per_seq, 0)),
                   vt_spec(DIFF_TILE), vt_spec(BAND_TILE)],
        out_shape=[jax.ShapeDtypeStruct((t // seq, seq, n_out), BF16),
                   vt_shape(DIFF_TILE), vt_shape(BAND_TILE)],
        compiler_params=pltpu.CompilerParams(
            dimension_semantics=("arbitrary",), vmem_limit_bytes=VMEM_LIMIT),
        name="ln_qkv",
    )(x2, g, b, w_bf, col_scale)


def _diff_attn_kernel(c_ref, lam_ref, g_ref, boff_ref, bdiag_ref, q_ref, k_ref, vt_ref, o_ref, s_scr):
    tq = q_ref.shape[1]
    tk = vt_ref.shape[2]
    h = pl.program_id(1)
    i = pl.program_id(2)
    c = c_ref[h]
    q = q_ref[0]
    lane = lax.broadcasted_iota(jnp.int32, q.shape, 1)
    zero = jnp.zeros_like(q)
    qq = jnp.concatenate([jnp.where(lane < A_HEAD_DIM, q, zero),
                          jnp.where(lane >= A_HEAD_DIM, q, zero)], axis=0)
    qqt = qq.astype(F32).T.astype(BF16)
    n_before = i * (tq // tk)

    def scores(j):
        kb = k_ref[0, pl.ds(pl.multiple_of(j * tk, tk), tk), :]
        return jnp.dot(kb, qqt, preferred_element_type=F32)

    def update(slot, table_ref, j, carry):
        m, l, acc = carry
        s = s_scr[slot] + table_ref[0]
        shift = c * (j * tk).astype(F32)
        m_new = jnp.maximum(m, jnp.max(s, axis=0, keepdims=True) + shift)
        alpha = jnp.exp2(m - m_new)
        p = jnp.exp2(s - (m_new - shift))
        l = alpha * l + jnp.sum(p, axis=0, keepdims=True)
        acc = alpha * acc + jnp.dot(vt_ref[j], p.astype(BF16), preferred_element_type=F32)
        return m_new, l, acc

    s_scr[0] = scores(0)

    def pair(t, carry):
        j = 2 * t
        s_scr[1] = scores(j + 1)
        carry = update(0, boff_ref, j, carry)
        s_scr[0] = scores(j + 2)
        return update(1, boff_ref, j + 1, carry)

    def quad(t, carry):
        return pair(2 * t + 1, pair(2 * t, carry))

    init = (jnp.full((1, 2 * tq), NEG, F32), jnp.zeros((1, 2 * tq), F32),
            jnp.zeros((LANES, 2 * tq), F32))
    carry = lax.fori_loop(0, n_before // 4, quad, init)
    carry = lax.fori_loop(n_before // 4 * 2, n_before // 2, pair, carry)

    def odd_tail(carry):
        s_scr[1] = scores(n_before)
        carry = update(0, boff_ref, n_before - 1, carry)
        return update(1, bdiag_ref, n_before, carry)

    def even_tail(carry):
        return update(0, bdiag_ref, n_before, carry)

    _, l, acc = lax.cond(n_before % 2 == 1, odd_tail, even_tail, carry)

    o_all = acc / l
    lv = lam_ref[...]
    lam = (jnp.exp(jnp.sum(lv[0:1] * lv[1:2], axis=1, keepdims=True))
           - jnp.exp(jnp.sum(lv[2:3] * lv[3:4], axis=1, keepdims=True)) + LAM_INIT)
    o = o_all[:, :tq] - lam * o_all[:, tq:]
    ms = jnp.mean(o * o, axis=0, keepdims=True)
    o = o * lax.rsqrt(ms + RMS_EPS) * (g_ref[...] * (1.0 - LAM_INIT))
    o_ref[0] = o.T.astype(BF16)


def _diff_attn(proj3, vt3, lam4, subln_g):
    bsz, seq, _ = proj3.shape
    tq = tk = DIFF_TILE
    c = jnp.asarray([2.0 ** (-8.0 * (h + 1) / A_HEADS) for h in range(A_HEADS)], F32) * LOG2E
    r = jnp.arange(tk, dtype=jnp.int32)[:, None]
    qrel = jnp.arange(2 * tq, dtype=jnp.int32)[None, :] % tq
    boff = c[:, None, None] * jnp.broadcast_to(r, (tk, 2 * tq)).astype(F32)
    allowed = (r // CHUNK) <= (qrel // CHUNK)
    bdiag = jnp.where(allowed, c[:, None, None] * (qrel - jnp.abs(qrel - r)).astype(F32), NEG)
    kblk = A_WIDTH // LANES
    return pl.pallas_call(
        _diff_attn_kernel,
        grid=(bsz, A_HEADS, seq // tq),
        in_specs=[
            pl.BlockSpec(memory_space=pltpu.SMEM),
            pl.BlockSpec((4, A_HEAD_DIM), lambda b, h, i: (0, 0)),
            pl.BlockSpec((LANES, 1), lambda b, h, i: (0, 0)),
            pl.BlockSpec((1, tk, 2 * tq), lambda b, h, i: (h, 0, 0)),
            pl.BlockSpec((1, tk, 2 * tq), lambda b, h, i: (h, 0, 0)),
            pl.BlockSpec((1, tq, LANES), lambda b, h, i: (b, i, h)),
            pl.BlockSpec((1, seq, LANES), lambda b, h, i: (b, 0, kblk + h)),
            pl.BlockSpec((seq // tk, LANES, tk), lambda b, h, i: (b, h, 0)),
        ],
        out_specs=pl.BlockSpec((1, tq, LANES), lambda b, h, i: (b, i, h)),
        out_shape=jax.ShapeDtypeStruct((bsz, seq, A_WIDTH), BF16),
        scratch_shapes=[pltpu.VMEM((2, tk, 2 * tq), F32)],
        compiler_params=pltpu.CompilerParams(
            dimension_semantics=("arbitrary", "arbitrary", "arbitrary"),
            vmem_limit_bytes=VMEM_LIMIT),
        name="diff_attn",
    )(c, lam4, subln_g.reshape(LANES, 1), boff, bdiag, proj3, proj3, vt3)


def _band_attn_kernel(bias_ref, q_ref, k_ref, vt_ref, o_ref, s_scr):
    i = pl.program_id(2)
    tq = BAND_TILE
    units = q_ref.shape[1] // (BAND_PAIR * tq)
    key_tiles = BAND_TILES + BAND_PAIR - 1
    row = lax.broadcasted_iota(jnp.int32, (LANES, tq), 0)

    def key_tile(n, t):
        jt = (i * units + n) * BAND_PAIR - (BAND_TILES - 1) + t
        return jnp.maximum(jt, 0), jt < 0

    def scores(n):
        qs = []
        for a in range(BAND_PAIR):
            q = q_ref[0, (n * BAND_PAIR + a) * tq:(n * BAND_PAIR + a + 1) * tq, :]
            lane = lax.broadcasted_iota(jnp.int32, q.shape, 1)
            zero = jnp.zeros_like(q)
            qs += [jnp.where(lane < B_HEAD_DIM, q, zero), jnp.where(lane >= B_HEAD_DIM, q, zero)]
        qqt = jnp.concatenate(qs, axis=0).astype(F32).T.astype(BF16)
        ks = [k_ref[0, pl.ds(pl.multiple_of(key_tile(n, t)[0] * tq, tq), tq), :] for t in range(key_tiles)]
        return jnp.dot(jnp.concatenate(ks, axis=0), qqt, preferred_element_type=F32)

    def finish(n):
        bias = [bias_ref[0, jnp.where(key_tile(n, t)[1], 1, 0), t] for t in range(key_tiles)]
        s = s_scr[n % 2] + jnp.concatenate(bias, axis=0)
        m = jnp.max(s, axis=0, keepdims=True)
        p = jnp.exp2(s - m)
        l = jnp.sum(p, axis=0, keepdims=True)
        vt = jnp.concatenate([vt_ref[key_tile(n, t)[0]] for t in range(key_tiles)], axis=1)
        o = jnp.dot(vt, p.astype(BF16), preferred_element_type=F32) / l
        for a in range(BAND_PAIR):
            oa = jnp.where(row < B_HEAD_DIM, o[:, 2 * a * tq:(2 * a + 1) * tq],
                           o[:, (2 * a + 1) * tq:(2 * a + 2) * tq])
            o_ref[0, (n * BAND_PAIR + a) * tq:(n * BAND_PAIR + a + 1) * tq, :] = oa.T.astype(BF16)

    s_scr[0] = scores(0)
    for n in range(units):
        if n + 1 < units:
            s_scr[(n + 1) % 2] = scores(n + 1)
        finish(n)


def _band_bias(rel_bias, tq):
    past = B_PAST_CHUNKS * CHUNK
    band = tq + past
    assert tq - 1 <= REL_CLIP <= past
    qi = jnp.arange(tq)
    kj = jnp.arange(band)
    cq = qi[:, None] // CHUNK
    ck = kj[None, :] // CHUNK
    allowed = (ck >= cq) & (ck <= cq + B_PAST_CHUNKS)
    tab = rel_bias.astype(F32) * LOG2E
    n_diag = band + tq - 1
    n_unclipped = REL_CLIP + tq
    w = jnp.concatenate([tab[:, REL_CLIP - (tq - 1):],
                         jnp.broadcast_to(tab[:, -1:], (B_HEADS, n_diag - n_unclipped))], axis=1)
    shifted = jnp.tile(w, (1, tq + 1))[:, :tq * (n_diag + 1)].reshape(B_HEADS, tq, n_diag + 1)
    bias = jnp.flip(shifted[:, :, :band], axis=2)
    return jnp.where(allowed[None], bias, NEG)


def _band_attn(proj3, vtb3, rel_bias):
    bsz, seq, _ = proj3.shape
    tq = BAND_TILE
    groups = B_HEADS // 2
    key_tiles = BAND_TILES + BAND_PAIR - 1
    cols = BAND_PAIR * 2 * tq
    bias = _band_bias(rel_bias, tq)
    bias = bias.reshape(groups, 2, tq, BAND_TILES, tq)
    masked = jnp.full((groups, 2, tq, 1, tq), NEG, F32)
    per_tile = [jnp.concatenate([masked] * a + [bias] + [masked] * (BAND_PAIR - 1 - a), axis=3)
                for a in range(BAND_PAIR)]
    bias = jnp.stack(per_tile, axis=1)
    bias = bias.transpose(0, 4, 5, 1, 2, 3).reshape(groups, 1, key_tiles, tq, cols)
    bias = jnp.concatenate([bias, jnp.full_like(bias, NEG)], axis=1)
    qblk = 3 * A_WIDTH // LANES
    kblk = qblk + B_WIDTH // LANES
    rows = min(BAND_STEP_ROWS, seq)
    return pl.pallas_call(
        _band_attn_kernel,
        grid=(bsz, groups, seq // rows),
        in_specs=[
            pl.BlockSpec((1, 2, key_tiles, tq, cols), lambda b, g, i: (g, 0, 0, 0, 0)),
            pl.BlockSpec((1, rows, LANES), lambda b, g, i: (b, i, qblk + g)),
            pl.BlockSpec((1, seq, LANES), lambda b, g, i: (b, 0, kblk + g)),
            pl.BlockSpec((seq // tq, LANES, tq), lambda b, g, i: (b, g, 0)),
        ],
        out_specs=pl.BlockSpec((1, rows, LANES), lambda b, g, i: (b, i, g)),
        out_shape=jax.ShapeDtypeStruct((bsz, seq, B_WIDTH), BF16),
        scratch_shapes=[pltpu.VMEM((2, key_tiles * tq, cols), F32)],
        compiler_params=pltpu.CompilerParams(
            dimension_semantics=("arbitrary", "arbitrary", "arbitrary"),
            vmem_limit_bytes=VMEM_LIMIT),
        name="band_attn",
    )(bias, proj3, proj3, vtb3)


def _out_router_kernel(x_ref, gi_ref, bi_ref, oa_ref, ob_ref, wa_ref, wb_ref, g1_ref, b1_ref,
                       rwt_ref, rb_ref, tri_ref, h1_ref, idx_ref, gate_ref, rank_ref, cnt_ref,
                       carry_ref, *, steps_per_tile):
    h = _layer_norm(x_ref[...], gi_ref[...], bi_ref[...])
    mix = (jnp.dot(oa_ref[0], wa_ref[...], preferred_element_type=F32)
           + jnp.dot(ob_ref[0], wb_ref[...], preferred_element_type=F32))
    h1 = _layer_norm(DEEPNORM_ALPHA * h + mix, g1_ref[...], b1_ref[...])
    nchunk = h1.shape[1] // LANES
    for c in range(nchunk):
        h1_ref[pl.ds(c, ROW_TILE, stride=nchunk), :] = h1[:, c * LANES:(c + 1) * LANES]
    lt = lax.dot_general(rwt_ref[...], h1, (((1,), (1,)), ((), ())),
                         precision=lax.Precision.HIGHEST, preferred_element_type=F32)
    lt = lt + rb_ref[...]
    eidx = lax.broadcasted_iota(jnp.int32, lt.shape, 0)
    vals, idxs, hots = [], [], []
    for _ in range(TOP_K):
        mx = jnp.max(lt, axis=0, keepdims=True)
        am = jnp.min(jnp.where(lt == mx, eidx, N_EXPERTS), axis=0, keepdims=True)
        hit = eidx == am
        vals.append(mx)
        idxs.append(am)
        hots.append(jnp.where(hit, 1.0, 0.0))
        lt = jnp.where(hit, -jnp.inf, lt)
    ex = [jnp.exp(v - vals[0]) for v in vals]
    den = ex[0] + ex[1] + ex[2] + ex[3]
    idx_ref[...] = jnp.concatenate(idxs, axis=0)
    gate_ref[...] = jnp.concatenate([e / den for e in ex], axis=0)

    @pl.when(pl.program_id(0) % steps_per_tile == 0)
    def _():
        carry_ref[...] = jnp.zeros_like(carry_ref)

    hot = (hots[0] + hots[1]) + (hots[2] + hots[3])
    before = jnp.dot(hot.astype(BF16), tri_ref[...], preferred_element_type=F32) + carry_ref[...]
    rank_ref[...] = jnp.concatenate(
        [jnp.sum(hk * before, axis=0, keepdims=True) for hk in hots], axis=0).astype(jnp.int32)
    total = carry_ref[...] + jnp.sum(hot, axis=1, keepdims=True)
    carry_ref[...] = total
    cnt_ref[0] = total.astype(jnp.int32)


def _out_router(x2, gi, bi, o_a, o_b, wa, wb, g1, b1, rwt, rb, moe_tile):
    t, d = x2.shape
    nchunk = d // LANES
    row = lambda i: (i, 0)
    fixed = lambda i: (0, 0)
    steps_per_seq = o_a.shape[1] // ROW_TILE
    seq_row = lambda i: (i // steps_per_seq, i % steps_per_seq, 0)
    tri = jnp.triu(jnp.ones((ROW_TILE, ROW_TILE), BF16), k=1)
    return pl.pallas_call(
        functools.partial(_out_router_kernel, steps_per_tile=moe_tile // ROW_TILE),
        grid=(t // ROW_TILE,),
        in_specs=[
            pl.BlockSpec((ROW_TILE, d), row),
            pl.BlockSpec((1, d), fixed),
            pl.BlockSpec((1, d), fixed),
            pl.BlockSpec((1, ROW_TILE, A_WIDTH), seq_row),
            pl.BlockSpec((1, ROW_TILE, B_WIDTH), seq_row),
            pl.BlockSpec((A_WIDTH, d), fixed),
            pl.BlockSpec((B_WIDTH, d), fixed),
            pl.BlockSpec((1, d), fixed),
            pl.BlockSpec((1, d), fixed),
            pl.BlockSpec((N_EXPERTS, d), fixed),
            pl.BlockSpec((N_EXPERTS, 1), fixed),
            pl.BlockSpec((ROW_TILE, ROW_TILE), fixed),
        ],
        out_specs=[
            pl.BlockSpec((ROW_TILE * nchunk, LANES), row),
            pl.BlockSpec((TOP_K, ROW_TILE), lambda i: (0, i)),
            pl.BlockSpec((TOP_K, ROW_TILE), lambda i: (0, i)),
            pl.BlockSpec((TOP_K, ROW_TILE), lambda i: (0, i)),
            pl.BlockSpec((1, N_EXPERTS, 1), lambda i: (i, 0, 0)),
        ],
        out_shape=[
            jax.ShapeDtypeStruct((t * nchunk, LANES), F32),
            jax.ShapeDtypeStruct((TOP_K, t), jnp.int32),
            jax.ShapeDtypeStruct((TOP_K, t), F32),
            jax.ShapeDtypeStruct((TOP_K, t), jnp.int32),
            jax.ShapeDtypeStruct((t // ROW_TILE, N_EXPERTS, 1), jnp.int32),
        ],
        scratch_shapes=[pltpu.VMEM((N_EXPERTS, 1), F32)],
        compiler_params=pltpu.CompilerParams(
            dimension_semantics=("arbitrary",), vmem_limit_bytes=VMEM_LIMIT),
        name="out_router",
    )(x2, gi, bi, o_a, o_b, wa, wb, g1, b1, rwt, rb, tri)


def _split_w1_kernel(w_ref, perm_ref, g_ref, l_ref):
    rows = w_ref.shape[1]
    even = (lax.broadcasted_iota(jnp.int32, (rows, LANES), 1) & 1) == 0
    perm = perm_ref[...]
    for g in range(w_ref.shape[2] // (2 * LANES)):
        v0 = w_ref[0, :, 2 * g * LANES:(2 * g + 1) * LANES]
        v1 = w_ref[0, :, (2 * g + 1) * LANES:(2 * g + 2) * LANES]
        glu = jnp.where(even, v0, pltpu.roll(v1, 1, axis=1)).astype(BF16)
        lin = jnp.where(even, pltpu.roll(v0, LANES - 1, axis=1), v1).astype(BF16)
        g_ref[0, :, g * LANES:(g + 1) * LANES] = jnp.dot(
            glu, perm, preferred_element_type=F32).astype(BF16)
        l_ref[0, :, g * LANES:(g + 1) * LANES] = jnp.dot(
            lin, perm, preferred_element_type=F32).astype(BF16)


def _split_w1(w1e):
    e, d, f2 = w1e.shape
    rows = 512
    half = LANES // 2
    unit = jnp.arange(LANES)
    perm = (jnp.arange(LANES)[:, None] == (2 * (unit % half) + unit // half)[None, :]).astype(BF16)
    return pl.pallas_call(
        _split_w1_kernel,
        grid=(e, d // rows),
        in_specs=[pl.BlockSpec((1, rows, f2), lambda i, j: (i, j, 0)),
                  pl.BlockSpec((LANES, LANES), lambda i, j: (0, 0))],
        out_specs=[pl.BlockSpec((1, rows, f2 // 2), lambda i, j: (i, j, 0)),
                   pl.BlockSpec((1, rows, f2 // 2), lambda i, j: (i, j, 0))],
        out_shape=[jax.ShapeDtypeStruct((e, d, f2 // 2), BF16),
                   jax.ShapeDtypeStruct((e, d, f2 // 2), BF16)],
        compiler_params=pltpu.CompilerParams(
            dimension_semantics=("arbitrary", "arbitrary"), vmem_limit_bytes=VMEM_LIMIT),
        name="split_w1",
    )(w1e, perm)


def _moe_kernel(cnt_ref, offs_ref,
                w1g_ref, w1l_ref, b1g_ref, b1l_ref, w2_ref, b2_ref, g2_ref, bb2_ref,
                pos_hbm, gate_hbm, h_hbm,
                out_hbm, hbuf, ybuf, stage, out_buf, list_tok, list_gate, pos_in, gate_in, n_sorted, sem,
                *, tile):
    b = pl.program_id(0)
    e = pl.program_id(1)
    n_tiles = pl.num_programs(0)
    n_exp = pl.num_programs(1)
    d_model = w1g_ref.shape[1]
    d_ff = w1g_ref.shape[2]
    nchunk = d_model // LANES
    tile_rows = tile * nchunk
    pairs = tile * TOP_K
    n_sort_chunks = pairs // SORT_CHUNK
    half = (b % 2) * pairs
    other_half = pairs - half

    def stage_rows(r):
        return pl.ds(pl.multiple_of(r * nchunk, nchunk), nchunk)

    def listed_rows(p):
        return pl.ds(pl.multiple_of(list_tok[p], nchunk), nchunk)

    def sort_pair(j, first_row, dst_half):
        p = pos_in[j] + dst_half
        list_tok[p] = first_row
        list_gate[p] = gate_in[j]

    def sort_chunks(lo, hi, dst_half):
        def body(jo, c):
            for u in range(SORT_UNROLL):
                j = jo * SORT_UNROLL + u
                sort_pair(j, (j & (tile - 1)) * nchunk, dst_half)
            return c
        lax.fori_loop(lo * (SORT_CHUNK // SORT_UNROLL), hi * (SORT_CHUNK // SORT_UNROLL), body, 0)

    def load_pairs(t):
        rows = pl.ds(pl.multiple_of(t * pairs, pairs), pairs)
        copies = [pltpu.make_async_copy(pos_hbm.at[rows], pos_in, sem.at[2]),
                  pltpu.make_async_copy(gate_hbm.at[rows], gate_in, sem.at[3])]
        for cp in copies:
            cp.start()
        for cp in copies:
            cp.wait()

    @pl.when(jnp.logical_and(b == 0, e == 0))
    def _():
        stage[...] = jnp.zeros_like(stage)

        def pad_body(j, c):
            list_tok[pairs + j] = 0
            list_tok[2 * pairs + j] = 0
            return c
        lax.fori_loop(0, LIST_PAD, pad_body, 0)
        load_pairs(0)
        n_sorted[0] = 0

    @pl.when(e == 0)
    def _():
        src = h_hbm.at[pl.ds(pl.multiple_of(b * tile_rows, tile_rows), tile_rows), :]
        load = pltpu.make_async_copy(src, hbuf, sem.at[0])
        load.start()
        ybuf[...] = jnp.zeros_like(ybuf)
        sort_chunks(jnp.minimum(n_sorted[0], n_sort_chunks), n_sort_chunks, half)
        n_sorted[0] = 0

        @pl.when(b + 1 < n_tiles)
        def _():
            load_pairs(b + 1)
        load.wait()

    n = cnt_ref[b * n_exp + e]
    base = offs_ref[b * n_exp + e] + half

    def run_block(m_rows, p0, nrows):
        def gather_body(ci, c):
            for u in range(ROW_UNROLL):
                r = ci * ROW_UNROLL + u
                stage[stage_rows(r), :] = hbuf[listed_rows(p0 + r), :]
            return c
        lax.fori_loop(0, (nrows + ROW_UNROLL - 1) // ROW_UNROLL, gather_body, 0)

        if m_rows == MOE_BLOCK:
            chunk = jnp.minimum(n_sorted[0], n_sort_chunks - 1)
            n_sorted[0] = n_sorted[0] + 1
            j0 = chunk * SORT_CHUNK
            row0 = (j0 & (tile - 1)) * nchunk
            for u in range(SORT_CHUNK):
                sort_pair(j0 + u, row0 + u * nchunk, other_half)

        x = jnp.concatenate(
            [stage[pl.ds(c, m_rows, stride=nchunk), :] for c in range(nchunk)],
            axis=1).astype(BF16)
        y = b2_ref[0]
        for hh in range(d_ff // FF_CHUNK):
            sl = slice(hh * FF_CHUNK, (hh + 1) * FF_CHUNK)
            hg = jnp.dot(x, w1g_ref[0, :, sl], preferred_element_type=F32) + b1g_ref[0, :, sl]
            hl = jnp.dot(x, w1l_ref[0, :, sl], preferred_element_type=F32) + b1l_ref[0, :, sl]
            xg = jnp.minimum(hg, SWIGLU_LIMIT)
            xl = jnp.clip(hl, -SWIGLU_LIMIT, SWIGLU_LIMIT)
            act = xg * jax.nn.sigmoid(SWIGLU_ALPHA * xg) * (xl + 1.0)
            y = y + jnp.dot(act.astype(BF16), w2_ref[0, sl, :], preferred_element_type=F32)
        for c in range(nchunk):
            stage[pl.ds(c, m_rows, stride=nchunk), :] = y[:, c * LANES:(c + 1) * LANES]

        def add_body(ci, c):
            rows, vals = [], []
            for u in range(SUBLANES):
                r = ci * SUBLANES + u
                dst = listed_rows(p0 + r)
                rows.append(dst)
                vals.append(ybuf[dst, :] + list_gate[p0 + r] * stage[stage_rows(r), :])
            for dst, val in zip(rows, vals):
                ybuf[dst, :] = val
            return c
        n_full = nrows // SUBLANES
        lax.fori_loop(0, n_full, add_body, 0)

        def add_tail(r, c):
            dst = listed_rows(p0 + r)
            ybuf[dst, :] = ybuf[dst, :] + list_gate[p0 + r] * stage[stage_rows(r), :]
            return c
        lax.fori_loop(n_full * SUBLANES, nrows, add_tail, 0)

    n_big = n // MOE_BLOCK
    rem = n - n_big * MOE_BLOCK
    rem_is_big = rem > MOE_BLOCK - SMALL_BLOCK
    n_big_blocks = n_big + rem_is_big.astype(jnp.int32)
    n_small_blocks = jnp.where(rem_is_big, 0, (rem + SMALL_BLOCK - 1) // SMALL_BLOCK)

    def big_body(s, c):
        run_block(MOE_BLOCK, base + s * MOE_BLOCK, jnp.minimum(MOE_BLOCK, n - s * MOE_BLOCK))
        return c
    lax.fori_loop(0, n_big_blocks, big_body, 0)

    def small_body(s, c):
        run_block(SMALL_BLOCK, base + n_big * MOE_BLOCK + s * SMALL_BLOCK,
                  jnp.minimum(SMALL_BLOCK, rem - s * SMALL_BLOCK))
        return c
    lax.fori_loop(0, n_small_blocks, small_body, 0)

    @pl.when(e == n_exp - 1)
    def _():
        def out_copy(c, slot):
            rows = pl.ds(pl.multiple_of(b * tile + c * ROW_TILE, ROW_TILE), ROW_TILE)
            return pltpu.make_async_copy(out_buf.at[slot], out_hbm.at[rows, :], sem.at[4 + slot])

        def chunk_body(c, carry):
            slot = c % 2

            @pl.when(c >= 2)
            def _():
                out_copy(c - 2, slot).wait()
            first = c * (ROW_TILE * nchunk)
            pieces = [DEEPNORM_ALPHA * hbuf[pl.ds(first + k, ROW_TILE, stride=nchunk), :]
                      + ybuf[pl.ds(first + k, ROW_TILE, stride=nchunk), :] for k in range(nchunk)]
            out_buf[slot] = _layer_norm(jnp.concatenate(pieces, axis=1), g2_ref[...], bb2_ref[...])
            out_copy(c, slot).start()
            return carry
        n_out = tile // ROW_TILE
        lax.fori_loop(0, n_out, chunk_body, 0)
        for c in range(max(n_out - 2, 0), n_out):
            out_copy(c, c % 2).wait()


def _moe(h1t, cnt, offs, pos_flat, gate_flat, w1g, w1l, b1g, b1l, w2, b2, g2, bb2, tile):
    n_exp, d, f = w1g.shape
    nchunk = d // LANES
    n_tiles = h1t.shape[0] // (tile * nchunk)
    pairs = tile * TOP_K
    assert tile % SORT_CHUNK == 0
    expert = lambda b, e, cnt, offs: (e, 0, 0)
    grid_spec = pltpu.PrefetchScalarGridSpec(
        num_scalar_prefetch=2,
        grid=(n_tiles, n_exp),
        in_specs=[
            pl.BlockSpec((1, d, f), expert),
            pl.BlockSpec((1, d, f), expert),
            pl.BlockSpec((1, 1, f), expert),
            pl.BlockSpec((1, 1, f), expert),
            pl.BlockSpec((1, f, d), expert),
            pl.BlockSpec((1, 1, d), expert),
            pl.BlockSpec((1, d), lambda b, e, cnt, offs: (0, 0)),
            pl.BlockSpec((1, d), lambda b, e, cnt, offs: (0, 0)),
            pl.BlockSpec(memory_space=pl.ANY),
            pl.BlockSpec(memory_space=pl.ANY),
            pl.BlockSpec(memory_space=pl.ANY),
        ],
        out_specs=pl.BlockSpec(memory_space=pl.ANY),
        scratch_shapes=[
            pltpu.VMEM((tile * nchunk, LANES), F32),
            pltpu.VMEM((tile * nchunk, LANES), F32),
            pltpu.VMEM((MOE_BLOCK * nchunk, LANES), F32),
            pltpu.VMEM((2, ROW_TILE, d), F32),
            pltpu.SMEM((2 * pairs + LIST_PAD,), jnp.int32),
            pltpu.SMEM((2 * pairs + LIST_PAD,), F32),
            pltpu.SMEM((pairs,), jnp.int32),
            pltpu.SMEM((pairs,), F32),
            pltpu.SMEM((1,), jnp.int32),
            pltpu.SemaphoreType.DMA((6,)),
        ],
    )
    return pl.pallas_call(
        functools.partial(_moe_kernel, tile=tile),
        grid_spec=grid_spec,
        out_shape=jax.ShapeDtypeStruct((n_tiles * tile, d), F32),
        compiler_params=pltpu.CompilerParams(
            dimension_semantics=("arbitrary", "arbitrary"), vmem_limit_bytes=MOE_VMEM_LIMIT),
        name="moe",
    )(cnt, offs, w1g, w1l, b1g, b1l, w2, b2, g2, bb2, pos_flat, gate_flat, h1t)


def kernel(x, ln_in_g, ln_in_b, w_in, lambda_q1, lambda_k1, lambda_q2, lambda_k2, subln_g, rel_bias,
           w_out, ln1_g, ln1_b, router_w, router_b, w1, b1, w2, b2, ln2_g, ln2_b):
    bsz, seq, d = x.shape
    t = bsz * seq
    x2 = x.reshape(t, d)
    row = lambda v: v.reshape(1, -1).astype(F32)

    qs = A_HEAD_DIM ** -0.5 * LOG2E
    col_scale = jnp.concatenate([
        jnp.full((A_WIDTH,), qs, F32), jnp.ones((2 * A_WIDTH,), F32),
        jnp.full((B_WIDTH,), B_HEAD_DIM ** -0.5 * LOG2E, F32), jnp.ones((2 * B_WIDTH,), F32)]).reshape(1, -1)
    proj3, vt3, vtb3 = _ln_qkv(x2, row(ln_in_g), row(ln_in_b), w_in[0].astype(BF16), col_scale, seq)

    lam4 = jnp.stack([lambda_q1[0], lambda_k1[0], lambda_q2[0], lambda_k2[0]]).astype(F32)
    o_a = _diff_attn(proj3, vt3, lam4, subln_g[0].astype(F32))
    o_b = _band_attn(proj3, vtb3, rel_bias[0])

    w_o = w_out[0].astype(BF16)
    tile = min(MOE_TILE, t)
    n_tiles = t // tile
    h1t, top_idx, gates, rank, run_cnt = _out_router(
        x2, row(ln_in_g), row(ln_in_b), o_a, o_b,
        w_o[:A_WIDTH], w_o[A_WIDTH:], row(ln1_g[0]), row(ln1_b[0]),
        router_w[0].T.astype(F32), router_b[0].reshape(-1, 1).astype(F32), tile)

    steps = tile // ROW_TILE
    cnt = run_cnt[steps - 1::steps, :, 0]
    offs = jnp.cumsum(cnt, axis=1) - cnt
    hot = top_idx.reshape(TOP_K, n_tiles, tile, 1) == jnp.arange(N_EXPERTS, dtype=jnp.int32)
    pos = rank + jnp.sum(jnp.where(hot, offs[None, :, None, :], 0), axis=-1).reshape(TOP_K, t)
    per_tile = lambda a: a.reshape(TOP_K, n_tiles, tile).transpose(1, 0, 2).reshape(-1)
    w1g, w1l = _split_w1(w1[0])
    b1e = b1[0].astype(F32)[:, None, :]
    out = _moe(h1t, cnt.reshape(-1), offs.reshape(-1), per_tile(pos), per_tile(gates),
               w1g, w1l, b1e[:, :, 0::2], b1e[:, :, 1::2],
               w2[0].astype(BF16), b2[0][:, None, :].astype(F32), row(ln2_g[0]), row(ln2_b[0]), tile)
    return out.reshape(bsz, seq, d)
```

```python
import functools
import math

import jax
import jax.numpy as jnp
from jax import lax
from jax.experimental import pallas as pl
from jax.experimental.pallas import tpu as pltpu

F32 = jnp.float32
BF16 = jnp.bfloat16

CHUNK = 64
A_HEADS = 4
A_HEAD_DIM = 64
A_WIDTH = A_HEADS * 2 * A_HEAD_DIM
B_HEADS = 8
B_HEAD_DIM = 64
B_WIDTH = B_HEADS * B_HEAD_DIM
B_PAST_CHUNKS = 8
REL_CLIP = 256
N_EXPERTS = 32
TOP_K = 4
SWIGLU_ALPHA = 1.702
SWIGLU_LIMIT = 7.0
MOE_BLOCK = 512
LN_EPS = 1e-5
RMS_EPS = 1e-5
DEPTH = 1
DEEPNORM_ALPHA = (2 * DEPTH) ** 0.25
LAM_INIT = 0.8 - 0.6 * math.exp(-0.3 * 0)

LOG2E = 1.4426950408889634
NEG = -1e30
LANES = 128
SUBLANES = 8
ROW_TILE = 512
VMEM_LIMIT = 48 * 1024 * 1024
DIFF_TILE = 256
BAND_TILE = 128
BAND_TILES = (BAND_TILE + B_PAST_CHUNKS * CHUNK) // BAND_TILE
BAND_PAIR = 2
BAND_STEP_ROWS = 4096
MOE_TILE = 4096
MOE_VMEM_LIMIT = 60 * 1024 * 1024
SMALL_BLOCK = 128
FF_CHUNK = 1024
SORT_UNROLL = 8
SORT_CHUNK = 512
ROW_UNROLL = 32
LIST_PAD = 128


def _layer_norm(x, g, b):
    mu = jnp.mean(x, axis=-1, keepdims=True)
    xc = x - mu
    var = jnp.mean(xc * xc, axis=-1, keepdims=True)
    return xc * lax.rsqrt(var + LN_EPS) * g + b


def _ln_qkv_kernel(x_ref, g_ref, b_ref, w_ref, cs_ref, o_ref, vta_ref, vtb_ref):
    h = _layer_norm(x_ref[...], g_ref[...], b_ref[...])
    hb = h.astype(BF16)
    n_out = w_ref.shape[1]
    for c in range(n_out // ROW_TILE):
        sl = slice(c * ROW_TILE, (c + 1) * ROW_TILE)
        val = jnp.dot(hb, w_ref[:, sl], preferred_element_type=F32) * cs_ref[:, sl]
        o_ref[0, :, sl] = val.astype(BF16)
        for start, vt_ref in ((2 * A_WIDTH, vta_ref), (3 * A_WIDTH + 2 * B_WIDTH, vtb_ref)):
            if sl.start == start:
                tile = vt_ref.shape[2]
                for kt in range(ROW_TILE // tile):
                    vt_ref[kt] = val[kt * tile:(kt + 1) * tile, :].T.astype(BF16)


def _ln_qkv(x2, g, b, w_bf, col_scale, seq):
    assert A_WIDTH == ROW_TILE and B_WIDTH == ROW_TILE
    t, d = x2.shape
    n_out = w_bf.shape[1]
    steps_per_seq = seq // ROW_TILE
    vt_spec = lambda tile: pl.BlockSpec((ROW_TILE // tile, ROW_TILE, tile), lambda i: (i, 0, 0))
    vt_shape = lambda tile: jax.ShapeDtypeStruct((t // tile, ROW_TILE, tile), BF16)
    return pl.pallas_call(
        _ln_qkv_kernel,
        grid=(t // ROW_TILE,),
        in_specs=[
            pl.BlockSpec((ROW_TILE, d), lambda i: (i, 0)),
            pl.BlockSpec((1, d), lambda i: (0, 0)),
            pl.BlockSpec((1, d), lambda i: (0, 0)),
            pl.BlockSpec((d, n_out), lambda i: (0, 0)),
            pl.BlockSpec((1, n_out), lambda i: (0, 0)),
        ],
        out_specs=[pl.BlockSpec((1, ROW_TILE, n_out),
                                lambda i: (i // steps_per_seq, i % steps_per_seq, 0)),
                   vt_spec(DIFF_TILE), vt_spec(BAND_TILE)],
        out_shape=[jax.ShapeDtypeStruct((t // seq, seq, n_out), BF16),
                   vt_shape(DIFF_TILE), vt_shape(BAND_TILE)],
        compiler_params=pltpu.CompilerParams(
            dimension_semantics=("arbitrary",), vmem_limit_bytes=VMEM_LIMIT),
        name="ln_qkv",
    )(x2, g, b, w_bf, col_scale)


def _diff_attn_kernel(c_ref, lam_ref, g_ref, boff_ref, bdiag_ref, q_ref, k_ref, vt_ref, o_ref, s_scr):
    tq = q_ref.shape[1]
    tk = vt_ref.shape[2]
    h = pl.program_id(1)
    i = pl.program_id(2)
    c = c_ref[h]
    q = q_ref[0]
    lane = lax.broadcasted_iota(jnp.int32, q.shape, 1)
    zero = jnp.zeros_like(q)
    qq = jnp.concatenate([jnp.where(lane < A_HEAD_DIM, q, zero),
                          jnp.where(lane >= A_HEAD_DIM, q, zero)], axis=0)
    qqt = qq.astype(F32).T.astype(BF16)
    n_before = i * (tq // tk)

    def scores(j):
        kb = k_ref[0, pl.ds(pl.multiple_of(j * tk, tk), tk), :]
        return jnp.dot(kb, qqt, preferred_element_type=F32)

    def update(slot, table_ref, j, carry):
        m, l, acc = carry
        s = s_scr[slot] + table_ref[0]
        shift = c * (j * tk).astype(F32)
        m_new = jnp.maximum(m, jnp.max(s, axis=0, keepdims=True) + shift)
        alpha = jnp.exp2(m - m_new)
        p = jnp.exp2(s - (m_new - shift))
        l = alpha * l + jnp.sum(p, axis=0, keepdims=True)
        acc = alpha * acc + jnp.dot(vt_ref[j], p.astype(BF16), preferred_element_type=F32)
        return m_new, l, acc

    s_scr[0] = scores(0)

    def pair(t, carry):
        j = 2 * t
        s_scr[1] = scores(j + 1)
        carry = update(0, boff_ref, j, carry)
        s_scr[0] = scores(j + 2)
        return update(1, boff_ref, j + 1, carry)

    def quad(t, carry):
        return pair(2 * t + 1, pair(2 * t, carry))

    init = (jnp.full((1, 2 * tq), NEG, F32), jnp.zeros((1, 2 * tq), F32),
            jnp.zeros((LANES, 2 * tq), F32))
    carry = lax.fori_loop(0, n_before // 4, quad, init)
    carry = lax.fori_loop(n_before // 4 * 2, n_before // 2, pair, carry)

    def odd_tail(carry):
        s_scr[1] = scores(n_before)
        carry = update(0, boff_ref, n_before - 1, carry)
        return update(1, bdiag_ref, n_before, carry)

    def even_tail(carry):
        return update(0, bdiag_ref, n_before, carry)

    _, l, acc = lax.cond(n_before % 2 == 1, odd_tail, even_tail, carry)

    o_all = acc / l
    lv = lam_ref[...]
    lam = (jnp.exp(jnp.sum(lv[0:1] * lv[1:2], axis=1, keepdims=True))
           - jnp.exp(jnp.sum(lv[2:3] * lv[3:4], axis=1, keepdims=True)) + LAM_INIT)
    o = o_all[:, :tq] - lam * o_all[:, tq:]
    ms = jnp.mean(o * o, axis=0, keepdims=True)
    o = o * lax.rsqrt(ms + RMS_EPS) * (g_ref[...] * (1.0 - LAM_INIT))
    o_ref[0] = o.T.astype(BF16)


def _diff_attn(proj3, vt3, lam4, subln_g):
    bsz, seq, _ = proj3.shape
    tq = tk = DIFF_TILE
    c = jnp.asarray([2.0 ** (-8.0 * (h + 1) / A_HEADS) for h in range(A_HEADS)], F32) * LOG2E
    r = jnp.arange(tk, dtype=jnp.int32)[:, None]
    qrel = jnp.arange(2 * tq, dtype=jnp.int32)[None, :] % tq
    boff = c[:, None, None] * jnp.broadcast_to(r, (tk, 2 * tq)).astype(F32)
    allowed = (r // CHUNK) <= (qrel // CHUNK)
    bdiag = jnp.where(allowed, c[:, None, None] * (qrel - jnp.abs(qrel - r)).astype(F32), NEG)
    kblk = A_WIDTH // LANES
    return pl.pallas_call(
        _diff_attn_kernel,
        grid=(bsz, A_HEADS, seq // tq),
        in_specs=[
            pl.BlockSpec(memory_space=pltpu.SMEM),
            pl.BlockSpec((4, A_HEAD_DIM), lambda b, h, i: (0, 0)),
            pl.BlockSpec((LANES, 1), lambda b, h, i: (0, 0)),
            pl.BlockSpec((1, tk, 2 * tq), lambda b, h, i: (h, 0, 0)),
            pl.BlockSpec((1, tk, 2 * tq), lambda b, h, i: (h, 0, 0)),
            pl.BlockSpec((1, tq, LANES), lambda b, h, i: (b, i, h)),
            pl.BlockSpec((1, seq, LANES), lambda b, h, i: (b, 0, kblk + h)),
            pl.BlockSpec((seq // tk, LANES, tk), lambda b, h, i: (b, h, 0)),
        ],
        out_specs=pl.BlockSpec((1, tq, LANES), lambda b, h, i: (b, i, h)),
        out_shape=jax.ShapeDtypeStruct((bsz, seq, A_WIDTH), BF16),
        scratch_shapes=[pltpu.VMEM((2, tk, 2 * tq), F32)],
        compiler_params=pltpu.CompilerParams(
            dimension_semantics=("arbitrary", "arbitrary", "arbitrary"),
            vmem_limit_bytes=VMEM_LIMIT),
        name="diff_attn",
    )(c, lam4, subln_g.reshape(LANES, 1), boff, bdiag, proj3, proj3, vt3)


def _band_attn_kernel(bias_ref, q_ref, k_ref, vt_ref, o_ref, s_scr):
    i = pl.program_id(2)
    tq = BAND_TILE
    units = q_ref.shape[1] // (BAND_PAIR * tq)
    key_tiles = BAND_TILES + BAND_PAIR - 1
    row = lax.broadcasted_iota(jnp.int32, (LANES, tq), 0)

    def key_tile(n, t):
        jt = (i * units + n) * BAND_PAIR - (BAND_TILES - 1) + t
        return jnp.maximum(jt, 0), jt < 0

    def scores(n):
        qs = []
        for a in range(BAND_PAIR):
            q = q_ref[0, (n * BAND_PAIR + a) * tq:(n * BAND_PAIR + a + 1) * tq, :]
            lane = lax.broadcasted_iota(jnp.int32, q.shape, 1)
            zero = jnp.zeros_like(q)
            qs += [jnp.where(lane < B_HEAD_DIM, q, zero), jnp.where(lane >= B_HEAD_DIM, q, zero)]
        qqt = jnp.concatenate(qs, axis=0).astype(F32).T.astype(BF16)
        ks = [k_ref[0, pl.ds(pl.multiple_of(key_tile(n, t)[0] * tq, tq), tq), :] for t in range(key_tiles)]
        return jnp.dot(jnp.concatenate(ks, axis=0), qqt, preferred_element_type=F32)

    def finish(n):
        bias = [bias_ref[0, jnp.where(key_tile(n, t)[1], 1, 0), t] for t in range(key_tiles)]
        s = s_scr[n % 2] + jnp.concatenate(bias, axis=0)
        m = jnp.max(s, axis=0, keepdims=True)
        p = jnp.exp2(s - m)
        l = jnp.sum(p, axis=0, keepdims=True)
        vt = jnp.concatenate([vt_ref[key_tile(n, t)[0]] for t in range(key_tiles)], axis=1)
        o = jnp.dot(vt, p.astype(BF16), preferred_element_type=F32) / l
        for a in range(BAND_PAIR):
            oa = jnp.where(row < B_HEAD_DIM, o[:, 2 * a * tq:(2 * a + 1) * tq],
                           o[:, (2 * a + 1) * tq:(2 * a + 2) * tq])
            o_ref[0, (n * BAND_PAIR + a) * tq:(n * BAND_PAIR + a + 1) * tq, :] = oa.T.astype(BF16)

    s_scr[0] = scores(0)
    for n in range(units):
        if n + 1 < units:
            s_scr[(n + 1) % 2] = scores(n + 1)
        finish(n)


def _band_bias(rel_bias, tq):
    past = B_PAST_CHUNKS * CHUNK
    band = tq + past
    assert tq - 1 <= REL_CLIP <= past
    qi = jnp.arange(tq)
    kj = jnp.arange(band)
    cq = qi[:, None] // CHUNK
    ck = kj[None, :] // CHUNK
    allowed = (ck >= cq) & (ck <= cq + B_PAST_CHUNKS)
    tab = rel_bias.astype(F32) * LOG2E
    n_diag = band + tq - 1
    n_unclipped = REL_CLIP + tq
    w = jnp.concatenate([tab[:, REL_CLIP - (tq - 1):],
                         jnp.broadcast_to(tab[:, -1:], (B_HEADS, n_diag - n_unclipped))], axis=1)
    shifted = jnp.tile(w, (1, tq + 1))[:, :tq * (n_diag + 1)].reshape(B_HEADS, tq, n_diag + 1)
    bias = jnp.flip(shifted[:, :, :band], axis=2)
    return jnp.where(allowed[None], bias, NEG)


def _band_attn(proj3, vtb3, rel_bias):
    bsz, seq, _ = proj3.shape
    tq = BAND_TILE
    groups = B_HEADS // 2
    key_tiles = BAND_TILES + BAND_PAIR - 1
    cols = BAND_PAIR * 2 * tq
    bias = _band_bias(rel_bias, tq)
    bias = bias.reshape(groups, 2, tq, BAND_TILES, tq)
    masked = jnp.full((groups, 2, tq, 1, tq), NEG, F32)
    per_tile = [jnp.concatenate([masked] * a + [bias] + [masked] * (BAND_PAIR - 1 - a), axis=3)
                for a in range(BAND_PAIR)]
    bias = jnp.stack(per_tile, axis=1)
    bias = bias.transpose(0, 4, 5, 1, 2, 3).reshape(groups, 1, key_tiles, tq, cols)
    bias = jnp.concatenate([bias, jnp.full_like(bias, NEG)], axis=1)
    qblk = 3 * A_WIDTH // LANES
    kblk = qblk + B_WIDTH // LANES
    rows = min(BAND_STEP_ROWS, seq)
    return pl.pallas_call(
        _band_attn_kernel,
        grid=(bsz, groups, seq // rows),
        in_specs=[
            pl.BlockSpec((1, 2, key_tiles, tq, cols), lambda b, g, i: (g, 0, 0, 0, 0)),
            pl.BlockSpec((1, rows, LANES), lambda b, g, i: (b, i, qblk + g)),
            pl.BlockSpec((1, seq, LANES), lambda b, g, i: (b, 0, kblk + g)),
            pl.BlockSpec((seq // tq, LANES, tq), lambda b, g, i: (b, g, 0)),
        ],
        out_specs=pl.BlockSpec((1, rows, LANES), lambda b, g, i: (b, i, g)),
        out_shape=jax.ShapeDtypeStruct((bsz, seq, B_WIDTH), BF16),
        scratch_shapes=[pltpu.VMEM((2, key_tiles * tq, cols), F32)],
        compiler_params=pltpu.CompilerParams(
            dimension_semantics=("arbitrary", "arbitrary", "arbitrary"),
            vmem_limit_bytes=VMEM_LIMIT),
        name="band_attn",
    )(bias, proj3, proj3, vtb3)


def _out_router_kernel(x_ref, gi_ref, bi_ref, oa_ref, ob_ref, wa_ref, wb_ref, g1_ref, b1_ref,
                       rwt_ref, rb_ref, tri_ref, h1_ref, idx_ref, gate_ref, rank_ref, cnt_ref,
                       carry_ref, *, steps_per_tile):
    h = _layer_norm(x_ref[...], gi_ref[...], bi_ref[...])
    mix = (jnp.dot(oa_ref[0], wa_ref[...], preferred_element_type=F32)
           + jnp.dot(ob_ref[0], wb_ref[...], preferred_element_type=F32))
    h1 = _layer_norm(DEEPNORM_ALPHA * h + mix, g1_ref[...], b1_ref[...])
    nchunk = h1.shape[1] // LANES
    for c in range(nchunk):
        h1_ref[pl.ds(c, ROW_TILE, stride=nchunk), :] = h1[:, c * LANES:(c + 1) * LANES]
    lt = lax.dot_general(rwt_ref[...], h1, (((1,), (1,)), ((), ())),
                         precision=lax.Precision.HIGHEST, preferred_element_type=F32)
    lt = lt + rb_ref[...]
    eidx = lax.broadcasted_iota(jnp.int32, lt.shape, 0)
    vals, idxs, hots = [], [], []
    for _ in range(TOP_K):
        mx = jnp.max(lt, axis=0, keepdims=True)
        am = jnp.min(jnp.where(lt == mx, eidx, N_EXPERTS), axis=0, keepdims=True)
        hit = eidx == am
        vals.append(mx)
        idxs.append(am)
        hots.append(jnp.where(hit, 1.0, 0.0))
        lt = jnp.where(hit, -jnp.inf, lt)
    ex = [jnp.exp(v - vals[0]) for v in vals]
    den = ex[0] + ex[1] + ex[2] + ex[3]
    idx_ref[...] = jnp.concatenate(idxs, axis=0)
    gate_ref[...] = jnp.concatenate([e / den for e in ex], axis=0)

    @pl.when(pl.program_id(0) % steps_per_tile == 0)
    def _():
        carry_ref[...] = jnp.zeros_like(carry_ref)

    hot = (hots[0] + hots[1]) + (hots[2] + hots[3])
    before = jnp.dot(hot.astype(BF16), tri_ref[...], preferred_element_type=F32) + carry_ref[...]
    rank_ref[...] = jnp.concatenate(
        [jnp.sum(hk * before, axis=0, keepdims=True) for hk in hots], axis=0).astype(jnp.int32)
    total = carry_ref[...] + jnp.sum(hot, axis=1, keepdims=True)
    carry_ref[...] = total
    cnt_ref[0] = total.astype(jnp.int32)


def _out_router(x2, gi, bi, o_a, o_b, wa, wb, g1, b1, rwt, rb, moe_tile):
    t, d = x2.shape
    nchunk = d // LANES
    row = lambda i: (i, 0)
    fixed = lambda i: (0, 0)
    steps_per_seq = o_a.shape[1] // ROW_TILE
    seq_row = lambda i: (i // steps_per_seq, i % steps_per_seq, 0)
    tri = jnp.triu(jnp.ones((ROW_TILE, ROW_TILE), BF16), k=1)
    return pl.pallas_call(
        functools.partial(_out_router_kernel, steps_per_tile=moe_tile // ROW_TILE),
        grid=(t // ROW_TILE,),
        in_specs=[
            pl.BlockSpec((ROW_TILE, d), row),
            pl.BlockSpec((1, d), fixed),
            pl.BlockSpec((1, d), fixed),
            pl.BlockSpec((1, ROW_TILE, A_WIDTH), seq_row),
            pl.BlockSpec((1, ROW_TILE, B_WIDTH), seq_row),
            pl.BlockSpec((A_WIDTH, d), fixed),
            pl.BlockSpec((B_WIDTH, d), fixed),
            pl.BlockSpec((1, d), fixed),
            pl.BlockSpec((1, d), fixed),
            pl.BlockSpec((N_EXPERTS, d), fixed),
            pl.BlockSpec((N_EXPERTS, 1), fixed),
            pl.BlockSpec((ROW_TILE, ROW_TILE), fixed),
        ],
        out_specs=[
            pl.BlockSpec((ROW_TILE * nchunk, LANES), row),
            pl.BlockSpec((TOP_K, ROW_TILE), lambda i: (0, i)),
            pl.BlockSpec((TOP_K, ROW_TILE), lambda i: (0, i)),
            pl.BlockSpec((TOP_K, ROW_TILE), lambda i: (0, i)),
            pl.BlockSpec((1, N_EXPERTS, 1), lambda i: (i, 0, 0)),
        ],
        out_shape=[
            jax.ShapeDtypeStruct((t * nchunk, LANES), F32),
            jax.ShapeDtypeStruct((TOP_K, t), jnp.int32),
            jax.ShapeDtypeStruct((TOP_K, t), F32),
            jax.ShapeDtypeStruct((TOP_K, t), jnp.int32),
            jax.ShapeDtypeStruct((t // ROW_TILE, N_EXPERTS, 1), jnp.int32),
        ],
        scratch_shapes=[pltpu.VMEM((N_EXPERTS, 1), F32)],
        compiler_params=pltpu.CompilerParams(
            dimension_semantics=("arbitrary",), vmem_limit_bytes=VMEM_LIMIT),
        name="out_router",
    )(x2, gi, bi, o_a, o_b, wa, wb, g1, b1, rwt, rb, tri)


def _split_w1_kernel(w_ref, perm_ref, g_ref, l_ref):
    rows = w_ref.shape[1]
    even = (lax.broadcasted_iota(jnp.int32, (rows, LANES), 1) & 1) == 0
    perm = perm_ref[...]
    for g in range(w_ref.shape[2] // (2 * LANES)):
        v0 = w_ref[0, :, 2 * g * LANES:(2 * g + 1) * LANES]
        v1 = w_ref[0, :, (2 * g + 1) * LANES:(2 * g + 2) * LANES]
        glu = jnp.where(even, v0, pltpu.roll(v1, 1, axis=1)).astype(BF16)
        lin = jnp.where(even, pltpu.roll(v0, LANES - 1, axis=1), v1).astype(BF16)
        g_ref[0, :, g * LANES:(g + 1) * LANES] = jnp.dot(
            glu, perm, preferred_element_type=F32).astype(BF16)
        l_ref[0, :, g * LANES:(g + 1) * LANES] = jnp.dot(
            lin, perm, preferred_element_type=F32).astype(BF16)


def _split_w1(w1e):
    e, d, f2 = w1e.shape
    rows = 1024
    half = LANES // 2
    unit = jnp.arange(LANES)
    perm = (jnp.arange(LANES)[:, None] == (2 * (unit % half) + unit // half)[None, :]).astype(BF16)
    return pl.pallas_call(
        _split_w1_kernel,
        grid=(e, d // rows),
        in_specs=[pl.BlockSpec((1, rows, f2), lambda i, j: (i, j, 0)),
                  pl.BlockSpec((LANES, LANES), lambda i, j: (0, 0))],
        out_specs=[pl.BlockSpec((1, rows, f2 // 2), lambda i, j: (i, j, 0)),
                   pl.BlockSpec((1, rows, f2 // 2), lambda i, j: (i, j, 0))],
        out_shape=[jax.ShapeDtypeStruct((e, d, f2 // 2), BF16),
                   jax.ShapeDtypeStruct((e, d, f2 // 2), BF16)],
        compiler_params=pltpu.CompilerParams(
            dimension_semantics=("arbitrary", "arbitrary"), vmem_limit_bytes=VMEM_LIMIT),
        name="split_w1",
    )(w1e, perm)


def _moe_kernel(cnt_ref, offs_ref,
                w1g_ref, w1l_ref, b1g_ref, b1l_ref, w2_ref, b2_ref, g2_ref, bb2_ref,
                pos_hbm, gate_hbm, h_hbm,
                out_hbm, hbuf, ybuf, stage, out_buf, list_tok, list_gate, pos_in, gate_in, n_sorted, sem,
                *, tile):
    b = pl.program_id(0)
    e = pl.program_id(1)
    n_tiles = pl.num_programs(0)
    n_exp = pl.num_programs(1)
    d_model = w1g_ref.shape[1]
    d_ff = w1g_ref.shape[2]
    nchunk = d_model // LANES
    tile_rows = tile * nchunk
    pairs = tile * TOP_K
    n_sort_chunks = pairs // SORT_CHUNK
    half = (b % 2) * pairs
    other_half = pairs - half

    def stage_rows(r):
        return pl.ds(pl.multiple_of(r * nchunk, nchunk), nchunk)

    def listed_rows(p):
        return pl.ds(pl.multiple_of(list_tok[p], nchunk), nchunk)

    def sort_pair(j, first_row, dst_half):
        p = pos_in[j] + dst_half
        list_tok[p] = first_row
        list_gate[p] = gate_in[j]

    def sort_chunks(lo, hi, dst_half):
        def body(jo, c):
            for u in range(SORT_UNROLL):
                j = jo * SORT_UNROLL + u
                sort_pair(j, (j & (tile - 1)) * nchunk, dst_half)
            return c
        lax.fori_loop(lo * (SORT_CHUNK // SORT_UNROLL), hi * (SORT_CHUNK // SORT_UNROLL), body, 0)

    def load_pairs(t):
        rows = pl.ds(pl.multiple_of(t * pairs, pairs), pairs)
        copies = [pltpu.make_async_copy(pos_hbm.at[rows], pos_in, sem.at[2]),
                  pltpu.make_async_copy(gate_hbm.at[rows], gate_in, sem.at[3])]
        for cp in copies:
            cp.start()
        for cp in copies:
            cp.wait()

    @pl.when(jnp.logical_and(b == 0, e == 0))
    def _():
        stage[...] = jnp.zeros_like(stage)

        def pad_body(j, c):
            list_tok[pairs + j] = 0
            list_tok[2 * pairs + j] = 0
            return c
        lax.fori_loop(0, LIST_PAD, pad_body, 0)
        load_pairs(0)
        n_sorted[0] = 0

    @pl.when(e == 0)
    def _():
        src = h_hbm.at[pl.ds(pl.multiple_of(b * tile_rows, tile_rows), tile_rows), :]
        load = pltpu.make_async_copy(src, hbuf, sem.at[0])
        load.start()
        ybuf[...] = jnp.zeros_like(ybuf)
        sort_chunks(jnp.minimum(n_sorted[0], n_sort_chunks), n_sort_chunks, half)
        n_sorted[0] = 0

        @pl.when(b + 1 < n_tiles)
        def _():
            load_pairs(b + 1)
        load.wait()

    n = cnt_ref[b * n_exp + e]
    base = offs_ref[b * n_exp + e] + half

    def run_block(m_rows, p0, nrows):
        def gather_body(ci, c):
            for u in range(ROW_UNROLL):
                r = ci * ROW_UNROLL + u
                stage[stage_rows(r), :] = hbuf[listed_rows(p0 + r), :]
            return c
        lax.fori_loop(0, (nrows + ROW_UNROLL - 1) // ROW_UNROLL, gather_body, 0)

        if m_rows == MOE_BLOCK:
            chunk = jnp.minimum(n_sorted[0], n_sort_chunks - 1)
            n_sorted[0] = n_sorted[0] + 1
            j0 = chunk * SORT_CHUNK
            row0 = (j0 & (tile - 1)) * nchunk
            for u in range(SORT_CHUNK):
                sort_pair(j0 + u, row0 + u * nchunk, other_half)

        x = jnp.concatenate(
            [stage[pl.ds(c, m_rows, stride=nchunk), :] for c in range(nchunk)],
            axis=1).astype(BF16)
        y = b2_ref[0]
        for hh in range(d_ff // FF_CHUNK):
            sl = slice(hh * FF_CHUNK, (hh + 1) * FF_CHUNK)
            hg = jnp.dot(x, w1g_ref[0, :, sl], preferred_element_type=F32) + b1g_ref[0, :, sl]
            hl = jnp.dot(x, w1l_ref[0, :, sl], preferred_element_type=F32) + b1l_ref[0, :, sl]
            xg = jnp.minimum(hg, SWIGLU_LIMIT)
            xl = jnp.clip(hl, -SWIGLU_LIMIT, SWIGLU_LIMIT)
            act = xg * jax.nn.sigmoid(SWIGLU_ALPHA * xg) * (xl + 1.0)
            y = y + jnp.dot(act.astype(BF16), w2_ref[0, sl, :], preferred_element_type=F32)
        for c in range(nchunk):
            stage[pl.ds(c, m_rows, stride=nchunk), :] = y[:, c * LANES:(c + 1) * LANES]

        def add_body(ci, c):
            rows, vals = [], []
            for u in range(SUBLANES):
                r = ci * SUBLANES + u
                dst = listed_rows(p0 + r)
                rows.append(dst)
                vals.append(ybuf[dst, :] + list_gate[p0 + r] * stage[stage_rows(r), :])
            for dst, val in zip(rows, vals):
                ybuf[dst, :] = val
            return c
        n_full = nrows // SUBLANES
        lax.fori_loop(0, n_full, add_body, 0)

        def add_tail(r, c):
            dst = listed_rows(p0 + r)
            ybuf[dst, :] = ybuf[dst, :] + list_gate[p0 + r] * stage[stage_rows(r), :]
            return c
        lax.fori_loop(n_full * SUBLANES, nrows, add_tail, 0)

    n_big = n // MOE_BLOCK
    rem = n - n_big * MOE_BLOCK
    rem_is_big = rem > MOE_BLOCK - SMALL_BLOCK
    n_big_blocks = n_big + rem_is_big.astype(jnp.int32)
    n_small_blocks = jnp.where(rem_is_big, 0, (rem + SMALL_BLOCK - 1) // SMALL_BLOCK)

    def big_body(s, c):
        run_block(MOE_BLOCK, base + s * MOE_BLOCK, jnp.minimum(MOE_BLOCK, n - s * MOE_BLOCK))
        return c
    lax.fori_loop(0, n_big_blocks, big_body, 0)

    def small_body(s, c):
        run_block(SMALL_BLOCK, base + n_big * MOE_BLOCK + s * SMALL_BLOCK,
                  jnp.minimum(SMALL_BLOCK, rem - s * SMALL_BLOCK))
        return c
    lax.fori_loop(0, n_small_blocks, small_body, 0)

    @pl.when(e == n_exp - 1)
    def _():
        def out_copy(c, slot):
            rows = pl.ds(pl.multiple_of(b * tile + c * ROW_TILE, ROW_TILE), ROW_TILE)
            return pltpu.make_async_copy(out_buf.at[slot], out_hbm.at[rows, :], sem.at[4 + slot])

        def chunk_body(c, carry):
            slot = c % 2

            @pl.when(c >= 2)
            def _():
                out_copy(c - 2, slot).wait()
            first = c * (ROW_TILE * nchunk)
            pieces = [DEEPNORM_ALPHA * hbuf[pl.ds(first + k, ROW_TILE, stride=nchunk), :]
                      + ybuf[pl.ds(first + k, ROW_TILE, stride=nchunk), :] for k in range(nchunk)]
            out_buf[slot] = _layer_norm(jnp.concatenate(pieces, axis=1), g2_ref[...], bb2_ref[...])
            out_copy(c, slot).start()
            return carry
        n_out = tile // ROW_TILE
        lax.fori_loop(0, n_out, chunk_body, 0)
        for c in range(max(n_out - 2, 0), n_out):
            out_copy(c, c % 2).wait()


def _moe(h1t, cnt, offs, pos_flat, gate_flat, w1g, w1l, b1g, b1l, w2, b2, g2, bb2, tile):
    n_exp, d, f = w1g.shape
    nchunk = d // LANES
    n_tiles = h1t.shape[0] // (tile * nchunk)
    pairs = tile * TOP_K
    assert tile % SORT_CHUNK == 0
    expert = lambda b, e, cnt, offs: (e, 0, 0)
    grid_spec = pltpu.PrefetchScalarGridSpec(
        num_scalar_prefetch=2,
        grid=(n_tiles, n_exp),
        in_specs=[
            pl.BlockSpec((1, d, f), expert),
            pl.BlockSpec((1, d, f), expert),
            pl.BlockSpec((1, 1, f), expert),
            pl.BlockSpec((1, 1, f), expert),
            pl.BlockSpec((1, f, d), expert),
            pl.BlockSpec((1, 1, d), expert),
            pl.BlockSpec((1, d), lambda b, e, cnt, offs: (0, 0)),
            pl.BlockSpec((1, d), lambda b, e, cnt, offs: (0, 0)),
            pl.BlockSpec(memory_space=pl.ANY),
            pl.BlockSpec(memory_space=pl.ANY),
            pl.BlockSpec(memory_space=pl.ANY),
        ],
        out_specs=pl.BlockSpec(memory_space=pl.ANY),
        scratch_shapes=[
            pltpu.VMEM((tile * nchunk, LANES), F32),
            pltpu.VMEM((tile * nchunk, LANES), F32),
            pltpu.VMEM((MOE_BLOCK * nchunk, LANES), F32),
            pltpu.VMEM((2, ROW_TILE, d), F32),
            pltpu.SMEM((2 * pairs + LIST_PAD,), jnp.int32),
            pltpu.SMEM((2 * pairs + LIST_PAD,), F32),
            pltpu.SMEM((pairs,), jnp.int32),
            pltpu.SMEM((pairs,), F32),
            pltpu.SMEM((1,), jnp.int32),
            pltpu.SemaphoreType.DMA((6,)),
        ],
    )
    return pl.pallas_call(
        functools.partial(_moe_kernel, tile=tile),
        grid_spec=grid_spec,
        out_shape=jax.ShapeDtypeStruct((n_tiles * tile, d), F32),
        compiler_params=pltpu.CompilerParams(
            dimension_semantics=("arbitrary", "arbitrary"), vmem_limit_bytes=MOE_VMEM_LIMIT),
        name="moe",
    )(cnt, offs, w1g, w1l, b1g, b1l, w2, b2, g2, bb2, pos_flat, gate_flat, h1t)


def kernel(x, ln_in_g, ln_in_b, w_in, lambda_q1, lambda_k1, lambda_q2, lambda_k2, subln_g, rel_bias,
           w_out, ln1_g, ln1_b, router_w, router_b, w1, b1, w2, b2, ln2_g, ln2_b):
    bsz, seq, d = x.shape
    t = bsz * seq
    x2 = x.reshape(t, d)
    row = lambda v: v.reshape(1, -1).astype(F32)

    qs = A_HEAD_DIM ** -0.5 * LOG2E
    col_scale = jnp.concatenate([
        jnp.full((A_WIDTH,), qs, F32), jnp.ones((2 * A_WIDTH,), F32),
        jnp.full((B_WIDTH,), B_HEAD_DIM ** -0.5 * LOG2E, F32), jnp.ones((2 * B_WIDTH,), F32)]).reshape(1, -1)
    proj3, vt3, vtb3 = _ln_qkv(x2, row(ln_in_g), row(ln_in_b), w_in[0].astype(BF16), col_scale, seq)

    lam4 = jnp.stack([lambda_q1[0], lambda_k1[0], lambda_q2[0], lambda_k2[0]]).astype(F32)
    o_a = _diff_attn(proj3, vt3, lam4, subln_g[0].astype(F32))
    o_b = _band_attn(proj3, vtb3, rel_bias[0])

    w_o = w_out[0].astype(BF16)
    tile = min(MOE_TILE, t)
    n_tiles = t // tile
    h1t, top_idx, gates, rank, run_cnt = _out_router(
        x2, row(ln_in_g), row(ln_in_b), o_a, o_b,
        w_o[:A_WIDTH], w_o[A_WIDTH:], row(ln1_g[0]), row(ln1_b[0]),
        router_w[0].T.astype(F32), router_b[0].reshape(-1, 1).astype(F32), tile)

    steps = tile // ROW_TILE
    cnt = run_cnt[steps - 1::steps, :, 0]
    offs = jnp.cumsum(cnt, axis=1) - cnt
    hot = top_idx.reshape(TOP_K, n_tiles, tile, 1) == jnp.arange(N_EXPERTS, dtype=jnp.int32)
    pos = rank + jnp.sum(jnp.where(hot, offs[None, :, None, :], 0), axis=-1).reshape(TOP_K, t)
    per_tile = lambda a: a.reshape(TOP_K, n_tiles, tile).transpose(1, 0, 2).reshape(-1)
    w1g, w1l = _split_w1(w1[0])
    b1e = b1[0].astype(F32)[:, None, :]
    out = _moe(h1t, cnt.reshape(-1), offs.reshape(-1), per_tile(pos), per_tile(gates),
               w1g, w1l, b1e[:, :, 0::2], b1e[:, :, 1::2],
               w2[0].astype(BF16), b2[0][:, None, :].astype(F32), row(ln2_g[0]), row(ln2_b[0]), tile)
    return out.reshape(bsz, seq, d)
```

```python
import functools
import math

import jax
import jax.numpy as jnp
from jax import lax
from jax.experimental import pallas as pl
from jax.experimental.pallas import tpu as pltpu

F32 = jnp.float32
BF16 = jnp.bfloat16

CHUNK = 64
A_HEADS = 4
A_HEAD_DIM = 64
A_WIDTH = A_HEADS * 2 * A_HEAD_DIM
B_HEADS = 8
B_HEAD_DIM = 64
B_WIDTH = B_HEADS * B_HEAD_DIM
B_PAST_CHUNKS = 8
REL_CLIP = 256
N_EXPERTS = 32
TOP_K = 4
SWIGLU_ALPHA = 1.702
SWIGLU_LIMIT = 7.0
MOE_BLOCK = 512
LN_EPS = 1e-5
RMS_EPS = 1e-5
DEPTH = 1
DEEPNORM_ALPHA = (2 * DEPTH) ** 0.25
LAM_INIT = 0.8 - 0.6 * math.exp(-0.3 * 0)

LOG2E = 1.4426950408889634
NEG = -1e30
LANES = 128
SUBLANES = 8
ROW_TILE = 512
VMEM_LIMIT = 48 * 1024 * 1024
DIFF_TILE = 256
BAND_TILE = 128
BAND_TILES = (BAND_TILE + B_PAST_CHUNKS * CHUNK) // BAND_TILE
BAND_PAIR = 2
BAND_STEP_ROWS = 4096
MOE_TILE = 4096
MOE_VMEM_LIMIT = 60 * 1024 * 1024
SMALL_BLOCK = 128
FF_CHUNK = 1024
SORT_UNROLL = 8
SORT_CHUNK = 512
ROW_UNROLL = 32
LIST_PAD = 128


def _layer_norm(x, g, b):
    mu = jnp.mean(x, axis=-1, keepdims=True)
    xc = x - mu
    var = jnp.mean(xc * xc, axis=-1, keepdims=True)
    return xc * lax.rsqrt(var + LN_EPS) * g + b


def _ln_qkv_kernel(x_ref, g_ref, b_ref, w_ref, cs_ref, o_ref, vta_ref, vtb_ref):
    h = _layer_norm(x_ref[...], g_ref[...], b_ref[...])
    hb = h.astype(BF16)
    n_out = w_ref.shape[1]
    for c in range(n_out // ROW_TILE):
        sl = slice(c * ROW_TILE, (c + 1) * ROW_TILE)
        val = jnp.dot(hb, w_ref[:, sl], preferred_element_type=F32) * cs_ref[:, sl]
        o_ref[0, :, sl] = val.astype(BF16)
        for start, vt_ref in ((2 * A_WIDTH, vta_ref), (3 * A_WIDTH + 2 * B_WIDTH, vtb_ref)):
            if sl.start == start:
                tile = vt_ref.shape[2]
                for kt in range(ROW_TILE // tile):
                    vt_ref[kt] = val[kt * tile:(kt + 1) * tile, :].T.astype(BF16)


def _ln_qkv(x2, g, b, w_bf, col_scale, seq):
    assert A_WIDTH == ROW_TILE and B_WIDTH == ROW_TILE
    t, d = x2.shape
    n_out = w_bf.shape[1]
    steps_per_seq = seq // ROW_TILE
    vt_spec = lambda tile: pl.BlockSpec((ROW_TILE // tile, ROW_TILE, tile), lambda i: (i, 0, 0))
    vt_shape = lambda tile: jax.ShapeDtypeStruct((t // tile, ROW_TILE, tile), BF16)
    return pl.pallas_call(
        _ln_qkv_kernel,
        grid=(t // ROW_TILE,),
        in_specs=[
            pl.BlockSpec((ROW_TILE, d), lambda i: (i, 0)),
            pl.BlockSpec((1, d), lambda i: (0, 0)),
            pl.BlockSpec((1, d), lambda i: (0, 0)),
            pl.BlockSpec((d, n_out), lambda i: (0, 0)),
            pl.BlockSpec((1, n_out), lambda i: (0, 0)),
        ],
        out_specs=[pl.BlockSpec((1, ROW_TILE, n_out),
                                lambda i: (i // steps_per_seq, i % steps_per_seq, 0)),
                   vt_spec(DIFF_TILE), vt_spec(BAND_TILE)],
        out_shape=[jax.ShapeDtypeStruct((t // seq, seq, n_out), BF16),
                   vt_shape(DIFF_TILE), vt_shape(BAND_TILE)],
        compiler_params=pltpu.CompilerParams(
            dimension_semantics=("arbitrary",), vmem_limit_bytes=VMEM_LIMIT),
        name="ln_qkv",
    )(x2, g, b, w_bf, col_scale)


def _diff_attn_kernel(lam_ref, g_ref, kf_ref, qf_ref, bdiag_ref, q_ref, k_ref, vt_ref, o_ref, s_scr):
    tq = q_ref.shape[1]
    tk = vt_ref.shape[2]
    i = pl.program_id(2)
    q = q_ref[0]
    lane = lax.broadcasted_iota(jnp.int32, q.shape, 1)
    zero = jnp.zeros_like(q)
    qq = jnp.concatenate([jnp.where(lane < A_HEAD_DIM, q, zero),
                          jnp.where(lane >= A_HEAD_DIM, q, zero)], axis=0)
    qqt = jnp.concatenate([qq.astype(F32).T.astype(BF16), qf_ref[0]], axis=0)
    n_before = i * (tq // tk)

    def scores(j):
        rows = pl.ds(pl.multiple_of(j * tk, tk), tk)
        kb = jnp.concatenate([k_ref[0, rows, :], kf_ref[rows, :]], axis=1)
        return jnp.dot(kb, qqt, preferred_element_type=F32)

    def update(slot, table_ref, j, carry):
        m, l, acc = carry
        s = s_scr[slot]
        if table_ref is not None:
            s = s + table_ref[0]
        m_new = jnp.maximum(m, jnp.max(s, axis=0, keepdims=True))
        alpha = jnp.exp2(m - m_new)
        p = jnp.exp2(s - m_new)
        l = alpha * l + jnp.sum(p, axis=0, keepdims=True)
        acc = alpha * acc + jnp.dot(vt_ref[j], p.astype(BF16), preferred_element_type=F32)
        return m_new, l, acc

    s_scr[0] = scores(0)

    def pair(t, carry):
        j = 2 * t
        s_scr[1] = scores(j + 1)
        carry = update(0, None, j, carry)
        s_scr[0] = scores(j + 2)
        return update(1, None, j + 1, carry)

    def quad(t, carry):
        return pair(2 * t + 1, pair(2 * t, carry))

    init = (jnp.full((1, 2 * tq), NEG, F32), jnp.zeros((1, 2 * tq), F32),
            jnp.zeros((LANES, 2 * tq), F32))
    carry = lax.fori_loop(0, n_before // 4, quad, init)
    carry = lax.fori_loop(n_before // 4 * 2, n_before // 2, pair, carry)

    def odd_tail(carry):
        s_scr[1] = scores(n_before)
        carry = update(0, None, n_before - 1, carry)
        return update(1, bdiag_ref, n_before, carry)

    def even_tail(carry):
        return update(0, bdiag_ref, n_before, carry)

    _, l, acc = lax.cond(n_before % 2 == 1, odd_tail, even_tail, carry)

    o_all = acc / l
    lv = lam_ref[...]
    lam = (jnp.exp(jnp.sum(lv[0:1] * lv[1:2], axis=1, keepdims=True))
           - jnp.exp(jnp.sum(lv[2:3] * lv[3:4], axis=1, keepdims=True)) + LAM_INIT)
    o = o_all[:, :tq] - lam * o_all[:, tq:]
    ms = jnp.mean(o * o, axis=0, keepdims=True)
    o = o * lax.rsqrt(ms + RMS_EPS) * (g_ref[...] * (1.0 - LAM_INIT))
    o_ref[0] = o.T.astype(BF16)


def _diff_attn(proj3, vt3, lam4, subln_g):
    bsz, seq, _ = proj3.shape
    tq = tk = DIFF_TILE
    assert seq <= CHUNK * CHUNK
    c = jnp.asarray([2.0 ** (-8.0 * (h + 1) / A_HEADS) for h in range(A_HEADS)], F32) * LOG2E
    r = jnp.arange(tk, dtype=jnp.int32)[:, None]
    qrel = jnp.arange(2 * tq, dtype=jnp.int32)[None, :] % tq
    allowed = (r // CHUNK) <= (qrel // CHUNK)
    kpos = jnp.arange(seq, dtype=jnp.int32)
    digits = jnp.stack([kpos // CHUNK] * 3 + [kpos % CHUNK] * 3, axis=1).astype(BF16)
    kfeat = jnp.pad(digits, ((0, 0), (0, LANES - digits.shape[1])))

    def pieces(v):
        hi = v.astype(BF16)
        mid = (v - hi.astype(F32)).astype(BF16)
        lo = (v - hi.astype(F32) - mid.astype(F32)).astype(BF16)
        return [hi, mid, lo]
    rows = jnp.stack(pieces(c * CHUNK) + pieces(c), axis=1)
    qfeat = jnp.broadcast_to(jnp.pad(rows, ((0, 0), (0, LANES - rows.shape[1])))[:, :, None],
                             (A_HEADS, LANES, 2 * tq))
    bdiag = jnp.where(allowed, c[:, None, None] * (qrel - jnp.abs(qrel - r) - r).astype(F32), NEG)
    kblk = A_WIDTH // LANES
    return pl.pallas_call(
        _diff_attn_kernel,
        grid=(bsz, A_HEADS, seq // tq),
        in_specs=[
            pl.BlockSpec((4, A_HEAD_DIM), lambda b, h, i: (0, 0)),
            pl.BlockSpec((LANES, 1), lambda b, h, i: (0, 0)),
            pl.BlockSpec((seq, LANES), lambda b, h, i: (0, 0)),
            pl.BlockSpec((1, LANES, 2 * tq), lambda b, h, i: (h, 0, 0)),
            pl.BlockSpec((1, tk, 2 * tq), lambda b, h, i: (h, 0, 0)),
            pl.BlockSpec((1, tq, LANES), lambda b, h, i: (b, i, h)),
            pl.BlockSpec((1, seq, LANES), lambda b, h, i: (b, 0, kblk + h)),
            pl.BlockSpec((seq // tk, LANES, tk), lambda b, h, i: (b, h, 0)),
        ],
        out_specs=pl.BlockSpec((1, tq, LANES), lambda b, h, i: (b, i, h)),
        out_shape=jax.ShapeDtypeStruct((bsz, seq, A_WIDTH), BF16),
        scratch_shapes=[pltpu.VMEM((2, tk, 2 * tq), F32)],
        compiler_params=pltpu.CompilerParams(
            dimension_semantics=("arbitrary", "arbitrary", "arbitrary"),
            vmem_limit_bytes=VMEM_LIMIT),
        name="diff_attn",
    )(lam4, subln_g.reshape(LANES, 1), kfeat, qfeat, bdiag, proj3, proj3, vt3)


def _band_attn_kernel(bias_ref, q_ref, k_ref, vt_ref, o_ref, s_scr):
    i = pl.program_id(2)
    tq = BAND_TILE
    units = q_ref.shape[1] // (BAND_PAIR * tq)
    key_tiles = BAND_TILES + BAND_PAIR - 1
    row = lax.broadcasted_iota(jnp.int32, (LANES, tq), 0)

    def key_tile(n, t):
        jt = (i * units + n) * BAND_PAIR - (BAND_TILES - 1) + t
        return jnp.maximum(jt, 0), jt < 0

    def scores(n):
        qs = []
        for a in range(BAND_PAIR):
            q = q_ref[0, (n * BAND_PAIR + a) * tq:(n * BAND_PAIR + a + 1) * tq, :]
            lane = lax.broadcasted_iota(jnp.int32, q.shape, 1)
            zero = jnp.zeros_like(q)
            qs += [jnp.where(lane < B_HEAD_DIM, q, zero), jnp.where(lane >= B_HEAD_DIM, q, zero)]
        qqt = jnp.concatenate(qs, axis=0).astype(F32).T.astype(BF16)
        ks = [k_ref[0, pl.ds(pl.multiple_of(key_tile(n, t)[0] * tq, tq), tq), :] for t in range(key_tiles)]
        return jnp.dot(jnp.concatenate(ks, axis=0), qqt, preferred_element_type=F32)

    def finish(n):
        bias = [bias_ref[0, jnp.where(key_tile(n, t)[1], 1, 0), t] for t in range(key_tiles)]
        s = s_scr[n % 2] + jnp.concatenate(bias, axis=0)
        m = jnp.max(s, axis=0, keepdims=True)
        p = jnp.exp2(s - m)
        l = jnp.sum(p, axis=0, keepdims=True)
        vt = jnp.concatenate([vt_ref[key_tile(n, t)[0]] for t in range(key_tiles)], axis=1)
        o = jnp.dot(vt, p.astype(BF16), preferred_element_type=F32) / l
        for a in range(BAND_PAIR):
            oa = jnp.where(row < B_HEAD_DIM, o[:, 2 * a * tq:(2 * a + 1) * tq],
                           o[:, (2 * a + 1) * tq:(2 * a + 2) * tq])
            o_ref[0, (n * BAND_PAIR + a) * tq:(n * BAND_PAIR + a + 1) * tq, :] = oa.T.astype(BF16)

    s_scr[0] = scores(0)
    for n in range(units):
        if n + 1 < units:
            s_scr[(n + 1) % 2] = scores(n + 1)
        finish(n)


def _band_bias(rel_bias, tq):
    past = B_PAST_CHUNKS * CHUNK
    band = tq + past
    assert tq - 1 <= REL_CLIP <= past
    qi = jnp.arange(tq)
    kj = jnp.arange(band)
    cq = qi[:, None] // CHUNK
    ck = kj[None, :] // CHUNK
    allowed = (ck >= cq) & (ck <= cq + B_PAST_CHUNKS)
    tab = rel_bias.astype(F32) * LOG2E
    n_diag = band + tq - 1
    n_unclipped = REL_CLIP + tq
    w = jnp.concatenate([tab[:, REL_CLIP - (tq - 1):],
                         jnp.broadcast_to(tab[:, -1:], (B_HEADS, n_diag - n_unclipped))], axis=1)
    shifted = jnp.tile(w, (1, tq + 1))[:, :tq * (n_diag + 1)].reshape(B_HEADS, tq, n_diag + 1)
    bias = jnp.flip(shifted[:, :, :band], axis=2)
    return jnp.where(allowed[None], bias, NEG)


def _band_attn(proj3, vtb3, rel_bias):
    bsz, seq, _ = proj3.shape
    tq = BAND_TILE
    groups = B_HEADS // 2
    key_tiles = BAND_TILES + BAND_PAIR - 1
    cols = BAND_PAIR * 2 * tq
    bias = _band_bias(rel_bias, tq)
    bias = bias.reshape(groups, 2, tq, BAND_TILES, tq)
    masked = jnp.full((groups, 2, tq, 1, tq), NEG, F32)
    per_tile = [jnp.concatenate([masked] * a + [bias] + [masked] * (BAND_PAIR - 1 - a), axis=3)
                for a in range(BAND_PAIR)]
    bias = jnp.stack(per_tile, axis=1)
    bias = bias.transpose(0, 4, 5, 1, 2, 3).reshape(groups, 1, key_tiles, tq, cols)
    bias = jnp.concatenate([bias, jnp.full_like(bias, NEG)], axis=1)
    qblk = 3 * A_WIDTH // LANES
    kblk = qblk + B_WIDTH // LANES
    rows = min(BAND_STEP_ROWS, seq)
    return pl.pallas_call(
        _band_attn_kernel,
        grid=(bsz, groups, seq // rows),
        in_specs=[
            pl.BlockSpec((1, 2, key_tiles, tq, cols), lambda b, g, i: (g, 0, 0, 0, 0)),
            pl.BlockSpec((1, rows, LANES), lambda b, g, i: (b, i, qblk + g)),
            pl.BlockSpec((1, seq, LANES), lambda b, g, i: (b, 0, kblk + g)),
            pl.BlockSpec((seq // tq, LANES, tq), lambda b, g, i: (b, g, 0)),
        ],
        out_specs=pl.BlockSpec((1, rows, LANES), lambda b, g, i: (b, i, g)),
        out_shape=jax.ShapeDtypeStruct((bsz, seq, B_WIDTH), BF16),
        scratch_shapes=[pltpu.VMEM((2, key_tiles * tq, cols), F32)],
        compiler_params=pltpu.CompilerParams(
            dimension_semantics=("arbitrary", "arbitrary", "arbitrary"),
            vmem_limit_bytes=VMEM_LIMIT),
        name="band_attn",
    )(bias, proj3, proj3, vtb3)


def _out_router_kernel(x_ref, gi_ref, bi_ref, oa_ref, ob_ref, wa_ref, wb_ref, g1_ref, b1_ref,
                       rwt_ref, rb_ref, tri_ref, h1_ref, idx_ref, gate_ref, rank_ref, cnt_ref,
                       carry_ref, *, steps_per_tile):
    h = _layer_norm(x_ref[...], gi_ref[...], bi_ref[...])
    mix = (jnp.dot(oa_ref[0], wa_ref[...], preferred_element_type=F32)
           + jnp.dot(ob_ref[0], wb_ref[...], preferred_element_type=F32))
    h1 = _layer_norm(DEEPNORM_ALPHA * h + mix, g1_ref[...], b1_ref[...])
    nchunk = h1.shape[1] // LANES
    for c in range(nchunk):
        h1_ref[pl.ds(c, ROW_TILE, stride=nchunk), :] = h1[:, c * LANES:(c + 1) * LANES]
    lt = lax.dot_general(rwt_ref[...], h1, (((1,), (1,)), ((), ())),
                         precision=lax.Precision.HIGHEST, preferred_element_type=F32)
    lt = lt + rb_ref[...]
    eidx = lax.broadcasted_iota(jnp.int32, lt.shape, 0)
    vals, idxs, hots = [], [], []
    for _ in range(TOP_K):
        mx = jnp.max(lt, axis=0, keepdims=True)
        am = jnp.min(jnp.where(lt == mx, eidx, N_EXPERTS), axis=0, keepdims=True)
        hit = eidx == am
        vals.append(mx)
        idxs.append(am)
        hots.append(jnp.where(hit, 1.0, 0.0))
        lt = jnp.where(hit, -jnp.inf, lt)
    ex = [jnp.exp(v - vals[0]) for v in vals]
    den = ex[0] + ex[1] + ex[2] + ex[3]
    idx_ref[...] = jnp.concatenate(idxs, axis=0)
    gate_ref[...] = jnp.concatenate([e / den for e in ex], axis=0)

    @pl.when(pl.program_id(0) % steps_per_tile == 0)
    def _():
        carry_ref[...] = jnp.zeros_like(carry_ref)

    hot = (hots[0] + hots[1]) + (hots[2] + hots[3])
    before = jnp.dot(hot.astype(BF16), tri_ref[...], preferred_element_type=F32) + carry_ref[...]
    rank_ref[...] = jnp.concatenate(
        [jnp.sum(hk * before, axis=0, keepdims=True) for hk in hots], axis=0).astype(jnp.int32)
    total = carry_ref[...] + jnp.sum(hot, axis=1, keepdims=True)
    carry_ref[...] = total
    cnt_ref[0] = total.astype(jnp.int32)


def _out_router(x2, gi, bi, o_a, o_b, wa, wb, g1, b1, rwt, rb, moe_tile):
    t, d = x2.shape
    nchunk = d // LANES
    row = lambda i: (i, 0)
    fixed = lambda i: (0, 0)
    steps_per_seq = o_a.shape[1] // ROW_TILE
    seq_row = lambda i: (i // steps_per_seq, i % steps_per_seq, 0)
    tri = jnp.triu(jnp.ones((ROW_TILE, ROW_TILE), BF16), k=1)
    return pl.pallas_call(
        functools.partial(_out_router_kernel, steps_per_tile=moe_tile // ROW_TILE),
        grid=(t // ROW_TILE,),
        in_specs=[
            pl.BlockSpec((ROW_TILE, d), row),
            pl.BlockSpec((1, d), fixed),
            pl.BlockSpec((1, d), fixed),
            pl.BlockSpec((1, ROW_TILE, A_WIDTH), seq_row),
            pl.BlockSpec((1, ROW_TILE, B_WIDTH), seq_row),
            pl.BlockSpec((A_WIDTH, d), fixed),
            pl.BlockSpec((B_WIDTH, d), fixed),
            pl.BlockSpec((1, d), fixed),
            pl.BlockSpec((1, d), fixed),
            pl.BlockSpec((N_EXPERTS, d), fixed),
            pl.BlockSpec((N_EXPERTS, 1), fixed),
            pl.BlockSpec((ROW_TILE, ROW_TILE), fixed),
        ],
        out_specs=[
            pl.BlockSpec((ROW_TILE * nchunk, LANES), row),
            pl.BlockSpec((TOP_K, ROW_TILE), lambda i: (0, i)),
            pl.BlockSpec((TOP_K, ROW_TILE), lambda i: (0, i)),
            pl.BlockSpec((TOP_K, ROW_TILE), lambda i: (0, i)),
            pl.BlockSpec((1, N_EXPERTS, 1), lambda i: (i, 0, 0)),
        ],
        out_shape=[
            jax.ShapeDtypeStruct((t * nchunk, LANES), F32),
            jax.ShapeDtypeStruct((TOP_K, t), jnp.int32),
            jax.ShapeDtypeStruct((TOP_K, t), F32),
            jax.ShapeDtypeStruct((TOP_K, t), jnp.int32),
            jax.ShapeDtypeStruct((t // ROW_TILE, N_EXPERTS, 1), jnp.int32),
        ],
        scratch_shapes=[pltpu.VMEM((N_EXPERTS, 1), F32)],
        compiler_params=pltpu.CompilerParams(
            dimension_semantics=("arbitrary",), vmem_limit_bytes=VMEM_LIMIT),
        name="out_router",
    )(x2, gi, bi, o_a, o_b, wa, wb, g1, b1, rwt, rb, tri)


def _split_w1_kernel(w_ref, perm_ref, g_ref, l_ref):
    rows = w_ref.shape[1]
    even = (lax.broadcasted_iota(jnp.int32, (rows, LANES), 1) & 1) == 0
    perm = perm_ref[...]
    for g in range(w_ref.shape[2] // (2 * LANES)):
        v0 = w_ref[0, :, 2 * g * LANES:(2 * g + 1) * LANES]
        v1 = w_ref[0, :, (2 * g + 1) * LANES:(2 * g + 2) * LANES]
        glu = jnp.where(even, v0, pltpu.roll(v1, 1, axis=1)).astype(BF16)
        lin = jnp.where(even, pltpu.roll(v0, LANES - 1, axis=1), v1).astype(BF16)
        g_ref[0, :, g * LANES:(g + 1) * LANES] = jnp.dot(
            glu, perm, preferred_element_type=F32).astype(BF16)
        l_ref[0, :, g * LANES:(g + 1) * LANES] = jnp.dot(
            lin, perm, preferred_element_type=F32).astype(BF16)


def _split_w1(w1e):
    e, d, f2 = w1e.shape
    rows = 1024
    half = LANES // 2
    unit = jnp.arange(LANES)
    perm = (jnp.arange(LANES)[:, None] == (2 * (unit % half) + unit // half)[None, :]).astype(BF16)
    return pl.pallas_call(
        _split_w1_kernel,
        grid=(e, d // rows),
        in_specs=[pl.BlockSpec((1, rows, f2), lambda i, j: (i, j, 0)),
                  pl.BlockSpec((LANES, LANES), lambda i, j: (0, 0))],
        out_specs=[pl.BlockSpec((1, rows, f2 // 2), lambda i, j: (i, j, 0)),
                   pl.BlockSpec((1, rows, f2 // 2), lambda i, j: (i, j, 0))],
        out_shape=[jax.ShapeDtypeStruct((e, d, f2 // 2), BF16),
                   jax.ShapeDtypeStruct((e, d, f2 // 2), BF16)],
        compiler_params=pltpu.CompilerParams(
            dimension_semantics=("arbitrary", "arbitrary"), vmem_limit_bytes=VMEM_LIMIT),
        name="split_w1",
    )(w1e, perm)


def _moe_kernel(cnt_ref, offs_ref,
                w1g_ref, w1l_ref, b1g_ref, b1l_ref, w2_ref, b2_ref, g2_ref, bb2_ref,
                pos_hbm, gate_hbm, h_hbm,
                out_hbm, hbuf, ybuf, stage, out_buf, list_tok, list_gate, pos_in, gate_in, n_sorted, sem,
                *, tile):
    b = pl.program_id(0)
    e = pl.program_id(1)
    n_tiles = pl.num_programs(0)
    n_exp = pl.num_programs(1)
    d_model = w1g_ref.shape[1]
    d_ff = w1g_ref.shape[2]
    nchunk = d_model // LANES
    tile_rows = tile * nchunk
    pairs = tile * TOP_K
    n_sort_chunks = pairs // SORT_CHUNK
    half = (b % 2) * pairs
    other_half = pairs - half

    def stage_rows(r):
        return pl.ds(pl.multiple_of(r * nchunk, nchunk), nchunk)

    def listed_rows(p):
        return pl.ds(pl.multiple_of(list_tok[p], nchunk), nchunk)

    def sort_pair(j, first_row, dst_half):
        p = pos_in[j] + dst_half
        list_tok[p] = first_row
        list_gate[p] = gate_in[j]

    def sort_chunks(lo, hi, dst_half):
        def body(jo, c):
            for u in range(SORT_UNROLL):
                j = jo * SORT_UNROLL + u
                sort_pair(j, (j & (tile - 1)) * nchunk, dst_half)
            return c
        lax.fori_loop(lo * (SORT_CHUNK // SORT_UNROLL), hi * (SORT_CHUNK // SORT_UNROLL), body, 0)

    def load_pairs(t):
        rows = pl.ds(pl.multiple_of(t * pairs, pairs), pairs)
        copies = [pltpu.make_async_copy(pos_hbm.at[rows], pos_in, sem.at[2]),
                  pltpu.make_async_copy(gate_hbm.at[rows], gate_in, sem.at[3])]
        for cp in copies:
            cp.start()
        for cp in copies:
            cp.wait()

    @pl.when(jnp.logical_and(b == 0, e == 0))
    def _():
        stage[...] = jnp.zeros_like(stage)

        def pad_body(j, c):
            list_tok[pairs + j] = 0
            list_tok[2 * pairs + j] = 0
            return c
        lax.fori_loop(0, LIST_PAD, pad_body, 0)
        load_pairs(0)
        n_sorted[0] = 0

    @pl.when(e == 0)
    def _():
        src = h_hbm.at[pl.ds(pl.multiple_of(b * tile_rows, tile_rows), tile_rows), :]
        load = pltpu.make_async_copy(src, hbuf, sem.at[0])
        load.start()
        ybuf[...] = jnp.zeros_like(ybuf)
        sort_chunks(jnp.minimum(n_sorted[0], n_sort_chunks), n_sort_chunks, half)
        n_sorted[0] = 0

        @pl.when(b + 1 < n_tiles)
        def _():
            load_pairs(b + 1)
        load.wait()

    n = cnt_ref[b * n_exp + e]
    base = offs_ref[b * n_exp + e] + half

    def run_block(m_rows, p0, nrows):
        def gather_body(ci, c):
            for u in range(ROW_UNROLL):
                r = ci * ROW_UNROLL + u
                stage[stage_rows(r), :] = hbuf[listed_rows(p0 + r), :]
            return c
        lax.fori_loop(0, (nrows + ROW_UNROLL - 1) // ROW_UNROLL, gather_body, 0)

        if m_rows == MOE_BLOCK:
            chunk = jnp.minimum(n_sorted[0], n_sort_chunks - 1)
            n_sorted[0] = n_sorted[0] + 1
            j0 = chunk * SORT_CHUNK
            row0 = (j0 & (tile - 1)) * nchunk
            for u in range(SORT_CHUNK):
                sort_pair(j0 + u, row0 + u * nchunk, other_half)

        x = jnp.concatenate(
            [stage[pl.ds(c, m_rows, stride=nchunk), :] for c in range(nchunk)],
            axis=1).astype(BF16)
        y = b2_ref[0]
        for hh in range(d_ff // FF_CHUNK):
            sl = slice(hh * FF_CHUNK, (hh + 1) * FF_CHUNK)
            hg = jnp.dot(x, w1g_ref[0, :, sl], preferred_element_type=F32) + b1g_ref[0, :, sl]
            hl = jnp.dot(x, w1l_ref[0, :, sl], preferred_element_type=F32) + b1l_ref[0, :, sl]
            xg = jnp.minimum(hg, SWIGLU_LIMIT)
            xl = jnp.clip(hl, -SWIGLU_LIMIT, SWIGLU_LIMIT)
            act = xg * jax.nn.sigmoid(SWIGLU_ALPHA * xg) * (xl + 1.0)
            y = y + jnp.dot(act.astype(BF16), w2_ref[0, sl, :], preferred_element_type=F32)
        for c in range(nchunk):
            stage[pl.ds(c, m_rows, stride=nchunk), :] = y[:, c * LANES:(c + 1) * LANES]

        def add_body(ci, c):
            rows, vals = [], []
            for u in range(SUBLANES):
                r = ci * SUBLANES + u
                dst = listed_rows(p0 + r)
                rows.append(dst)
                vals.append(ybuf[dst, :] + list_gate[p0 + r] * stage[stage_rows(r), :])
            for dst, val in zip(rows, vals):
                ybuf[dst, :] = val
            return c
        n_full = nrows // SUBLANES
        lax.fori_loop(0, n_full, add_body, 0)

        def add_tail(r, c):
            dst = listed_rows(p0 + r)
            ybuf[dst, :] = ybuf[dst, :] + list_gate[p0 + r] * stage[stage_rows(r), :]
            return c
        lax.fori_loop(n_full * SUBLANES, nrows, add_tail, 0)

    n_big = n // MOE_BLOCK
    rem = n - n_big * MOE_BLOCK
    rem_is_big = rem > MOE_BLOCK - SMALL_BLOCK
    n_big_blocks = n_big + rem_is_big.astype(jnp.int32)
    n_small_blocks = jnp.where(rem_is_big, 0, (rem + SMALL_BLOCK - 1) // SMALL_BLOCK)

    def big_body(s, c):
        run_block(MOE_BLOCK, base + s * MOE_BLOCK, jnp.minimum(MOE_BLOCK, n - s * MOE_BLOCK))
        return c
    lax.fori_loop(0, n_big_blocks, big_body, 0)

    def small_body(s, c):
        run_block(SMALL_BLOCK, base + n_big * MOE_BLOCK + s * SMALL_BLOCK,
                  jnp.minimum(SMALL_BLOCK, rem - s * SMALL_BLOCK))
        return c
    lax.fori_loop(0, n_small_blocks, small_body, 0)

    @pl.when(e == n_exp - 1)
    def _():
        def out_copy(c, slot):
            rows = pl.ds(pl.multiple_of(b * tile + c * ROW_TILE, ROW_TILE), ROW_TILE)
            return pltpu.make_async_copy(out_buf.at[slot], out_hbm.at[rows, :], sem.at[4 + slot])

        def chunk_body(c, carry):
            slot = c % 2

            @pl.when(c >= 2)
            def _():
                out_copy(c - 2, slot).wait()
            first = c * (ROW_TILE * nchunk)
            pieces = [DEEPNORM_ALPHA * hbuf[pl.ds(first + k, ROW_TILE, stride=nchunk), :]
                      + ybuf[pl.ds(first + k, ROW_TILE, stride=nchunk), :] for k in range(nchunk)]
            out_buf[slot] = _layer_norm(jnp.concatenate(pieces, axis=1), g2_ref[...], bb2_ref[...])
            out_copy(c, slot).start()
            return carry
        n_out = tile // ROW_TILE
        lax.fori_loop(0, n_out, chunk_body, 0)
        for c in range(max(n_out - 2, 0), n_out):
            out_copy(c, c % 2).wait()


def _moe(h1t, cnt, offs, pos_flat, gate_flat, w1g, w1l, b1g, b1l, w2, b2, g2, bb2, tile):
    n_exp, d, f = w1g.shape
    nchunk = d // LANES
    n_tiles = h1t.shape[0] // (tile * nchunk)
    pairs = tile * TOP_K
    assert tile % SORT_CHUNK == 0
    expert = lambda b, e, cnt, offs: (e, 0, 0)
    grid_spec = pltpu.PrefetchScalarGridSpec(
        num_scalar_prefetch=2,
        grid=(n_tiles, n_exp),
        in_specs=[
            pl.BlockSpec((1, d, f), expert),
            pl.BlockSpec((1, d, f), expert),
            pl.BlockSpec((1, 1, f), expert),
            pl.BlockSpec((1, 1, f), expert),
            pl.BlockSpec((1, f, d), expert),
            pl.BlockSpec((1, 1, d), expert),
            pl.BlockSpec((1, d), lambda b, e, cnt, offs: (0, 0)),
            pl.BlockSpec((1, d), lambda b, e, cnt, offs: (0, 0)),
            pl.BlockSpec(memory_space=pl.ANY),
            pl.BlockSpec(memory_space=pl.ANY),
            pl.BlockSpec(memory_space=pl.ANY),
        ],
        out_specs=pl.BlockSpec(memory_space=pl.ANY),
        scratch_shapes=[
            pltpu.VMEM((tile * nchunk, LANES), F32),
            pltpu.VMEM((tile * nchunk, LANES), F32),
            pltpu.VMEM((MOE_BLOCK * nchunk, LANES), F32),
            pltpu.VMEM((2, ROW_TILE, d), F32),
            pltpu.SMEM((2 * pairs + LIST_PAD,), jnp.int32),
            pltpu.SMEM((2 * pairs + LIST_PAD,), F32),
            pltpu.SMEM((pairs,), jnp.int32),
            pltpu.SMEM((pairs,), F32),
            pltpu.SMEM((1,), jnp.int32),
            pltpu.SemaphoreType.DMA((6,)),
        ],
    )
    return pl.pallas_call(
        functools.partial(_moe_kernel, tile=tile),
        grid_spec=grid_spec,
        out_shape=jax.ShapeDtypeStruct((n_tiles * tile, d), F32),
        compiler_params=pltpu.CompilerParams(
            dimension_semantics=("arbitrary", "arbitrary"), vmem_limit_bytes=MOE_VMEM_LIMIT),
        name="moe",
    )(cnt, offs, w1g, w1l, b1g, b1l, w2, b2, g2, bb2, pos_flat, gate_flat, h1t)


def kernel(x, ln_in_g, ln_in_b, w_in, lambda_q1, lambda_k1, lambda_q2, lambda_k2, subln_g, rel_bias,
           w_out, ln1_g, ln1_b, router_w, router_b, w1, b1, w2, b2, ln2_g, ln2_b):
    bsz, seq, d = x.shape
    t = bsz * seq
    x2 = x.reshape(t, d)
    row = lambda v: v.reshape(1, -1).astype(F32)

    qs = A_HEAD_DIM ** -0.5 * LOG2E
    col_scale = jnp.concatenate([
        jnp.full((A_WIDTH,), qs, F32), jnp.ones((2 * A_WIDTH,), F32),
        jnp.full((B_WIDTH,), B_HEAD_DIM ** -0.5 * LOG2E, F32), jnp.ones((2 * B_WIDTH,), F32)]).reshape(1, -1)
    proj3, vt3, vtb3 = _ln_qkv(x2, row(ln_in_g), row(ln_in_b), w_in[0].astype(BF16), col_scale, seq)

    lam4 = jnp.stack([lambda_q1[0], lambda_k1[0], lambda_q2[0], lambda_k2[0]]).astype(F32)
    o_a = _diff_attn(proj3, vt3, lam4, subln_g[0].astype(F32))
    o_b = _band_attn(proj3, vtb3, rel_bias[0])

    w_o = w_out[0].astype(BF16)
    tile = min(MOE_TILE, t)
    n_tiles = t // tile
    h1t, top_idx, gates, rank, run_cnt = _out_router(
        x2, row(ln_in_g), row(ln_in_b), o_a, o_b,
        w_o[:A_WIDTH], w_o[A_WIDTH:], row(ln1_g[0]), row(ln1_b[0]),
        router_w[0].T.astype(F32), router_b[0].reshape(-1, 1).astype(F32), tile)

    steps = tile // ROW_TILE
    cnt = run_cnt[steps - 1::steps, :, 0]
    offs = jnp.cumsum(cnt, axis=1) - cnt
    hot = top_idx.reshape(TOP_K, n_tiles, tile, 1) == jnp.arange(N_EXPERTS, dtype=jnp.int32)
    pos = rank + jnp.sum(jnp.where(hot, offs[None, :, None, :], 0), axis=-1).reshape(TOP_K, t)
    per_tile = lambda a: a.reshape(TOP_K, n_tiles, tile).transpose(1, 0, 2).reshape(-1)
    w1g, w1l = _split_w1(w1[0])
    b1e = b1[0].astype(F32)[:, None, :]
    out = _moe(h1t, cnt.reshape(-1), offs.reshape(-1), per_tile(pos), per_tile(gates),
               w1g, w1l, b1e[:, :, 0::2], b1e[:, :, 1::2],
               w2[0].astype(BF16), b2[0][:, None, :].astype(F32), row(ln2_g[0]), row(ln2_b[0]), tile)
    return out.reshape(bsz, seq, d)
```

```python
import functools
import math

import jax
import jax.numpy as jnp
from jax import lax
from jax.experimental import pallas as pl
from jax.experimental.pallas import tpu as pltpu

F32 = jnp.float32
BF16 = jnp.bfloat16

CHUNK = 64
A_HEADS = 4
A_HEAD_DIM = 64
A_WIDTH = A_HEADS * 2 * A_HEAD_DIM
B_HEADS = 8
B_HEAD_DIM = 64
B_WIDTH = B_HEADS * B_HEAD_DIM
B_PAST_CHUNKS = 8
REL_CLIP = 256
N_EXPERTS = 32
TOP_K = 4
SWIGLU_ALPHA = 1.702
SWIGLU_LIMIT = 7.0
MOE_BLOCK = 512
LN_EPS = 1e-5
RMS_EPS = 1e-5
DEPTH = 1
DEEPNORM_ALPHA = (2 * DEPTH) ** 0.25
LAM_INIT = 0.8 - 0.6 * math.exp(-0.3 * 0)

LOG2E = 1.4426950408889634
NEG = -1e30
LANES = 128
SUBLANES = 8
ROW_TILE = 512
QKV_ROWS = 1024
VMEM_LIMIT = 48 * 1024 * 1024
DIFF_TILE = 256
BAND_TILE = 128
BAND_TILES = (BAND_TILE + B_PAST_CHUNKS * CHUNK) // BAND_TILE
BAND_PAIR = 2
BAND_STEP_ROWS = 4096
MOE_TILE = 4096
MOE_VMEM_LIMIT = 60 * 1024 * 1024
SMALL_BLOCK = 128
FF_CHUNK = 1024
SORT_UNROLL = 8
SORT_CHUNK = 512
ROW_UNROLL = 32
LIST_PAD = 128


def _layer_norm(x, g, b):
    mu = jnp.mean(x, axis=-1, keepdims=True)
    xc = x - mu
    var = jnp.mean(xc * xc, axis=-1, keepdims=True)
    return xc * lax.rsqrt(var + LN_EPS) * g + b


def _ln_qkv_kernel(x_ref, g_ref, b_ref, w_ref, cs_ref, o_ref, vta_ref, vtb_ref):
    h = _layer_norm(x_ref[...], g_ref[...], b_ref[...])
    hb = h.astype(BF16)
    n_out = w_ref.shape[1]
    rows = x_ref.shape[0]
    for c in range(n_out // A_WIDTH):
        sl = slice(c * A_WIDTH, (c + 1) * A_WIDTH)
        val = jnp.dot(hb, w_ref[:, sl], preferred_element_type=F32) * cs_ref[:, sl]
        o_ref[0, :, sl] = val.astype(BF16)
        for start, vt_ref in ((2 * A_WIDTH, vta_ref), (3 * A_WIDTH + 2 * B_WIDTH, vtb_ref)):
            if sl.start == start:
                tile = vt_ref.shape[2]
                for kt in range(rows // tile):
                    vt_ref[kt] = val[kt * tile:(kt + 1) * tile, :].T.astype(BF16)


def _ln_qkv(x2, g, b, w_bf, col_scale, seq):
    assert A_WIDTH == B_WIDTH
    t, d = x2.shape
    n_out = w_bf.shape[1]
    rows = min(QKV_ROWS, seq)
    steps_per_seq = seq // rows
    vt_spec = lambda tile: pl.BlockSpec((rows // tile, A_WIDTH, tile), lambda i: (i, 0, 0))
    vt_shape = lambda tile: jax.ShapeDtypeStruct((t // tile, A_WIDTH, tile), BF16)
    return pl.pallas_call(
        _ln_qkv_kernel,
        grid=(t // rows,),
        in_specs=[
            pl.BlockSpec((rows, d), lambda i: (i, 0)),
            pl.BlockSpec((1, d), lambda i: (0, 0)),
            pl.BlockSpec((1, d), lambda i: (0, 0)),
            pl.BlockSpec((d, n_out), lambda i: (0, 0)),
            pl.BlockSpec((1, n_out), lambda i: (0, 0)),
        ],
        out_specs=[pl.BlockSpec((1, rows, n_out),
                                lambda i: (i // steps_per_seq, i % steps_per_seq, 0)),
                   vt_spec(DIFF_TILE), vt_spec(BAND_TILE)],
        out_shape=[jax.ShapeDtypeStruct((t // seq, seq, n_out), BF16),
                   vt_shape(DIFF_TILE), vt_shape(BAND_TILE)],
        compiler_params=pltpu.CompilerParams(
            dimension_semantics=("arbitrary",), vmem_limit_bytes=VMEM_LIMIT),
        name="ln_qkv",
    )(x2, g, b, w_bf, col_scale)


def _diff_attn_kernel(lam_ref, g_ref, kf_ref, qf_ref, bdiag_ref, q_ref, k_ref, vt_ref, o_ref, s_scr):
    tq = q_ref.shape[1]
    tk = vt_ref.shape[2]
    i = pl.program_id(2)
    q = q_ref[0]
    lane = lax.broadcasted_iota(jnp.int32, q.shape, 1)
    zero = jnp.zeros_like(q)
    qq = jnp.concatenate([jnp.where(lane < A_HEAD_DIM, q, zero),
                          jnp.where(lane >= A_HEAD_DIM, q, zero)], axis=0)
    qqt = jnp.concatenate([qq.astype(F32).T.astype(BF16), qf_ref[0]], axis=0)
    n_before = i * (tq // tk)

    def scores(j):
        rows = pl.ds(pl.multiple_of(j * tk, tk), tk)
        kb = jnp.concatenate([k_ref[0, rows, :], kf_ref[rows, :]], axis=1)
        return jnp.dot(kb, qqt, preferred_element_type=F32)

    def update(slot, table_ref, j, carry):
        m, l, acc = carry
        s = s_scr[slot]
        if table_ref is not None:
            s = s + table_ref[0]
        m_new = jnp.maximum(m, jnp.max(s, axis=0, keepdims=True))
        alpha = jnp.exp2(m - m_new)
        p = jnp.exp2(s - m_new)
        l = alpha * l + jnp.sum(p, axis=0, keepdims=True)
        acc = alpha * acc + jnp.dot(vt_ref[j], p.astype(BF16), preferred_element_type=F32)
        return m_new, l, acc

    s_scr[0] = scores(0)

    def pair(t, carry):
        j = 2 * t
        s_scr[1] = scores(j + 1)
        carry = update(0, None, j, carry)
        s_scr[0] = scores(j + 2)
        return update(1, None, j + 1, carry)

    def quad(t, carry):
        return pair(2 * t + 1, pair(2 * t, carry))

    init = (jnp.full((1, 2 * tq), NEG, F32), jnp.zeros((1, 2 * tq), F32),
            jnp.zeros((LANES, 2 * tq), F32))
    carry = lax.fori_loop(0, n_before // 4, quad, init)
    carry = lax.fori_loop(n_before // 4 * 2, n_before // 2, pair, carry)

    def odd_tail(carry):
        s_scr[1] = scores(n_before)
        carry = update(0, None, n_before - 1, carry)
        return update(1, bdiag_ref, n_before, carry)

    def even_tail(carry):
        return update(0, bdiag_ref, n_before, carry)

    _, l, acc = lax.cond(n_before % 2 == 1, odd_tail, even_tail, carry)

    o_all = acc / l
    lv = lam_ref[...]
    lam = (jnp.exp(jnp.sum(lv[0:1] * lv[1:2], axis=1, keepdims=True))
           - jnp.exp(jnp.sum(lv[2:3] * lv[3:4], axis=1, keepdims=True)) + LAM_INIT)
    o = o_all[:, :tq] - lam * o_all[:, tq:]
    ms = jnp.mean(o * o, axis=0, keepdims=True)
    o = o * lax.rsqrt(ms + RMS_EPS) * (g_ref[...] * (1.0 - LAM_INIT))
    o_ref[0] = o.T.astype(BF16)


def _diff_attn(proj3, vt3, lam4, subln_g):
    bsz, seq, _ = proj3.shape
    tq = tk = DIFF_TILE
    assert seq <= CHUNK * CHUNK
    c = jnp.asarray([2.0 ** (-8.0 * (h + 1) / A_HEADS) for h in range(A_HEADS)], F32) * LOG2E
    r = jnp.arange(tk, dtype=jnp.int32)[:, None]
    qrel = jnp.arange(2 * tq, dtype=jnp.int32)[None, :] % tq
    allowed = (r // CHUNK) <= (qrel // CHUNK)
    kpos = jnp.arange(seq, dtype=jnp.int32)
    digits = jnp.stack([kpos // CHUNK] * 3 + [kpos % CHUNK] * 3, axis=1).astype(BF16)
    kfeat = jnp.pad(digits, ((0, 0), (0, LANES - digits.shape[1])))

    def pieces(v):
        hi = v.astype(BF16)
        mid = (v - hi.astype(F32)).astype(BF16)
        lo = (v - hi.astype(F32) - mid.astype(F32)).astype(BF16)
        return [hi, mid, lo]
    rows = jnp.stack(pieces(c * CHUNK) + pieces(c), axis=1)
    qfeat = jnp.broadcast_to(jnp.pad(rows, ((0, 0), (0, LANES - rows.shape[1])))[:, :, None],
                             (A_HEADS, LANES, 2 * tq))
    bdiag = jnp.where(allowed, c[:, None, None] * (qrel - jnp.abs(qrel - r) - r).astype(F32), NEG)
    kblk = A_WIDTH // LANES
    return pl.pallas_call(
        _diff_attn_kernel,
        grid=(bsz, A_HEADS, seq // tq),
        in_specs=[
            pl.BlockSpec((4, A_HEAD_DIM), lambda b, h, i: (0, 0)),
            pl.BlockSpec((LANES, 1), lambda b, h, i: (0, 0)),
            pl.BlockSpec((seq, LANES), lambda b, h, i: (0, 0)),
            pl.BlockSpec((1, LANES, 2 * tq), lambda b, h, i: (h, 0, 0)),
            pl.BlockSpec((1, tk, 2 * tq), lambda b, h, i: (h, 0, 0)),
            pl.BlockSpec((1, tq, LANES), lambda b, h, i: (b, i, h)),
            pl.BlockSpec((1, seq, LANES), lambda b, h, i: (b, 0, kblk + h)),
            pl.BlockSpec((seq // tk, LANES, tk), lambda b, h, i: (b, h, 0)),
        ],
        out_specs=pl.BlockSpec((1, tq, LANES), lambda b, h, i: (b, i, h)),
        out_shape=jax.ShapeDtypeStruct((bsz, seq, A_WIDTH), BF16),
        scratch_shapes=[pltpu.VMEM((2, tk, 2 * tq), F32)],
        compiler_params=pltpu.CompilerParams(
            dimension_semantics=("arbitrary", "arbitrary", "arbitrary"),
            vmem_limit_bytes=VMEM_LIMIT),
        name="diff_attn",
    )(lam4, subln_g.reshape(LANES, 1), kfeat, qfeat, bdiag, proj3, proj3, vt3)


def _band_attn_kernel(bias_ref, q_ref, k_ref, vt_ref, o_ref, s_scr):
    i = pl.program_id(2)
    tq = BAND_TILE
    units = q_ref.shape[1] // (BAND_PAIR * tq)
    key_tiles = BAND_TILES + BAND_PAIR - 1
    row = lax.broadcasted_iota(jnp.int32, (LANES, tq), 0)

    def key_tile(n, t):
        jt = (i * units + n) * BAND_PAIR - (BAND_TILES - 1) + t
        return jnp.maximum(jt, 0), jt < 0

    def scores(n):
        qs = []
        for a in range(BAND_PAIR):
            q = q_ref[0, (n * BAND_PAIR + a) * tq:(n * BAND_PAIR + a + 1) * tq, :]
            lane = lax.broadcasted_iota(jnp.int32, q.shape, 1)
            zero = jnp.zeros_like(q)
            qs += [jnp.where(lane < B_HEAD_DIM, q, zero), jnp.where(lane >= B_HEAD_DIM, q, zero)]
        qqt = jnp.concatenate(qs, axis=0).astype(F32).T.astype(BF16)
        ks = [k_ref[0, pl.ds(pl.multiple_of(key_tile(n, t)[0] * tq, tq), tq), :] for t in range(key_tiles)]
        return jnp.dot(jnp.concatenate(ks, axis=0), qqt, preferred_element_type=F32)

    def finish(n):
        bias = [bias_ref[0, jnp.where(key_tile(n, t)[1], 1, 0), t] for t in range(key_tiles)]
        s = s_scr[n % 2] + jnp.concatenate(bias, axis=0)
        m = jnp.max(s, axis=0, keepdims=True)
        p = jnp.exp2(s - m)
        l = jnp.sum(p, axis=0, keepdims=True)
        vt = jnp.concatenate([vt_ref[key_tile(n, t)[0]] for t in range(key_tiles)], axis=1)
        o = jnp.dot(vt, p.astype(BF16), preferred_element_type=F32) / l
        for a in range(BAND_PAIR):
            oa = jnp.where(row < B_HEAD_DIM, o[:, 2 * a * tq:(2 * a + 1) * tq],
                           o[:, (2 * a + 1) * tq:(2 * a + 2) * tq])
            o_ref[0, (n * BAND_PAIR + a) * tq:(n * BAND_PAIR + a + 1) * tq, :] = oa.T.astype(BF16)

    s_scr[0] = scores(0)
    for n in range(units):
        if n + 1 < units:
            s_scr[(n + 1) % 2] = scores(n + 1)
        finish(n)


def _band_bias(rel_bias, tq):
    past = B_PAST_CHUNKS * CHUNK
    band = tq + past
    assert tq - 1 <= REL_CLIP <= past
    qi = jnp.arange(tq)
    kj = jnp.arange(band)
    cq = qi[:, None] // CHUNK
    ck = kj[None, :] // CHUNK
    allowed = (ck >= cq) & (ck <= cq + B_PAST_CHUNKS)
    tab = rel_bias.astype(F32) * LOG2E
    n_diag = band + tq - 1
    n_unclipped = REL_CLIP + tq
    w = jnp.concatenate([tab[:, REL_CLIP - (tq - 1):],
                         jnp.broadcast_to(tab[:, -1:], (B_HEADS, n_diag - n_unclipped))], axis=1)
    shifted = jnp.tile(w, (1, tq + 1))[:, :tq * (n_diag + 1)].reshape(B_HEADS, tq, n_diag + 1)
    bias = jnp.flip(shifted[:, :, :band], axis=2)
    return jnp.where(allowed[None], bias, NEG)


def _band_attn(proj3, vtb3, rel_bias):
    bsz, seq, _ = proj3.shape
    tq = BAND_TILE
    groups = B_HEADS // 2
    key_tiles = BAND_TILES + BAND_PAIR - 1
    cols = BAND_PAIR * 2 * tq
    bias = _band_bias(rel_bias, tq)
    bias = bias.reshape(groups, 2, tq, BAND_TILES, tq)
    masked = jnp.full((groups, 2, tq, 1, tq), NEG, F32)
    per_tile = [jnp.concatenate([masked] * a + [bias] + [masked] * (BAND_PAIR - 1 - a), axis=3)
                for a in range(BAND_PAIR)]
    bias = jnp.stack(per_tile, axis=1)
    bias = bias.transpose(0, 4, 5, 1, 2, 3).reshape(groups, 1, key_tiles, tq, cols)
    bias = jnp.concatenate([bias, jnp.full_like(bias, NEG)], axis=1)
    qblk = 3 * A_WIDTH // LANES
    kblk = qblk + B_WIDTH // LANES
    rows = min(BAND_STEP_ROWS, seq)
    return pl.pallas_call(
        _band_attn_kernel,
        grid=(bsz, groups, seq // rows),
        in_specs=[
            pl.BlockSpec((1, 2, key_tiles, tq, cols), lambda b, g, i: (g, 0, 0, 0, 0)),
            pl.BlockSpec((1, rows, LANES), lambda b, g, i: (b, i, qblk + g)),
            pl.BlockSpec((1, seq, LANES), lambda b, g, i: (b, 0, kblk + g)),
            pl.BlockSpec((seq // tq, LANES, tq), lambda b, g, i: (b, g, 0)),
        ],
        out_specs=pl.BlockSpec((1, rows, LANES), lambda b, g, i: (b, i, g)),
        out_shape=jax.ShapeDtypeStruct((bsz, seq, B_WIDTH), BF16),
        scratch_shapes=[pltpu.VMEM((2, key_tiles * tq, cols), F32)],
        compiler_params=pltpu.CompilerParams(
            dimension_semantics=("arbitrary", "arbitrary", "arbitrary"),
            vmem_limit_bytes=VMEM_LIMIT),
        name="band_attn",
    )(bias, proj3, proj3, vtb3)


def _out_router_kernel(x_ref, gi_ref, bi_ref, oa_ref, ob_ref, wa_ref, wb_ref, g1_ref, b1_ref,
                       rwt_ref, rb_ref, tri_ref, h1_ref, idx_ref, gate_ref, rank_ref, cnt_ref,
                       carry_ref, *, steps_per_tile):
    h = _layer_norm(x_ref[...], gi_ref[...], bi_ref[...])
    mix = (jnp.dot(oa_ref[0], wa_ref[...], preferred_element_type=F32)
           + jnp.dot(ob_ref[0], wb_ref[...], preferred_element_type=F32))
    h1 = _layer_norm(DEEPNORM_ALPHA * h + mix, g1_ref[...], b1_ref[...])
    nchunk = h1.shape[1] // LANES
    for c in range(nchunk):
        h1_ref[pl.ds(c, ROW_TILE, stride=nchunk), :] = h1[:, c * LANES:(c + 1) * LANES]
    lt = lax.dot_general(rwt_ref[...], h1, (((1,), (1,)), ((), ())),
                         precision=lax.Precision.HIGHEST, preferred_element_type=F32)
    lt = lt + rb_ref[...]
    eidx = lax.broadcasted_iota(jnp.int32, lt.shape, 0)
    vals, idxs, hots = [], [], []
    for _ in range(TOP_K):
        mx = jnp.max(lt, axis=0, keepdims=True)
        am = jnp.min(jnp.where(lt == mx, eidx, N_EXPERTS), axis=0, keepdims=True)
        hit = eidx == am
        vals.append(mx)
        idxs.append(am)
        hots.append(jnp.where(hit, 1.0, 0.0))
        lt = jnp.where(hit, -jnp.inf, lt)
    ex = [jnp.exp(v - vals[0]) for v in vals]
    den = ex[0] + ex[1] + ex[2] + ex[3]
    idx_ref[...] = jnp.concatenate(idxs, axis=0)
    gate_ref[...] = jnp.concatenate([e / den for e in ex], axis=0)

    @pl.when(pl.program_id(0) % steps_per_tile == 0)
    def _():
        carry_ref[...] = jnp.zeros_like(carry_ref)

    hot = (hots[0] + hots[1]) + (hots[2] + hots[3])
    before = jnp.dot(hot.astype(BF16), tri_ref[...], preferred_element_type=F32) + carry_ref[...]
    rank_ref[...] = jnp.concatenate(
        [jnp.sum(hk * before, axis=0, keepdims=True) for hk in hots], axis=0).astype(jnp.int32)
    total = carry_ref[...] + jnp.sum(hot, axis=1, keepdims=True)
    carry_ref[...] = total
    cnt_ref[0] = total.astype(jnp.int32)


def _out_router(x2, gi, bi, o_a, o_b, wa, wb, g1, b1, rwt, rb, moe_tile):
    t, d = x2.shape
    nchunk = d // LANES
    row = lambda i: (i, 0)
    fixed = lambda i: (0, 0)
    steps_per_seq = o_a.shape[1] // ROW_TILE
    seq_row = lambda i: (i // steps_per_seq, i % steps_per_seq, 0)
    tri = jnp.triu(jnp.ones((ROW_TILE, ROW_TILE), BF16), k=1)
    return pl.pallas_call(
        functools.partial(_out_router_kernel, steps_per_tile=moe_tile // ROW_TILE),
        grid=(t // ROW_TILE,),
        in_specs=[
            pl.BlockSpec((ROW_TILE, d), row),
            pl.BlockSpec((1, d), fixed),
            pl.BlockSpec((1, d), fixed),
            pl.BlockSpec((1, ROW_TILE, A_WIDTH), seq_row),
            pl.BlockSpec((1, ROW_TILE, B_WIDTH), seq_row),
            pl.BlockSpec((A_WIDTH, d), fixed),
            pl.BlockSpec((B_WIDTH, d), fixed),
            pl.BlockSpec((1, d), fixed),
            pl.BlockSpec((1, d), fixed),
            pl.BlockSpec((N_EXPERTS, d), fixed),
            pl.BlockSpec((N_EXPERTS, 1), fixed),
            pl.BlockSpec((ROW_TILE, ROW_TILE), fixed),
        ],
        out_specs=[
            pl.BlockSpec((ROW_TILE * nchunk, LANES), row),
            pl.BlockSpec((TOP_K, ROW_TILE), lambda i: (0, i)),
            pl.BlockSpec((TOP_K, ROW_TILE), lambda i: (0, i)),
            pl.BlockSpec((TOP_K, ROW_TILE), lambda i: (0, i)),
            pl.BlockSpec((1, N_EXPERTS, 1), lambda i: (i, 0, 0)),
        ],
        out_shape=[
            jax.ShapeDtypeStruct((t * nchunk, LANES), F32),
            jax.ShapeDtypeStruct((TOP_K, t), jnp.int32),
            jax.ShapeDtypeStruct((TOP_K, t), F32),
            jax.ShapeDtypeStruct((TOP_K, t), jnp.int32),
            jax.ShapeDtypeStruct((t // ROW_TILE, N_EXPERTS, 1), jnp.int32),
        ],
        scratch_shapes=[pltpu.VMEM((N_EXPERTS, 1), F32)],
        compiler_params=pltpu.CompilerParams(
            dimension_semantics=("arbitrary",), vmem_limit_bytes=VMEM_LIMIT),
        name="out_router",
    )(x2, gi, bi, o_a, o_b, wa, wb, g1, b1, rwt, rb, tri)


def _split_w1_kernel(w_ref, perm_ref, g_ref, l_ref):
    rows = w_ref.shape[1]
    even = (lax.broadcasted_iota(jnp.int32, (rows, LANES), 1) & 1) == 0
    perm = perm_ref[...]
    for g in range(w_ref.shape[2] // (2 * LANES)):
        v0 = w_ref[0, :, 2 * g * LANES:(2 * g + 1) * LANES]
        v1 = w_ref[0, :, (2 * g + 1) * LANES:(2 * g + 2) * LANES]
        glu = jnp.where(even, v0, pltpu.roll(v1, 1, axis=1)).astype(BF16)
        lin = jnp.where(even, pltpu.roll(v0, LANES - 1, axis=1), v1).astype(BF16)
        g_ref[0, :, g * LANES:(g + 1) * LANES] = jnp.dot(
            glu, perm, preferred_element_type=F32).astype(BF16)
        l_ref[0, :, g * LANES:(g + 1) * LANES] = jnp.dot(
            lin, perm, preferred_element_type=F32).astype(BF16)


def _split_w1(w1e):
    e, d, f2 = w1e.shape
    rows = 1024
    half = LANES // 2
    unit = jnp.arange(LANES)
    perm = (jnp.arange(LANES)[:, None] == (2 * (unit % half) + unit // half)[None, :]).astype(BF16)
    return pl.pallas_call(
        _split_w1_kernel,
        grid=(e, d // rows),
        in_specs=[pl.BlockSpec((1, rows, f2), lambda i, j: (i, j, 0)),
                  pl.BlockSpec((LANES, LANES), lambda i, j: (0, 0))],
        out_specs=[pl.BlockSpec((1, rows, f2 // 2), lambda i, j: (i, j, 0)),
                   pl.BlockSpec((1, rows, f2 // 2), lambda i, j: (i, j, 0))],
        out_shape=[jax.ShapeDtypeStruct((e, d, f2 // 2), BF16),
                   jax.ShapeDtypeStruct((e, d, f2 // 2), BF16)],
        compiler_params=pltpu.CompilerParams(
            dimension_semantics=("arbitrary", "arbitrary"), vmem_limit_bytes=VMEM_LIMIT),
        name="split_w1",
    )(w1e, perm)


def _moe_kernel(cnt_ref, offs_ref,
                w1g_ref, w1l_ref, b1g_ref, b1l_ref, w2_ref, b2_ref, g2_ref, bb2_ref,
                pos_hbm, gate_hbm, h_hbm,
                out_hbm, hbuf, ybuf, stage, out_buf, list_tok, list_gate, pos_in, gate_in, n_sorted, sem,
                *, tile):
    b = pl.program_id(0)
    e = pl.program_id(1)
    n_tiles = pl.num_programs(0)
    n_exp = pl.num_programs(1)
    d_model = w1g_ref.shape[1]
    d_ff = w1g_ref.shape[2]
    nchunk = d_model // LANES
    tile_rows = tile * nchunk
    pairs = tile * TOP_K
    n_sort_chunks = pairs // SORT_CHUNK
    half = (b % 2) * pairs
    other_half = pairs - half

    def stage_rows(r):
        return pl.ds(pl.multiple_of(r * nchunk, nchunk), nchunk)

    def listed_rows(p):
        return pl.ds(pl.multiple_of(list_tok[p], nchunk), nchunk)

    def sort_pair(j, first_row, dst_half):
        p = pos_in[j] + dst_half
        list_tok[p] = first_row
        list_gate[p] = gate_in[j]

    def sort_chunks(lo, hi, dst_half):
        def body(jo, c):
            for u in range(SORT_UNROLL):
                j = jo * SORT_UNROLL + u
                sort_pair(j, (j & (tile - 1)) * nchunk, dst_half)
            return c
        lax.fori_loop(lo * (SORT_CHUNK // SORT_UNROLL), hi * (SORT_CHUNK // SORT_UNROLL), body, 0)

    def load_pairs(t):
        rows = pl.ds(pl.multiple_of(t * pairs, pairs), pairs)
        copies = [pltpu.make_async_copy(pos_hbm.at[rows], pos_in, sem.at[2]),
                  pltpu.make_async_copy(gate_hbm.at[rows], gate_in, sem.at[3])]
        for cp in copies:
            cp.start()
        for cp in copies:
            cp.wait()

    @pl.when(jnp.logical_and(b == 0, e == 0))
    def _():
        stage[...] = jnp.zeros_like(stage)

        def pad_body(j, c):
            list_tok[pairs + j] = 0
            list_tok[2 * pairs + j] = 0
            return c
        lax.fori_loop(0, LIST_PAD, pad_body, 0)
        load_pairs(0)
        n_sorted[0] = 0

    @pl.when(e == 0)
    def _():
        src = h_hbm.at[pl.ds(pl.multiple_of(b * tile_rows, tile_rows), tile_rows), :]
        load = pltpu.make_async_copy(src, hbuf, sem.at[0])
        load.start()
        ybuf[...] = jnp.zeros_like(ybuf)
        sort_chunks(jnp.minimum(n_sorted[0], n_sort_chunks), n_sort_chunks, half)
        n_sorted[0] = 0

        @pl.when(b + 1 < n_tiles)
        def _():
            load_pairs(b + 1)
        load.wait()

    n = cnt_ref[b * n_exp + e]
    base = offs_ref[b * n_exp + e] + half

    def run_block(m_rows, p0, nrows):
        def gather_body(ci, c):
            for u in range(ROW_UNROLL):
                r = ci * ROW_UNROLL + u
                stage[stage_rows(r), :] = hbuf[listed_rows(p0 + r), :]
            return c
        lax.fori_loop(0, (nrows + ROW_UNROLL - 1) // ROW_UNROLL, gather_body, 0)

        if m_rows == MOE_BLOCK:
            chunk = jnp.minimum(n_sorted[0], n_sort_chunks - 1)
            n_sorted[0] = n_sorted[0] + 1
            j0 = chunk * SORT_CHUNK
            row0 = (j0 & (tile - 1)) * nchunk
            for u in range(SORT_CHUNK):
                sort_pair(j0 + u, row0 + u * nchunk, other_half)

        x = jnp.concatenate(
            [stage[pl.ds(c, m_rows, stride=nchunk), :] for c in range(nchunk)],
            axis=1).astype(BF16)
        y = b2_ref[0]
        for hh in range(d_ff // FF_CHUNK):
            sl = slice(hh * FF_CHUNK, (hh + 1) * FF_CHUNK)
            hg = jnp.dot(x, w1g_ref[0, :, sl], preferred_element_type=F32) + b1g_ref[0, :, sl]
            hl = jnp.dot(x, w1l_ref[0, :, sl], preferred_element_type=F32) + b1l_ref[0, :, sl]
            xg = jnp.minimum(hg, SWIGLU_LIMIT)
            xl = jnp.clip(hl, -SWIGLU_LIMIT, SWIGLU_LIMIT)
            act = xg * jax.nn.sigmoid(SWIGLU_ALPHA * xg) * (xl + 1.0)
            y = y + jnp.dot(act.astype(BF16), w2_ref[0, sl, :], preferred_element_type=F32)
        for c in range(nchunk):
            stage[pl.ds(c, m_rows, stride=nchunk), :] = y[:, c * LANES:(c + 1) * LANES]

        def add_body(ci, c):
            rows, vals = [], []
            for u in range(SUBLANES):
                r = ci * SUBLANES + u
                dst = listed_rows(p0 + r)
                rows.append(dst)
                vals.append(ybuf[dst, :] + list_gate[p0 + r] * stage[stage_rows(r), :])
            for dst, val in zip(rows, vals):
                ybuf[dst, :] = val
            return c
        n_full = nrows // SUBLANES
        lax.fori_loop(0, n_full, add_body, 0)

        def add_tail(r, c):
            dst = listed_rows(p0 + r)
            ybuf[dst, :] = ybuf[dst, :] + list_gate[p0 + r] * stage[stage_rows(r), :]
            return c
        lax.fori_loop(n_full * SUBLANES, nrows, add_tail, 0)

    n_big = n // MOE_BLOCK
    rem = n - n_big * MOE_BLOCK
    rem_is_big = rem > MOE_BLOCK - SMALL_BLOCK
    n_big_blocks = n_big + rem_is_big.astype(jnp.int32)
    n_small_blocks = jnp.where(rem_is_big, 0, (rem + SMALL_BLOCK - 1) // SMALL_BLOCK)

    def big_body(s, c):
        run_block(MOE_BLOCK, base + s * MOE_BLOCK, jnp.minimum(MOE_BLOCK, n - s * MOE_BLOCK))
        return c
    lax.fori_loop(0, n_big_blocks, big_body, 0)

    def small_body(s, c):
        run_block(SMALL_BLOCK, base + n_big * MOE_BLOCK + s * SMALL_BLOCK,
                  jnp.minimum(SMALL_BLOCK, rem - s * SMALL_BLOCK))
        return c
    lax.fori_loop(0, n_small_blocks, small_body, 0)

    @pl.when(e == n_exp - 1)
    def _():
        def out_copy(c, slot):
            rows = pl.ds(pl.multiple_of(b * tile + c * ROW_TILE, ROW_TILE), ROW_TILE)
            return pltpu.make_async_copy(out_buf.at[slot], out_hbm.at[rows, :], sem.at[4 + slot])

        def chunk_body(c, carry):
            slot = c % 2

            @pl.when(c >= 2)
            def _():
                out_copy(c - 2, slot).wait()
            first = c * (ROW_TILE * nchunk)
            pieces = [DEEPNORM_ALPHA * hbuf[pl.ds(first + k, ROW_TILE, stride=nchunk), :]
                      + ybuf[pl.ds(first + k, ROW_TILE, stride=nchunk), :] for k in range(nchunk)]
            out_buf[slot] = _layer_norm(jnp.concatenate(pieces, axis=1), g2_ref[...], bb2_ref[...])
            out_copy(c, slot).start()
            return carry
        n_out = tile // ROW_TILE
        lax.fori_loop(0, n_out, chunk_body, 0)
        for c in range(max(n_out - 2, 0), n_out):
            out_copy(c, c % 2).wait()


def _moe(h1t, cnt, offs, pos_flat, gate_flat, w1g, w1l, b1g, b1l, w2, b2, g2, bb2, tile):
    n_exp, d, f = w1g.shape
    nchunk = d // LANES
    n_tiles = h1t.shape[0] // (tile * nchunk)
    pairs = tile * TOP_K
    assert tile % SORT_CHUNK == 0
    expert = lambda b, e, cnt, offs: (e, 0, 0)
    grid_spec = pltpu.PrefetchScalarGridSpec(
        num_scalar_prefetch=2,
        grid=(n_tiles, n_exp),
        in_specs=[
            pl.BlockSpec((1, d, f), expert),
            pl.BlockSpec((1, d, f), expert),
            pl.BlockSpec((1, 1, f), expert),
            pl.BlockSpec((1, 1, f), expert),
            pl.BlockSpec((1, f, d), expert),
            pl.BlockSpec((1, 1, d), expert),
            pl.BlockSpec((1, d), lambda b, e, cnt, offs: (0, 0)),
            pl.BlockSpec((1, d), lambda b, e, cnt, offs: (0, 0)),
            pl.BlockSpec(memory_space=pl.ANY),
            pl.BlockSpec(memory_space=pl.ANY),
            pl.BlockSpec(memory_space=pl.ANY),
        ],
        out_specs=pl.BlockSpec(memory_space=pl.ANY),
        scratch_shapes=[
            pltpu.VMEM((tile * nchunk, LANES), F32),
            pltpu.VMEM((tile * nchunk, LANES), F32),
            pltpu.VMEM((MOE_BLOCK * nchunk, LANES), F32),
            pltpu.VMEM((2, ROW_TILE, d), F32),
            pltpu.SMEM((2 * pairs + LIST_PAD,), jnp.int32),
            pltpu.SMEM((2 * pairs + LIST_PAD,), F32),
            pltpu.SMEM((pairs,), jnp.int32),
            pltpu.SMEM((pairs,), F32),
            pltpu.SMEM((1,), jnp.int32),
            pltpu.SemaphoreType.DMA((6,)),
        ],
    )
    return pl.pallas_call(
        functools.partial(_moe_kernel, tile=tile),
        grid_spec=grid_spec,
        out_shape=jax.ShapeDtypeStruct((n_tiles * tile, d), F32),
        compiler_params=pltpu.CompilerParams(
            dimension_semantics=("arbitrary", "arbitrary"), vmem_limit_bytes=MOE_VMEM_LIMIT),
        name="moe",
    )(cnt, offs, w1g, w1l, b1g, b1l, w2, b2, g2, bb2, pos_flat, gate_flat, h1t)


def kernel(x, ln_in_g, ln_in_b, w_in, lambda_q1, lambda_k1, lambda_q2, lambda_k2, subln_g, rel_bias,
           w_out, ln1_g, ln1_b, router_w, router_b, w1, b1, w2, b2, ln2_g, ln2_b):
    bsz, seq, d = x.shape
    t = bsz * seq
    x2 = x.reshape(t, d)
    row = lambda v: v.reshape(1, -1).astype(F32)

    qs = A_HEAD_DIM ** -0.5 * LOG2E
    col_scale = jnp.concatenate([
        jnp.full((A_WIDTH,), qs, F32), jnp.ones((2 * A_WIDTH,), F32),
        jnp.full((B_WIDTH,), B_HEAD_DIM ** -0.5 * LOG2E, F32), jnp.ones((2 * B_WIDTH,), F32)]).reshape(1, -1)
    proj3, vt3, vtb3 = _ln_qkv(x2, row(ln_in_g), row(ln_in_b), w_in[0].astype(BF16), col_scale, seq)

    lam4 = jnp.stack([lambda_q1[0], lambda_k1[0], lambda_q2[0], lambda_k2[0]]).astype(F32)
    o_a = _diff_attn(proj3, vt3, lam4, subln_g[0].astype(F32))
    o_b = _band_attn(proj3, vtb3, rel_bias[0])

    w_o = w_out[0].astype(BF16)
    tile = min(MOE_TILE, t)
    n_tiles = t // tile
    h1t, top_idx, gates, rank, run_cnt = _out_router(
        x2, row(ln_in_g), row(ln_in_b), o_a, o_b,
        w_o[:A_WIDTH], w_o[A_WIDTH:], row(ln1_g[0]), row(ln1_b[0]),
        router_w[0].T.astype(F32), router_b[0].reshape(-1, 1).astype(F32), tile)

    steps = tile // ROW_TILE
    cnt = run_cnt[steps - 1::steps, :, 0]
    offs = jnp.cumsum(cnt, axis=1) - cnt
    hot = top_idx.reshape(TOP_K, n_tiles, tile, 1) == jnp.arange(N_EXPERTS, dtype=jnp.int32)
    pos = rank + jnp.sum(jnp.where(hot, offs[None, :, None, :], 0), axis=-1).reshape(TOP_K, t)
    per_tile = lambda a: a.reshape(TOP_K, n_tiles, tile).transpose(1, 0, 2).reshape(-1)
    w1g, w1l = _split_w1(w1[0])
    b1e = b1[0].astype(F32)[:, None, :]
    out = _moe(h1t, cnt.reshape(-1), offs.reshape(-1), per_tile(pos), per_tile(gates),
               w1g, w1l, b1e[:, :, 0::2], b1e[:, :, 1::2],
               w2[0].astype(BF16), b2[0][:, None, :].astype(F32), row(ln2_g[0]), row(ln2_b[0]), tile)
    return out.reshape(bsz, seq, d)
```

```python
import functools
import math

import jax
import jax.numpy as jnp
from jax import lax
from jax.experimental import pallas as pl
from jax.experimental.pallas import tpu as pltpu

F32 = jnp.float32
BF16 = jnp.bfloat16

CHUNK = 64
A_HEADS = 4
A_HEAD_DIM = 64
A_WIDTH = A_HEADS * 2 * A_HEAD_DIM
B_HEADS = 8
B_HEAD_DIM = 64
B_WIDTH = B_HEADS * B_HEAD_DIM
B_PAST_CHUNKS = 8
REL_CLIP = 256
N_EXPERTS = 32
TOP_K = 4
SWIGLU_ALPHA = 1.702
SWIGLU_LIMIT = 7.0
MOE_BLOCK = 512
LN_EPS = 1e-5
RMS_EPS = 1e-5
DEPTH = 1
DEEPNORM_ALPHA = (2 * DEPTH) ** 0.25
LAM_INIT = 0.8 - 0.6 * math.exp(-0.3 * 0)

LOG2E = 1.4426950408889634
NEG = -1e30
LANES = 128
SUBLANES = 8
ROW_TILE = 512
VMEM_LIMIT = 48 * 1024 * 1024
DIFF_TILE = 256
BAND_TILE = 128
BAND_TILES = (BAND_TILE + B_PAST_CHUNKS * CHUNK) // BAND_TILE
BAND_PAIR = 2
BAND_STEP_ROWS = 4096
MOE_TILE = 4096
MOE_VMEM_LIMIT = 60 * 1024 * 1024
SMALL_BLOCK = 128
FF_CHUNK = 1024
SORT_UNROLL = 8
SORT_CHUNK = 512
ROW_UNROLL = 32
LIST_PAD = 128


def _layer_norm(x, g, b):
    mu = jnp.mean(x, axis=-1, keepdims=True)
    xc = x - mu
    var = jnp.mean(xc * xc, axis=-1, keepdims=True)
    return xc * lax.rsqrt(var + LN_EPS) * g + b


def _ln_qkv_kernel(x_ref, g_ref, b_ref, w_ref, cs_ref, o_ref, vta_ref, vtb_ref):
    h = _layer_norm(x_ref[...], g_ref[...], b_ref[...])
    hb = h.astype(BF16)
    n_out = w_ref.shape[1]
    for c in range(n_out // ROW_TILE):
        sl = slice(c * ROW_TILE, (c + 1) * ROW_TILE)
        val = jnp.dot(hb, w_ref[:, sl], preferred_element_type=F32) * cs_ref[:, sl]
        o_ref[0, :, sl] = val.astype(BF16)
        for start, vt_ref in ((2 * A_WIDTH, vta_ref), (3 * A_WIDTH + 2 * B_WIDTH, vtb_ref)):
            if sl.start == start:
                tile = vt_ref.shape[2]
                for kt in range(ROW_TILE // tile):
                    vt_ref[kt] = val[kt * tile:(kt + 1) * tile, :].T.astype(BF16)


def _ln_qkv(x2, g, b, w_bf, col_scale, seq):
    assert A_WIDTH == ROW_TILE and B_WIDTH == ROW_TILE
    t, d = x2.shape
    n_out = w_bf.shape[1]
    steps_per_seq = seq // ROW_TILE
    vt_spec = lambda tile: pl.BlockSpec((ROW_TILE // tile, ROW_TILE, tile), lambda i: (i, 0, 0))
    vt_shape = lambda tile: jax.ShapeDtypeStruct((t // tile, ROW_TILE, tile), BF16)
    return pl.pallas_call(
        _ln_qkv_kernel,
        grid=(t // ROW_TILE,),
        in_specs=[
            pl.BlockSpec((ROW_TILE, d), lambda i: (i, 0)),
            pl.BlockSpec((1, d), lambda i: (0, 0)),
            pl.BlockSpec((1, d), lambda i: (0, 0)),
            pl.BlockSpec((d, n_out), lambda i: (0, 0)),
            pl.BlockSpec((1, n_out), lambda i: (0, 0)),
        ],
        out_specs=[pl.BlockSpec((1, ROW_TILE, n_out),
                                lambda i: (i // steps_per_seq, i % steps_per_seq, 0)),
                   vt_spec(DIFF_TILE), vt_spec(BAND_TILE)],
        out_shape=[jax.ShapeDtypeStruct((t // seq, seq, n_out), BF16),
                   vt_shape(DIFF_TILE), vt_shape(BAND_TILE)],
        compiler_params=pltpu.CompilerParams(
            dimension_semantics=("arbitrary",), vmem_limit_bytes=VMEM_LIMIT),
        name="ln_qkv",
    )(x2, g, b, w_bf, col_scale)


def _diff_attn_kernel(lam_ref, g_ref, kf_ref, qf_ref, bdiag_ref, q_ref, k_ref, vt_ref, o_ref, s_scr):
    tq = q_ref.shape[1]
    tk = vt_ref.shape[2]
    i = pl.program_id(2)
    q = q_ref[0]
    lane = lax.broadcasted_iota(jnp.int32, q.shape, 1)
    zero = jnp.zeros_like(q)
    qq = jnp.concatenate([jnp.where(lane < A_HEAD_DIM, q, zero),
                          jnp.where(lane >= A_HEAD_DIM, q, zero)], axis=0)
    qqt = jnp.concatenate([qq.astype(F32).T.astype(BF16), qf_ref[0]], axis=0)
    n_before = i * (tq // tk)

    def scores(j):
        rows = pl.ds(pl.multiple_of(j * tk, tk), tk)
        kb = jnp.concatenate([k_ref[0, rows, :], kf_ref[rows, :]], axis=1)
        return jnp.dot(kb, qqt, preferred_element_type=F32)

    def update(slot, table_ref, j, carry):
        m, l, acc = carry
        s = s_scr[slot]
        if table_ref is not None:
            s = s + table_ref[0]
        m_new = jnp.maximum(m, jnp.max(s, axis=0, keepdims=True))
        alpha = jnp.exp2(m - m_new)
        p = jnp.exp2(s - m_new)
        l = alpha * l + jnp.sum(p, axis=0, keepdims=True)
        acc = alpha * acc + jnp.dot(vt_ref[j], p.astype(BF16), preferred_element_type=F32)
        return m_new, l, acc

    s_scr[0] = scores(0)

    def pair(t, carry):
        j = 2 * t
        s_scr[1] = scores(j + 1)
        carry = update(0, None, j, carry)
        s_scr[0] = scores(j + 2)
        return update(1, None, j + 1, carry)

    def quad(t, carry):
        return pair(2 * t + 1, pair(2 * t, carry))

    init = (jnp.full((1, 2 * tq), NEG, F32), jnp.zeros((1, 2 * tq), F32),
            jnp.zeros((LANES, 2 * tq), F32))
    carry = lax.fori_loop(0, n_before // 4, quad, init)
    carry = lax.fori_loop(n_before // 4 * 2, n_before // 2, pair, carry)

    def odd_tail(carry):
        s_scr[1] = scores(n_before)
        carry = update(0, None, n_before - 1, carry)
        return update(1, bdiag_ref, n_before, carry)

    def even_tail(carry):
        return update(0, bdiag_ref, n_before, carry)

    _, l, acc = lax.cond(n_before % 2 == 1, odd_tail, even_tail, carry)

    o_all = acc / l
    lv = lam_ref[...]
    lam = (jnp.exp(jnp.sum(lv[0:1] * lv[1:2], axis=1, keepdims=True))
           - jnp.exp(jnp.sum(lv[2:3] * lv[3:4], axis=1, keepdims=True)) + LAM_INIT)
    o = o_all[:, :tq] - lam * o_all[:, tq:]
    ms = jnp.mean(o * o, axis=0, keepdims=True)
    o = o * lax.rsqrt(ms + RMS_EPS) * (g_ref[...] * (1.0 - LAM_INIT))
    o_ref[0] = o.T.astype(BF16)


def _diff_attn(proj3, vt3, lam4, subln_g):
    bsz, seq, _ = proj3.shape
    tq = tk = DIFF_TILE
    assert seq <= CHUNK * CHUNK
    c = jnp.asarray([2.0 ** (-8.0 * (h + 1) / A_HEADS) for h in range(A_HEADS)], F32) * LOG2E
    r = jnp.arange(tk, dtype=jnp.int32)[:, None]
    qrel = jnp.arange(2 * tq, dtype=jnp.int32)[None, :] % tq
    allowed = (r // CHUNK) <= (qrel // CHUNK)
    kpos = jnp.arange(seq, dtype=jnp.int32)
    digits = jnp.stack([kpos // CHUNK] * 3 + [kpos % CHUNK] * 3, axis=1).astype(BF16)
    kfeat = jnp.pad(digits, ((0, 0), (0, LANES - digits.shape[1])))

    def pieces(v):
        hi = v.astype(BF16)
        mid = (v - hi.astype(F32)).astype(BF16)
        lo = (v - hi.astype(F32) - mid.astype(F32)).astype(BF16)
        return [hi, mid, lo]
    rows = jnp.stack(pieces(c * CHUNK) + pieces(c), axis=1)
    qfeat = jnp.broadcast_to(jnp.pad(rows, ((0, 0), (0, LANES - rows.shape[1])))[:, :, None],
                             (A_HEADS, LANES, 2 * tq))
    bdiag = jnp.where(allowed, c[:, None, None] * (qrel - jnp.abs(qrel - r) - r).astype(F32), NEG)
    kblk = A_WIDTH // LANES
    return pl.pallas_call(
        _diff_attn_kernel,
        grid=(bsz, A_HEADS, seq // tq),
        in_specs=[
            pl.BlockSpec((4, A_HEAD_DIM), lambda b, h, i: (0, 0)),
            pl.BlockSpec((LANES, 1), lambda b, h, i: (0, 0)),
            pl.BlockSpec((seq, LANES), lambda b, h, i: (0, 0)),
            pl.BlockSpec((1, LANES, 2 * tq), lambda b, h, i: (h, 0, 0)),
            pl.BlockSpec((1, tk, 2 * tq), lambda b, h, i: (h, 0, 0)),
            pl.BlockSpec((1, tq, LANES), lambda b, h, i: (b, i, h)),
            pl.BlockSpec((1, seq, LANES), lambda b, h, i: (b, 0, kblk + h)),
            pl.BlockSpec((seq // tk, LANES, tk), lambda b, h, i: (b, h, 0)),
        ],
        out_specs=pl.BlockSpec((1, tq, LANES), lambda b, h, i: (b, i, h)),
        out_shape=jax.ShapeDtypeStruct((bsz, seq, A_WIDTH), BF16),
        scratch_shapes=[pltpu.VMEM((2, tk, 2 * tq), F32)],
        compiler_params=pltpu.CompilerParams(
            dimension_semantics=("arbitrary", "arbitrary", "arbitrary"),
            vmem_limit_bytes=VMEM_LIMIT),
        name="diff_attn",
    )(lam4, subln_g.reshape(LANES, 1), kfeat, qfeat, bdiag, proj3, proj3, vt3)


def _band_attn_kernel(bias_ref, q_ref, k_ref, vt_ref, o_ref, s_scr):
    i = pl.program_id(2)
    tq = BAND_TILE
    units = q_ref.shape[1] // (BAND_PAIR * tq)
    key_tiles = BAND_TILES + BAND_PAIR - 1
    row = lax.broadcasted_iota(jnp.int32, (LANES, tq), 0)

    def key_tile(n, t):
        jt = (i * units + n) * BAND_PAIR - (BAND_TILES - 1) + t
        return jnp.maximum(jt, 0), jt < 0

    def scores(n):
        qs = []
        for a in range(BAND_PAIR):
            q = q_ref[0, (n * BAND_PAIR + a) * tq:(n * BAND_PAIR + a + 1) * tq, :]
            lane = lax.broadcasted_iota(jnp.int32, q.shape, 1)
            zero = jnp.zeros_like(q)
            qs += [jnp.where(lane < B_HEAD_DIM, q, zero), jnp.where(lane >= B_HEAD_DIM, q, zero)]
        qqt = jnp.concatenate(qs, axis=0).astype(F32).T.astype(BF16)
        ks = [k_ref[0, pl.ds(pl.multiple_of(key_tile(n, t)[0] * tq, tq), tq), :] for t in range(key_tiles)]
        return jnp.dot(jnp.concatenate(ks, axis=0), qqt, preferred_element_type=F32)

    def finish(n):
        bias = [bias_ref[0, jnp.where(key_tile(n, t)[1], 1, 0), t] for t in range(key_tiles)]
        s = s_scr[n % 2] + jnp.concatenate(bias, axis=0)
        m = jnp.max(s, axis=0, keepdims=True)
        p = jnp.exp2(s - m)
        l = jnp.sum(p, axis=0, keepdims=True)
        vt = jnp.concatenate([vt_ref[key_tile(n, t)[0]] for t in range(key_tiles)], axis=1)
        o = jnp.dot(vt, p.astype(BF16), preferred_element_type=F32) / l
        for a in range(BAND_PAIR):
            oa = jnp.where(row < B_HEAD_DIM, o[:, 2 * a * tq:(2 * a + 1) * tq],
                           o[:, (2 * a + 1) * tq:(2 * a + 2) * tq])
            o_ref[0, (n * BAND_PAIR + a) * tq:(n * BAND_PAIR + a + 1) * tq, :] = oa.T.astype(BF16)

    s_scr[0] = scores(0)
    for n in range(units):
        if n + 1 < units:
            s_scr[(n + 1) % 2] = scores(n + 1)
        finish(n)


def _band_bias(rel_bias, tq):
    past = B_PAST_CHUNKS * CHUNK
    band = tq + past
    assert tq - 1 <= REL_CLIP <= past
    qi = jnp.arange(tq)
    kj = jnp.arange(band)
    cq = qi[:, None] // CHUNK
    ck = kj[None, :] // CHUNK
    allowed = (ck >= cq) & (ck <= cq + B_PAST_CHUNKS)
    tab = rel_bias.astype(F32) * LOG2E
    n_diag = band + tq - 1
    n_unclipped = REL_CLIP + tq
    w = jnp.concatenate([tab[:, REL_CLIP - (tq - 1):],
                         jnp.broadcast_to(tab[:, -1:], (B_HEADS, n_diag - n_unclipped))], axis=1)
    shifted = jnp.tile(w, (1, tq + 1))[:, :tq * (n_diag + 1)].reshape(B_HEADS, tq, n_diag + 1)
    bias = jnp.flip(shifted[:, :, :band], axis=2)
    return jnp.where(allowed[None], bias, NEG)


def _band_attn(proj3, vtb3, rel_bias):
    bsz, seq, _ = proj3.shape
    tq = BAND_TILE
    groups = B_HEADS // 2
    key_tiles = BAND_TILES + BAND_PAIR - 1
    cols = BAND_PAIR * 2 * tq
    bias = _band_bias(rel_bias, tq)
    bias = bias.reshape(groups, 2, tq, BAND_TILES, tq)
    masked = jnp.full((groups, 2, tq, 1, tq), NEG, F32)
    per_tile = [jnp.concatenate([masked] * a + [bias] + [masked] * (BAND_PAIR - 1 - a), axis=3)
                for a in range(BAND_PAIR)]
    bias = jnp.stack(per_tile, axis=1)
    bias = bias.transpose(0, 4, 5, 1, 2, 3).reshape(groups, 1, key_tiles, tq, cols)
    bias = jnp.concatenate([bias, jnp.full_like(bias, NEG)], axis=1)
    qblk = 3 * A_WIDTH // LANES
    kblk = qblk + B_WIDTH // LANES
    rows = min(BAND_STEP_ROWS, seq)
    return pl.pallas_call(
        _band_attn_kernel,
        grid=(bsz, groups, seq // rows),
        in_specs=[
            pl.BlockSpec((1, 2, key_tiles, tq, cols), lambda b, g, i: (g, 0, 0, 0, 0)),
            pl.BlockSpec((1, rows, LANES), lambda b, g, i: (b, i, qblk + g)),
            pl.BlockSpec((1, seq, LANES), lambda b, g, i: (b, 0, kblk + g)),
            pl.BlockSpec((seq // tq, LANES, tq), lambda b, g, i: (b, g, 0)),
        ],
        out_specs=pl.BlockSpec((1, rows, LANES), lambda b, g, i: (b, i, g)),
        out_shape=jax.ShapeDtypeStruct((bsz, seq, B_WIDTH), BF16),
        scratch_shapes=[pltpu.VMEM((2, key_tiles * tq, cols), F32)],
        compiler_params=pltpu.CompilerParams(
            dimension_semantics=("arbitrary", "arbitrary", "arbitrary"),
            vmem_limit_bytes=VMEM_LIMIT),
        name="band_attn",
    )(bias, proj3, proj3, vtb3)


def _out_router_kernel(x_ref, gi_ref, bi_ref, oa_ref, ob_ref, wa_ref, wb_ref, g1_ref, b1_ref,
                       rwt_ref, rb_ref, tri_ref, h1_ref, idx_ref, gate_ref, rank_ref, cnt_ref,
                       carry_ref, *, steps_per_tile):
    h = _layer_norm(x_ref[...], gi_ref[...], bi_ref[...])
    mix = (jnp.dot(oa_ref[0], wa_ref[...], preferred_element_type=F32)
           + jnp.dot(ob_ref[0], wb_ref[...], preferred_element_type=F32))
    h1 = _layer_norm(DEEPNORM_ALPHA * h + mix, g1_ref[...], b1_ref[...])
    nchunk = h1.shape[1] // LANES
    for c in range(nchunk):
        h1_ref[pl.ds(c, ROW_TILE, stride=nchunk), :] = h1[:, c * LANES:(c + 1) * LANES]
    lt = lax.dot_general(rwt_ref[...], h1, (((1,), (1,)), ((), ())),
                         precision=lax.Precision.HIGHEST, preferred_element_type=F32)
    lt = lt + rb_ref[...]
    eidx = lax.broadcasted_iota(jnp.int32, lt.shape, 0)
    vals, idxs, hots = [], [], []
    for _ in range(TOP_K):
        mx = jnp.max(lt, axis=0, keepdims=True)
        am = jnp.min(jnp.where(lt == mx, eidx, N_EXPERTS), axis=0, keepdims=True)
        hit = eidx == am
        vals.append(mx)
        idxs.append(am)
        hots.append(jnp.where(hit, 1.0, 0.0))
        lt = jnp.where(hit, -jnp.inf, lt)
    ex = [jnp.exp(v - vals[0]) for v in vals]
    den = ex[0] + ex[1] + ex[2] + ex[3]
    idx_ref[...] = jnp.concatenate(idxs, axis=0)
    gate_ref[...] = jnp.concatenate([e / den for e in ex], axis=0)

    @pl.when(pl.program_id(0) % steps_per_tile == 0)
    def _():
        carry_ref[...] = jnp.zeros_like(carry_ref)

    hot = (hots[0] + hots[1]) + (hots[2] + hots[3])
    before = jnp.dot(hot.astype(BF16), tri_ref[...], preferred_element_type=F32) + carry_ref[...]
    rank_ref[...] = jnp.concatenate(
        [jnp.sum(hk * before, axis=0, keepdims=True) for hk in hots], axis=0).astype(jnp.int32)
    total = carry_ref[...] + jnp.sum(hot, axis=1, keepdims=True)
    carry_ref[...] = total
    cnt_ref[0] = total.astype(jnp.int32)


def _out_router(x2, gi, bi, o_a, o_b, wa, wb, g1, b1, rwt, rb, moe_tile):
    t, d = x2.shape
    nchunk = d // LANES
    row = lambda i: (i, 0)
    fixed = lambda i: (0, 0)
    steps_per_seq = o_a.shape[1] // ROW_TILE
    seq_row = lambda i: (i // steps_per_seq, i % steps_per_seq, 0)
    tri = jnp.triu(jnp.ones((ROW_TILE, ROW_TILE), BF16), k=1)
    return pl.pallas_call(
        functools.partial(_out_router_kernel, steps_per_tile=moe_tile // ROW_TILE),
        grid=(t // ROW_TILE,),
        in_specs=[
            pl.BlockSpec((ROW_TILE, d), row),
            pl.BlockSpec((1, d), fixed),
            pl.BlockSpec((1, d), fixed),
            pl.BlockSpec((1, ROW_TILE, A_WIDTH), seq_row),
            pl.BlockSpec((1, ROW_TILE, B_WIDTH), seq_row),
            pl.BlockSpec((A_WIDTH, d), fixed),
            pl.BlockSpec((B_WIDTH, d), fixed),
            pl.BlockSpec((1, d), fixed),
            pl.BlockSpec((1, d), fixed),
            pl.BlockSpec((N_EXPERTS, d), fixed),
            pl.BlockSpec((N_EXPERTS, 1), fixed),
            pl.BlockSpec((ROW_TILE, ROW_TILE), fixed),
        ],
        out_specs=[
            pl.BlockSpec((ROW_TILE * nchunk, LANES), row),
            pl.BlockSpec((TOP_K, ROW_TILE), lambda i: (0, i)),
            pl.BlockSpec((TOP_K, ROW_TILE), lambda i: (0, i)),
            pl.BlockSpec((TOP_K, ROW_TILE), lambda i: (0, i)),
            pl.BlockSpec((1, N_EXPERTS, 1), lambda i: (i, 0, 0)),
        ],
        out_shape=[
            jax.ShapeDtypeStruct((t * nchunk, LANES), F32),
            jax.ShapeDtypeStruct((TOP_K, t), jnp.int32),
            jax.ShapeDtypeStruct((TOP_K, t), F32),
            jax.ShapeDtypeStruct((TOP_K, t), jnp.int32),
            jax.ShapeDtypeStruct((t // ROW_TILE, N_EXPERTS, 1), jnp.int32),
        ],
        scratch_shapes=[pltpu.VMEM((N_EXPERTS, 1), F32)],
        compiler_params=pltpu.CompilerParams(
            dimension_semantics=("arbitrary",), vmem_limit_bytes=VMEM_LIMIT),
        name="out_router",
    )(x2, gi, bi, o_a, o_b, wa, wb, g1, b1, rwt, rb, tri)


def _split_w1_kernel(w_ref, perm_ref, g_ref, l_ref):
    rows = w_ref.shape[1]
    even = (lax.broadcasted_iota(jnp.int32, (rows, LANES), 1) & 1) == 0
    perm = perm_ref[...]
    for g in range(w_ref.shape[2] // (2 * LANES)):
        v0 = w_ref[0, :, 2 * g * LANES:(2 * g + 1) * LANES]
        v1 = w_ref[0, :, (2 * g + 1) * LANES:(2 * g + 2) * LANES]
        glu = jnp.where(even, v0, pltpu.roll(v1, 1, axis=1)).astype(BF16)
        lin = jnp.where(even, pltpu.roll(v0, LANES - 1, axis=1), v1).astype(BF16)
        g_ref[0, :, g * LANES:(g + 1) * LANES] = jnp.dot(
            glu, perm, preferred_element_type=F32).astype(BF16)
        l_ref[0, :, g * LANES:(g + 1) * LANES] = jnp.dot(
            lin, perm, preferred_element_type=F32).astype(BF16)


def _split_w1(w1e):
    e, d, f2 = w1e.shape
    rows = 1024
    half = LANES // 2
    unit = jnp.arange(LANES)
    perm = (jnp.arange(LANES)[:, None] == (2 * (unit % half) + unit // half)[None, :]).astype(BF16)
    return pl.pallas_call(
        _split_w1_kernel,
        grid=(e, d // rows),
        in_specs=[pl.BlockSpec((1, rows, f2), lambda i, j: (i, j, 0)),
                  pl.BlockSpec((LANES, LANES), lambda i, j: (0, 0))],
        out_specs=[pl.BlockSpec((1, rows, f2 // 2), lambda i, j: (i, j, 0)),
                   pl.BlockSpec((1, rows, f2 // 2), lambda i, j: (i, j, 0))],
        out_shape=[jax.ShapeDtypeStruct((e, d, f2 // 2), BF16),
                   jax.ShapeDtypeStruct((e, d, f2 // 2), BF16)],
        compiler_params=pltpu.CompilerParams(
            dimension_semantics=("arbitrary", "arbitrary"), vmem_limit_bytes=VMEM_LIMIT),
        name="split_w1",
    )(w1e, perm)


def _moe_kernel(cnt_ref, offs_ref,
                w1g_ref, w1l_ref, b1g_ref, b1l_ref, w2_ref, b2_ref, g2_ref, bb2_ref,
                pos_hbm, gate_hbm, h_hbm,
                out_hbm, hbuf, ybuf, stage, out_buf, list_tok, list_gate, pos_in, gate_in, n_sorted, sem,
                *, tile):
    b = pl.program_id(0)
    e = pl.program_id(1)
    n_tiles = pl.num_programs(0)
    n_exp = pl.num_programs(1)
    d_model = w1g_ref.shape[1]
    d_ff = w1g_ref.shape[2]
    nchunk = d_model // LANES
    tile_rows = tile * nchunk
    pairs = tile * TOP_K
    n_sort_chunks = pairs // SORT_CHUNK
    half = (b % 2) * pairs
    other_half = pairs - half

    def stage_rows(r):
        return pl.ds(pl.multiple_of(r * nchunk, nchunk), nchunk)

    def listed_rows(p):
        return pl.ds(pl.multiple_of(list_tok[p], nchunk), nchunk)

    def sort_pair(j, first_row, dst_half):
        p = pos_in[j] + dst_half
        list_tok[p] = first_row
        list_gate[p] = gate_in[j]

    def sort_chunks(lo, hi, dst_half):
        def body(jo, c):
            for u in range(SORT_UNROLL):
                j = jo * SORT_UNROLL + u
                sort_pair(j, (j & (tile - 1)) * nchunk, dst_half)
            return c
        lax.fori_loop(lo * (SORT_CHUNK // SORT_UNROLL), hi * (SORT_CHUNK // SORT_UNROLL), body, 0)

    def load_pairs(t):
        rows = pl.ds(pl.multiple_of(t * pairs, pairs), pairs)
        copies = [pltpu.make_async_copy(pos_hbm.at[rows], pos_in, sem.at[2]),
                  pltpu.make_async_copy(gate_hbm.at[rows], gate_in, sem.at[3])]
        for cp in copies:
            cp.start()
        for cp in copies:
            cp.wait()

    @pl.when(jnp.logical_and(b == 0, e == 0))
    def _():
        stage[...] = jnp.zeros_like(stage)

        def pad_body(j, c):
            list_tok[pairs + j] = 0
            list_tok[2 * pairs + j] = 0
            return c
        lax.fori_loop(0, LIST_PAD, pad_body, 0)
        load_pairs(0)
        n_sorted[0] = 0

    @pl.when(e == 0)
    def _():
        src = h_hbm.at[pl.ds(pl.multiple_of(b * tile_rows, tile_rows), tile_rows), :]
        load = pltpu.make_async_copy(src, hbuf, sem.at[0])

        @pl.when(b == 0)
        def _():
            load.start()
            ybuf[...] = jnp.zeros_like(ybuf)
        sort_chunks(jnp.minimum(n_sorted[0], n_sort_chunks), n_sort_chunks, half)
        n_sorted[0] = 0

        @pl.when(b + 1 < n_tiles)
        def _():
            load_pairs(b + 1)
        load.wait()

    n = cnt_ref[b * n_exp + e]
    base = offs_ref[b * n_exp + e] + half

    def run_block(m_rows, p0, nrows):
        def gather_body(ci, c):
            for u in range(ROW_UNROLL):
                r = ci * ROW_UNROLL + u
                stage[stage_rows(r), :] = hbuf[listed_rows(p0 + r), :]
            return c
        lax.fori_loop(0, (nrows + ROW_UNROLL - 1) // ROW_UNROLL, gather_body, 0)

        if m_rows == MOE_BLOCK:
            chunk = jnp.minimum(n_sorted[0], n_sort_chunks - 1)
            n_sorted[0] = n_sorted[0] + 1
            j0 = chunk * SORT_CHUNK
            row0 = (j0 & (tile - 1)) * nchunk
            for u in range(SORT_CHUNK):
                sort_pair(j0 + u, row0 + u * nchunk, other_half)

        x = jnp.concatenate(
            [stage[pl.ds(c, m_rows, stride=nchunk), :] for c in range(nchunk)],
            axis=1).astype(BF16)
        y = b2_ref[0]
        for hh in range(d_ff // FF_CHUNK):
            sl = slice(hh * FF_CHUNK, (hh + 1) * FF_CHUNK)
            hg = jnp.dot(x, w1g_ref[0, :, sl], preferred_element_type=F32) + b1g_ref[0, :, sl]
            hl = jnp.dot(x, w1l_ref[0, :, sl], preferred_element_type=F32) + b1l_ref[0, :, sl]
            xg = jnp.minimum(hg, SWIGLU_LIMIT)
            xl = jnp.clip(hl, -SWIGLU_LIMIT, SWIGLU_LIMIT)
            act = xg * jax.nn.sigmoid(SWIGLU_ALPHA * xg) * (xl + 1.0)
            y = y + jnp.dot(act.astype(BF16), w2_ref[0, sl, :], preferred_element_type=F32)
        for c in range(nchunk):
            stage[pl.ds(c, m_rows, stride=nchunk), :] = y[:, c * LANES:(c + 1) * LANES]

        def add_body(ci, c):
            rows, vals = [], []
            for u in range(SUBLANES):
                r = ci * SUBLANES + u
                dst = listed_rows(p0 + r)
                rows.append(dst)
                vals.append(ybuf[dst, :] + list_gate[p0 + r] * stage[stage_rows(r), :])
            for dst, val in zip(rows, vals):
                ybuf[dst, :] = val
            return c
        n_full = nrows // SUBLANES
        lax.fori_loop(0, n_full, add_body, 0)

        def add_tail(r, c):
            dst = listed_rows(p0 + r)
            ybuf[dst, :] = ybuf[dst, :] + list_gate[p0 + r] * stage[stage_rows(r), :]
            return c
        lax.fori_loop(n_full * SUBLANES, nrows, add_tail, 0)

    n_big = n // MOE_BLOCK
    rem = n - n_big * MOE_BLOCK
    rem_is_big = rem > MOE_BLOCK - SMALL_BLOCK
    n_big_blocks = n_big + rem_is_big.astype(jnp.int32)
    n_small_blocks = jnp.where(rem_is_big, 0, (rem + SMALL_BLOCK - 1) // SMALL_BLOCK)

    def big_body(s, c):
        run_block(MOE_BLOCK, base + s * MOE_BLOCK, jnp.minimum(MOE_BLOCK, n - s * MOE_BLOCK))
        return c
    lax.fori_loop(0, n_big_blocks, big_body, 0)

    def small_body(s, c):
        run_block(SMALL_BLOCK, base + n_big * MOE_BLOCK + s * SMALL_BLOCK,
                  jnp.minimum(SMALL_BLOCK, rem - s * SMALL_BLOCK))
        return c
    lax.fori_loop(0, n_small_blocks, small_body, 0)

    @pl.when(e == n_exp - 1)
    def _():
        def out_copy(c, slot):
            rows = pl.ds(pl.multiple_of(b * tile + c * ROW_TILE, ROW_TILE), ROW_TILE)
            return pltpu.make_async_copy(out_buf.at[slot], out_hbm.at[rows, :], sem.at[4 + slot])

        def chunk_body(c, carry):
            slot = c % 2

            @pl.when(c >= 2)
            def _():
                out_copy(c - 2, slot).wait()
            first = c * (ROW_TILE * nchunk)
            pieces = [DEEPNORM_ALPHA * hbuf[pl.ds(first + k, ROW_TILE, stride=nchunk), :]
                      + ybuf[pl.ds(first + k, ROW_TILE, stride=nchunk), :] for k in range(nchunk)]
            out_buf[slot] = _layer_norm(jnp.concatenate(pieces, axis=1), g2_ref[...], bb2_ref[...])
            out_copy(c, slot).start()
            done = pl.ds(pl.multiple_of(first, ROW_TILE * nchunk), ROW_TILE * nchunk)
            ybuf[done, :] = jnp.zeros((ROW_TILE * nchunk, LANES), F32)

            @pl.when(b + 1 < n_tiles)
            def _():
                nxt = pl.ds(pl.multiple_of((b + 1) * tile_rows + first, ROW_TILE * nchunk), ROW_TILE * nchunk)
                pltpu.make_async_copy(h_hbm.at[nxt, :], hbuf.at[done, :], sem.at[0]).start()
            return carry
        n_out = tile // ROW_TILE
        lax.fori_loop(0, n_out, chunk_body, 0)
        for c in range(max(n_out - 2, 0), n_out):
            out_copy(c, c % 2).wait()


def _moe(h1t, cnt, offs, pos_flat, gate_flat, w1g, w1l, b1g, b1l, w2, b2, g2, bb2, tile):
    n_exp, d, f = w1g.shape
    nchunk = d // LANES
    n_tiles = h1t.shape[0] // (tile * nchunk)
    pairs = tile * TOP_K
    assert tile % SORT_CHUNK == 0
    expert = lambda b, e, cnt, offs: (e, 0, 0)
    grid_spec = pltpu.PrefetchScalarGridSpec(
        num_scalar_prefetch=2,
        grid=(n_tiles, n_exp),
        in_specs=[
            pl.BlockSpec((1, d, f), expert),
            pl.BlockSpec((1, d, f), expert),
            pl.BlockSpec((1, 1, f), expert),
            pl.BlockSpec((1, 1, f), expert),
            pl.BlockSpec((1, f, d), expert),
            pl.BlockSpec((1, 1, d), expert),
            pl.BlockSpec((1, d), lambda b, e, cnt, offs: (0, 0)),
            pl.BlockSpec((1, d), lambda b, e, cnt, offs: (0, 0)),
            pl.BlockSpec(memory_space=pl.ANY),
            pl.BlockSpec(memory_space=pl.ANY),
            pl.BlockSpec(memory_space=pl.ANY),
        ],
        out_specs=pl.BlockSpec(memory_space=pl.ANY),
        scratch_shapes=[
            pltpu.VMEM((tile * nchunk, LANES), F32),
            pltpu.VMEM((tile * nchunk, LANES), F32),
            pltpu.VMEM((MOE_BLOCK * nchunk, LANES), F32),
            pltpu.VMEM((2, ROW_TILE, d), F32),
            pltpu.SMEM((2 * pairs + LIST_PAD,), jnp.int32),
            pltpu.SMEM((2 * pairs + LIST_PAD,), F32),
            pltpu.SMEM((pairs,), jnp.int32),
            pltpu.SMEM((pairs,), F32),
            pltpu.SMEM((1,), jnp.int32),
            pltpu.SemaphoreType.DMA((6,)),
        ],
    )
    return pl.pallas_call(
        functools.partial(_moe_kernel, tile=tile),
        grid_spec=grid_spec,
        out_shape=jax.ShapeDtypeStruct((n_tiles * tile, d), F32),
        compiler_params=pltpu.CompilerParams(
            dimension_semantics=("arbitrary", "arbitrary"), vmem_limit_bytes=MOE_VMEM_LIMIT),
        name="moe",
    )(cnt, offs, w1g, w1l, b1g, b1l, w2, b2, g2, bb2, pos_flat, gate_flat, h1t)


def kernel(x, ln_in_g, ln_in_b, w_in, lambda_q1, lambda_k1, lambda_q2, lambda_k2, subln_g, rel_bias,
           w_out, ln1_g, ln1_b, router_w, router_b, w1, b1, w2, b2, ln2_g, ln2_b):
    bsz, seq, d = x.shape
    t = bsz * seq
    x2 = x.reshape(t, d)
    row = lambda v: v.reshape(1, -1).astype(F32)

    qs = A_HEAD_DIM ** -0.5 * LOG2E
    col_scale = jnp.concatenate([
        jnp.full((A_WIDTH,), qs, F32), jnp.ones((2 * A_WIDTH,), F32),
        jnp.full((B_WIDTH,), B_HEAD_DIM ** -0.5 * LOG2E, F32), jnp.ones((2 * B_WIDTH,), F32)]).reshape(1, -1)
    proj3, vt3, vtb3 = _ln_qkv(x2, row(ln_in_g), row(ln_in_b), w_in[0].astype(BF16), col_scale, seq)

    lam4 = jnp.stack([lambda_q1[0], lambda_k1[0], lambda_q2[0], lambda_k2[0]]).astype(F32)
    o_a = _diff_attn(proj3, vt3, lam4, subln_g[0].astype(F32))
    o_b = _band_attn(proj3, vtb3, rel_bias[0])

    w_o = w_out[0].astype(BF16)
    tile = min(MOE_TILE, t)
    n_tiles = t // tile
    h1t, top_idx, gates, rank, run_cnt = _out_router(
        x2, row(ln_in_g), row(ln_in_b), o_a, o_b,
        w_o[:A_WIDTH], w_o[A_WIDTH:], row(ln1_g[0]), row(ln1_b[0]),
        router_w[0].T.astype(F32), router_b[0].reshape(-1, 1).astype(F32), tile)

    steps = tile // ROW_TILE
    cnt = run_cnt[steps - 1::steps, :, 0]
    offs = jnp.cumsum(cnt, axis=1) - cnt
    hot = top_idx.reshape(TOP_K, n_tiles, tile, 1) == jnp.arange(N_EXPERTS, dtype=jnp.int32)
    pos = rank + jnp.sum(jnp.where(hot, offs[None, :, None, :], 0), axis=-1).reshape(TOP_K, t)
    per_tile = lambda a: a.reshape(TOP_K, n_tiles, tile).transpose(1, 0, 2).reshape(-1)
    w1g, w1l = _split_w1(w1[0])
    b1e = b1[0].astype(F32)[:, None, :]
    out = _moe(h1t, cnt.reshape(-1), offs.reshape(-1), per_tile(pos), per_tile(gates),
               w1g, w1l, b1e[:, :, 0::2], b1e[:, :, 1::2],
               w2[0].astype(BF16), b2[0][:, None, :].astype(F32), row(ln2_g[0]), row(ln2_b[0]), tile)
    return out.reshape(bsz, seq, d)
```

```python
import functools
import math

import jax
import jax.numpy as jnp
from jax import lax
from jax.experimental import pallas as pl
from jax.experimental.pallas import tpu as pltpu

F32 = jnp.float32
BF16 = jnp.bfloat16

CHUNK = 64
A_HEADS = 4
A_HEAD_DIM = 64
A_WIDTH = A_HEADS * 2 * A_HEAD_DIM
B_HEADS = 8
B_HEAD_DIM = 64
B_WIDTH = B_HEADS * B_HEAD_DIM
B_PAST_CHUNKS = 8
REL_CLIP = 256
N_EXPERTS = 32
TOP_K = 4
SWIGLU_ALPHA = 1.702
SWIGLU_LIMIT = 7.0
MOE_BLOCK = 512
LN_EPS = 1e-5
RMS_EPS = 1e-5
DEPTH = 1
DEEPNORM_ALPHA = (2 * DEPTH) ** 0.25
LAM_INIT = 0.8 - 0.6 * math.exp(-0.3 * 0)

LOG2E = 1.4426950408889634
NEG = -1e30
LANES = 128
SUBLANES = 8
ROW_TILE = 512
VMEM_LIMIT = 48 * 1024 * 1024
DIFF_TILE = 256
BAND_TILE = 128
BAND_TILES = (BAND_TILE + B_PAST_CHUNKS * CHUNK) // BAND_TILE
BAND_PAIR = 2
BAND_STEP_ROWS = 4096
MOE_TILE = 4096
MOE_VMEM_LIMIT = 60 * 1024 * 1024
SMALL_BLOCK = 128
FF_CHUNK = 1024
SORT_UNROLL = 8
SORT_CHUNK = 512
ROW_UNROLL = 32
LIST_PAD = 128


def _layer_norm(x, g, b):
    mu = jnp.mean(x, axis=-1, keepdims=True)
    xc = x - mu
    var = jnp.mean(xc * xc, axis=-1, keepdims=True)
    return xc * lax.rsqrt(var + LN_EPS) * g + b


def _ln_qkv_kernel(x_ref, g_ref, b_ref, w_ref, cs_ref, o_ref, vta_ref, vtb_ref, h_ref):
    h = _layer_norm(x_ref[...], g_ref[...], b_ref[...])
    h_ref[...] = h
    hb = h.astype(BF16)
    n_out = w_ref.shape[1]
    for c in range(n_out // ROW_TILE):
        sl = slice(c * ROW_TILE, (c + 1) * ROW_TILE)
        val = jnp.dot(hb, w_ref[:, sl], preferred_element_type=F32) * cs_ref[:, sl]
        o_ref[0, :, sl] = val.astype(BF16)
        for start, vt_ref in ((2 * A_WIDTH, vta_ref), (3 * A_WIDTH + 2 * B_WIDTH, vtb_ref)):
            if sl.start == start:
                tile = vt_ref.shape[2]
                for kt in range(ROW_TILE // tile):
                    vt_ref[kt] = val[kt * tile:(kt + 1) * tile, :].T.astype(BF16)


def _ln_qkv(x2, g, b, w_bf, col_scale, seq):
    assert A_WIDTH == ROW_TILE and B_WIDTH == ROW_TILE
    t, d = x2.shape
    n_out = w_bf.shape[1]
    steps_per_seq = seq // ROW_TILE
    vt_spec = lambda tile: pl.BlockSpec((ROW_TILE // tile, ROW_TILE, tile), lambda i: (i, 0, 0))
    vt_shape = lambda tile: jax.ShapeDtypeStruct((t // tile, ROW_TILE, tile), BF16)
    return pl.pallas_call(
        _ln_qkv_kernel,
        grid=(t // ROW_TILE,),
        in_specs=[
            pl.BlockSpec((ROW_TILE, d), lambda i: (i, 0)),
            pl.BlockSpec((1, d), lambda i: (0, 0)),
            pl.BlockSpec((1, d), lambda i: (0, 0)),
            pl.BlockSpec((d, n_out), lambda i: (0, 0)),
            pl.BlockSpec((1, n_out), lambda i: (0, 0)),
        ],
        out_specs=[pl.BlockSpec((1, ROW_TILE, n_out),
                                lambda i: (i // steps_per_seq, i % steps_per_seq, 0)),
                   vt_spec(DIFF_TILE), vt_spec(BAND_TILE),
                   pl.BlockSpec((ROW_TILE, d), lambda i: (i, 0))],
        out_shape=[jax.ShapeDtypeStruct((t // seq, seq, n_out), BF16),
                   vt_shape(DIFF_TILE), vt_shape(BAND_TILE),
                   jax.ShapeDtypeStruct((t, d), F32)],
        compiler_params=pltpu.CompilerParams(
            dimension_semantics=("arbitrary",), vmem_limit_bytes=VMEM_LIMIT),
        name="ln_qkv",
    )(x2, g, b, w_bf, col_scale)


def _diff_attn_kernel(lam_ref, g_ref, kf_ref, qf_ref, bdiag_ref, q_ref, k_ref, vt_ref, o_ref, s_scr):
    tq = q_ref.shape[1]
    tk = vt_ref.shape[2]
    i = pl.program_id(2)
    q = q_ref[0]
    lane = lax.broadcasted_iota(jnp.int32, q.shape, 1)
    zero = jnp.zeros_like(q)
    qq = jnp.concatenate([jnp.where(lane < A_HEAD_DIM, q, zero),
                          jnp.where(lane >= A_HEAD_DIM, q, zero)], axis=0)
    qqt = jnp.concatenate([qq.astype(F32).T.astype(BF16), qf_ref[0]], axis=0)
    n_before = i * (tq // tk)

    def scores(j):
        rows = pl.ds(pl.multiple_of(j * tk, tk), tk)
        kb = jnp.concatenate([k_ref[0, rows, :], kf_ref[rows, :]], axis=1)
        return jnp.dot(kb, qqt, preferred_element_type=F32)

    def update(slot, table_ref, j, carry):
        m, l, acc = carry
        s = s_scr[slot]
        if table_ref is not None:
            s = s + table_ref[0]
        m_new = jnp.maximum(m, jnp.max(s, axis=0, keepdims=True))
        alpha = jnp.exp2(m - m_new)
        p = jnp.exp2(s - m_new)
        l = alpha * l + jnp.sum(p, axis=0, keepdims=True)
        acc = alpha * acc + jnp.dot(vt_ref[j], p.astype(BF16), preferred_element_type=F32)
        return m_new, l, acc

    s_scr[0] = scores(0)

    def pair(t, carry):
        j = 2 * t
        s_scr[1] = scores(j + 1)
        carry = update(0, None, j, carry)
        s_scr[0] = scores(j + 2)
        return update(1, None, j + 1, carry)

    def quad(t, carry):
        return pair(2 * t + 1, pair(2 * t, carry))

    init = (jnp.full((1, 2 * tq), NEG, F32), jnp.zeros((1, 2 * tq), F32),
            jnp.zeros((LANES, 2 * tq), F32))
    carry = lax.fori_loop(0, n_before // 4, quad, init)
    carry = lax.fori_loop(n_before // 4 * 2, n_before // 2, pair, carry)

    def odd_tail(carry):
        s_scr[1] = scores(n_before)
        carry = update(0, None, n_before - 1, carry)
        return update(1, bdiag_ref, n_before, carry)

    def even_tail(carry):
        return update(0, bdiag_ref, n_before, carry)

    _, l, acc = lax.cond(n_before % 2 == 1, odd_tail, even_tail, carry)

    o_all = acc / l
    lv = lam_ref[...]
    lam = (jnp.exp(jnp.sum(lv[0:1] * lv[1:2], axis=1, keepdims=True))
           - jnp.exp(jnp.sum(lv[2:3] * lv[3:4], axis=1, keepdims=True)) + LAM_INIT)
    o = o_all[:, :tq] - lam * o_all[:, tq:]
    ms = jnp.mean(o * o, axis=0, keepdims=True)
    o = o * lax.rsqrt(ms + RMS_EPS) * (g_ref[...] * (1.0 - LAM_INIT))
    o_ref[0] = o.T.astype(BF16)


def _diff_attn(proj3, vt3, lam4, subln_g):
    bsz, seq, _ = proj3.shape
    tq = tk = DIFF_TILE
    assert seq <= CHUNK * CHUNK
    c = jnp.asarray([2.0 ** (-8.0 * (h + 1) / A_HEADS) for h in range(A_HEADS)], F32) * LOG2E
    r = jnp.arange(tk, dtype=jnp.int32)[:, None]
    qrel = jnp.arange(2 * tq, dtype=jnp.int32)[None, :] % tq
    allowed = (r // CHUNK) <= (qrel // CHUNK)
    kpos = jnp.arange(seq, dtype=jnp.int32)
    digits = jnp.stack([kpos // CHUNK] * 3 + [kpos % CHUNK] * 3, axis=1).astype(BF16)
    kfeat = jnp.pad(digits, ((0, 0), (0, LANES - digits.shape[1])))

    def pieces(v):
        hi = v.astype(BF16)
        mid = (v - hi.astype(F32)).astype(BF16)
        lo = (v - hi.astype(F32) - mid.astype(F32)).astype(BF16)
        return [hi, mid, lo]
    rows = jnp.stack(pieces(c * CHUNK) + pieces(c), axis=1)
    qfeat = jnp.broadcast_to(jnp.pad(rows, ((0, 0), (0, LANES - rows.shape[1])))[:, :, None],
                             (A_HEADS, LANES, 2 * tq))
    bdiag = jnp.where(allowed, c[:, None, None] * (qrel - jnp.abs(qrel - r) - r).astype(F32), NEG)
    kblk = A_WIDTH // LANES
    return pl.pallas_call(
        _diff_attn_kernel,
        grid=(bsz, A_HEADS, seq // tq),
        in_specs=[
            pl.BlockSpec((4, A_HEAD_DIM), lambda b, h, i: (0, 0)),
            pl.BlockSpec((LANES, 1), lambda b, h, i: (0, 0)),
            pl.BlockSpec((seq, LANES), lambda b, h, i: (0, 0)),
            pl.BlockSpec((1, LANES, 2 * tq), lambda b, h, i: (h, 0, 0)),
            pl.BlockSpec((1, tk, 2 * tq), lambda b, h, i: (h, 0, 0)),
            pl.BlockSpec((1, tq, LANES), lambda b, h, i: (b, i, h)),
            pl.BlockSpec((1, seq, LANES), lambda b, h, i: (b, 0, kblk + h)),
            pl.BlockSpec((seq // tk, LANES, tk), lambda b, h, i: (b, h, 0)),
        ],
        out_specs=pl.BlockSpec((1, tq, LANES), lambda b, h, i: (b, i, h)),
        out_shape=jax.ShapeDtypeStruct((bsz, seq, A_WIDTH), BF16),
        scratch_shapes=[pltpu.VMEM((2, tk, 2 * tq), F32)],
        compiler_params=pltpu.CompilerParams(
            dimension_semantics=("arbitrary", "arbitrary", "arbitrary"),
            vmem_limit_bytes=VMEM_LIMIT),
        name="diff_attn",
    )(lam4, subln_g.reshape(LANES, 1), kfeat, qfeat, bdiag, proj3, proj3, vt3)


def _band_attn_kernel(bias_ref, q_ref, k_ref, vt_ref, o_ref, s_scr):
    i = pl.program_id(2)
    tq = BAND_TILE
    units = q_ref.shape[1] // (BAND_PAIR * tq)
    key_tiles = BAND_TILES + BAND_PAIR - 1
    row = lax.broadcasted_iota(jnp.int32, (LANES, tq), 0)

    def key_tile(n, t):
        jt = (i * units + n) * BAND_PAIR - (BAND_TILES - 1) + t
        return jnp.maximum(jt, 0), jt < 0

    def scores(n):
        qs = []
        for a in range(BAND_PAIR):
            q = q_ref[0, (n * BAND_PAIR + a) * tq:(n * BAND_PAIR + a + 1) * tq, :]
            lane = lax.broadcasted_iota(jnp.int32, q.shape, 1)
            zero = jnp.zeros_like(q)
            qs += [jnp.where(lane < B_HEAD_DIM, q, zero), jnp.where(lane >= B_HEAD_DIM, q, zero)]
        qqt = jnp.concatenate(qs, axis=0).astype(F32).T.astype(BF16)
        ks = [k_ref[0, pl.ds(pl.multiple_of(key_tile(n, t)[0] * tq, tq), tq), :] for t in range(key_tiles)]
        return jnp.dot(jnp.concatenate(ks, axis=0), qqt, preferred_element_type=F32)

    def finish(n):
        bias = [bias_ref[0, jnp.where(key_tile(n, t)[1], 1, 0), t] for t in range(key_tiles)]
        s = s_scr[n % 2] + jnp.concatenate(bias, axis=0)
        m = jnp.max(s, axis=0, keepdims=True)
        p = jnp.exp2(s - m)
        l = jnp.sum(p, axis=0, keepdims=True)
        vt = jnp.concatenate([vt_ref[key_tile(n, t)[0]] for t in range(key_tiles)], axis=1)
        o = jnp.dot(vt, p.astype(BF16), preferred_element_type=F32) / l
        for a in range(BAND_PAIR):
            oa = jnp.where(row < B_HEAD_DIM, o[:, 2 * a * tq:(2 * a + 1) * tq],
                           o[:, (2 * a + 1) * tq:(2 * a + 2) * tq])
            o_ref[0, (n * BAND_PAIR + a) * tq:(n * BAND_PAIR + a + 1) * tq, :] = oa.T.astype(BF16)

    s_scr[0] = scores(0)
    for n in range(units):
        if n + 1 < units:
            s_scr[(n + 1) % 2] = scores(n + 1)
        finish(n)


def _band_bias(rel_bias, tq):
    past = B_PAST_CHUNKS * CHUNK
    band = tq + past
    assert tq - 1 <= REL_CLIP <= past
    qi = jnp.arange(tq)
    kj = jnp.arange(band)
    cq = qi[:, None] // CHUNK
    ck = kj[None, :] // CHUNK
    allowed = (ck >= cq) & (ck <= cq + B_PAST_CHUNKS)
    tab = rel_bias.astype(F32) * LOG2E
    n_diag = band + tq - 1
    n_unclipped = REL_CLIP + tq
    w = jnp.concatenate([tab[:, REL_CLIP - (tq - 1):],
                         jnp.broadcast_to(tab[:, -1:], (B_HEADS, n_diag - n_unclipped))], axis=1)
    shifted = jnp.tile(w, (1, tq + 1))[:, :tq * (n_diag + 1)].reshape(B_HEADS, tq, n_diag + 1)
    bias = jnp.flip(shifted[:, :, :band], axis=2)
    return jnp.where(allowed[None], bias, NEG)


def _band_attn(proj3, vtb3, rel_bias):
    bsz, seq, _ = proj3.shape
    tq = BAND_TILE
    groups = B_HEADS // 2
    key_tiles = BAND_TILES + BAND_PAIR - 1
    cols = BAND_PAIR * 2 * tq
    bias = _band_bias(rel_bias, tq)
    bias = bias.reshape(groups, 2, tq, BAND_TILES, tq)
    masked = jnp.full((groups, 2, tq, 1, tq), NEG, F32)
    per_tile = [jnp.concatenate([masked] * a + [bias] + [masked] * (BAND_PAIR - 1 - a), axis=3)
                for a in range(BAND_PAIR)]
    bias = jnp.stack(per_tile, axis=1)
    bias = bias.transpose(0, 4, 5, 1, 2, 3).reshape(groups, 1, key_tiles, tq, cols)
    bias = jnp.concatenate([bias, jnp.full_like(bias, NEG)], axis=1)
    qblk = 3 * A_WIDTH // LANES
    kblk = qblk + B_WIDTH // LANES
    rows = min(BAND_STEP_ROWS, seq)
    return pl.pallas_call(
        _band_attn_kernel,
        grid=(bsz, groups, seq // rows),
        in_specs=[
            pl.BlockSpec((1, 2, key_tiles, tq, cols), lambda b, g, i: (g, 0, 0, 0, 0)),
            pl.BlockSpec((1, rows, LANES), lambda b, g, i: (b, i, qblk + g)),
            pl.BlockSpec((1, seq, LANES), lambda b, g, i: (b, 0, kblk + g)),
            pl.BlockSpec((seq // tq, LANES, tq), lambda b, g, i: (b, g, 0)),
        ],
        out_specs=pl.BlockSpec((1, rows, LANES), lambda b, g, i: (b, i, g)),
        out_shape=jax.ShapeDtypeStruct((bsz, seq, B_WIDTH), BF16),
        scratch_shapes=[pltpu.VMEM((2, key_tiles * tq, cols), F32)],
        compiler_params=pltpu.CompilerParams(
            dimension_semantics=("arbitrary", "arbitrary", "arbitrary"),
            vmem_limit_bytes=VMEM_LIMIT),
        name="band_attn",
    )(bias, proj3, proj3, vtb3)


def _out_router_kernel(h_ref, oa_ref, ob_ref, wa_ref, wb_ref, g1_ref, b1_ref,
                       rwt_ref, rb_ref, tri_ref, h1_ref, idx_ref, gate_ref, rank_ref, cnt_ref,
                       carry_ref, *, steps_per_tile):
    h = h_ref[...]
    mix = (jnp.dot(oa_ref[0], wa_ref[...], preferred_element_type=F32)
           + jnp.dot(ob_ref[0], wb_ref[...], preferred_element_type=F32))
    h1 = _layer_norm(DEEPNORM_ALPHA * h + mix, g1_ref[...], b1_ref[...])
    nchunk = h1.shape[1] // LANES
    for c in range(nchunk):
        h1_ref[pl.ds(c, ROW_TILE, stride=nchunk), :] = h1[:, c * LANES:(c + 1) * LANES]
    lt = lax.dot_general(rwt_ref[...], h1, (((1,), (1,)), ((), ())),
                         precision=lax.Precision.HIGHEST, preferred_element_type=F32)
    lt = lt + rb_ref[...]
    eidx = lax.broadcasted_iota(jnp.int32, lt.shape, 0)
    vals, idxs, hots = [], [], []
    for _ in range(TOP_K):
        mx = jnp.max(lt, axis=0, keepdims=True)
        am = jnp.min(jnp.where(lt == mx, eidx, N_EXPERTS), axis=0, keepdims=True)
        hit = eidx == am
        vals.append(mx)
        idxs.append(am)
        hots.append(jnp.where(hit, 1.0, 0.0))
        lt = jnp.where(hit, -jnp.inf, lt)
    ex = [jnp.exp(v - vals[0]) for v in vals]
    den = ex[0] + ex[1] + ex[2] + ex[3]
    idx_ref[...] = jnp.concatenate(idxs, axis=0)
    gate_ref[...] = jnp.concatenate([e / den for e in ex], axis=0)

    @pl.when(pl.program_id(0) % steps_per_tile == 0)
    def _():
        carry_ref[...] = jnp.zeros_like(carry_ref)

    hot = (hots[0] + hots[1]) + (hots[2] + hots[3])
    before = jnp.dot(hot.astype(BF16), tri_ref[...], preferred_element_type=F32) + carry_ref[...]
    rank_ref[...] = jnp.concatenate(
        [jnp.sum(hk * before, axis=0, keepdims=True) for hk in hots], axis=0).astype(jnp.int32)
    total = carry_ref[...] + jnp.sum(hot, axis=1, keepdims=True)
    carry_ref[...] = total
    cnt_ref[0] = total.astype(jnp.int32)


def _out_router(h0, o_a, o_b, wa, wb, g1, b1, rwt, rb, moe_tile):
    t, d = h0.shape
    nchunk = d // LANES
    row = lambda i: (i, 0)
    fixed = lambda i: (0, 0)
    steps_per_seq = o_a.shape[1] // ROW_TILE
    seq_row = lambda i: (i // steps_per_seq, i % steps_per_seq, 0)
    tri = jnp.triu(jnp.ones((ROW_TILE, ROW_TILE), BF16), k=1)
    return pl.pallas_call(
        functools.partial(_out_router_kernel, steps_per_tile=moe_tile // ROW_TILE),
        grid=(t // ROW_TILE,),
        in_specs=[
            pl.BlockSpec((ROW_TILE, d), row),
            pl.BlockSpec((1, ROW_TILE, A_WIDTH), seq_row),
            pl.BlockSpec((1, ROW_TILE, B_WIDTH), seq_row),
            pl.BlockSpec((A_WIDTH, d), fixed),
            pl.BlockSpec((B_WIDTH, d), fixed),
            pl.BlockSpec((1, d), fixed),
            pl.BlockSpec((1, d), fixed),
            pl.BlockSpec((N_EXPERTS, d), fixed),
            pl.BlockSpec((N_EXPERTS, 1), fixed),
            pl.BlockSpec((ROW_TILE, ROW_TILE), fixed),
        ],
        out_specs=[
            pl.BlockSpec((ROW_TILE * nchunk, LANES), row),
            pl.BlockSpec((TOP_K, ROW_TILE), lambda i: (0, i)),
            pl.BlockSpec((TOP_K, ROW_TILE), lambda i: (0, i)),
            pl.BlockSpec((TOP_K, ROW_TILE), lambda i: (0, i)),
            pl.BlockSpec((1, N_EXPERTS, 1), lambda i: (i, 0, 0)),
        ],
        out_shape=[
            jax.ShapeDtypeStruct((t * nchunk, LANES), F32),
            jax.ShapeDtypeStruct((TOP_K, t), jnp.int32),
            jax.ShapeDtypeStruct((TOP_K, t), F32),
            jax.ShapeDtypeStruct((TOP_K, t), jnp.int32),
            jax.ShapeDtypeStruct((t // ROW_TILE, N_EXPERTS, 1), jnp.int32),
        ],
        scratch_shapes=[pltpu.VMEM((N_EXPERTS, 1), F32)],
        compiler_params=pltpu.CompilerParams(
            dimension_semantics=("arbitrary",), vmem_limit_bytes=VMEM_LIMIT),
        name="out_router",
    )(h0, o_a, o_b, wa, wb, g1, b1, rwt, rb, tri)


def _split_w1_kernel(w_ref, perm_ref, g_ref, l_ref):
    rows = w_ref.shape[1]
    even = (lax.broadcasted_iota(jnp.int32, (rows, LANES), 1) & 1) == 0
    perm = perm_ref[...]
    for g in range(w_ref.shape[2] // (2 * LANES)):
        v0 = w_ref[0, :, 2 * g * LANES:(2 * g + 1) * LANES]
        v1 = w_ref[0, :, (2 * g + 1) * LANES:(2 * g + 2) * LANES]
        glu = jnp.where(even, v0, pltpu.roll(v1, 1, axis=1)).astype(BF16)
        lin = jnp.where(even, pltpu.roll(v0, LANES - 1, axis=1), v1).astype(BF16)
        g_ref[0, :, g * LANES:(g + 1) * LANES] = jnp.dot(
            glu, perm, preferred_element_type=F32).astype(BF16)
        l_ref[0, :, g * LANES:(g + 1) * LANES] = jnp.dot(
            lin, perm, preferred_element_type=F32).astype(BF16)


def _split_w1(w1e):
    e, d, f2 = w1e.shape
    rows = 1024
    half = LANES // 2
    unit = jnp.arange(LANES)
    perm = (jnp.arange(LANES)[:, None] == (2 * (unit % half) + unit // half)[None, :]).astype(BF16)
    return pl.pallas_call(
        _split_w1_kernel,
        grid=(e, d // rows),
        in_specs=[pl.BlockSpec((1, rows, f2), lambda i, j: (i, j, 0)),
                  pl.BlockSpec((LANES, LANES), lambda i, j: (0, 0))],
        out_specs=[pl.BlockSpec((1, rows, f2 // 2), lambda i, j: (i, j, 0)),
                   pl.BlockSpec((1, rows, f2 // 2), lambda i, j: (i, j, 0))],
        out_shape=[jax.ShapeDtypeStruct((e, d, f2 // 2), BF16),
                   jax.ShapeDtypeStruct((e, d, f2 // 2), BF16)],
        compiler_params=pltpu.CompilerParams(
            dimension_semantics=("arbitrary", "arbitrary"), vmem_limit_bytes=VMEM_LIMIT),
        name="split_w1",
    )(w1e, perm)


def _moe_kernel(cnt_ref, offs_ref,
                w1g_ref, w1l_ref, b1g_ref, b1l_ref, w2_ref, b2_ref, g2_ref, bb2_ref,
                pos_hbm, gate_hbm, h_hbm,
                out_hbm, hbuf, ybuf, stage, out_buf, list_tok, list_gate, pos_in, gate_in, n_sorted, sem,
                *, tile):
    b = pl.program_id(0)
    e = pl.program_id(1)
    n_tiles = pl.num_programs(0)
    n_exp = pl.num_programs(1)
    d_model = w1g_ref.shape[1]
    d_ff = w1g_ref.shape[2]
    nchunk = d_model // LANES
    tile_rows = tile * nchunk
    pairs = tile * TOP_K
    n_sort_chunks = pairs // SORT_CHUNK
    half = (b % 2) * pairs
    other_half = pairs - half

    def stage_rows(r):
        return pl.ds(pl.multiple_of(r * nchunk, nchunk), nchunk)

    def listed_rows(p):
        return pl.ds(pl.multiple_of(list_tok[p], nchunk), nchunk)

    def sort_pair(j, first_row, dst_half):
        p = pos_in[j] + dst_half
        list_tok[p] = first_row
        list_gate[p] = gate_in[j]

    def sort_chunks(lo, hi, dst_half):
        def body(jo, c):
            for u in range(SORT_UNROLL):
                j = jo * SORT_UNROLL + u
                sort_pair(j, (j & (tile - 1)) * nchunk, dst_half)
            return c
        lax.fori_loop(lo * (SORT_CHUNK // SORT_UNROLL), hi * (SORT_CHUNK // SORT_UNROLL), body, 0)

    def load_pairs(t):
        rows = pl.ds(pl.multiple_of(t * pairs, pairs), pairs)
        copies = [pltpu.make_async_copy(pos_hbm.at[rows], pos_in, sem.at[2]),
                  pltpu.make_async_copy(gate_hbm.at[rows], gate_in, sem.at[3])]
        for cp in copies:
            cp.start()
        for cp in copies:
            cp.wait()

    @pl.when(jnp.logical_and(b == 0, e == 0))
    def _():
        stage[...] = jnp.zeros_like(stage)

        def pad_body(j, c):
            list_tok[pairs + j] = 0
            list_tok[2 * pairs + j] = 0
            return c
        lax.fori_loop(0, LIST_PAD, pad_body, 0)
        load_pairs(0)
        n_sorted[0] = 0

    @pl.when(e == 0)
    def _():
        src = h_hbm.at[pl.ds(pl.multiple_of(b * tile_rows, tile_rows), tile_rows), :]
        load = pltpu.make_async_copy(src, hbuf, sem.at[0])

        @pl.when(b == 0)
        def _():
            load.start()
            ybuf[...] = jnp.zeros_like(ybuf)
        sort_chunks(jnp.minimum(n_sorted[0], n_sort_chunks), n_sort_chunks, half)
        n_sorted[0] = 0

        @pl.when(b + 1 < n_tiles)
        def _():
            load_pairs(b + 1)
        load.wait()

    n = cnt_ref[b * n_exp + e]
    base = offs_ref[b * n_exp + e] + half

    def run_block(m_rows, p0, nrows):
        def gather_body(ci, c):
            for u in range(ROW_UNROLL):
                r = ci * ROW_UNROLL + u
                stage[stage_rows(r), :] = hbuf[listed_rows(p0 + r), :]
            return c
        lax.fori_loop(0, (nrows + ROW_UNROLL - 1) // ROW_UNROLL, gather_body, 0)

        if m_rows == MOE_BLOCK:
            chunk = jnp.minimum(n_sorted[0], n_sort_chunks - 1)
            n_sorted[0] = n_sorted[0] + 1
            j0 = chunk * SORT_CHUNK
            row0 = (j0 & (tile - 1)) * nchunk
            for u in range(SORT_CHUNK):
                sort_pair(j0 + u, row0 + u * nchunk, other_half)

        x = jnp.concatenate(
            [stage[pl.ds(c, m_rows, stride=nchunk), :] for c in range(nchunk)],
            axis=1).astype(BF16)
        y = b2_ref[0]
        for hh in range(d_ff // FF_CHUNK):
            sl = slice(hh * FF_CHUNK, (hh + 1) * FF_CHUNK)
            hg = jnp.dot(x, w1g_ref[0, :, sl], preferred_element_type=F32) + b1g_ref[0, :, sl]
            hl = jnp.dot(x, w1l_ref[0, :, sl], preferred_element_type=F32) + b1l_ref[0, :, sl]
            xg = jnp.minimum(hg, SWIGLU_LIMIT)
            xl = jnp.clip(hl, -SWIGLU_LIMIT, SWIGLU_LIMIT)
            act = xg * jax.nn.sigmoid(SWIGLU_ALPHA * xg) * (xl + 1.0)
            y = y + jnp.dot(act.astype(BF16), w2_ref[0, sl, :], preferred_element_type=F32)
        for c in range(nchunk):
            stage[pl.ds(c, m_rows, stride=nchunk), :] = y[:, c * LANES:(c + 1) * LANES]

        def add_body(ci, c):
            rows, vals = [], []
            for u in range(SUBLANES):
                r = ci * SUBLANES + u
                dst = listed_rows(p0 + r)
                rows.append(dst)
                vals.append(ybuf[dst, :] + list_gate[p0 + r] * stage[stage_rows(r), :])
            for dst, val in zip(rows, vals):
                ybuf[dst, :] = val
            return c
        n_full = nrows // SUBLANES
        lax.fori_loop(0, n_full, add_body, 0)

        def add_tail(r, c):
            dst = listed_rows(p0 + r)
            ybuf[dst, :] = ybuf[dst, :] + list_gate[p0 + r] * stage[stage_rows(r), :]
            return c
        lax.fori_loop(n_full * SUBLANES, nrows, add_tail, 0)

    n_big = n // MOE_BLOCK
    rem = n - n_big * MOE_BLOCK
    rem_is_big = rem > MOE_BLOCK - SMALL_BLOCK
    n_big_blocks = n_big + rem_is_big.astype(jnp.int32)
    n_small_blocks = jnp.where(rem_is_big, 0, (rem + SMALL_BLOCK - 1) // SMALL_BLOCK)

    def big_body(s, c):
        run_block(MOE_BLOCK, base + s * MOE_BLOCK, jnp.minimum(MOE_BLOCK, n - s * MOE_BLOCK))
        return c
    lax.fori_loop(0, n_big_blocks, big_body, 0)

    def small_body(s, c):
        run_block(SMALL_BLOCK, base + n_big * MOE_BLOCK + s * SMALL_BLOCK,
                  jnp.minimum(SMALL_BLOCK, rem - s * SMALL_BLOCK))
        return c
    lax.fori_loop(0, n_small_blocks, small_body, 0)

    @pl.when(e == n_exp - 1)
    def _():
        def out_copy(c, slot):
            rows = pl.ds(pl.multiple_of(b * tile + c * ROW_TILE, ROW_TILE), ROW_TILE)
            return pltpu.make_async_copy(out_buf.at[slot], out_hbm.at[rows, :], sem.at[4 + slot])

        def chunk_body(c, carry):
            slot = c % 2

            @pl.when(c >= 2)
            def _():
                out_copy(c - 2, slot).wait()
            first = c * (ROW_TILE * nchunk)
            pieces = [DEEPNORM_ALPHA * hbuf[pl.ds(first + k, ROW_TILE, stride=nchunk), :]
                      + ybuf[pl.ds(first + k, ROW_TILE, stride=nchunk), :] for k in range(nchunk)]
            out_buf[slot] = _layer_norm(jnp.concatenate(pieces, axis=1), g2_ref[...], bb2_ref[...])
            out_copy(c, slot).start()
            done = pl.ds(pl.multiple_of(first, ROW_TILE * nchunk), ROW_TILE * nchunk)
            ybuf[done, :] = jnp.zeros((ROW_TILE * nchunk, LANES), F32)

            @pl.when(b + 1 < n_tiles)
            def _():
                nxt = pl.ds(pl.multiple_of((b + 1) * tile_rows + first, ROW_TILE * nchunk), ROW_TILE * nchunk)
                pltpu.make_async_copy(h_hbm.at[nxt, :], hbuf.at[done, :], sem.at[0]).start()
            return carry
        n_out = tile // ROW_TILE
        lax.fori_loop(0, n_out, chunk_body, 0)
        for c in range(max(n_out - 2, 0), n_out):
            out_copy(c, c % 2).wait()


def _moe(h1t, cnt, offs, pos_flat, gate_flat, w1g, w1l, b1g, b1l, w2, b2, g2, bb2, tile):
    n_exp, d, f = w1g.shape
    nchunk = d // LANES
    n_tiles = h1t.shape[0] // (tile * nchunk)
    pairs = tile * TOP_K
    assert tile % SORT_CHUNK == 0
    expert = lambda b, e, cnt, offs: (e, 0, 0)
    grid_spec = pltpu.PrefetchScalarGridSpec(
        num_scalar_prefetch=2,
        grid=(n_tiles, n_exp),
        in_specs=[
            pl.BlockSpec((1, d, f), expert),
            pl.BlockSpec((1, d, f), expert),
            pl.BlockSpec((1, 1, f), expert),
            pl.BlockSpec((1, 1, f), expert),
            pl.BlockSpec((1, f, d), expert),
            pl.BlockSpec((1, 1, d), expert),
            pl.BlockSpec((1, d), lambda b, e, cnt, offs: (0, 0)),
            pl.BlockSpec((1, d), lambda b, e, cnt, offs: (0, 0)),
            pl.BlockSpec(memory_space=pl.ANY),
            pl.BlockSpec(memory_space=pl.ANY),
            pl.BlockSpec(memory_space=pl.ANY),
        ],
        out_specs=pl.BlockSpec(memory_space=pl.ANY),
        scratch_shapes=[
            pltpu.VMEM((tile * nchunk, LANES), F32),
            pltpu.VMEM((tile * nchunk, LANES), F32),
            pltpu.VMEM((MOE_BLOCK * nchunk, LANES), F32),
            pltpu.VMEM((2, ROW_TILE, d), F32),
            pltpu.SMEM((2 * pairs + LIST_PAD,), jnp.int32),
            pltpu.SMEM((2 * pairs + LIST_PAD,), F32),
            pltpu.SMEM((pairs,), jnp.int32),
            pltpu.SMEM((pairs,), F32),
            pltpu.SMEM((1,), jnp.int32),
            pltpu.SemaphoreType.DMA((6,)),
        ],
    )
    return pl.pallas_call(
        functools.partial(_moe_kernel, tile=tile),
        grid_spec=grid_spec,
        out_shape=jax.ShapeDtypeStruct((n_tiles * tile, d), F32),
        compiler_params=pltpu.CompilerParams(
            dimension_semantics=("arbitrary", "arbitrary"), vmem_limit_bytes=MOE_VMEM_LIMIT),
        name="moe",
    )(cnt, offs, w1g, w1l, b1g, b1l, w2, b2, g2, bb2, pos_flat, gate_flat, h1t)


def kernel(x, ln_in_g, ln_in_b, w_in, lambda_q1, lambda_k1, lambda_q2, lambda_k2, subln_g, rel_bias,
           w_out, ln1_g, ln1_b, router_w, router_b, w1, b1, w2, b2, ln2_g, ln2_b):
    bsz, seq, d = x.shape
    t = bsz * seq
    x2 = x.reshape(t, d)
    row = lambda v: v.reshape(1, -1).astype(F32)

    qs = A_HEAD_DIM ** -0.5 * LOG2E
    col_scale = jnp.concatenate([
        jnp.full((A_WIDTH,), qs, F32), jnp.ones((2 * A_WIDTH,), F32),
        jnp.full((B_WIDTH,), B_HEAD_DIM ** -0.5 * LOG2E, F32), jnp.ones((2 * B_WIDTH,), F32)]).reshape(1, -1)
    proj3, vt3, vtb3, h0 = _ln_qkv(x2, row(ln_in_g), row(ln_in_b), w_in[0].astype(BF16), col_scale, seq)

    lam4 = jnp.stack([lambda_q1[0], lambda_k1[0], lambda_q2[0], lambda_k2[0]]).astype(F32)
    o_a = _diff_attn(proj3, vt3, lam4, subln_g[0].astype(F32))
    o_b = _band_attn(proj3, vtb3, rel_bias[0])

    w_o = w_out[0].astype(BF16)
    tile = min(MOE_TILE, t)
    n_tiles = t // tile
    h1t, top_idx, gates, rank, run_cnt = _out_router(
        h0, o_a, o_b,
        w_o[:A_WIDTH], w_o[A_WIDTH:], row(ln1_g[0]), row(ln1_b[0]),
        router_w[0].T.astype(F32), router_b[0].reshape(-1, 1).astype(F32), tile)

    steps = tile // ROW_TILE
    cnt = run_cnt[steps - 1::steps, :, 0]
    offs = jnp.cumsum(cnt, axis=1) - cnt
    hot = top_idx.reshape(TOP_K, n_tiles, tile, 1) == jnp.arange(N_EXPERTS, dtype=jnp.int32)
    pos = rank + jnp.sum(jnp.where(hot, offs[None, :, None, :], 0), axis=-1).reshape(TOP_K, t)
    per_tile = lambda a: a.reshape(TOP_K, n_tiles, tile).transpose(1, 0, 2).reshape(-1)
    w1g, w1l = _split_w1(w1[0])
    b1e = b1[0].astype(F32)[:, None, :]
    out = _moe(h1t, cnt.reshape(-1), offs.reshape(-1), per_tile(pos), per_tile(gates),
               w1g, w1l, b1e[:, :, 0::2], b1e[:, :, 1::2],
               w2[0].astype(BF16), b2[0][:, None, :].astype(F32), row(ln2_g[0]), row(ln2_b[0]), tile)
    return out.reshape(bsz, seq, d)
```
